```python
import math
import jax, jax.numpy as jnp
from jax import lax
import numpy as np

D_MODEL = 1024
BATCH = 2
SEQ = 8192
DEPTH = 1

HG_HEADS = 4
HG_KDIM = 128
HG_VDIM = 128
HG_WIDTH = HG_HEADS * HG_VDIM
HG_FDIM = HG_HEADS * HG_KDIM
HG_CHUNK = 64
ATT_HEADS = 4
ATT_HEAD_DIM = 128
ATT_WIDTH = ATT_HEADS * ATT_HEAD_DIM
IDX_HEADS = 8
IDX_DIM = 64
INDEX_TOPK_MAX = 256
QUERY_BLOCK = 128
D_MIX = HG_WIDTH + ATT_WIDTH
ROPE_THETA = 500000.0
ROPE_FRACTION = 4
D_FF = -(-8 * D_MODEL // (3 * 256)) * 256
N_MOD = 6
EPS = 1e-6

IN_SIZES = (HG_FDIM, HG_FDIM, HG_WIDTH, HG_WIDTH,
            ATT_WIDTH, ATT_WIDTH, ATT_WIDTH,
            IDX_HEADS * IDX_DIM, IDX_DIM, IDX_HEADS)
D_IN = sum(IN_SIZES)

kernel_name = "hymba_hgrn2_dsa_block"


def rmsnorm(x, gain):
    xf = x.astype(jnp.float32)
    y = xf * lax.rsqrt(jnp.mean(xf * xf, axis=-1, keepdims=True) + EPS)
    return (y * gain.astype(jnp.float32)).astype(x.dtype)


def layernorm(x, w, b):
    xf = x.astype(jnp.float32)
    mu = jnp.mean(xf, axis=-1, keepdims=True)
    var = jnp.mean(jnp.square(xf - mu), axis=-1, keepdims=True)
    y = (xf - mu) * lax.rsqrt(var + EPS)
    return (y * w.astype(jnp.float32) + b.astype(jnp.float32)).astype(x.dtype)


def partial_rope(x, positions):
    d = x.shape[-1]
    rot = d // ROPE_FRACTION
    half = rot // 2
    inv_freq = ROPE_THETA ** (-jnp.arange(half, dtype=jnp.float32) / half)
    ang = positions.astype(jnp.float32)[..., None] * inv_freq
    cos = jnp.cos(ang)[:, :, None, :]
    sin = jnp.sin(ang)[:, :, None, :]
    xf = x.astype(jnp.float32)
    x1, x2, xp = xf[..., :half], xf[..., half:rot], xf[..., rot:]
    out = jnp.concatenate([x1 * cos - x2 * sin, x2 * cos + x1 * sin, xp], axis=-1)
    return out.astype(x.dtype)


def hgrn2_mixer(q, f_raw, i, g_out, lower_bound, out_norm):
    B, T, _ = q.shape
    nc = T // HG_CHUNK
    dt = q.dtype
    qf = jax.nn.silu(q.astype(jnp.float32)) * HG_KDIM ** -0.5
    f = lower_bound + (1.0 - lower_bound) * jax.nn.sigmoid(f_raw.astype(jnp.float32))
    k = 1.0 - f
    logf = jnp.log(f)

    def chunks(a, d):
        return a.reshape(B, nc, HG_CHUNK, HG_HEADS, d).transpose(1, 0, 3, 2, 4)

    qc, kc = chunks(qf, HG_KDIM), chunks(k, HG_KDIM)
    gc = jnp.cumsum(chunks(logf, HG_KDIM), axis=3)
    vc = chunks(i.astype(jnp.float32), HG_VDIM)
    causal = jnp.tril(jnp.ones((HG_CHUNK, HG_CHUNK), dtype=bool))

    def step(S, inp):
        qb, kb, vb, gb = inp
        o_inter = jnp.einsum('bhtk,bhkv->bhtv', qb * jnp.exp(gb), S)
        diff = gb[:, :, :, None, :] - gb[:, :, None, :, :]
        decay = jnp.exp(jnp.where(causal[None, None, :, :, None], diff, -jnp.inf))
        A = jnp.einsum('bhtk,bhsk,bhtsk->bhts', qb, kb, decay)
        o_intra = jnp.einsum('bhts,bhsv->bhtv', A, vb)
        g_last = gb[:, :, -1]
        S_new = jnp.exp(g_last)[..., None] * S + jnp.einsum(
            'bhsk,bhsv->bhkv', kb * jnp.exp(g_last[:, :, None, :] - gb), vb)
        return S_new, o_inter + o_intra

    S0 = jnp.zeros((B, HG_HEADS, HG_KDIM, HG_VDIM), jnp.float32)
    _, o = lax.scan(step, S0, (qc, kc, vc, gc))
    o = o.transpose(1, 0, 3, 2, 4).reshape(B, T, HG_HEADS, HG_VDIM)
    o = o * lax.rsqrt(jnp.mean(o * o, axis=-1, keepdims=True) + EPS)
    o = o * out_norm.astype(jnp.float32).reshape(HG_HEADS, HG_VDIM)
    o = o.reshape(B, T, HG_WIDTH) * jax.nn.silu(g_out.astype(jnp.float32))
    return o.astype(dt)


def dsa_mixer(q, k, v, q_idx, k_idx, w_idx, positions, k_norm_w, k_norm_b):
    B, T, _ = q.shape
    topk = min(INDEX_TOPK_MAX, T // 4)
    nb = T // QUERY_BLOCK
    q = partial_rope(q.reshape(B, T, ATT_HEADS, ATT_HEAD_DIM), positions)
    k = partial_rope(k.reshape(B, T, ATT_HEADS, ATT_HEAD_DIM), positions)
    v = v.reshape(B, T, ATT_HEADS, ATT_HEAD_DIM)
    q_idx = partial_rope(q_idx.reshape(B, T, IDX_HEADS, IDX_DIM), positions)
    k_idx = layernorm(k_idx, k_norm_w, k_norm_b)
    k_idx = partial_rope(k_idx[:, :, None, :], positions)[:, :, 0, :]
    w_idx = w_idx * (IDX_HEADS ** -0.5 * IDX_DIM ** -0.5)
    key_pos = jnp.arange(T, dtype=jnp.int32)
    gather = jax.vmap(lambda a, ix: a[ix])
    scale = ATT_HEAD_DIM ** -0.5

    def to_blocks(a):
        return a.reshape(B, nb, QUERY_BLOCK, *a.shape[2:]).swapaxes(0, 1)

    def block(args):
        qi, wi, qb, tq = args
        s = jnp.einsum('bqhd,bsd->bqhs', qi, k_idx)
        score = jnp.einsum('bqh,bqhs->bqs', wi, jax.nn.relu(s)).astype(jnp.float32)
        score = jnp.where(key_pos[None, None, :] <= tq[None, :, None], score, -jnp.inf)
        _, idx = lax.top_k(score, topk)
        valid = idx <= tq[None, :, None]
        ks = gather(k, idx)
        vs = gather(v, idx)
        logits = jnp.einsum('bqhd,bqkhd->bhqk', qb, ks).astype(jnp.float32) * scale
        logits = jnp.where(valid[:, None], logits, -jnp.inf)
        p = jax.nn.softmax(logits, axis=-1).astype(vs.dtype)
        return jnp.einsum('bhqk,bqkhd->bqhd', p, vs)

    o = lax.map(block, (to_blocks(q_idx), to_blocks(w_idx), to_blocks(q),
                        key_pos.reshape(nb, QUERY_BLOCK)))
    return o.swapaxes(0, 1).reshape(B, T, ATT_WIDTH)


def setup_inputs(seed: int = 0) -> dict:
    key = jax.random.key(seed)
    ks = jax.random.split(key, 20)
    nrm = lambda k, shape, s: jax.random.normal(k, shape, jnp.float32) * s
    gain = lambda k, shape: 1.0 + 0.02 * jax.random.normal(k, shape, jnp.float32)
    x = nrm(ks[0], (BATCH, SEQ, D_MODEL), 1.0)
    c = nrm(ks[1], (BATCH, D_MODEL), 1.0)
    positions = (jnp.arange(SEQ, dtype=jnp.int32)[None, :]
                 + jax.random.randint(ks[2], (BATCH, 1), 0, 4096, dtype=jnp.int32))
    return {
        "x": x,
        "c": c,
        "positions": positions,
        "w_ada": nrm(ks[3], (DEPTH, D_MODEL, N_MOD * D_MODEL), 0.5 * D_MODEL ** -0.5),
        "b_ada": nrm(ks[4], (DEPTH, N_MOD * D_MODEL), 0.01),
        "norm_pre_mix": gain(ks[5], (DEPTH, D_MODEL)),
        "norm_post_mix": gain(ks[6], (DEPTH, D_MODEL)),
        "norm_pre_ffn": gain(ks[7], (DEPTH, D_MODEL)),
        "norm_post_ffn": gain(ks[8], (DEPTH, D_MODEL)),
        "w_in": nrm(ks[9], (DEPTH, D_MODEL, D_IN), D_MODEL ** -0.5),
        "hgrn_lower_bound": nrm(ks[10], (DEPTH + 1, HG_FDIM), 1.0),
        "hgrn_out_norm": gain(ks[11], (DEPTH, HG_WIDTH)),
        "idx_k_norm_w": gain(ks[12], (DEPTH, IDX_DIM)),
        "idx_k_norm_b": nrm(ks[13], (DEPTH, IDX_DIM), 0.02),
        "w_out": nrm(ks[14], (DEPTH, D_MIX, D_MODEL), D_MIX ** -0.5),
        "w_ffn_in": nrm(ks[15], (DEPTH, D_MODEL, 2 * D_FF), D_MODEL ** -0.5),
        "w_ffn_out": nrm(ks[16], (DEPTH, D_FF, D_MODEL), D_FF ** -0.5),
    }


def reference(x, c, positions, w_ada, b_ada, norm_pre_mix, norm_post_mix, norm_pre_ffn,
              norm_post_ffn, w_in, hgrn_lower_bound, hgrn_out_norm, idx_k_norm_w, idx_k_norm_b,
              w_out, w_ffn_in, w_ffn_out):
    split_points = np.cumsum(IN_SIZES)[:-1].tolist()
    lb_all = jnp.cumsum(jax.nn.softmax(hgrn_lower_bound.astype(jnp.float32), axis=0), axis=0)
    for layer in range(DEPTH):
        mod = jax.nn.silu(c) @ w_ada[layer] + b_ada[layer]
        sh_m, sc_m, gt_m, sh_f, sc_f, gt_f = [m[:, None, :] for m in jnp.split(mod, N_MOD, axis=-1)]

        h = rmsnorm(x, norm_pre_mix[layer]) * (1.0 + sc_m) + sh_m
        proj = h @ w_in[layer]
        (hq, hf, hi, hg, aq, ak, av, iq, ik, iw) = jnp.split(proj, split_points, axis=-1)
        o_hg = hgrn2_mixer(hq, hf, hi, hg, lb_all[layer], hgrn_out_norm[layer])
        o_att = dsa_mixer(aq, ak, av, iq, ik, iw, positions,
                          idx_k_norm_w[layer], idx_k_norm_b[layer])
        y = jnp.concatenate([o_hg, o_att], axis=-1) @ w_out[layer]
        x = x + gt_m * rmsnorm(y, norm_post_mix[layer])

        h = rmsnorm(x, norm_pre_ffn[layer]) * (1.0 + sc_f) + sh_f
        gate, up = jnp.split(h @ w_ffn_in[layer], 2, axis=-1)
        y = (jax.nn.silu(gate) * up) @ w_ffn_out[layer]
        x = x + gt_f * rmsnorm(y, norm_post_ffn[layer])
    return x
```

```python
import functools

import numpy as np
import jax
import jax.numpy as jnp
from jax import lax
from jax.experimental import pallas as pl
from jax.experimental.pallas import tpu as pltpu

F32 = jnp.float32
BF16 = jnp.bfloat16

HG_HEADS = 4
HG_DIM = 128
ATT_HEADS = 4
ATT_DIM = 128
IDX_HEADS = 8
IDX_DIM = 64
TOPK_MAX = 256
ROPE_THETA = 500000.0
ROPE_FRACTION = 4
N_MOD = 6
EPS = 1e-6

GROUP = 512
LANES = 128
VMEM_LIMIT = 56 * 1024 * 1024

HG_CHUNK = 128
HG_LEVELS = (64, 32, 16, 8, 4, 2, 1)

NEG_BIG = -1e30


def _cparams(sem):
    return pltpu.CompilerParams(dimension_semantics=sem, vmem_limit_bytes=VMEM_LIMIT)


def _resident(shape, index_map):
    return pl.BlockSpec(shape, index_map, pipeline_mode=pl.Buffered(1))


def _split_bf16(a):
    hi = a.astype(BF16)
    lo = (a - hi.astype(F32)).astype(BF16)
    return hi, lo


def _dot(a, b):
    return jnp.dot(a, b, preferred_element_type=F32)


def _dot_nt(a, b):
    return lax.dot_general(a, b, (((1,), (1,)), ((), ())), preferred_element_type=F32)


def _silu(a):
    return a * jax.nn.sigmoid(a)


def _adaln_kernel(c_ref, w_ref, b_ref, o_ref):
    a = _silu(c_ref[...])
    a_hi, a_lo = _split_bf16(a)
    w_hi, w_lo = _split_bf16(w_ref[...])
    acc = _dot(a_hi, w_hi) + (_dot(a_hi, w_lo) + _dot(a_lo, w_hi))
    o_ref[...] = acc + b_ref[...]


def _adaln(c, w, b):
    B, D = c.shape
    N = w.shape[1]
    return pl.pallas_call(
        _adaln_kernel,
        grid=(N // D,),
        in_specs=[pl.BlockSpec((B, D), lambda j: (0, 0)),
                  pl.BlockSpec((D, D), lambda j: (0, j)),
                  pl.BlockSpec((1, D), lambda j: (0, j))],
        out_specs=pl.BlockSpec((B, D), lambda j: (0, j)),
        out_shape=jax.ShapeDtypeStruct((B, N), F32),
        compiler_params=_cparams(("arbitrary",)),
        name="adaln",
    )(c, w, b.reshape(1, N))


def _rope_table_kernel(pos_ref, fa_ref, fb_ref, o_ref):
    pos = pos_ref[0].astype(F32)
    ang_a = fa_ref[...] * pos
    ang_b = fb_ref[...] * pos
    ha = fa_ref.shape[0]
    hb = fb_ref.shape[0]
    o_ref[0, 0:ha, :] = jnp.cos(ang_a)
    o_ref[0, ha:2 * ha, :] = jnp.sin(ang_a)
    o_ref[0, 2 * ha:2 * ha + hb, :] = jnp.cos(ang_b)
    o_ref[0, 2 * ha + hb:2 * ha + 2 * hb, :] = jnp.sin(ang_b)


def _rope_tables(positions):
    B, T = positions.shape
    ha = ATT_DIM // ROPE_FRACTION // 2
    hb = IDX_DIM // ROPE_FRACTION // 2
    fa = (ROPE_THETA ** (-jnp.arange(ha, dtype=F32) / ha)).reshape(ha, 1)
    fb = (ROPE_THETA ** (-jnp.arange(hb, dtype=F32) / hb)).reshape(hb, 1)
    rows = 2 * ha + 2 * hb
    tab = pl.pallas_call(
        _rope_table_kernel,
        grid=(B,),
        in_specs=[pl.BlockSpec((1, 1, T), lambda b: (b, 0, 0)),
                  pl.BlockSpec((ha, 1), lambda b: (0, 0)),
                  pl.BlockSpec((hb, 1), lambda b: (0, 0))],
        out_specs=pl.BlockSpec((1, rows, T), lambda b: (b, 0, 0)),
        out_shape=jax.ShapeDtypeStruct((B, rows, T), F32),
        compiler_params=_cparams(("arbitrary",)),
        name="rope_table",
    )(positions.reshape(B, 1, T), fa, fb)
    tab = jnp.swapaxes(tab, 1, 2).reshape(B * T, rows)
    ca, sa = tab[:, 0:ha], tab[:, ha:2 * ha]
    cb, sb = tab[:, 2 * ha:2 * ha + hb], tab[:, 2 * ha + hb:]
    one_a = jnp.ones((B * T, ATT_DIM - 2 * ha), F32)
    one_b = jnp.ones((B * T, IDX_DIM - 2 * hb), F32)
    cos_a = jnp.concatenate([ca, ca, one_a], axis=1)
    sin_a = jnp.concatenate([-sa, sa, 0.0 * one_a], axis=1)
    cos_b = jnp.concatenate([cb, cb, one_b] * (LANES // IDX_DIM), axis=1)
    sin_b = jnp.concatenate([-sb, sb, 0.0 * one_b] * (LANES // IDX_DIM), axis=1)
    return cos_a, sin_a, cos_b, sin_b


def _rope(xb, cos, sin, half, period):
    lane = lax.broadcasted_iota(jnp.int32, xb.shape, 1)
    fwd = pltpu.roll(xb, LANES - half, axis=1)
    bwd = pltpu.roll(xb, half, axis=1)
    partner = jnp.where((lane & (period - 1)) < half, fwd, bwd)
    return xb * cos + partner * sin


def _in_proj_kernel(x_ref, mod_ref, gain_ref, w_ref, lbp_ref, lnw_ref, lnb_ref,
                    cosa_ref, sina_ref, cosb_ref, sinb_ref,
                    hq_ref, lf_ref, hk_ref, hv_ref, hg_ref,
                    aq_ref, ak_ref, av_ref, iq_ref, ik_ref, iw_ref):
    x = x_ref[...]
    ms = jnp.mean(x * x, axis=-1, keepdims=True)
    mod = mod_ref[0]
    h = x * lax.rsqrt(ms + EPS) * gain_ref[...]
    h = h * (1.0 + mod[1:2, :]) + mod[0:1, :]
    hb = h.astype(BF16)

    def proj(g, width=GROUP):
        return _dot(hb, w_ref[:, g * GROUP:g * GROUP + width])

    hq_ref[...] = _silu(proj(0)) * (HG_DIM ** -0.5)
    a = lbp_ref[...]
    amax = jnp.max(a, axis=0, keepdims=True)
    e = jnp.exp(a - amax)
    lb = e[0:1, :] / jnp.sum(e, axis=0, keepdims=True)
    fr = proj(1)
    f = lb + (1.0 - lb) * jax.nn.sigmoid(fr)
    lf_ref[...] = jnp.log(f)
    hk_ref[...] = (1.0 - lb) * jax.nn.sigmoid(-fr)
    hv_ref[...] = proj(2)
    hg_ref[...] = _silu(proj(3))

    cosa, sina = cosa_ref[...], sina_ref[...]
    half_a = ATT_DIM // ROPE_FRACTION // 2
    q = proj(4)
    k = proj(5)
    for hh in range(ATT_HEADS):
        sl = slice(hh * ATT_DIM, (hh + 1) * ATT_DIM)
        aq_ref[:, sl] = (_rope(q[:, sl], cosa, sina, half_a, ATT_DIM) * (ATT_DIM ** -0.5)).astype(BF16)
        ak_ref[:, sl] = _rope(k[:, sl], cosa, sina, half_a, ATT_DIM).astype(BF16)
    av_ref[...] = proj(6).astype(BF16)

    cosb, sinb = cosb_ref[...], sinb_ref[...]
    half_b = IDX_DIM // ROPE_FRACTION // 2
    qi = proj(7)
    for cc in range(GROUP // LANES):
        sl = slice(cc * LANES, (cc + 1) * LANES)
        iq_ref[:, sl] = _rope(qi[:, sl], cosb, sinb, half_b, IDX_DIM).astype(BF16)
    kw = proj(8, 2 * LANES)
    ki = kw[:, 0:LANES]
    lane = lax.broadcasted_iota(jnp.int32, ki.shape, 1)
    real = lane < IDX_DIM
    mu = jnp.sum(ki, axis=-1, keepdims=True) * (1.0 / IDX_DIM)
    d = jnp.where(real, ki - mu, 0.0)
    var = jnp.sum(d * d, axis=-1, keepdims=True) * (1.0 / IDX_DIM)
    kn = d * lax.rsqrt(var + EPS) * lnw_ref[...] + lnb_ref[...]
    kn = _rope(kn, cosb, sinb, half_b, IDX_DIM)
    ik_ref[...] = kn[:, 0:IDX_DIM].astype(BF16)
    iw_ref[...] = kw[:, LANES:LANES + IDX_HEADS] * (IDX_HEADS ** -0.5 * IDX_DIM ** -0.5)


def _in_proj(x2, mod3, gain, w_pad, lbp, lnw, lnb, tabs, T, tm):
    M, D = x2.shape
    nt = T // tm
    wcols = w_pad.shape[1]
    row = lambda i: (i, 0)
    const = lambda i: (0, 0)
    f32o = jax.ShapeDtypeStruct((M, GROUP), F32)
    bf16o = jax.ShapeDtypeStruct((M, GROUP), BF16)
    tab_spec = pl.BlockSpec((tm, LANES), row)
    grp_spec = pl.BlockSpec((tm, GROUP), row)
    return pl.pallas_call(
        _in_proj_kernel,
        grid=(M // tm,),
        in_specs=[pl.BlockSpec((tm, D), row),
                  pl.BlockSpec((1, N_MOD, D), lambda i: (i // nt, 0, 0)),
                  pl.BlockSpec((1, D), const),
                  _resident((D, wcols), const),
                  pl.BlockSpec(lbp.shape, const),
                  pl.BlockSpec((1, LANES), const),
                  pl.BlockSpec((1, LANES), const),
                  tab_spec, tab_spec, tab_spec, tab_spec],
        out_specs=[grp_spec] * 9 + [pl.BlockSpec((tm, IDX_DIM), row), pl.BlockSpec((tm, IDX_HEADS), row)],
        out_shape=[f32o] * 5 + [bf16o] * 4 + [jax.ShapeDtypeStruct((M, IDX_DIM), BF16),
                                              jax.ShapeDtypeStruct((M, IDX_HEADS), F32)],
        compiler_params=_cparams(("parallel",)),
        name="in_proj",
    )(x2, mod3, gain, w_pad, lbp, lnw, lnb, *tabs)


def _hgrn_consts():
    C = HG_CHUNK
    t = np.arange(C)
    tri = (t[None, :] <= t[:, None]).astype(np.float32)
    blocks = [tri]
    for m in HG_LEVELS:
        split = (t // (2 * m)) * (2 * m) + m - 1
        blocks.append(tri - tri[split])
    mall = np.concatenate(blocks, axis=0)
    x = t[:, None] ^ t[None, :]
    lvl = np.full((C, C), len(HG_LEVELS) + 1, np.int32)
    for li, m in enumerate(HG_LEVELS):
        lvl[(t[:, None] > t[None, :]) & (x >= m) & (x < 2 * m)] = li
    lvl[t[:, None] == t[None, :]] = len(HG_LEVELS)
    return jnp.asarray(mall, BF16), jnp.asarray(lvl)


def _hgrn_kernel(hq_ref, lf_ref, hk_ref, hv_ref, hg_ref, mall_ref, lvl_ref, onorm_ref, o_ref, st_ref):
    C = HG_CHUNK
    nl = len(HG_LEVELS)

    @pl.when(pl.program_id(2) == 0)
    def _():
        st_ref[...] = jnp.zeros_like(st_ref)

    lvl = lvl_ref[...]
    row = lax.broadcasted_iota(jnp.int32, (C, HG_DIM), 0)
    mall = mall_ref[...]
    onorm = onorm_ref[...]

    for ci in range(hq_ref.shape[0] // C):
        rs = slice(ci * C, (ci + 1) * C)
        q, lf, k, v = hq_ref[rs, :], lf_ref[rs, :], hk_ref[rs, :], hv_ref[rs, :]
        lf_hi, lf_lo = _split_bf16(lf)
        dall = _dot(mall, lf_hi) + _dot(mall, lf_lo)
        g = dall[0:C]
        qb, kb, vb = q.astype(BF16), k.astype(BF16), v.astype(BF16)

        a = jnp.zeros((C, C), F32)
        for li, m in enumerate(HG_LEVELS):
            dm = dall[(li + 1) * C:(li + 2) * C]
            xm = (jnp.where((row & m) != 0, q, k) * jnp.exp(-jnp.abs(dm))).astype(BF16)
            a = jnp.where(lvl == li, _dot_nt(xm, xm), a)
        a = jnp.where(lvl == nl, _dot_nt(qb, kb), a)

        st = st_ref[...]
        o = _dot_nt((q * jnp.exp(g)).astype(BF16), st.astype(BF16)) + _dot(a.astype(BF16), vb)
        g_last = g[C - 1:C, :]
        kd = (k * jnp.exp(g_last - g)).astype(BF16)
        st_ref[...] = jnp.exp(g_last) * st + _dot(v.T.astype(BF16), kd)

        o = o * lax.rsqrt(jnp.mean(o * o, axis=-1, keepdims=True) + EPS)
        o_ref[rs, :] = (o * onorm * hg_ref[rs, :]).astype(BF16)


def _hgrn(hq, lf, hk, hv, hg, onorm, B, T, ct):
    M = hq.shape[0]
    nct = T // ct
    mall, lvl = _hgrn_consts()
    blk = pl.BlockSpec((ct, HG_DIM), lambda b, h, c: (b * nct + c, h))
    const = lambda b, h, c: (0, 0)
    return pl.pallas_call(
        _hgrn_kernel,
        grid=(B, HG_HEADS, nct),
        in_specs=[blk] * 5 + [pl.BlockSpec(mall.shape, const), pl.BlockSpec(lvl.shape, const),
                              pl.BlockSpec((1, HG_DIM), lambda b, h, c: (0, h))],
        out_specs=blk,
        out_shape=jax.ShapeDtypeStruct((M, GROUP), BF16),
        scratch_shapes=[pltpu.VMEM((HG_DIM, HG_DIM), F32)],
        compiler_params=_cparams(("parallel", "parallel", "arbitrary")),
        name="hgrn2",
    )(hq, lf, hk, hv, hg, mall, lvl, onorm)


def _dsa_kernel(iq_ref, iw_ref, ik_ref, aq_ref, ak_ref, av_ref, o_ref,
                sc_ref, wb_ref, acc_ref, *, qb, kt, topk, seq):
    i = pl.program_id(1)
    q0 = i * qb
    nkt = (q0 + qb + kt - 1) // kt
    qpos = q0 + lax.broadcasted_iota(jnp.int32, (qb, 1), 0)
    kk = jnp.minimum(qpos + 1, topk).astype(F32)

    for h in range(IDX_HEADS):
        wb_ref[h] = jnp.broadcast_to(iw_ref[:, h:h + 1], (qb, LANES))

    def score_tile(j, carry):
        mn, mx = carry
        ks = pl.multiple_of(j * kt, kt)
        ki = ik_ref[pl.ds(ks, kt), :]
        kpos = ks + lax.broadcasted_iota(jnp.int32, (qb, kt), 1)
        causal = kpos <= qpos
        s = jnp.zeros((qb, kt), F32)
        for h in range(IDX_HEADS):
            sh = _dot_nt(iq_ref[:, h * IDX_DIM:(h + 1) * IDX_DIM], ki)
            wh = jnp.concatenate([wb_ref[h]] * (kt // LANES), axis=1)
            s = s + wh * jnp.maximum(sh, 0.0)
        sc_ref[:, pl.ds(ks, kt)] = jnp.where(causal, s, -jnp.inf)
        mn = jnp.minimum(mn, jnp.min(jnp.where(causal, s, jnp.inf), axis=1, keepdims=True))
        mx = jnp.maximum(mx, jnp.max(jnp.where(causal, s, -jnp.inf), axis=1, keepdims=True))
        return mn, mx

    mn, mx = lax.fori_loop(0, nkt, score_tile,
                           (jnp.full((qb, 1), jnp.inf, F32), jnp.full((qb, 1), -jnp.inf, F32)))

    def count_ge(th):
        def body(j, c):
            ks = pl.multiple_of(j * kt, kt)
            return c + jnp.sum((sc_ref[:, pl.ds(ks, kt)] >= th).astype(F32), axis=1, keepdims=True)
        return lax.fori_loop(0, nkt, body, jnp.zeros((qb, 1), F32))

    lo0 = mn
    hi0 = mx + (mx - mn) + jnp.abs(mx) * 1e-6 + 1e-30
    cnt_lo0 = (qpos + 1).astype(F32)
    cnt_hi0 = jnp.zeros((qb, 1), F32)

    def midpoint(lo, hi):
        return lo + 0.5 * (hi - lo)

    def active(lo, hi, cnt_lo):
        mid = midpoint(lo, hi)
        return (cnt_lo != kk) & (mid > lo) & (mid < hi)

    def bis_cond(c):
        it, lo, hi, cnt_lo, cnt_hi = c
        return (it < 200) & (jnp.max(active(lo, hi, cnt_lo).astype(jnp.int32)) > 0)

    def bis_body(c):
        it, lo, hi, cnt_lo, cnt_hi = c
        mid = midpoint(lo, hi)
        cnt = count_ge(mid)
        act = active(lo, hi, cnt_lo)
        up = act & (cnt >= kk)
        dn = act & (cnt < kk)
        return (it + 1, jnp.where(up, mid, lo), jnp.where(dn, mid, hi),
                jnp.where(up, cnt, cnt_lo), jnp.where(dn, cnt, cnt_hi))

    _, lo, hi, cnt_lo, cnt_hi = lax.while_loop(bis_cond, bis_body, (jnp.int32(0), lo0, hi0, cnt_lo0, cnt_hi0))

    need = kk - cnt_hi
    tied = cnt_lo != kk

    def count_tie(jm):
        def body(j, c):
            ks = pl.multiple_of(j * kt, kt)
            s = sc_ref[:, pl.ds(ks, kt)]
            kpos = ks + lax.broadcasted_iota(jnp.int32, (qb, kt), 1)
            hit = (s >= lo) & (s < hi) & (kpos <= jm)
            return c + jnp.sum(hit.astype(F32), axis=1, keepdims=True)
        return lax.fori_loop(0, nkt, body, jnp.zeros((qb, 1), F32))

    def tie_body(_, c):
        jlo, jhi = c
        jm = (jlo + jhi) // 2
        ok = count_tie(jm) >= need
        return jnp.where(ok, jlo, jm), jnp.where(ok, jm, jhi)

    def resolve_ties():
        steps = int(np.ceil(np.log2(seq))) + 1
        _, jhi = lax.fori_loop(0, steps, tie_body,
                               (jnp.full((qb, 1), -1, jnp.int32), jnp.full((qb, 1), seq - 1, jnp.int32)))
        return jnp.where(tied, jhi, seq)

    jmax = lax.cond(jnp.max(tied.astype(jnp.int32)) > 0, resolve_ties,
                    lambda: jnp.full((qb, 1), seq, jnp.int32))

    acc_ref[...] = jnp.zeros_like(acc_ref)

    def attn_tile(j, carry):
        ms, ls = carry
        ks = pl.multiple_of(j * kt, kt)
        s = sc_ref[:, pl.ds(ks, kt)]
        kpos = ks + lax.broadcasted_iota(jnp.int32, (qb, kt), 1)
        sel = (s >= hi) | ((s >= lo) & (kpos <= jmax))
        bias = jnp.where(sel, 0.0, NEG_BIG)
        new_ms, new_ls = [], []
        for h in range(ATT_HEADS):
            sl = slice(h * ATT_DIM, (h + 1) * ATT_DIM)
            lg = _dot_nt(aq_ref[:, sl], ak_ref[pl.ds(ks, kt), sl]) + bias
            m_new = jnp.maximum(ms[h], jnp.max(lg, axis=1, keepdims=True))
            alpha = jnp.exp(ms[h] - m_new)
            p = jnp.exp(lg - m_new)
            new_ls.append(alpha * ls[h] + jnp.sum(p, axis=1, keepdims=True))
            new_ms.append(m_new)
            acc_ref[:, sl] = alpha * acc_ref[:, sl] + _dot(p.astype(BF16), av_ref[pl.ds(ks, kt), sl])
        return tuple(new_ms), tuple(new_ls)

    m0 = tuple(jnp.full((qb, 1), NEG_BIG, F32) for _ in range(ATT_HEADS))
    l0 = tuple(jnp.zeros((qb, 1), F32) for _ in range(ATT_HEADS))
    _, ls = lax.fori_loop(0, nkt, attn_tile, (m0, l0))
    for h in range(ATT_HEADS):
        sl = slice(h * ATT_DIM, (h + 1) * ATT_DIM)
        o_ref[:, sl] = (acc_ref[:, sl] / ls[h]).astype(BF16)


def _dsa(iq, iw, ik, aq, ak, av, B, T, qb, kt):
    M = iq.shape[0]
    nq = T // qb
    topk = min(TOPK_MAX, T // 4)
    qrow = lambda b, i: (b * nq + i, 0)
    brow = lambda b, i: (b, 0)
    kern = functools.partial(_dsa_kernel, qb=qb, kt=kt, topk=topk, seq=T)
    return pl.pallas_call(
        kern,
        grid=(B, nq),
        in_specs=[pl.BlockSpec((qb, GROUP), qrow),
                  pl.BlockSpec((qb, IDX_HEADS), qrow),
                  _resident((T, IDX_DIM), brow),
                  pl.BlockSpec((qb, GROUP), qrow),
                  _resident((T, GROUP), brow),
                  _resident((T, GROUP), brow)],
        out_specs=pl.BlockSpec((qb, GROUP), qrow),
        out_shape=jax.ShapeDtypeStruct((M, GROUP), BF16),
        scratch_shapes=[pltpu.VMEM((qb, T), F32),
                        pltpu.VMEM((IDX_HEADS, qb, LANES), F32),
                        pltpu.VMEM((qb, GROUP), F32)],
        compiler_params=_cparams(("parallel", "arbitrary")),
        name="dsa",
    )(iq, iw, ik, aq, ak, av)


def _rms(a, gain):
    return a * lax.rsqrt(jnp.mean(a * a, axis=-1, keepdims=True) + EPS) * gain


def _out_ffn_kernel(ohg_ref, oatt_ref, x_ref, mod_ref, gains_ref, wo_ref, wi_ref, wf_ref, o_ref, acc_ref,
                    *, d_ff, fc):
    mod = mod_ref[0]
    gains = gains_ref[...]
    hw = ohg_ref.shape[1]
    y = _dot(ohg_ref[...], wo_ref[0:hw, :]) + _dot(oatt_ref[...], wo_ref[hw:, :])
    x1 = x_ref[...] + mod[2:3, :] * _rms(y, gains[0:1, :])
    h2 = (_rms(x1, gains[1:2, :]) * (1.0 + mod[4:5, :]) + mod[3:4, :]).astype(BF16)
    for c in range(d_ff // fc):
        gate = _dot(h2, wi_ref[:, c * fc:(c + 1) * fc])
        up = _dot(h2, wi_ref[:, d_ff + c * fc:d_ff + (c + 1) * fc])
        part = _dot((_silu(gate) * up).astype(BF16), wf_ref[c * fc:(c + 1) * fc, :])
        if c == 0:
            acc_ref[...] = part
        else:
            acc_ref[...] += part
    o_ref[...] = x1 + mod[5:6, :] * _rms(acc_ref[...], gains[2:3, :])


def _out_ffn(ohg, oatt, x2, mod3, gains, wo, wi, wf, T, tm):
    M, D = x2.shape
    nt = T // tm
    d_ff = wf.shape[0]
    row = lambda i: (i, 0)
    const = lambda i: (0, 0)
    kern = functools.partial(_out_ffn_kernel, d_ff=d_ff, fc=256)
    return pl.pallas_call(
        kern,
        grid=(M // tm,),
        in_specs=[pl.BlockSpec((tm, GROUP), row),
                  pl.BlockSpec((tm, GROUP), row),
                  pl.BlockSpec((tm, D), row),
                  pl.BlockSpec((1, N_MOD, D), lambda i: (i // nt, 0, 0)),
                  pl.BlockSpec((3, D), const),
                  _resident(wo.shape, const),
                  _resident(wi.shape, const),
                  _resident(wf.shape, const)],
        out_specs=pl.BlockSpec((tm, D), row),
        out_shape=jax.ShapeDtypeStruct((M, D), F32),
        scratch_shapes=[pltpu.VMEM((tm, D), F32)],
        compiler_params=_cparams(("parallel",)),
        name="out_ffn",
    )(ohg, oatt, x2, mod3, gains, wo, wi, wf)


def kernel(x, c, positions, w_ada, b_ada, norm_pre_mix, norm_post_mix, norm_pre_ffn, norm_post_ffn, w_in,
           hgrn_lower_bound, hgrn_out_norm, idx_k_norm_w, idx_k_norm_b, w_out, w_ffn_in, w_ffn_out):
    B, T, D = x.shape
    depth = w_ada.shape[0]
    assert depth == 1 and hgrn_lower_bound.shape[0] == 2
    assert T % 512 == 0 and D % LANES == 0
    tm = 512

    mod3 = _adaln(c, w_ada[0], b_ada[0]).reshape(B, N_MOD, D)
    tabs = _rope_tables(positions)

    main = 8 * GROUP
    w = w_in[0]
    zpad = lambda n: jnp.zeros((D, n), w.dtype)
    w_pad = jnp.concatenate([w[:, :main], w[:, main:main + IDX_DIM], zpad(LANES - IDX_DIM),
                             w[:, main + IDX_DIM:], zpad(LANES - IDX_HEADS)], axis=1).astype(BF16)
    lnw = jnp.pad(idx_k_norm_w[0], (0, LANES - IDX_DIM)).reshape(1, LANES)
    lnb = jnp.pad(idx_k_norm_b[0], (0, LANES - IDX_DIM)).reshape(1, LANES)

    x2 = x.reshape(B * T, D)
    hq, lf, hk, hv, hg, aq, ak, av, iq, ik, iw = _in_proj(
        x2, mod3, norm_pre_mix, w_pad, hgrn_lower_bound, lnw, lnb, tabs, T, tm)

    o_hg = _hgrn(hq, lf, hk, hv, hg, hgrn_out_norm, B, T, 512)
    o_att = _dsa(iq, iw, ik, aq, ak, av, B, T, 128, 512)

    gains = jnp.concatenate([norm_post_mix, norm_pre_ffn, norm_post_ffn], axis=0)
    out = _out_ffn(o_hg, o_att, x2, mod3, gains, w_out[0].astype(BF16), w_ffn_in[0].astype(BF16),
                   w_ffn_out[0].astype(BF16), T, tm)
    return out.reshape(B, T, D)
```

```python
import functools

import numpy as np
import jax
import jax.numpy as jnp
from jax import lax
from jax.experimental import pallas as pl
from jax.experimental.pallas import tpu as pltpu

F32 = jnp.float32
BF16 = jnp.bfloat16

HG_HEADS = 4
HG_DIM = 128
ATT_HEADS = 4
ATT_DIM = 128
IDX_HEADS = 8
IDX_DIM = 64
TOPK_MAX = 256
ROPE_THETA = 500000.0
ROPE_FRACTION = 4
N_MOD = 6
EPS = 1e-6

GROUP = 512
LANES = 128
VMEM_LIMIT = 56 * 1024 * 1024

HG_CHUNK = 128
HG_LEVELS = (64, 32, 16, 8, 4, 2, 1)

NEG_BIG = -1e30
Q_SCALE = ATT_DIM ** -0.5 * 1.4426950408889634
V_ROWS = ATT_DIM + 16


def _cparams(sem):
    return pltpu.CompilerParams(dimension_semantics=sem, vmem_limit_bytes=VMEM_LIMIT)


def _resident(shape, index_map):
    return pl.BlockSpec(shape, index_map, pipeline_mode=pl.Buffered(1))


def _split_bf16(a):
    hi = a.astype(BF16)
    lo = (a - hi.astype(F32)).astype(BF16)
    return hi, lo


def _dot(a, b):
    return jnp.dot(a, b, preferred_element_type=F32)


def _dot_nt(a, b):
    return lax.dot_general(a, b, (((1,), (1,)), ((), ())), preferred_element_type=F32)


def _silu(a):
    return a * jax.nn.sigmoid(a)


def _adaln_kernel(c_ref, w_ref, b_ref, o_ref):
    a = _silu(c_ref[...])
    a_hi, a_lo = _split_bf16(a)
    w_hi, w_lo = _split_bf16(w_ref[...])
    acc = _dot(a_hi, w_hi) + (_dot(a_hi, w_lo) + _dot(a_lo, w_hi))
    o_ref[...] = acc + b_ref[...]


def _adaln(c, w, b):
    B, D = c.shape
    N = w.shape[1]
    return pl.pallas_call(
        _adaln_kernel,
        grid=(N // D,),
        in_specs=[pl.BlockSpec((B, D), lambda j: (0, 0)),
                  pl.BlockSpec((D, D), lambda j: (0, j)),
                  pl.BlockSpec((1, D), lambda j: (0, j))],
        out_specs=pl.BlockSpec((B, D), lambda j: (0, j)),
        out_shape=jax.ShapeDtypeStruct((B, N), F32),
        compiler_params=_cparams(("arbitrary",)),
        name="adaln",
    )(c, w, b.reshape(1, N))


def _rope_table_kernel(pos_ref, fa_ref, fb_ref, o_ref):
    pos = pos_ref[0].astype(F32)
    ang_a = fa_ref[...] * pos
    ang_b = fb_ref[...] * pos
    ha = fa_ref.shape[0]
    hb = fb_ref.shape[0]
    o_ref[0, 0:ha, :] = jnp.cos(ang_a)
    o_ref[0, ha:2 * ha, :] = jnp.sin(ang_a)
    o_ref[0, 2 * ha:2 * ha + hb, :] = jnp.cos(ang_b)
    o_ref[0, 2 * ha + hb:2 * ha + 2 * hb, :] = jnp.sin(ang_b)


def _rope_tables(positions):
    B, T = positions.shape
    ha = ATT_DIM // ROPE_FRACTION // 2
    hb = IDX_DIM // ROPE_FRACTION // 2
    fa = (ROPE_THETA ** (-jnp.arange(ha, dtype=F32) / ha)).reshape(ha, 1)
    fb = (ROPE_THETA ** (-jnp.arange(hb, dtype=F32) / hb)).reshape(hb, 1)
    rows = 2 * ha + 2 * hb
    tab = pl.pallas_call(
        _rope_table_kernel,
        grid=(B,),
        in_specs=[pl.BlockSpec((1, 1, T), lambda b: (b, 0, 0)),
                  pl.BlockSpec((ha, 1), lambda b: (0, 0)),
                  pl.BlockSpec((hb, 1), lambda b: (0, 0))],
        out_specs=pl.BlockSpec((1, rows, T), lambda b: (b, 0, 0)),
        out_shape=jax.ShapeDtypeStruct((B, rows, T), F32),
        compiler_params=_cparams(("arbitrary",)),
        name="rope_table",
    )(positions.reshape(B, 1, T), fa, fb)
    tab = jnp.swapaxes(tab, 1, 2).reshape(B * T, rows)
    ca, sa = tab[:, 0:ha], tab[:, ha:2 * ha]
    cb, sb = tab[:, 2 * ha:2 * ha + hb], tab[:, 2 * ha + hb:]
    one_a = jnp.ones((B * T, ATT_DIM - 2 * ha), F32)
    one_b = jnp.ones((B * T, IDX_DIM - 2 * hb), F32)
    cos_a = jnp.concatenate([ca, ca, one_a], axis=1)
    sin_a = jnp.concatenate([-sa, sa, 0.0 * one_a], axis=1)
    cos_b = jnp.concatenate([cb, cb, one_b] * (LANES // IDX_DIM), axis=1)
    sin_b = jnp.concatenate([-sb, sb, 0.0 * one_b] * (LANES // IDX_DIM), axis=1)
    return cos_a, sin_a, cos_b, sin_b


def _rope(xb, cos, sin, half, period):
    lane = lax.broadcasted_iota(jnp.int32, xb.shape, 1)
    fwd = pltpu.roll(xb, LANES - half, axis=1)
    bwd = pltpu.roll(xb, half, axis=1)
    partner = jnp.where((lane & (period - 1)) < half, fwd, bwd)
    return xb * cos + partner * sin


def _in_proj_kernel(x_ref, mod_ref, gain_ref, w_ref, lbp_ref, lnw_ref, lnb_ref,
                    cosa_ref, sina_ref, cosb_ref, sinb_ref,
                    hq_ref, lf_ref, hk_ref, hv_ref, hg_ref,
                    aq_ref, ak_ref, avt_ref, iq_ref, ik_ref, iwt_ref):
    x = x_ref[...]
    ms = jnp.mean(x * x, axis=-1, keepdims=True)
    mod = mod_ref[0]
    h = x * lax.rsqrt(ms + EPS) * gain_ref[...]
    h = h * (1.0 + mod[1:2, :]) + mod[0:1, :]
    hb = h.astype(BF16)

    def proj(g, width=GROUP):
        return _dot(hb, w_ref[:, g * GROUP:g * GROUP + width])

    hq_ref[...] = _silu(proj(0)) * (HG_DIM ** -0.5)
    a = lbp_ref[...]
    amax = jnp.max(a, axis=0, keepdims=True)
    e = jnp.exp(a - amax)
    lb = e[0:1, :] / jnp.sum(e, axis=0, keepdims=True)
    fr = proj(1)
    f = lb + (1.0 - lb) * jax.nn.sigmoid(fr)
    lf_ref[...] = jnp.log(f)
    hk_ref[...] = (1.0 - lb) * jax.nn.sigmoid(-fr)
    hv_ref[...] = proj(2)
    hg_ref[...] = _silu(proj(3))

    cosa, sina = cosa_ref[...], sina_ref[...]
    half_a = ATT_DIM // ROPE_FRACTION // 2
    q = proj(4)
    k = proj(5)
    for hh in range(ATT_HEADS):
        sl = slice(hh * ATT_DIM, (hh + 1) * ATT_DIM)
        aq_ref[:, sl] = (_rope(q[:, sl], cosa, sina, half_a, ATT_DIM) * Q_SCALE).astype(BF16)
        ak_ref[:, sl] = _rope(k[:, sl], cosa, sina, half_a, ATT_DIM).astype(BF16)
    v = proj(6)
    for hh in range(ATT_HEADS):
        r0 = hh * V_ROWS
        avt_ref[0, 0, r0:r0 + ATT_DIM, :] = v[:, hh * ATT_DIM:(hh + 1) * ATT_DIM].T.astype(BF16)
        avt_ref[0, 0, r0 + ATT_DIM:r0 + V_ROWS, :] = jnp.ones((V_ROWS - ATT_DIM, v.shape[0]), BF16)

    cosb, sinb = cosb_ref[...], sinb_ref[...]
    half_b = IDX_DIM // ROPE_FRACTION // 2
    qi = proj(7)
    for cc in range(GROUP // LANES):
        qr = _rope(qi[:, cc * LANES:(cc + 1) * LANES], cosb, sinb, half_b, IDX_DIM).astype(BF16)
        for hh in range(LANES // IDX_DIM):
            iq_ref[cc * (LANES // IDX_DIM) + hh] = qr[:, hh * IDX_DIM:(hh + 1) * IDX_DIM]
    kw = proj(8, 2 * LANES)
    ki = kw[:, 0:LANES]
    lane = lax.broadcasted_iota(jnp.int32, ki.shape, 1)
    real = lane < IDX_DIM
    mu = jnp.sum(ki, axis=-1, keepdims=True) * (1.0 / IDX_DIM)
    d = jnp.where(real, ki - mu, 0.0)
    var = jnp.sum(d * d, axis=-1, keepdims=True) * (1.0 / IDX_DIM)
    kn = d * lax.rsqrt(var + EPS) * lnw_ref[...] + lnb_ref[...]
    kn = _rope(kn, cosb, sinb, half_b, IDX_DIM)
    ik_ref[...] = kn[:, 0:IDX_DIM].astype(BF16)
    iwt_ref[...] = kw[:, LANES:2 * LANES].T[0:IDX_HEADS, :] * (IDX_HEADS ** -0.5 * IDX_DIM ** -0.5)


def _in_proj(x2, mod3, gain, w_pad, lbp, lnw, lnb, tabs, T, tm):
    M, D = x2.shape
    nt = T // tm
    wcols = w_pad.shape[1]
    row = lambda i: (i, 0)
    const = lambda i: (0, 0)
    f32o = jax.ShapeDtypeStruct((M, GROUP), F32)
    bf16o = jax.ShapeDtypeStruct((M, GROUP), BF16)
    tab_spec = pl.BlockSpec((tm, LANES), row)
    grp_spec = pl.BlockSpec((tm, GROUP), row)
    return pl.pallas_call(
        _in_proj_kernel,
        grid=(M // tm,),
        in_specs=[pl.BlockSpec((tm, D), row),
                  pl.BlockSpec((1, N_MOD, D), lambda i: (i // nt, 0, 0)),
                  pl.BlockSpec((1, D), const),
                  _resident((D, wcols), const),
                  pl.BlockSpec(lbp.shape, const),
                  pl.BlockSpec((1, LANES), const),
                  pl.BlockSpec((1, LANES), const),
                  tab_spec, tab_spec, tab_spec, tab_spec],
        out_specs=[grp_spec] * 7 + [pl.BlockSpec((1, 1, ATT_HEADS * V_ROWS, tm), lambda i: (i // nt, i % nt, 0, 0)),
                                    pl.BlockSpec((IDX_HEADS, tm, IDX_DIM), lambda i: (0, i, 0)),
                                    pl.BlockSpec((tm, IDX_DIM), row),
                                    pl.BlockSpec((IDX_HEADS, tm), lambda i: (0, i))],
        out_shape=[f32o] * 5 + [bf16o] * 2 + [jax.ShapeDtypeStruct((M // T, nt, ATT_HEADS * V_ROWS, tm), BF16),
                                              jax.ShapeDtypeStruct((IDX_HEADS, M, IDX_DIM), BF16),
                                              jax.ShapeDtypeStruct((M, IDX_DIM), BF16),
                                              jax.ShapeDtypeStruct((IDX_HEADS, M), F32)],
        compiler_params=_cparams(("parallel",)),
        name="in_proj",
    )(x2, mod3, gain, w_pad, lbp, lnw, lnb, *tabs)


def _hgrn_consts():
    C = HG_CHUNK
    t = np.arange(C)
    tri = (t[None, :] <= t[:, None]).astype(np.float32)
    blocks = [tri]
    for m in HG_LEVELS:
        split = (t // (2 * m)) * (2 * m) + m - 1
        blocks.append(tri - tri[split])
    mall = np.concatenate(blocks, axis=0)
    x = t[:, None] ^ t[None, :]
    lvl = np.full((C, C), len(HG_LEVELS) + 1, np.int32)
    for li, m in enumerate(HG_LEVELS):
        lvl[(t[:, None] > t[None, :]) & (x >= m) & (x < 2 * m)] = li
    lvl[t[:, None] == t[None, :]] = len(HG_LEVELS)
    return jnp.asarray(mall, BF16), jnp.asarray(lvl)


def _hgrn_kernel(hq_ref, lf_ref, hk_ref, hv_ref, hg_ref, mall_ref, lvl_ref, onorm_ref, o_ref, st_ref):
    C = HG_CHUNK
    nl = len(HG_LEVELS)

    @pl.when(pl.program_id(2) == 0)
    def _():
        st_ref[...] = jnp.zeros_like(st_ref)

    lvl = lvl_ref[...]
    row = lax.broadcasted_iota(jnp.int32, (C, HG_DIM), 0)
    mall = mall_ref[...]
    onorm = onorm_ref[...]

    for ci in range(hq_ref.shape[0] // C):
        rs = slice(ci * C, (ci + 1) * C)
        q, lf, k, v = hq_ref[rs, :], lf_ref[rs, :], hk_ref[rs, :], hv_ref[rs, :]
        lf_hi, lf_lo = _split_bf16(lf)
        dall = _dot(mall, lf_hi) + _dot(mall, lf_lo)
        g = dall[0:C]
        qb, kb, vb = q.astype(BF16), k.astype(BF16), v.astype(BF16)

        a = jnp.zeros((C, C), F32)
        for li, m in enumerate(HG_LEVELS):
            dm = dall[(li + 1) * C:(li + 2) * C]
            xm = (jnp.where((row & m) != 0, q, k) * jnp.exp(-jnp.abs(dm))).astype(BF16)
            a = jnp.where(lvl == li, _dot_nt(xm, xm), a)
        a = jnp.where(lvl == nl, _dot_nt(qb, kb), a)

        st = st_ref[...]
        o = _dot_nt((q * jnp.exp(g)).astype(BF16), st.astype(BF16)) + _dot(a.astype(BF16), vb)
        g_last = g[C - 1:C, :]
        kd = (k * jnp.exp(g_last - g)).astype(BF16)
        st_ref[...] = jnp.exp(g_last) * st + _dot(v.T.astype(BF16), kd)

        o = o * lax.rsqrt(jnp.mean(o * o, axis=-1, keepdims=True) + EPS)
        o_ref[rs, :] = (o * onorm * hg_ref[rs, :]).astype(BF16)


def _hgrn(hq, lf, hk, hv, hg, onorm, B, T, ct):
    M = hq.shape[0]
    nct = T // ct
    mall, lvl = _hgrn_consts()
    blk = pl.BlockSpec((ct, HG_DIM), lambda b, h, c: (b * nct + c, h))
    const = lambda b, h, c: (0, 0)
    return pl.pallas_call(
        _hgrn_kernel,
        grid=(B, HG_HEADS, nct),
        in_specs=[blk] * 5 + [pl.BlockSpec(mall.shape, const), pl.BlockSpec(lvl.shape, const),
                              pl.BlockSpec((1, HG_DIM), lambda b, h, c: (0, h))],
        out_specs=blk,
        out_shape=jax.ShapeDtypeStruct((M, GROUP), BF16),
        scratch_shapes=[pltpu.VMEM((HG_DIM, HG_DIM), F32)],
        compiler_params=_cparams(("parallel", "parallel", "arbitrary")),
        name="hgrn2",
    )(hq, lf, hk, hv, hg, mall, lvl, onorm)


def _fold8(a, op):
    chains = [None] * 4
    for r in range(a.shape[0] // 8):
        part = a[r * 8:(r + 1) * 8, :]
        c = r % len(chains)
        chains[c] = part if chains[c] is None else op(chains[c], part)
    return op(op(chains[0], chains[1]), op(chains[2], chains[3]))


def _dsa_kernel(iq_ref, iwt_ref, ik_ref, aq_ref, ak_ref, avt_ref, o_ref,
                sc_ref, mm_ref, acc_ref, m_ref, *, qb, kt, topk, seq):
    i = pl.program_id(1)
    q0 = i * qb
    nfull = q0 // kt
    nkt = nfull + 1
    qpos = q0 + lax.broadcasted_iota(jnp.int32, (1, qb), 1)
    kk = jnp.minimum(qpos + 1, topk).astype(F32)

    mm_ref[0:8, :] = jnp.full((8, qb), jnp.inf, F32)
    mm_ref[8:16, :] = jnp.full((8, qb), -jnp.inf, F32)

    def score_tile(j, diag):
        ks = pl.multiple_of(j * kt, kt)
        ki = ik_ref[pl.ds(ks, kt), :]
        s = None
        for h in range(IDX_HEADS):
            t = jnp.maximum(_dot_nt(ki, iq_ref[h]), 0.0) * iwt_ref[h:h + 1, :]
            s = t if s is None else s + t
        if diag:
            causal = ks + lax.broadcasted_iota(jnp.int32, (kt, qb), 0) <= qpos
            s_hi = jnp.where(causal, s, -jnp.inf)
            s_lo = jnp.where(causal, s, jnp.inf)
        else:
            s_hi = s_lo = s
        sc_ref[pl.ds(ks, kt), :] = s_hi
        mm_ref[0:8, :] = jnp.minimum(mm_ref[0:8, :], _fold8(s_lo, jnp.minimum))
        mm_ref[8:16, :] = jnp.maximum(mm_ref[8:16, :], _fold8(s_hi, jnp.maximum))

    def full_tile(j, c):
        score_tile(j, False)
        return c

    lax.fori_loop(0, nfull, full_tile, 0)
    score_tile(nfull, True)
    mn = jnp.min(mm_ref[0:8, :], axis=0, keepdims=True)
    mx = jnp.max(mm_ref[8:16, :], axis=0, keepdims=True)

    def count_keys(hit_fn):
        def body(j, acc):
            ks = pl.multiple_of(j * kt, kt)
            return acc + _fold8(hit_fn(sc_ref[pl.ds(ks, kt), :], ks), jnp.add)
        acc = lax.fori_loop(0, nkt, body, jnp.zeros((8, qb), F32))
        return jnp.sum(acc, axis=0, keepdims=True)

    def count_ge(th):
        return count_keys(lambda s, ks: jnp.where(s >= th, 1.0, 0.0))

    lo0 = mn
    hi0 = mx + jnp.abs(mx) * 1e-6 + 1e-30
    cnt_lo0 = (qpos + 1).astype(F32)
    cnt_hi0 = jnp.zeros((1, qb), F32)

    def midpoint(lo, hi):
        return lo + 0.5 * (hi - lo)

    def unresolved(lo, hi, cnt_lo):
        mid = midpoint(lo, hi)
        return jnp.max(((cnt_lo != kk) & (mid > lo) & (mid < hi)).astype(jnp.int32)) > 0

    def bis_cond(c):
        it, lo, hi, cnt_lo, cnt_hi = c
        return (it < 256) & unresolved(lo, hi, cnt_lo)

    def bis_body(c):
        it, lo, hi, cnt_lo, cnt_hi = c
        mid = midpoint(lo, hi)
        cnt = count_ge(mid)
        up = cnt >= kk
        return (it + 1, jnp.where(up, mid, lo), jnp.where(up, hi, mid),
                jnp.where(up, cnt, cnt_lo), jnp.where(up, cnt_hi, cnt))

    _, lo, hi, cnt_lo, cnt_hi = lax.while_loop(bis_cond, bis_body, (jnp.int32(0), lo0, hi0, cnt_lo0, cnt_hi0))

    tied = cnt_lo != kk

    @pl.when(jnp.max(tied.astype(jnp.int32)) > 0)
    def _():
        need = kk - cnt_hi

        def in_tie(s):
            return (s >= lo) & (s < hi)

        def key_pos(ks):
            return ks + lax.broadcasted_iota(jnp.int32, (kt, qb), 0)

        def count_tie(jm):
            return count_keys(lambda s, ks: jnp.where(in_tie(s) & (key_pos(ks) <= jm), 1.0, 0.0))

        def tie_body(_, c):
            jlo, jhi = c
            jm = (jlo + jhi) // 2
            ok = count_tie(jm) >= need
            return jnp.where(ok, jlo, jm), jnp.where(ok, jm, jhi)

        steps = int(np.ceil(np.log2(seq))) + 1
        _, jhi = lax.fori_loop(0, steps, tie_body,
                               (jnp.full((1, qb), -1, jnp.int32), jnp.full((1, qb), seq - 1, jnp.int32)))
        jmax = jnp.where(tied, jhi, seq)

        def strike(j, c):
            ks = pl.multiple_of(j * kt, kt)
            s = sc_ref[pl.ds(ks, kt), :]
            sc_ref[pl.ds(ks, kt), :] = jnp.where(in_tie(s) & (key_pos(ks) > jmax), -jnp.inf, s)
            return c

        lax.fori_loop(0, nkt, strike, 0)

    acc_ref[...] = jnp.zeros_like(acc_ref)
    m_ref[...] = jnp.full(m_ref.shape, NEG_BIG, F32)

    def attn_tile(j, c):
        ks = pl.multiple_of(j * kt, kt)
        bias = jnp.where(sc_ref[pl.ds(ks, kt), :] >= lo, 0.0, NEG_BIG)
        for h in range(ATT_HEADS):
            sl = slice(h * ATT_DIM, (h + 1) * ATT_DIM)
            vr = slice(h * V_ROWS, (h + 1) * V_ROWS)
            lg = _dot_nt(ak_ref[pl.ds(ks, kt), sl], aq_ref[:, sl]) + bias
            m_old = m_ref[h:h + 1, :]
            m_new = jnp.maximum(m_old, jnp.max(_fold8(lg, jnp.maximum), axis=0, keepdims=True))
            p = jnp.exp2(lg - m_new).astype(BF16)
            acc_ref[vr, :] = jnp.exp2(m_old - m_new) * acc_ref[vr, :] + _dot(avt_ref[0, j, vr, :], p)
            m_ref[h:h + 1, :] = m_new
        return c

    lax.fori_loop(0, nkt, attn_tile, 0)
    for h in range(ATT_HEADS):
        r0 = h * V_ROWS
        o_t = acc_ref[r0:r0 + ATT_DIM, :] / acc_ref[r0 + ATT_DIM:r0 + ATT_DIM + 1, :]
        o_ref[:, h * ATT_DIM:(h + 1) * ATT_DIM] = o_t.T.astype(BF16)


def _dsa(iq, iwt, ik, aq, ak, avt, B, T, qb, kt):
    M = ak.shape[0]
    nq = T // qb
    assert kt % qb == 0 and T % kt == 0 and avt.shape[1:] == (T // kt, ATT_HEADS * V_ROWS, kt)
    topk = min(TOPK_MAX, T // 4)
    qrow = lambda b, i: (b * nq + i, 0)
    brow = lambda b, i: (b, 0)
    kern = functools.partial(_dsa_kernel, qb=qb, kt=kt, topk=topk, seq=T)
    return pl.pallas_call(
        kern,
        grid=(B, nq),
        in_specs=[pl.BlockSpec((IDX_HEADS, qb, IDX_DIM), lambda b, i: (0, b * nq + i, 0)),
                  pl.BlockSpec((IDX_HEADS, qb), lambda b, i: (0, b * nq + i)),
                  _resident((T, IDX_DIM), brow),
                  pl.BlockSpec((qb, GROUP), qrow),
                  _resident((T, GROUP), brow),
                  _resident((1,) + avt.shape[1:], lambda b, i: (b, 0, 0, 0))],
        out_specs=pl.BlockSpec((qb, GROUP), qrow),
        out_shape=jax.ShapeDtypeStruct((M, GROUP), BF16),
        scratch_shapes=[pltpu.VMEM((T, qb), F32),
                        pltpu.VMEM((16, qb), F32),
                        pltpu.VMEM((ATT_HEADS * V_ROWS, qb), F32),
                        pltpu.VMEM((8, qb), F32)],
        compiler_params=_cparams(("parallel", "arbitrary")),
        name="dsa",
    )(iq, iwt, ik, aq, ak, avt)


def _rms(a, gain):
    return a * lax.rsqrt(jnp.mean(a * a, axis=-1, keepdims=True) + EPS) * gain


def _out_ffn_kernel(ohg_ref, oatt_ref, x_ref, mod_ref, gains_ref, wo_ref, wi_ref, wf_ref, o_ref, acc_ref,
                    *, d_ff, fc):
    mod = mod_ref[0]
    gains = gains_ref[...]
    hw = ohg_ref.shape[1]
    y = _dot(ohg_ref[...], wo_ref[0:hw, :]) + _dot(oatt_ref[...], wo_ref[hw:, :])
    x1 = x_ref[...] + mod[2:3, :] * _rms(y, gains[0:1, :])
    h2 = (_rms(x1, gains[1:2, :]) * (1.0 + mod[4:5, :]) + mod[3:4, :]).astype(BF16)
    for c in range(d_ff // fc):
        gate = _dot(h2, wi_ref[:, c * fc:(c + 1) * fc])
        up = _dot(h2, wi_ref[:, d_ff + c * fc:d_ff + (c + 1) * fc])
        part = _dot((_silu(gate) * up).astype(BF16), wf_ref[c * fc:(c + 1) * fc, :])
        if c == 0:
            acc_ref[...] = part
        else:
            acc_ref[...] += part
    o_ref[...] = x1 + mod[5:6, :] * _rms(acc_ref[...], gains[2:3, :])


def _out_ffn(ohg, oatt, x2, mod3, gains, wo, wi, wf, T, tm):
    M, D = x2.shape
    nt = T // tm
    d_ff = wf.shape[0]
    row = lambda i: (i, 0)
    const = lambda i: (0, 0)
    kern = functools.partial(_out_ffn_kernel, d_ff=d_ff, fc=256)
    return pl.pallas_call(
        kern,
        grid=(M // tm,),
        in_specs=[pl.BlockSpec((tm, GROUP), row),
                  pl.BlockSpec((tm, GROUP), row),
                  pl.BlockSpec((tm, D), row),
                  pl.BlockSpec((1, N_MOD, D), lambda i: (i // nt, 0, 0)),
                  pl.BlockSpec((3, D), const),
                  _resident(wo.shape, const),
                  _resident(wi.shape, const),
                  _resident(wf.shape, const)],
        out_specs=pl.BlockSpec((tm, D), row),
        out_shape=jax.ShapeDtypeStruct((M, D), F32),
        scratch_shapes=[pltpu.VMEM((tm, D), F32)],
        compiler_params=_cparams(("parallel",)),
        name="out_ffn",
    )(ohg, oatt, x2, mod3, gains, wo, wi, wf)


def kernel(x, c, positions, w_ada, b_ada, norm_pre_mix, norm_post_mix, norm_pre_ffn, norm_post_ffn, w_in,
           hgrn_lower_bound, hgrn_out_norm, idx_k_norm_w, idx_k_norm_b, w_out, w_ffn_in, w_ffn_out):
    B, T, D = x.shape
    depth = w_ada.shape[0]
    assert depth == 1 and hgrn_lower_bound.shape[0] == 2
    assert T % 512 == 0 and D % LANES == 0
    tm = 512

    mod3 = _adaln(c, w_ada[0], b_ada[0]).reshape(B, N_MOD, D)
    tabs = _rope_tables(positions)

    main = 8 * GROUP
    w = w_in[0]
    zpad = lambda n: jnp.zeros((D, n), w.dtype)
    w_pad = jnp.concatenate([w[:, :main], w[:, main:main + IDX_DIM], zpad(LANES - IDX_DIM),
                             w[:, main + IDX_DIM:], zpad(LANES - IDX_HEADS)], axis=1).astype(BF16)
    lnw = jnp.pad(idx_k_norm_w[0], (0, LANES - IDX_DIM)).reshape(1, LANES)
    lnb = jnp.pad(idx_k_norm_b[0], (0, LANES - IDX_DIM)).reshape(1, LANES)

    x2 = x.reshape(B * T, D)
    hq, lf, hk, hv, hg, aq, ak, avt, iq, ik, iwt = _in_proj(
        x2, mod3, norm_pre_mix, w_pad, hgrn_lower_bound, lnw, lnb, tabs, T, tm)

    o_hg = _hgrn(hq, lf, hk, hv, hg, hgrn_out_norm, B, T, 512)
    o_att = _dsa(iq, iwt, ik, aq, ak, avt, B, T, 512, tm)

    gains = jnp.concatenate([norm_post_mix, norm_pre_ffn, norm_post_ffn], axis=0)
    out = _out_ffn(o_hg, o_att, x2, mod3, gains, w_out[0].astype(BF16), w_ffn_in[0].astype(BF16),
                   w_ffn_out[0].astype(BF16), T, tm)
    return out.reshape(B, T, D)
```

```python
import functools

import numpy as np
import jax
import jax.numpy as jnp
from jax import lax
from jax.experimental import pallas as pl
from jax.experimental.pallas import tpu as pltpu

F32 = jnp.float32
BF16 = jnp.bfloat16

HG_HEADS = 4
HG_DIM = 128
ATT_HEADS = 4
ATT_DIM = 128
IDX_HEADS = 8
IDX_DIM = 64
TOPK_MAX = 256
ROPE_THETA = 500000.0
ROPE_FRACTION = 4
N_MOD = 6
EPS = 1e-6

GROUP = 512
LANES = 128
VMEM_LIMIT = 56 * 1024 * 1024

HG_CHUNK = 128
HG_LEVELS = (64, 32, 16, 8, 4, 2, 1)

NEG_BIG = -1e30
Q_SCALE = ATT_DIM ** -0.5 * 1.4426950408889634
V_ROWS = ATT_DIM + 16


def _cparams(sem):
    return pltpu.CompilerParams(dimension_semantics=sem, vmem_limit_bytes=VMEM_LIMIT)


def _resident(shape, index_map):
    return pl.BlockSpec(shape, index_map, pipeline_mode=pl.Buffered(1))


def _split_bf16(a):
    hi = a.astype(BF16)
    lo = (a - hi.astype(F32)).astype(BF16)
    return hi, lo


def _dot(a, b):
    return jnp.dot(a, b, preferred_element_type=F32)


def _dot_nt(a, b):
    return lax.dot_general(a, b, (((1,), (1,)), ((), ())), preferred_element_type=F32)


def _silu(a):
    return a * jax.nn.sigmoid(a)


def _adaln_kernel(c_ref, w_ref, b_ref, o_ref):
    a = _silu(c_ref[...])
    a_hi, a_lo = _split_bf16(a)
    w_hi, w_lo = _split_bf16(w_ref[...])
    acc = _dot(a_hi, w_hi) + (_dot(a_hi, w_lo) + _dot(a_lo, w_hi))
    o_ref[...] = acc + b_ref[...]


def _adaln(c, w, b):
    B, D = c.shape
    N = w.shape[1]
    return pl.pallas_call(
        _adaln_kernel,
        grid=(N // D,),
        in_specs=[pl.BlockSpec((B, D), lambda j: (0, 0)),
                  pl.BlockSpec((D, D), lambda j: (0, j)),
                  pl.BlockSpec((1, D), lambda j: (0, j))],
        out_specs=pl.BlockSpec((B, D), lambda j: (0, j)),
        out_shape=jax.ShapeDtypeStruct((B, N), F32),
        compiler_params=_cparams(("arbitrary",)),
        name="adaln",
    )(c, w, b.reshape(1, N))


def _rope_table_kernel(pos_ref, fa_ref, fb_ref, o_ref):
    pos = pos_ref[0].astype(F32)
    ang_a = fa_ref[...] * pos
    ang_b = fb_ref[...] * pos
    ha = fa_ref.shape[0]
    hb = fb_ref.shape[0]
    o_ref[0, 0:ha, :] = jnp.cos(ang_a)
    o_ref[0, ha:2 * ha, :] = jnp.sin(ang_a)
    o_ref[0, 2 * ha:2 * ha + hb, :] = jnp.cos(ang_b)
    o_ref[0, 2 * ha + hb:2 * ha + 2 * hb, :] = jnp.sin(ang_b)
    o_ref[0, 2 * ha + 2 * hb:, :] = jnp.zeros((LANES - 2 * ha - 2 * hb, pos.shape[1]), F32)


ROPE_HALF_A = ATT_DIM // ROPE_FRACTION // 2
ROPE_HALF_B = IDX_DIM // ROPE_FRACTION // 2


def _rope_table(positions):
    B, T = positions.shape
    ha, hb = ROPE_HALF_A, ROPE_HALF_B
    fa = (ROPE_THETA ** (-jnp.arange(ha, dtype=F32) / ha)).reshape(ha, 1)
    fb = (ROPE_THETA ** (-jnp.arange(hb, dtype=F32) / hb)).reshape(hb, 1)
    return pl.pallas_call(
        _rope_table_kernel,
        grid=(B,),
        in_specs=[pl.BlockSpec((1, 1, T), lambda b: (b, 0, 0)),
                  pl.BlockSpec((ha, 1), lambda b: (0, 0)),
                  pl.BlockSpec((hb, 1), lambda b: (0, 0))],
        out_specs=pl.BlockSpec((1, LANES, T), lambda b: (b, 0, 0)),
        out_shape=jax.ShapeDtypeStruct((B, LANES, T), F32),
        compiler_params=_cparams(("arbitrary",)),
        name="rope_table",
    )(positions.reshape(B, 1, T), fa, fb)


def _rope_patterns(tab):
    ha, hb = ROPE_HALF_A, ROPE_HALF_B
    lane = lax.broadcasted_iota(jnp.int32, tab.shape, 1)
    shifted = lambda sh: pltpu.roll(tab, sh % LANES, axis=1)
    cos_a = jnp.where(lane < ha, tab, jnp.where(lane < 2 * ha, shifted(ha), 1.0))
    sin_a = jnp.where(lane < ha, -shifted(-ha), jnp.where(lane < 2 * ha, tab, 0.0))
    l64 = lane & (IDX_DIM - 1)
    first = lane < IDX_DIM
    cb0, sb0 = 2 * ha, 2 * ha + hb
    cos_b = jnp.where(l64 < hb, jnp.where(first, shifted(-cb0), shifted(IDX_DIM - cb0)),
                      jnp.where(l64 < 2 * hb, jnp.where(first, shifted(hb - cb0), shifted(IDX_DIM + hb - cb0)), 1.0))
    sin_b = jnp.where(l64 < hb, -jnp.where(first, shifted(-sb0), shifted(IDX_DIM - sb0)),
                      jnp.where(l64 < 2 * hb, jnp.where(first, shifted(hb - sb0), shifted(IDX_DIM + hb - sb0)), 0.0))
    return cos_a, sin_a, cos_b, sin_b


def _rope(xb, cos, sin, half, period):
    lane = lax.broadcasted_iota(jnp.int32, xb.shape, 1)
    fwd = pltpu.roll(xb, LANES - half, axis=1)
    bwd = pltpu.roll(xb, half, axis=1)
    partner = jnp.where((lane & (period - 1)) < half, fwd, bwd)
    return xb * cos + partner * sin


def _in_proj_kernel(x_ref, mod_ref, gain_ref, w_ref, wt_ref, lbp_ref, lnw_ref, lnb_ref, tab_ref,
                    hq_ref, lf_ref, hk_ref, hv_ref, hg_ref,
                    aq_ref, ak_ref, avt_ref, iq_ref, ik_ref, iwt_ref):
    x = x_ref[...]
    ms = jnp.mean(x * x, axis=-1, keepdims=True)
    mod = mod_ref[0]
    h = x * lax.rsqrt(ms + EPS) * gain_ref[...]
    h = h * (1.0 + mod[1:2, :]) + mod[0:1, :]
    hb = h.astype(BF16)

    def proj(g, width=GROUP):
        return _dot(hb, w_ref[:, g * GROUP:g * GROUP + width])

    hq_ref[...] = (_silu(proj(0)) * (HG_DIM ** -0.5)).astype(BF16)
    a = lbp_ref[...]
    amax = jnp.max(a, axis=0, keepdims=True)
    e = jnp.exp(a - amax)
    lb = e[0:1, :] / jnp.sum(e, axis=0, keepdims=True)
    fr = proj(1)
    f = lb + (1.0 - lb) * jax.nn.sigmoid(fr)
    lf_ref[...] = jnp.log(f).astype(BF16)
    hk_ref[...] = ((1.0 - lb) * jax.nn.sigmoid(-fr)).astype(BF16)
    hv_ref[...] = proj(2).astype(BF16)
    hg_ref[...] = _silu(proj(3)).astype(BF16)

    cosa, sina, cosb, sinb = _rope_patterns(tab_ref[0].T)
    half_a = ROPE_HALF_A
    q = proj(4)
    k = proj(5)
    for hh in range(ATT_HEADS):
        sl = slice(hh * ATT_DIM, (hh + 1) * ATT_DIM)
        aq_ref[:, sl] = (_rope(q[:, sl], cosa, sina, half_a, ATT_DIM) * Q_SCALE).astype(BF16)
        ak_ref[:, sl] = _rope(k[:, sl], cosa, sina, half_a, ATT_DIM).astype(BF16)
    v = proj(6)
    for hh in range(ATT_HEADS):
        r0 = hh * V_ROWS
        avt_ref[0, 0, r0:r0 + ATT_DIM, :] = v[:, hh * ATT_DIM:(hh + 1) * ATT_DIM].T.astype(BF16)
        avt_ref[0, 0, r0 + ATT_DIM:r0 + V_ROWS, :] = jnp.ones((V_ROWS - ATT_DIM, v.shape[0]), BF16)

    half_b = ROPE_HALF_B
    qi = proj(7)
    for cc in range(GROUP // LANES):
        qr = _rope(qi[:, cc * LANES:(cc + 1) * LANES], cosb, sinb, half_b, IDX_DIM).astype(BF16)
        for hh in range(LANES // IDX_DIM):
            iq_ref[cc * (LANES // IDX_DIM) + hh] = qr[:, hh * IDX_DIM:(hh + 1) * IDX_DIM]
    kw = _dot(hb, wt_ref[...])
    ki = kw[:, 0:LANES]
    lane = lax.broadcasted_iota(jnp.int32, ki.shape, 1)
    real = lane < IDX_DIM
    mu = jnp.sum(ki, axis=-1, keepdims=True) * (1.0 / IDX_DIM)
    d = jnp.where(real, ki - mu, 0.0)
    var = jnp.sum(d * d, axis=-1, keepdims=True) * (1.0 / IDX_DIM)
    kn = d * lax.rsqrt(var + EPS) * lnw_ref[...] + lnb_ref[...]
    kn = _rope(kn, cosb, sinb, half_b, IDX_DIM)
    ik_ref[...] = kn[:, 0:IDX_DIM].astype(BF16)
    iwt_ref[...] = kw[:, LANES:2 * LANES].T[0:IDX_HEADS, :] * (IDX_HEADS ** -0.5 * IDX_DIM ** -0.5)


def _in_proj(x2, mod3, gain, w_main, w_tail, lbp, lnw, lnb, tab, T, tm):
    M, D = x2.shape
    nt = T // tm
    row = lambda i: (i, 0)
    const = lambda i: (0, 0)
    bf16o = jax.ShapeDtypeStruct((M, GROUP), BF16)
    grp_spec = pl.BlockSpec((tm, GROUP), row)
    return pl.pallas_call(
        _in_proj_kernel,
        grid=(M // tm,),
        in_specs=[pl.BlockSpec((tm, D), row),
                  pl.BlockSpec((1, N_MOD, D), lambda i: (i // nt, 0, 0)),
                  pl.BlockSpec((1, D), const),
                  _resident(w_main.shape, const),
                  _resident(w_tail.shape, const),
                  pl.BlockSpec(lbp.shape, const),
                  pl.BlockSpec((1, LANES), const),
                  pl.BlockSpec((1, LANES), const),
                  pl.BlockSpec((1, LANES, tm), lambda i: (i // nt, 0, i % nt))],
        out_specs=[grp_spec] * 7 + [pl.BlockSpec((1, 1, ATT_HEADS * V_ROWS, tm), lambda i: (i // nt, i % nt, 0, 0)),
                                    pl.BlockSpec((IDX_HEADS, tm, IDX_DIM), lambda i: (0, i, 0)),
                                    pl.BlockSpec((tm, IDX_DIM), row),
                                    pl.BlockSpec((IDX_HEADS, tm), lambda i: (0, i))],
        out_shape=[bf16o] * 7 + [jax.ShapeDtypeStruct((M // T, nt, ATT_HEADS * V_ROWS, tm), BF16),
                                              jax.ShapeDtypeStruct((IDX_HEADS, M, IDX_DIM), BF16),
                                              jax.ShapeDtypeStruct((M, IDX_DIM), BF16),
                                              jax.ShapeDtypeStruct((IDX_HEADS, M), F32)],
        compiler_params=_cparams(("parallel",)),
        name="in_proj",
    )(x2, mod3, gain, w_main, w_tail, lbp, lnw, lnb, tab)


def _hgrn_consts():
    C = HG_CHUNK
    t = np.arange(C)
    tri = (t[None, :] <= t[:, None]).astype(np.float32)
    blocks = [tri]
    for m in HG_LEVELS:
        split = (t // (2 * m)) * (2 * m) + m - 1
        blocks.append(tri - tri[split])
    mall = np.concatenate(blocks, axis=0)
    x = t[:, None] ^ t[None, :]
    lvl = np.full((C, C), len(HG_LEVELS) + 1, np.int32)
    for li, m in enumerate(HG_LEVELS):
        lvl[(t[:, None] > t[None, :]) & (x >= m) & (x < 2 * m)] = li
    lvl[t[:, None] == t[None, :]] = len(HG_LEVELS)
    return jnp.asarray(mall, BF16), jnp.asarray(lvl)


def _hgrn_kernel(hq_ref, lf_ref, hk_ref, hv_ref, hg_ref, mall_ref, lvl_ref, onorm_ref, o_ref, st_ref):
    C = HG_CHUNK
    nl = len(HG_LEVELS)

    @pl.when(pl.program_id(1) == 0)
    def _():
        st_ref[...] = jnp.zeros_like(st_ref)

    lvl = lvl_ref[...]
    row = lax.broadcasted_iota(jnp.int32, (C, HG_DIM), 0)
    mall = mall_ref[...]

    for ci in range(hq_ref.shape[0] // C):
        rs = slice(ci * C, (ci + 1) * C)
        for h in range(HG_HEADS):
            cs = slice(h * HG_DIM, (h + 1) * HG_DIM)
            q, k, v = hq_ref[rs, cs], hk_ref[rs, cs], hv_ref[rs, cs]
            dall = _dot(mall, lf_ref[rs, cs])
            g = dall[0:C]

            a = jnp.zeros((C, C), F32)
            for li, m in enumerate(HG_LEVELS):
                dm = dall[(li + 1) * C:(li + 2) * C]
                xm = (jnp.where((row & m) != 0, q, k) * jnp.exp(-jnp.abs(dm))).astype(BF16)
                a = jnp.where(lvl == li, _dot_nt(xm, xm), a)
            a = jnp.where(lvl == nl, _dot_nt(q, k), a)

            st = st_ref[h]
            o = _dot_nt((q * jnp.exp(g)).astype(BF16), st.astype(BF16)) + _dot(a.astype(BF16), v)
            g_last = g[C - 1:C, :]
            kd = (k * jnp.exp(g_last - g)).astype(BF16)
            st_ref[h] = jnp.exp(g_last) * st + _dot(v.astype(F32).T.astype(BF16), kd)

            o = o * lax.rsqrt(jnp.mean(o * o, axis=-1, keepdims=True) + EPS)
            o_ref[rs, cs] = (o * onorm_ref[:, cs] * hg_ref[rs, cs]).astype(BF16)


def _hgrn(hq, lf, hk, hv, hg, onorm, B, T, ct):
    M = hq.shape[0]
    nct = T // ct
    mall, lvl = _hgrn_consts()
    blk = pl.BlockSpec((ct, GROUP), lambda b, c: (b * nct + c, 0))
    const = lambda b, c: (0, 0)
    return pl.pallas_call(
        _hgrn_kernel,
        grid=(B, nct),
        in_specs=[blk] * 5 + [pl.BlockSpec(mall.shape, const), pl.BlockSpec(lvl.shape, const),
                              pl.BlockSpec((1, GROUP), const)],
        out_specs=blk,
        out_shape=jax.ShapeDtypeStruct((M, GROUP), BF16),
        scratch_shapes=[pltpu.VMEM((HG_HEADS, HG_DIM, HG_DIM), F32)],
        compiler_params=_cparams(("parallel", "arbitrary")),
        name="hgrn2",
    )(hq, lf, hk, hv, hg, mall, lvl, onorm)


def _fold8(a, op):
    chains = [None] * 4
    for r in range(a.shape[0] // 8):
        part = a[r * 8:(r + 1) * 8, :]
        c = r % len(chains)
        chains[c] = part if chains[c] is None else op(chains[c], part)
    return op(op(chains[0], chains[1]), op(chains[2], chains[3]))


def _dsa_kernel(iq_ref, iwt_ref, ik_ref, aq_ref, ak_ref, avt_ref, o_ref,
                sc_ref, mm_ref, acc_ref, m_ref, *, qb, kt, topk, seq):
    i = pl.program_id(1)
    q0 = i * qb
    nfull = q0 // kt
    nkt = nfull + 1
    qpos = q0 + lax.broadcasted_iota(jnp.int32, (1, qb), 1)
    kk = jnp.minimum(qpos + 1, topk).astype(F32)

    mm_ref[0:8, :] = jnp.full((8, qb), jnp.inf, F32)
    mm_ref[8:16, :] = jnp.full((8, qb), -jnp.inf, F32)
    mm_ref[16:32, :] = jnp.zeros((16, qb), F32)
    mm_ref[32:40, :] = jnp.full((8, qb), jnp.inf, F32)

    def score_tile(j, diag):
        ks = pl.multiple_of(j * kt, kt)
        ki = ik_ref[pl.ds(ks, kt), :]
        s = None
        for h in range(IDX_HEADS):
            t = jnp.maximum(_dot_nt(ki, iq_ref[h]), 0.0) * iwt_ref[h:h + 1, :]
            s = t if s is None else s + t
        if diag:
            causal = ks + lax.broadcasted_iota(jnp.int32, (kt, qb), 0) <= qpos
            s_hi = jnp.where(causal, s, -jnp.inf)
            s_lo = jnp.where(causal, s, jnp.inf)
        else:
            s_hi = s_lo = s
        sc_ref[pl.ds(ks, kt), :] = s_hi
        mm_ref[0:8, :] = jnp.minimum(mm_ref[0:8, :], _fold8(s_lo, jnp.minimum))
        mm_ref[8:16, :] = jnp.maximum(mm_ref[8:16, :], _fold8(s_hi, jnp.maximum))
        mm_ref[16:24, :] += _fold8(jnp.where(s_hi >= 0.0, 1.0, 0.0), jnp.add)
        mm_ref[24:32, :] += _fold8(jnp.where(s_hi > 0.0, 1.0, 0.0), jnp.add)
        mm_ref[32:40, :] = jnp.minimum(mm_ref[32:40, :], _fold8(jnp.where(s_hi > 0.0, s_hi, jnp.inf), jnp.minimum))

    def full_tile(j, c):
        score_tile(j, False)
        return c

    lax.fori_loop(0, nfull, full_tile, 0)
    score_tile(nfull, True)
    mn = jnp.min(mm_ref[0:8, :], axis=0, keepdims=True)
    mx = jnp.max(mm_ref[8:16, :], axis=0, keepdims=True)
    c_nonneg = jnp.sum(mm_ref[16:24, :], axis=0, keepdims=True)
    c_pos = jnp.sum(mm_ref[24:32, :], axis=0, keepdims=True)
    min_pos = jnp.min(mm_ref[32:40, :], axis=0, keepdims=True)

    def count_keys(hit_fn):
        def body(j, acc):
            ks = pl.multiple_of(j * kt, kt)
            return acc + _fold8(hit_fn(sc_ref[pl.ds(ks, kt), :], ks), jnp.add)
        acc = lax.fori_loop(0, nkt, body, jnp.zeros((8, qb), F32))
        return jnp.sum(acc, axis=0, keepdims=True)

    def count_ge(th):
        return count_keys(lambda s, ks: jnp.where(s >= th, 1.0, 0.0))

    above = c_pos >= kk
    below = c_nonneg < kk
    live = above | below
    zero = jnp.zeros((1, qb), F32)
    lo0 = jnp.where(below, mn, zero)
    cnt_lo0 = jnp.where(below, (qpos + 1).astype(F32), c_nonneg)
    hi0 = jnp.where(above, mx + jnp.abs(mx) * 1e-6 + 1e-30,
                    jnp.where(below, zero, min_pos))
    cnt_hi0 = jnp.where(above, zero, jnp.where(below, c_nonneg, c_pos))

    def midpoint(lo, hi):
        return lo + 0.5 * (hi - lo)

    def unresolved(lo, hi, cnt_lo):
        mid = midpoint(lo, hi)
        return jnp.max((live & (cnt_lo != kk) & (mid > lo) & (mid < hi)).astype(jnp.int32)) > 0

    def bis_cond(c):
        it, lo, hi, cnt_lo, cnt_hi = c
        return (it < 256) & unresolved(lo, hi, cnt_lo)

    def bis_body(c):
        it, lo, hi, cnt_lo, cnt_hi = c
        mid = midpoint(lo, hi)
        cnt = count_ge(mid)
        up = live & (cnt >= kk)
        dn = live & (cnt < kk)
        return (it + 1, jnp.where(up, mid, lo), jnp.where(dn, mid, hi),
                jnp.where(up, cnt, cnt_lo), jnp.where(dn, cnt, cnt_hi))

    _, lo, hi, cnt_lo, cnt_hi = lax.while_loop(bis_cond, bis_body, (jnp.int32(0), lo0, hi0, cnt_lo0, cnt_hi0))

    tied = cnt_lo != kk

    @pl.when(jnp.max(tied.astype(jnp.int32)) > 0)
    def _():
        need = kk - cnt_hi

        def in_tie(s):
            return (s >= lo) & (s < hi)

        def key_pos(ks):
            return ks + lax.broadcasted_iota(jnp.int32, (kt, qb), 0)

        def count_tie(jm):
            return count_keys(lambda s, ks: jnp.where(in_tie(s) & (key_pos(ks) <= jm), 1.0, 0.0))

        def tie_body(_, c):
            jlo, jhi = c
            jm = (jlo + jhi) // 2
            ok = count_tie(jm) >= need
            return jnp.where(ok, jlo, jm), jnp.where(ok, jm, jhi)

        steps = int(np.ceil(np.log2(seq))) + 1
        _, jhi = lax.fori_loop(0, steps, tie_body,
                               (jnp.full((1, qb), -1, jnp.int32), jnp.full((1, qb), seq - 1, jnp.int32)))
        jmax = jnp.where(tied, jhi, seq)

        def strike(j, c):
            ks = pl.multiple_of(j * kt, kt)
            s = sc_ref[pl.ds(ks, kt), :]
            sc_ref[pl.ds(ks, kt), :] = jnp.where(in_tie(s) & (key_pos(ks) > jmax), -jnp.inf, s)
            return c

        lax.fori_loop(0, nkt, strike, 0)

    acc_ref[...] = jnp.zeros_like(acc_ref)
    m_ref[...] = jnp.full(m_ref.shape, NEG_BIG, F32)

    def attn_tile(j, c):
        ks = pl.multiple_of(j * kt, kt)
        bias = jnp.where(sc_ref[pl.ds(ks, kt), :] >= lo, 0.0, NEG_BIG)
        for h in range(ATT_HEADS):
            sl = slice(h * ATT_DIM, (h + 1) * ATT_DIM)
            vr = slice(h * V_ROWS, (h + 1) * V_ROWS)
            lg = _dot_nt(ak_ref[pl.ds(ks, kt), sl], aq_ref[:, sl]) + bias
            m_old = m_ref[h:h + 1, :]
            m_new = jnp.maximum(m_old, jnp.max(_fold8(lg, jnp.maximum), axis=0, keepdims=True))
            p = jnp.exp2(lg - m_new).astype(BF16)
            acc_ref[vr, :] = jnp.exp2(m_old - m_new) * acc_ref[vr, :] + _dot(avt_ref[0, j, vr, :], p)
            m_ref[h:h + 1, :] = m_new
        return c

    lax.fori_loop(0, nkt, attn_tile, 0)
    for h in range(ATT_HEADS):
        r0 = h * V_ROWS
        o_t = acc_ref[r0:r0 + ATT_DIM, :] / acc_ref[r0 + ATT_DIM:r0 + ATT_DIM + 1, :]
        o_ref[:, h * ATT_DIM:(h + 1) * ATT_DIM] = o_t.T.astype(BF16)


def _dsa(iq, iwt, ik, aq, ak, avt, B, T, qb, kt):
    M = ak.shape[0]
    nq = T // qb
    assert kt % qb == 0 and T % kt == 0 and avt.shape[1:] == (T // kt, ATT_HEADS * V_ROWS, kt)
    topk = min(TOPK_MAX, T // 4)
    qrow = lambda b, i: (b * nq + i, 0)
    brow = lambda b, i: (b, 0)
    kern = functools.partial(_dsa_kernel, qb=qb, kt=kt, topk=topk, seq=T)
    return pl.pallas_call(
        kern,
        grid=(B, nq),
        in_specs=[pl.BlockSpec((IDX_HEADS, qb, IDX_DIM), lambda b, i: (0, b * nq + i, 0)),
                  pl.BlockSpec((IDX_HEADS, qb), lambda b, i: (0, b * nq + i)),
                  _resident((T, IDX_DIM), brow),
                  pl.BlockSpec((qb, GROUP), qrow),
                  _resident((T, GROUP), brow),
                  _resident((1,) + avt.shape[1:], lambda b, i: (b, 0, 0, 0))],
        out_specs=pl.BlockSpec((qb, GROUP), qrow),
        out_shape=jax.ShapeDtypeStruct((M, GROUP), BF16),
        scratch_shapes=[pltpu.VMEM((T, qb), F32),
                        pltpu.VMEM((40, qb), F32),
                        pltpu.VMEM((ATT_HEADS * V_ROWS, qb), F32),
                        pltpu.VMEM((8, qb), F32)],
        compiler_params=_cparams(("parallel", "arbitrary")),
        name="dsa",
    )(iq, iwt, ik, aq, ak, avt)


def _rms(a, gain):
    return a * lax.rsqrt(jnp.mean(a * a, axis=-1, keepdims=True) + EPS) * gain


def _out_ffn_kernel(ohg_ref, oatt_ref, x_ref, mod_ref, gains_ref, wo_ref, wi_ref, wf_ref, o_ref, acc_ref,
                    *, d_ff, fc):
    mod = mod_ref[0]
    gains = gains_ref[...]
    hw = ohg_ref.shape[1]
    y = _dot(ohg_ref[...], wo_ref[0:hw, :]) + _dot(oatt_ref[...], wo_ref[hw:, :])
    x1 = x_ref[...] + mod[2:3, :] * _rms(y, gains[0:1, :])
    h2 = (_rms(x1, gains[1:2, :]) * (1.0 + mod[4:5, :]) + mod[3:4, :]).astype(BF16)
    for c in range(d_ff // fc):
        gate = _dot(h2, wi_ref[:, c * fc:(c + 1) * fc])
        up = _dot(h2, wi_ref[:, d_ff + c * fc:d_ff + (c + 1) * fc])
        part = _dot((_silu(gate) * up).astype(BF16), wf_ref[c * fc:(c + 1) * fc, :])
        if c == 0:
            acc_ref[...] = part
        else:
            acc_ref[...] += part
    o_ref[...] = x1 + mod[5:6, :] * _rms(acc_ref[...], gains[2:3, :])


def _out_ffn(ohg, oatt, x2, mod3, gains, wo, wi, wf, T, tm):
    M, D = x2.shape
    nt = T // tm
    d_ff = wf.shape[0]
    row = lambda i: (i, 0)
    const = lambda i: (0, 0)
    kern = functools.partial(_out_ffn_kernel, d_ff=d_ff, fc=256)
    return pl.pallas_call(
        kern,
        grid=(M // tm,),
        in_specs=[pl.BlockSpec((tm, GROUP), row),
                  pl.BlockSpec((tm, GROUP), row),
                  pl.BlockSpec((tm, D), row),
                  pl.BlockSpec((1, N_MOD, D), lambda i: (i // nt, 0, 0)),
                  pl.BlockSpec((3, D), const),
                  _resident(wo.shape, const),
                  _resident(wi.shape, const),
                  _resident(wf.shape, const)],
        out_specs=pl.BlockSpec((tm, D), row),
        out_shape=jax.ShapeDtypeStruct((M, D), F32),
        scratch_shapes=[pltpu.VMEM((tm, D), F32)],
        compiler_params=_cparams(("parallel",)),
        name="out_ffn",
    )(ohg, oatt, x2, mod3, gains, wo, wi, wf)


def kernel(x, c, positions, w_ada, b_ada, norm_pre_mix, norm_post_mix, norm_pre_ffn, norm_post_ffn, w_in,
           hgrn_lower_bound, hgrn_out_norm, idx_k_norm_w, idx_k_norm_b, w_out, w_ffn_in, w_ffn_out):
    B, T, D = x.shape
    depth = w_ada.shape[0]
    assert depth == 1 and hgrn_lower_bound.shape[0] == 2
    assert T % 512 == 0 and D % LANES == 0
    tm = 512

    mod3 = _adaln(c, w_ada[0], b_ada[0]).reshape(B, N_MOD, D)
    tab = _rope_table(positions)

    main = 8 * GROUP
    w = w_in[0]
    w_main = w[:, :main].astype(BF16)
    w_tail = jnp.concatenate([jnp.pad(w[:, main:main + IDX_DIM], ((0, 0), (0, LANES - IDX_DIM))),
                              jnp.pad(w[:, main + IDX_DIM:], ((0, 0), (0, LANES - IDX_HEADS)))], axis=1).astype(BF16)
    lnw = jnp.pad(idx_k_norm_w[0], (0, LANES - IDX_DIM)).reshape(1, LANES)
    lnb = jnp.pad(idx_k_norm_b[0], (0, LANES - IDX_DIM)).reshape(1, LANES)

    x2 = x.reshape(B * T, D)
    hq, lf, hk, hv, hg, aq, ak, avt, iq, ik, iwt = _in_proj(
        x2, mod3, norm_pre_mix, w_main, w_tail, hgrn_lower_bound, lnw, lnb, tab, T, tm)

    o_hg = _hgrn(hq, lf, hk, hv, hg, hgrn_out_norm, B, T, 256)
    o_att = _dsa(iq, iwt, ik, aq, ak, avt, B, T, 512, tm)

    gains = jnp.concatenate([norm_post_mix, norm_pre_ffn, norm_post_ffn], axis=0)
    out = _out_ffn(o_hg, o_att, x2, mod3, gains, w_out[0].astype(BF16), w_ffn_in[0].astype(BF16),
                   w_ffn_out[0].astype(BF16), T, tm)
    return out.reshape(B, T, D)
```

```python
import functools

import numpy as np
import jax
import jax.numpy as jnp
from jax import lax
from jax.experimental import pallas as pl
from jax.experimental.pallas import tpu as pltpu

F32 = jnp.float32
BF16 = jnp.bfloat16

HG_HEADS = 4
HG_DIM = 128
ATT_HEADS = 4
ATT_DIM = 128
IDX_HEADS = 8
IDX_DIM = 64
TOPK_MAX = 256
ROPE_THETA = 500000.0
ROPE_FRACTION = 4
N_MOD = 6
EPS = 1e-6

GROUP = 512
LANES = 128
VMEM_LIMIT = 56 * 1024 * 1024

HG_CHUNK = 128
HG_LEVELS = (64, 32, 16, 8, 4, 2, 1)

NEG_BIG = -1e30
Q_SCALE = ATT_DIM ** -0.5 * 1.4426950408889634
V_ROWS = ATT_DIM + 16


def _cparams(sem):
    return pltpu.CompilerParams(dimension_semantics=sem, vmem_limit_bytes=VMEM_LIMIT)


def _resident(shape, index_map):
    return pl.BlockSpec(shape, index_map, pipeline_mode=pl.Buffered(1))


def _split_bf16(a):
    hi = a.astype(BF16)
    lo = (a - hi.astype(F32)).astype(BF16)
    return hi, lo


def _dot(a, b):
    return jnp.dot(a, b, preferred_element_type=F32)


def _dot_nt(a, b):
    return lax.dot_general(a, b, (((1,), (1,)), ((), ())), preferred_element_type=F32)


def _silu(a):
    return a * jax.nn.sigmoid(a)


def _adaln_kernel(c_ref, w_ref, b_ref, o_ref):
    a = _silu(c_ref[...])
    a_hi, a_lo = _split_bf16(a)
    w_hi, w_lo = _split_bf16(w_ref[...])
    acc = _dot(a_hi, w_hi) + (_dot(a_hi, w_lo) + _dot(a_lo, w_hi))
    o_ref[...] = acc + b_ref[...]


def _adaln(c, w, b):
    B, D = c.shape
    N = w.shape[1]
    return pl.pallas_call(
        _adaln_kernel,
        grid=(N // D,),
        in_specs=[pl.BlockSpec((B, D), lambda j: (0, 0)),
                  pl.BlockSpec((D, D), lambda j: (0, j)),
                  pl.BlockSpec((1, D), lambda j: (0, j))],
        out_specs=pl.BlockSpec((B, D), lambda j: (0, j)),
        out_shape=jax.ShapeDtypeStruct((B, N), F32),
        compiler_params=_cparams(("arbitrary",)),
        name="adaln",
    )(c, w, b.reshape(1, N))


def _rope_table_kernel(pos_ref, fa_ref, fb_ref, o_ref):
    pos = pos_ref[0].astype(F32)
    ang_a = fa_ref[...] * pos
    ang_b = fb_ref[...] * pos
    ha = fa_ref.shape[0]
    hb = fb_ref.shape[0]
    o_ref[0, 0:ha, :] = jnp.cos(ang_a)
    o_ref[0, ha:2 * ha, :] = jnp.sin(ang_a)
    o_ref[0, 2 * ha:2 * ha + hb, :] = jnp.cos(ang_b)
    o_ref[0, 2 * ha + hb:2 * ha + 2 * hb, :] = jnp.sin(ang_b)
    o_ref[0, 2 * ha + 2 * hb:, :] = jnp.zeros((LANES - 2 * ha - 2 * hb, pos.shape[1]), F32)


ROPE_HALF_A = ATT_DIM // ROPE_FRACTION // 2
ROPE_HALF_B = IDX_DIM // ROPE_FRACTION // 2


def _rope_table(positions):
    B, T = positions.shape
    ha, hb = ROPE_HALF_A, ROPE_HALF_B
    fa = (ROPE_THETA ** (-jnp.arange(ha, dtype=F32) / ha)).reshape(ha, 1)
    fb = (ROPE_THETA ** (-jnp.arange(hb, dtype=F32) / hb)).reshape(hb, 1)
    return pl.pallas_call(
        _rope_table_kernel,
        grid=(B,),
        in_specs=[pl.BlockSpec((1, 1, T), lambda b: (b, 0, 0)),
                  pl.BlockSpec((ha, 1), lambda b: (0, 0)),
                  pl.BlockSpec((hb, 1), lambda b: (0, 0))],
        out_specs=pl.BlockSpec((1, LANES, T), lambda b: (b, 0, 0)),
        out_shape=jax.ShapeDtypeStruct((B, LANES, T), F32),
        compiler_params=_cparams(("arbitrary",)),
        name="rope_table",
    )(positions.reshape(B, 1, T), fa, fb)


def _rope_patterns(tab):
    ha, hb = ROPE_HALF_A, ROPE_HALF_B
    lane = lax.broadcasted_iota(jnp.int32, tab.shape, 1)
    shifted = lambda sh: pltpu.roll(tab, sh % LANES, axis=1)
    cos_a = jnp.where(lane < ha, tab, jnp.where(lane < 2 * ha, shifted(ha), 1.0))
    sin_a = jnp.where(lane < ha, -shifted(-ha), jnp.where(lane < 2 * ha, tab, 0.0))
    l64 = lane & (IDX_DIM - 1)
    first = lane < IDX_DIM
    cb0, sb0 = 2 * ha, 2 * ha + hb
    cos_b = jnp.where(l64 < hb, jnp.where(first, shifted(-cb0), shifted(IDX_DIM - cb0)),
                      jnp.where(l64 < 2 * hb, jnp.where(first, shifted(hb - cb0), shifted(IDX_DIM + hb - cb0)), 1.0))
    sin_b = jnp.where(l64 < hb, -jnp.where(first, shifted(-sb0), shifted(IDX_DIM - sb0)),
                      jnp.where(l64 < 2 * hb, jnp.where(first, shifted(hb - sb0), shifted(IDX_DIM + hb - sb0)), 0.0))
    return cos_a, sin_a, cos_b, sin_b


def _rope(xb, cos, sin, half, period):
    lane = lax.broadcasted_iota(jnp.int32, xb.shape, 1)
    fwd = pltpu.roll(xb, LANES - half, axis=1)
    bwd = pltpu.roll(xb, half, axis=1)
    partner = jnp.where((lane & (period - 1)) < half, fwd, bwd)
    return xb * cos + partner * sin


def _in_proj_kernel(x_ref, mod_ref, gain_ref, w_ref, wt_ref, lbp_ref, lnw_ref, lnb_ref, tab_ref,
                    hq_ref, lf_ref, hk_ref, hv_ref, hg_ref,
                    aq_ref, ak_ref, avt_ref, iq_ref, ik_ref, iwt_ref):
    x = x_ref[...]
    ms = jnp.mean(x * x, axis=-1, keepdims=True)
    mod = mod_ref[0]
    h = x * lax.rsqrt(ms + EPS) * gain_ref[...]
    h = h * (1.0 + mod[1:2, :]) + mod[0:1, :]
    hb = h.astype(BF16)

    def proj(g, width=GROUP):
        return _dot(hb, w_ref[:, g * GROUP:g * GROUP + width])

    hq_ref[...] = (_silu(proj(0)) * (HG_DIM ** -0.5)).astype(BF16)
    a = lbp_ref[...]
    amax = jnp.max(a, axis=0, keepdims=True)
    e = jnp.exp(a - amax)
    lb = e[0:1, :] / jnp.sum(e, axis=0, keepdims=True)
    fr = proj(1)
    f = lb + (1.0 - lb) * jax.nn.sigmoid(fr)
    lf_ref[...] = jnp.log(f).astype(BF16)
    hk_ref[...] = ((1.0 - lb) * jax.nn.sigmoid(-fr)).astype(BF16)
    hv_ref[...] = proj(2).astype(BF16)
    hg_ref[...] = _silu(proj(3)).astype(BF16)

    cosa, sina, cosb, sinb = _rope_patterns(tab_ref[0].T)
    half_a = ROPE_HALF_A
    q = proj(4)
    k = proj(5)
    for hh in range(ATT_HEADS):
        sl = slice(hh * ATT_DIM, (hh + 1) * ATT_DIM)
        aq_ref[:, sl] = (_rope(q[:, sl], cosa, sina, half_a, ATT_DIM) * Q_SCALE).astype(BF16)
        ak_ref[:, sl] = _rope(k[:, sl], cosa, sina, half_a, ATT_DIM).astype(BF16)
    v = proj(6)
    for hh in range(ATT_HEADS):
        r0 = hh * V_ROWS
        avt_ref[0, 0, r0:r0 + ATT_DIM, :] = v[:, hh * ATT_DIM:(hh + 1) * ATT_DIM].T.astype(BF16)
        avt_ref[0, 0, r0 + ATT_DIM:r0 + V_ROWS, :] = jnp.ones((V_ROWS - ATT_DIM, v.shape[0]), BF16)

    half_b = ROPE_HALF_B
    qi = proj(7)
    for cc in range(GROUP // LANES):
        qr = _rope(qi[:, cc * LANES:(cc + 1) * LANES], cosb, sinb, half_b, IDX_DIM).astype(BF16)
        for hh in range(LANES // IDX_DIM):
            iq_ref[cc * (LANES // IDX_DIM) + hh] = qr[:, hh * IDX_DIM:(hh + 1) * IDX_DIM]
    kw = _dot(hb, wt_ref[...])
    ki = kw[:, 0:LANES]
    lane = lax.broadcasted_iota(jnp.int32, ki.shape, 1)
    real = lane < IDX_DIM
    mu = jnp.sum(ki, axis=-1, keepdims=True) * (1.0 / IDX_DIM)
    d = jnp.where(real, ki - mu, 0.0)
    var = jnp.sum(d * d, axis=-1, keepdims=True) * (1.0 / IDX_DIM)
    kn = d * lax.rsqrt(var + EPS) * lnw_ref[...] + lnb_ref[...]
    kn = _rope(kn, cosb, sinb, half_b, IDX_DIM)
    ik_ref[...] = kn[:, 0:IDX_DIM].astype(BF16)
    iwt_ref[...] = kw[:, LANES:2 * LANES].T[0:IDX_HEADS, :] * (IDX_HEADS ** -0.5 * IDX_DIM ** -0.5)


def _in_proj(x2, mod3, gain, w_main, w_tail, lbp, lnw, lnb, tab, T, tm):
    M, D = x2.shape
    nt = T // tm
    row = lambda i: (i, 0)
    const = lambda i: (0, 0)
    bf16o = jax.ShapeDtypeStruct((M, GROUP), BF16)
    grp_spec = pl.BlockSpec((tm, GROUP), row)
    return pl.pallas_call(
        _in_proj_kernel,
        grid=(M // tm,),
        in_specs=[pl.BlockSpec((tm, D), row),
                  pl.BlockSpec((1, N_MOD, D), lambda i: (i // nt, 0, 0)),
                  pl.BlockSpec((1, D), const),
                  _resident(w_main.shape, const),
                  _resident(w_tail.shape, const),
                  pl.BlockSpec(lbp.shape, const),
                  pl.BlockSpec((1, LANES), const),
                  pl.BlockSpec((1, LANES), const),
                  pl.BlockSpec((1, LANES, tm), lambda i: (i // nt, 0, i % nt))],
        out_specs=[grp_spec] * 7 + [pl.BlockSpec((1, 1, ATT_HEADS * V_ROWS, tm), lambda i: (i // nt, i % nt, 0, 0)),
                                    pl.BlockSpec((IDX_HEADS, tm, IDX_DIM), lambda i: (0, i, 0)),
                                    pl.BlockSpec((tm, IDX_DIM), row),
                                    pl.BlockSpec((IDX_HEADS, tm), lambda i: (0, i))],
        out_shape=[bf16o] * 7 + [jax.ShapeDtypeStruct((M // T, nt, ATT_HEADS * V_ROWS, tm), BF16),
                                              jax.ShapeDtypeStruct((IDX_HEADS, M, IDX_DIM), BF16),
                                              jax.ShapeDtypeStruct((M, IDX_DIM), BF16),
                                              jax.ShapeDtypeStruct((IDX_HEADS, M), F32)],
        compiler_params=_cparams(("parallel",)),
        name="in_proj",
    )(x2, mod3, gain, w_main, w_tail, lbp, lnw, lnb, tab)


def _hgrn_consts():
    C = HG_CHUNK
    t = np.arange(C)
    tri = (t[None, :] <= t[:, None]).astype(np.float32)
    blocks = [tri]
    for m in HG_LEVELS:
        split = (t // (2 * m)) * (2 * m) + m - 1
        blocks.append(tri - tri[split])
    mall = np.concatenate(blocks, axis=0)
    x = t[:, None] ^ t[None, :]
    lvl = np.full((C, C), len(HG_LEVELS) + 1, np.int32)
    for li, m in enumerate(HG_LEVELS):
        lvl[(t[:, None] > t[None, :]) & (x >= m) & (x < 2 * m)] = li
    lvl[t[:, None] == t[None, :]] = len(HG_LEVELS)
    return jnp.asarray(mall, BF16), jnp.asarray(lvl)


def _hgrn_kernel(hq_ref, lf_ref, hk_ref, hv_ref, hg_ref, mall_ref, lvl_ref, onorm_ref, o_ref, st_ref):
    C = HG_CHUNK
    nl = len(HG_LEVELS)

    @pl.when(pl.program_id(1) == 0)
    def _():
        st_ref[...] = jnp.zeros_like(st_ref)

    lvl = lvl_ref[...]
    row = lax.broadcasted_iota(jnp.int32, (C, HG_DIM), 0)
    mall = mall_ref[...]

    for ci in range(hq_ref.shape[0] // C):
        rs = slice(ci * C, (ci + 1) * C)
        for h in range(HG_HEADS):
            cs = slice(h * HG_DIM, (h + 1) * HG_DIM)
            q, k, v = hq_ref[rs, cs], hk_ref[rs, cs], hv_ref[rs, cs]
            dall = _dot(mall, lf_ref[rs, cs])
            g = dall[0:C]

            a = jnp.zeros((C, C), F32)
            for li, m in enumerate(HG_LEVELS):
                dm = dall[(li + 1) * C:(li + 2) * C]
                xm = (jnp.where((row & m) != 0, q, k) * jnp.exp(-jnp.abs(dm))).astype(BF16)
                a = jnp.where(lvl == li, _dot_nt(xm, xm), a)
            a = jnp.where(lvl == nl, _dot_nt(q, k), a)

            st = st_ref[h]
            o = _dot_nt((q * jnp.exp(g)).astype(BF16), st.astype(BF16)) + _dot(a.astype(BF16), v)
            g_last = g[C - 1:C, :]
            kd = (k * jnp.exp(g_last - g)).astype(BF16)
            st_ref[h] = jnp.exp(g_last) * st + _dot(v.astype(F32).T.astype(BF16), kd)

            o = o * lax.rsqrt(jnp.mean(o * o, axis=-1, keepdims=True) + EPS)
            o_ref[rs, cs] = (o * onorm_ref[:, cs] * hg_ref[rs, cs]).astype(BF16)


def _hgrn(hq, lf, hk, hv, hg, onorm, B, T, ct):
    M = hq.shape[0]
    nct = T // ct
    mall, lvl = _hgrn_consts()
    blk = pl.BlockSpec((ct, GROUP), lambda b, c: (b * nct + c, 0))
    const = lambda b, c: (0, 0)
    return pl.pallas_call(
        _hgrn_kernel,
        grid=(B, nct),
        in_specs=[blk] * 5 + [pl.BlockSpec(mall.shape, const), pl.BlockSpec(lvl.shape, const),
                              pl.BlockSpec((1, GROUP), const)],
        out_specs=blk,
        out_shape=jax.ShapeDtypeStruct((M, GROUP), BF16),
        scratch_shapes=[pltpu.VMEM((HG_HEADS, HG_DIM, HG_DIM), F32)],
        compiler_params=_cparams(("parallel", "arbitrary")),
        name="hgrn2",
    )(hq, lf, hk, hv, hg, mall, lvl, onorm)


def _fold8(a, op):
    chains = [None] * 4
    for r in range(a.shape[0] // 8):
        part = a[r * 8:(r + 1) * 8, :]
        c = r % len(chains)
        chains[c] = part if chains[c] is None else op(chains[c], part)
    return op(op(chains[0], chains[1]), op(chains[2], chains[3]))


def _dsa_kernel(iq_ref, iwt_ref, ik_ref, aq_ref, ak_ref, avt_ref, o_ref,
                sc_ref, mm_ref, acc_ref, m_ref, *, qb, kt, topk, seq):
    i = pl.program_id(1)
    q0 = i * qb
    nfull = q0 // kt
    nkt = nfull + 1
    qpos = q0 + lax.broadcasted_iota(jnp.int32, (1, qb), 1)
    kk = jnp.minimum(qpos + 1, topk).astype(F32)

    mm_ref[0:8, :] = jnp.full((8, qb), jnp.inf, F32)
    mm_ref[8:16, :] = jnp.full((8, qb), -jnp.inf, F32)
    mm_ref[16:32, :] = jnp.zeros((16, qb), F32)

    def score_tile(j, diag):
        ks = pl.multiple_of(j * kt, kt)
        ki = ik_ref[pl.ds(ks, kt), :]
        s = None
        for h in range(IDX_HEADS):
            t = jnp.maximum(_dot_nt(ki, iq_ref[h]), 0.0) * iwt_ref[h:h + 1, :]
            s = t if s is None else s + t
        if diag:
            causal = ks + lax.broadcasted_iota(jnp.int32, (kt, qb), 0) <= qpos
            s_hi = jnp.where(causal, s, -jnp.inf)
            s_lo = jnp.where(causal, s, jnp.inf)
        else:
            s_hi = s_lo = s
        sc_ref[pl.ds(ks, kt), :] = s_hi
        mm_ref[0:8, :] = jnp.minimum(mm_ref[0:8, :], _fold8(s_lo, jnp.minimum))
        mm_ref[8:16, :] = jnp.maximum(mm_ref[8:16, :], _fold8(s_hi, jnp.maximum))
        mm_ref[16:24, :] += _fold8(jnp.where(s_hi >= 0.0, 1.0, 0.0), jnp.add)
        mm_ref[24:32, :] += _fold8(jnp.where(s_hi > 0.0, 1.0, 0.0), jnp.add)

    def full_tile(j, c):
        score_tile(j, False)
        return c

    lax.fori_loop(0, nfull, full_tile, 0)
    score_tile(nfull, True)
    mn = jnp.min(mm_ref[0:8, :], axis=0, keepdims=True)
    mx = jnp.max(mm_ref[8:16, :], axis=0, keepdims=True)
    c_nonneg = jnp.sum(mm_ref[16:24, :], axis=0, keepdims=True)
    c_pos = jnp.sum(mm_ref[24:32, :], axis=0, keepdims=True)

    def count_keys(hit_fn):
        def body(j, acc):
            ks = pl.multiple_of(j * kt, kt)
            return acc + _fold8(hit_fn(sc_ref[pl.ds(ks, kt), :], ks), jnp.add)
        acc = lax.fori_loop(0, nkt, body, jnp.zeros((8, qb), F32))
        return jnp.sum(acc, axis=0, keepdims=True)

    def count_ge(th):
        return count_keys(lambda s, ks: jnp.where(s >= th, 1.0, 0.0))

    above = c_pos >= kk
    below = c_nonneg < kk
    live = above | below
    zero = jnp.zeros((1, qb), F32)
    lo0 = jnp.where(below, mn, zero)
    cnt_lo0 = jnp.where(below, (qpos + 1).astype(F32), c_nonneg)
    hi0 = jnp.where(above, mx + jnp.abs(mx) * 1e-6 + 1e-30, zero)
    cnt_hi0 = jnp.where(above, zero, jnp.where(below, c_nonneg, c_pos))

    def midpoint(lo, hi):
        return lo + 0.5 * (hi - lo)

    def unresolved(lo, hi, cnt_lo):
        mid = midpoint(lo, hi)
        return jnp.max((live & (cnt_lo != kk) & (mid > lo) & (mid < hi)).astype(jnp.int32)) > 0

    def bis_cond(c):
        it, lo, hi, cnt_lo, cnt_hi = c
        return (it < 320) & unresolved(lo, hi, cnt_lo)

    def bis_body(c):
        it, lo, hi, cnt_lo, cnt_hi = c
        mid = midpoint(lo, hi)
        cnt = count_ge(mid)
        up = live & (cnt >= kk)
        dn = live & (cnt < kk)
        return (it + 1, jnp.where(up, mid, lo), jnp.where(dn, mid, hi),
                jnp.where(up, cnt, cnt_lo), jnp.where(dn, cnt, cnt_hi))

    _, lo, hi, cnt_lo, cnt_hi = lax.while_loop(bis_cond, bis_body, (jnp.int32(0), lo0, hi0, cnt_lo0, cnt_hi0))

    tied = cnt_lo != kk
    need = kk - cnt_hi

    def resolve_ties(cs):
        lo_c, need_c, tied_c = lo[:, cs], need[:, cs], tied[:, cs]

        def tie_keys(j):
            ks = pl.multiple_of(j * kt, kt)
            rows = pl.ds(ks, kt)
            return rows, sc_ref[rows, cs], ks + lax.broadcasted_iota(jnp.int32, (kt, LANES), 0)

        def count_tie(jm):
            def body(j, acc):
                _, s, kpos = tie_keys(j)
                return acc + _fold8(jnp.where((s == lo_c) & (kpos <= jm), 1.0, 0.0), jnp.add)
            acc = lax.fori_loop(0, nkt, body, jnp.zeros((8, LANES), F32))
            return jnp.sum(acc, axis=0, keepdims=True)

        def tie_body(_, c):
            jlo, jhi = c
            jm = (jlo + jhi) // 2
            ok = count_tie(jm) >= need_c
            return jnp.where(ok, jlo, jm), jnp.where(ok, jm, jhi)

        steps = int(np.ceil(np.log2(seq))) + 1
        _, jhi = lax.fori_loop(0, steps, tie_body,
                               (jnp.full((1, LANES), -1, jnp.int32), jnp.full((1, LANES), seq - 1, jnp.int32)))
        jmax = jnp.where(tied_c, jhi, seq)

        def strike(j, c):
            rows, s, kpos = tie_keys(j)
            sc_ref[rows, cs] = jnp.where((s == lo_c) & (kpos > jmax), -jnp.inf, s)
            return c

        lax.fori_loop(0, nkt, strike, 0)

    for c in range(qb // LANES):
        cs = slice(c * LANES, (c + 1) * LANES)
        pl.when(jnp.max(tied[:, cs].astype(jnp.int32)) > 0)(functools.partial(resolve_ties, cs))

    acc_ref[...] = jnp.zeros_like(acc_ref)
    m_ref[...] = jnp.full(m_ref.shape, NEG_BIG, F32)

    def attn_tile(j, c):
        ks = pl.multiple_of(j * kt, kt)
        bias = jnp.where(sc_ref[pl.ds(ks, kt), :] >= lo, 0.0, NEG_BIG)
        for h in range(ATT_HEADS):
            sl = slice(h * ATT_DIM, (h + 1) * ATT_DIM)
            vr = slice(h * V_ROWS, (h + 1) * V_ROWS)
            lg = _dot_nt(ak_ref[pl.ds(ks, kt), sl], aq_ref[:, sl]) + bias
            m_old = m_ref[h:h + 1, :]
            m_new = jnp.maximum(m_old, jnp.max(_fold8(lg, jnp.maximum), axis=0, keepdims=True))
            p = jnp.exp2(lg - m_new).astype(BF16)
            acc_ref[vr, :] = jnp.exp2(m_old - m_new) * acc_ref[vr, :] + _dot(avt_ref[0, j, vr, :], p)
            m_ref[h:h + 1, :] = m_new
        return c

    lax.fori_loop(0, nkt, attn_tile, 0)
    for h in range(ATT_HEADS):
        r0 = h * V_ROWS
        o_t = acc_ref[r0:r0 + ATT_DIM, :] / acc_ref[r0 + ATT_DIM:r0 + ATT_DIM + 1, :]
        o_ref[:, h * ATT_DIM:(h + 1) * ATT_DIM] = o_t.T.astype(BF16)


def _dsa(iq, iwt, ik, aq, ak, avt, B, T, qb, kt):
    M = ak.shape[0]
    nq = T // qb
    assert kt % qb == 0 and T % kt == 0 and avt.shape[1:] == (T // kt, ATT_HEADS * V_ROWS, kt)
    topk = min(TOPK_MAX, T // 4)
    qrow = lambda b, i: (b * nq + i, 0)
    brow = lambda b, i: (b, 0)
    kern = functools.partial(_dsa_kernel, qb=qb, kt=kt, topk=topk, seq=T)
    return pl.pallas_call(
        kern,
        grid=(B, nq),
        in_specs=[pl.BlockSpec((IDX_HEADS, qb, IDX_DIM), lambda b, i: (0, b * nq + i, 0)),
                  pl.BlockSpec((IDX_HEADS, qb), lambda b, i: (0, b * nq + i)),
                  _resident((T, IDX_DIM), brow),
                  pl.BlockSpec((qb, GROUP), qrow),
                  _resident((T, GROUP), brow),
                  _resident((1,) + avt.shape[1:], lambda b, i: (b, 0, 0, 0))],
        out_specs=pl.BlockSpec((qb, GROUP), qrow),
        out_shape=jax.ShapeDtypeStruct((M, GROUP), BF16),
        scratch_shapes=[pltpu.VMEM((T, qb), F32),
                        pltpu.VMEM((32, qb), F32),
                        pltpu.VMEM((ATT_HEADS * V_ROWS, qb), F32),
                        pltpu.VMEM((8, qb), F32)],
        compiler_params=_cparams(("parallel", "arbitrary")),
        name="dsa",
    )(iq, iwt, ik, aq, ak, avt)


def _rms(a, gain):
    return a * lax.rsqrt(jnp.mean(a * a, axis=-1, keepdims=True) + EPS) * gain


def _out_ffn_kernel(ohg_ref, oatt_ref, x_ref, mod_ref, gains_ref, wo_ref, wi_ref, wf_ref, o_ref, acc_ref,
                    *, d_ff, fc):
    mod = mod_ref[0]
    gains = gains_ref[...]
    hw = ohg_ref.shape[1]
    y = _dot(ohg_ref[...], wo_ref[0:hw, :]) + _dot(oatt_ref[...], wo_ref[hw:, :])
    x1 = x_ref[...] + mod[2:3, :] * _rms(y, gains[0:1, :])
    h2 = (_rms(x1, gains[1:2, :]) * (1.0 + mod[4:5, :]) + mod[3:4, :]).astype(BF16)
    for c in range(d_ff // fc):
        gate = _dot(h2, wi_ref[:, c * fc:(c + 1) * fc])
        up = _dot(h2, wi_ref[:, d_ff + c * fc:d_ff + (c + 1) * fc])
        part = _dot((_silu(gate) * up).astype(BF16), wf_ref[c * fc:(c + 1) * fc, :])
        if c == 0:
            acc_ref[...] = part
        else:
            acc_ref[...] += part
    o_ref[...] = x1 + mod[5:6, :] * _rms(acc_ref[...], gains[2:3, :])


def _out_ffn(ohg, oatt, x2, mod3, gains, wo, wi, wf, T, tm):
    M, D = x2.shape
    nt = T // tm
    d_ff = wf.shape[0]
    row = lambda i: (i, 0)
    const = lambda i: (0, 0)
    kern = functools.partial(_out_ffn_kernel, d_ff=d_ff, fc=256)
    return pl.pallas_call(
        kern,
        grid=(M // tm,),
        in_specs=[pl.BlockSpec((tm, GROUP), row),
                  pl.BlockSpec((tm, GROUP), row),
                  pl.BlockSpec((tm, D), row),
                  pl.BlockSpec((1, N_MOD, D), lambda i: (i // nt, 0, 0)),
                  pl.BlockSpec((3, D), const),
                  _resident(wo.shape, const),
                  _resident(wi.shape, const),
                  _resident(wf.shape, const)],
        out_specs=pl.BlockSpec((tm, D), row),
        out_shape=jax.ShapeDtypeStruct((M, D), F32),
        scratch_shapes=[pltpu.VMEM((tm, D), F32)],
        compiler_params=_cparams(("parallel",)),
        name="out_ffn",
    )(ohg, oatt, x2, mod3, gains, wo, wi, wf)


def kernel(x, c, positions, w_ada, b_ada, norm_pre_mix, norm_post_mix, norm_pre_ffn, norm_post_ffn, w_in,
           hgrn_lower_bound, hgrn_out_norm, idx_k_norm_w, idx_k_norm_b, w_out, w_ffn_in, w_ffn_out):
    B, T, D = x.shape
    depth = w_ada.shape[0]
    assert depth == 1 and hgrn_lower_bound.shape[0] == 2
    assert T % 512 == 0 and D % LANES == 0
    tm = 512

    mod3 = _adaln(c, w_ada[0], b_ada[0]).reshape(B, N_MOD, D)
    tab = _rope_table(positions)

    main = 8 * GROUP
    w = w_in[0]
    w_main = w[:, :main].astype(BF16)
    w_tail = jnp.concatenate([jnp.pad(w[:, main:main + IDX_DIM], ((0, 0), (0, LANES - IDX_DIM))),
                              jnp.pad(w[:, main + IDX_DIM:], ((0, 0), (0, LANES - IDX_HEADS)))], axis=1).astype(BF16)
    lnw = jnp.pad(idx_k_norm_w[0], (0, LANES - IDX_DIM)).reshape(1, LANES)
    lnb = jnp.pad(idx_k_norm_b[0], (0, LANES - IDX_DIM)).reshape(1, LANES)

    x2 = x.reshape(B * T, D)
    hq, lf, hk, hv, hg, aq, ak, avt, iq, ik, iwt = _in_proj(
        x2, mod3, norm_pre_mix, w_main, w_tail, hgrn_lower_bound, lnw, lnb, tab, T, tm)

    o_hg = _hgrn(hq, lf, hk, hv, hg, hgrn_out_norm, B, T, 256)
    o_att = _dsa(iq, iwt, ik, aq, ak, avt, B, T, 512, tm)

    gains = jnp.concatenate([norm_post_mix, norm_pre_ffn, norm_post_ffn], axis=0)
    out = _out_ffn(o_hg, o_att, x2, mod3, gains, w_out[0].astype(BF16), w_ffn_in[0].astype(BF16),
                   w_ffn_out[0].astype(BF16), T, tm)
    return out.reshape(B, T, D)
```

```python
import functools

import numpy as np
import jax
import jax.numpy as jnp
from jax import lax
from jax.experimental import pallas as pl
from jax.experimental.pallas import tpu as pltpu

F32 = jnp.float32
BF16 = jnp.bfloat16

HG_HEADS = 4
HG_DIM = 128
ATT_HEADS = 4
ATT_DIM = 128
IDX_HEADS = 8
IDX_DIM = 64
TOPK_MAX = 256
ROPE_THETA = 500000.0
ROPE_FRACTION = 4
N_MOD = 6
EPS = 1e-6

GROUP = 512
LANES = 128
VMEM_LIMIT = 56 * 1024 * 1024

HG_CHUNK = 128
HG_LEVELS = (64, 32, 16, 8, 4, 2, 1)

NEG_BIG = -1e30
DENOM_FLOOR = 2.0 ** -64
NORM_SLACK = 1.01
Q_SCALE = ATT_DIM ** -0.5 * 1.4426950408889634
V_ROWS = ATT_DIM + 16


def _cparams(sem):
    return pltpu.CompilerParams(dimension_semantics=sem, vmem_limit_bytes=VMEM_LIMIT)


def _resident(shape, index_map):
    return pl.BlockSpec(shape, index_map, pipeline_mode=pl.Buffered(1))


def _split_bf16(a):
    hi = a.astype(BF16)
    lo = (a - hi.astype(F32)).astype(BF16)
    return hi, lo


def _dot(a, b):
    return jnp.dot(a, b, preferred_element_type=F32)


def _dot_nt(a, b):
    return lax.dot_general(a, b, (((1,), (1,)), ((), ())), preferred_element_type=F32)


def _silu(a):
    return a * jax.nn.sigmoid(a)


def _adaln_kernel(c_ref, w_ref, b_ref, o_ref):
    a = _silu(c_ref[...])
    a_hi, a_lo = _split_bf16(a)
    w_hi, w_lo = _split_bf16(w_ref[...])
    acc = _dot(a_hi, w_hi) + (_dot(a_hi, w_lo) + _dot(a_lo, w_hi))
    o_ref[...] = acc + b_ref[...]


def _adaln(c, w, b):
    B, D = c.shape
    N = w.shape[1]
    return pl.pallas_call(
        _adaln_kernel,
        grid=(N // D,),
        in_specs=[pl.BlockSpec((B, D), lambda j: (0, 0)),
                  pl.BlockSpec((D, D), lambda j: (0, j)),
                  pl.BlockSpec((1, D), lambda j: (0, j))],
        out_specs=pl.BlockSpec((B, D), lambda j: (0, j)),
        out_shape=jax.ShapeDtypeStruct((B, N), F32),
        compiler_params=_cparams(("arbitrary",)),
        name="adaln",
    )(c, w, b.reshape(1, N))


def _rope_table_kernel(pos_ref, fa_ref, fb_ref, o_ref):
    pos = pos_ref[0].astype(F32)
    ang_a = fa_ref[...] * pos
    ang_b = fb_ref[...] * pos
    ha = fa_ref.shape[0]
    hb = fb_ref.shape[0]
    o_ref[0, 0:ha, :] = jnp.cos(ang_a)
    o_ref[0, ha:2 * ha, :] = jnp.sin(ang_a)
    o_ref[0, 2 * ha:2 * ha + hb, :] = jnp.cos(ang_b)
    o_ref[0, 2 * ha + hb:2 * ha + 2 * hb, :] = jnp.sin(ang_b)
    o_ref[0, 2 * ha + 2 * hb:, :] = jnp.zeros((LANES - 2 * ha - 2 * hb, pos.shape[1]), F32)


ROPE_HALF_A = ATT_DIM // ROPE_FRACTION // 2
ROPE_HALF_B = IDX_DIM // ROPE_FRACTION // 2


def _rope_table(positions):
    B, T = positions.shape
    ha, hb = ROPE_HALF_A, ROPE_HALF_B
    fa = (ROPE_THETA ** (-jnp.arange(ha, dtype=F32) / ha)).reshape(ha, 1)
    fb = (ROPE_THETA ** (-jnp.arange(hb, dtype=F32) / hb)).reshape(hb, 1)
    return pl.pallas_call(
        _rope_table_kernel,
        grid=(B,),
        in_specs=[pl.BlockSpec((1, 1, T), lambda b: (b, 0, 0)),
                  pl.BlockSpec((ha, 1), lambda b: (0, 0)),
                  pl.BlockSpec((hb, 1), lambda b: (0, 0))],
        out_specs=pl.BlockSpec((1, LANES, T), lambda b: (b, 0, 0)),
        out_shape=jax.ShapeDtypeStruct((B, LANES, T), F32),
        compiler_params=_cparams(("arbitrary",)),
        name="rope_table",
    )(positions.reshape(B, 1, T), fa, fb)


def _rope_patterns(tab):
    ha, hb = ROPE_HALF_A, ROPE_HALF_B
    lane = lax.broadcasted_iota(jnp.int32, tab.shape, 1)
    shifted = lambda sh: pltpu.roll(tab, sh % LANES, axis=1)
    cos_a = jnp.where(lane < ha, tab, jnp.where(lane < 2 * ha, shifted(ha), 1.0))
    sin_a = jnp.where(lane < ha, -shifted(-ha), jnp.where(lane < 2 * ha, tab, 0.0))
    l64 = lane & (IDX_DIM - 1)
    first = lane < IDX_DIM
    cb0, sb0 = 2 * ha, 2 * ha + hb
    cos_b = jnp.where(l64 < hb, jnp.where(first, shifted(-cb0), shifted(IDX_DIM - cb0)),
                      jnp.where(l64 < 2 * hb, jnp.where(first, shifted(hb - cb0), shifted(IDX_DIM + hb - cb0)), 1.0))
    sin_b = jnp.where(l64 < hb, -jnp.where(first, shifted(-sb0), shifted(IDX_DIM - sb0)),
                      jnp.where(l64 < 2 * hb, jnp.where(first, shifted(hb - sb0), shifted(IDX_DIM + hb - sb0)), 0.0))
    return cos_a, sin_a, cos_b, sin_b


def _rope(xb, cos, sin, half, period):
    lane = lax.broadcasted_iota(jnp.int32, xb.shape, 1)
    fwd = pltpu.roll(xb, LANES - half, axis=1)
    bwd = pltpu.roll(xb, half, axis=1)
    partner = jnp.where((lane & (period - 1)) < half, fwd, bwd)
    return xb * cos + partner * sin


def _in_proj_kernel(x_ref, mod_ref, gain_ref, w_ref, wt_ref, lbp_ref, lnw_ref, lnb_ref, tab_ref,
                    hq_ref, lf_ref, hk_ref, hv_ref, hg_ref,
                    aq_ref, ak_ref, avt_ref, iq_ref, ik_ref, iwt_ref, nrm_ref):
    x = x_ref[...]
    ms = jnp.mean(x * x, axis=-1, keepdims=True)
    mod = mod_ref[0]
    h = x * lax.rsqrt(ms + EPS) * gain_ref[...]
    h = h * (1.0 + mod[1:2, :]) + mod[0:1, :]
    hb = h.astype(BF16)

    def proj(g, width=GROUP):
        return _dot(hb, w_ref[:, g * GROUP:g * GROUP + width])

    hq_ref[...] = (_silu(proj(0)) * (HG_DIM ** -0.5)).astype(BF16)
    a = lbp_ref[...]
    amax = jnp.max(a, axis=0, keepdims=True)
    e = jnp.exp(a - amax)
    lb = e[0:1, :] / jnp.sum(e, axis=0, keepdims=True)
    fr = proj(1)
    f = lb + (1.0 - lb) * jax.nn.sigmoid(fr)
    lf_ref[...] = jnp.log(f).astype(BF16)
    hk_ref[...] = ((1.0 - lb) * jax.nn.sigmoid(-fr)).astype(BF16)
    hv_ref[...] = proj(2).astype(BF16)
    hg_ref[...] = _silu(proj(3)).astype(BF16)

    cosa, sina, cosb, sinb = _rope_patterns(tab_ref[0].T)
    half_a = ROPE_HALF_A
    q = proj(4)
    k = proj(5)
    lane_a = lax.broadcasted_iota(jnp.int32, cosa.shape, 1)
    norms = jnp.zeros(cosa.shape, F32)
    for hh in range(ATT_HEADS):
        sl = slice(hh * ATT_DIM, (hh + 1) * ATT_DIM)
        qh = (_rope(q[:, sl], cosa, sina, half_a, ATT_DIM) * Q_SCALE).astype(BF16)
        kh = _rope(k[:, sl], cosa, sina, half_a, ATT_DIM).astype(BF16)
        aq_ref[:, sl] = qh
        ak_ref[:, sl] = kh
        for slot, a in ((hh, qh), (ATT_HEADS + hh, kh)):
            norms = jnp.where(lane_a == slot, jnp.sqrt(_dot(a * a, jnp.ones((ATT_DIM, LANES), BF16))), norms)
    nrm_ref[...] = norms.T[0:2 * ATT_HEADS, :]
    v = proj(6)
    for hh in range(ATT_HEADS):
        r0 = hh * V_ROWS
        avt_ref[0, 0, r0:r0 + ATT_DIM, :] = v[:, hh * ATT_DIM:(hh + 1) * ATT_DIM].T.astype(BF16)
        avt_ref[0, 0, r0 + ATT_DIM:r0 + V_ROWS, :] = jnp.ones((V_ROWS - ATT_DIM, v.shape[0]), BF16)

    half_b = ROPE_HALF_B
    qi = proj(7)
    for cc in range(GROUP // LANES):
        qr = _rope(qi[:, cc * LANES:(cc + 1) * LANES], cosb, sinb, half_b, IDX_DIM).astype(BF16)
        for hh in range(LANES // IDX_DIM):
            iq_ref[cc * (LANES // IDX_DIM) + hh] = qr[:, hh * IDX_DIM:(hh + 1) * IDX_DIM]
    kw = _dot(hb, wt_ref[...])
    ki = kw[:, 0:LANES]
    lane = lax.broadcasted_iota(jnp.int32, ki.shape, 1)
    real = lane < IDX_DIM
    mu = jnp.sum(ki, axis=-1, keepdims=True) * (1.0 / IDX_DIM)
    d = jnp.where(real, ki - mu, 0.0)
    var = jnp.sum(d * d, axis=-1, keepdims=True) * (1.0 / IDX_DIM)
    kn = d * lax.rsqrt(var + EPS) * lnw_ref[...] + lnb_ref[...]
    kn = _rope(kn, cosb, sinb, half_b, IDX_DIM)
    ik_ref[...] = kn[:, 0:IDX_DIM].astype(BF16)
    iwt_ref[...] = kw[:, LANES:2 * LANES].T[0:IDX_HEADS, :] * (IDX_HEADS ** -0.5 * IDX_DIM ** -0.5)


def _in_proj(x2, mod3, gain, w_main, w_tail, lbp, lnw, lnb, tab, T, tm):
    M, D = x2.shape
    nt = T // tm
    row = lambda i: (i, 0)
    const = lambda i: (0, 0)
    bf16o = jax.ShapeDtypeStruct((M, GROUP), BF16)
    grp_spec = pl.BlockSpec((tm, GROUP), row)
    return pl.pallas_call(
        _in_proj_kernel,
        grid=(M // tm,),
        in_specs=[pl.BlockSpec((tm, D), row),
                  pl.BlockSpec((1, N_MOD, D), lambda i: (i // nt, 0, 0)),
                  pl.BlockSpec((1, D), const),
                  _resident(w_main.shape, const),
                  _resident(w_tail.shape, const),
                  pl.BlockSpec(lbp.shape, const),
                  pl.BlockSpec((1, LANES), const),
                  pl.BlockSpec((1, LANES), const),
                  pl.BlockSpec((1, LANES, tm), lambda i: (i // nt, 0, i % nt))],
        out_specs=[grp_spec] * 7 + [pl.BlockSpec((1, 1, ATT_HEADS * V_ROWS, tm), lambda i: (i // nt, i % nt, 0, 0)),
                                    pl.BlockSpec((IDX_HEADS, tm, IDX_DIM), lambda i: (0, i, 0)),
                                    pl.BlockSpec((tm, IDX_DIM), row),
                                    pl.BlockSpec((IDX_HEADS, tm), lambda i: (0, i)),
                                    pl.BlockSpec((2 * ATT_HEADS, tm), lambda i: (0, i))],
        out_shape=[bf16o] * 7 + [jax.ShapeDtypeStruct((M // T, nt, ATT_HEADS * V_ROWS, tm), BF16),
                                 jax.ShapeDtypeStruct((IDX_HEADS, M, IDX_DIM), BF16),
                                 jax.ShapeDtypeStruct((M, IDX_DIM), BF16),
                                 jax.ShapeDtypeStruct((IDX_HEADS, M), F32),
                                 jax.ShapeDtypeStruct((2 * ATT_HEADS, M), F32)],
        compiler_params=_cparams(("parallel",)),
        name="in_proj",
    )(x2, mod3, gain, w_main, w_tail, lbp, lnw, lnb, tab)


def _hgrn_consts():
    C = HG_CHUNK
    t = np.arange(C)
    tri = (t[None, :] <= t[:, None]).astype(np.float32)
    blocks = [tri]
    for m in HG_LEVELS:
        split = (t // (2 * m)) * (2 * m) + m - 1
        blocks.append(tri - tri[split])
    mall = np.concatenate(blocks, axis=0)
    x = t[:, None] ^ t[None, :]
    lvl = np.full((C, C), len(HG_LEVELS) + 1, np.int32)
    for li, m in enumerate(HG_LEVELS):
        lvl[(t[:, None] > t[None, :]) & (x >= m) & (x < 2 * m)] = li
    lvl[t[:, None] == t[None, :]] = len(HG_LEVELS)
    return jnp.asarray(mall, BF16), jnp.asarray(lvl)


def _hgrn_kernel(hq_ref, lf_ref, hk_ref, hv_ref, hg_ref, mall_ref, lvl_ref, onorm_ref, o_ref, st_ref):
    C = HG_CHUNK
    nl = len(HG_LEVELS)

    @pl.when(pl.program_id(1) == 0)
    def _():
        st_ref[...] = jnp.zeros_like(st_ref)

    lvl = lvl_ref[...]
    row = lax.broadcasted_iota(jnp.int32, (C, HG_DIM), 0)
    mall = mall_ref[...]

    for ci in range(hq_ref.shape[0] // C):
        rs = slice(ci * C, (ci + 1) * C)
        for h in range(HG_HEADS):
            cs = slice(h * HG_DIM, (h + 1) * HG_DIM)
            q, k, v = hq_ref[rs, cs], hk_ref[rs, cs], hv_ref[rs, cs]
            dall = _dot(mall, lf_ref[rs, cs])
            g = dall[0:C]

            a = jnp.zeros((C, C), F32)
            for li, m in enumerate(HG_LEVELS):
                dm = dall[(li + 1) * C:(li + 2) * C]
                xm = (jnp.where((row & m) != 0, q, k) * jnp.exp(-jnp.abs(dm))).astype(BF16)
                a = jnp.where(lvl == li, _dot_nt(xm, xm), a)
            a = jnp.where(lvl == nl, _dot_nt(q, k), a)

            st = st_ref[h]
            o = _dot_nt((q * jnp.exp(g)).astype(BF16), st.astype(BF16)) + _dot(a.astype(BF16), v)
            g_last = g[C - 1:C, :]
            kd = (k * jnp.exp(g_last - g)).astype(BF16)
            st_ref[h] = jnp.exp(g_last) * st + _dot(v.astype(F32).T.astype(BF16), kd)

            o = o * lax.rsqrt(jnp.mean(o * o, axis=-1, keepdims=True) + EPS)
            o_ref[rs, cs] = (o * onorm_ref[:, cs] * hg_ref[rs, cs]).astype(BF16)


def _hgrn(hq, lf, hk, hv, hg, onorm, B, T, ct):
    M = hq.shape[0]
    nct = T // ct
    mall, lvl = _hgrn_consts()
    blk = pl.BlockSpec((ct, GROUP), lambda b, c: (b * nct + c, 0))
    const = lambda b, c: (0, 0)
    return pl.pallas_call(
        _hgrn_kernel,
        grid=(B, nct),
        in_specs=[blk] * 5 + [pl.BlockSpec(mall.shape, const), pl.BlockSpec(lvl.shape, const),
                              pl.BlockSpec((1, GROUP), const)],
        out_specs=blk,
        out_shape=jax.ShapeDtypeStruct((M, GROUP), BF16),
        scratch_shapes=[pltpu.VMEM((HG_HEADS, HG_DIM, HG_DIM), F32)],
        compiler_params=_cparams(("parallel", "arbitrary")),
        name="hgrn2",
    )(hq, lf, hk, hv, hg, mall, lvl, onorm)


def _fold8(a, op):
    chains = [None] * 4
    for r in range(a.shape[0] // 8):
        part = a[r * 8:(r + 1) * 8, :]
        c = r % len(chains)
        chains[c] = part if chains[c] is None else op(chains[c], part)
    return op(op(chains[0], chains[1]), op(chains[2], chains[3]))


def _dsa_kernel(iq_ref, iwt_ref, ik_ref, aq_ref, ak_ref, avt_ref, qnt_ref, knt_ref, o_ref,
                sc_ref, mm_ref, *head_refs, qb, kt, topk, seq):
    acc_refs, m_refs = head_refs[:ATT_HEADS], head_refs[ATT_HEADS:]
    i = pl.program_id(1)
    q0 = i * qb
    nfull = q0 // kt
    nkt = nfull + 1
    qpos = q0 + lax.broadcasted_iota(jnp.int32, (1, qb), 1)
    kk = jnp.minimum(qpos + 1, topk).astype(F32)

    mm_ref[0:8, :] = jnp.full((8, qb), jnp.inf, F32)
    mm_ref[8:16, :] = jnp.full((8, qb), -jnp.inf, F32)
    mm_ref[16:32, :] = jnp.zeros((16, qb), F32)

    def score_tile(j, diag):
        ks = pl.multiple_of(j * kt, kt)
        ki = ik_ref[pl.ds(ks, kt), :]
        s = None
        for h in range(IDX_HEADS):
            t = jnp.maximum(_dot_nt(ki, iq_ref[h]), 0.0) * iwt_ref[h:h + 1, :]
            s = t if s is None else s + t
        if diag:
            causal = ks + lax.broadcasted_iota(jnp.int32, (kt, qb), 0) <= qpos
            s_hi = jnp.where(causal, s, -jnp.inf)
            s_lo = jnp.where(causal, s, jnp.inf)
        else:
            s_hi = s_lo = s
        sc_ref[pl.ds(ks, kt), :] = s_hi
        mm_ref[0:8, :] = jnp.minimum(mm_ref[0:8, :], _fold8(s_lo, jnp.minimum))
        mm_ref[8:16, :] = jnp.maximum(mm_ref[8:16, :], _fold8(s_hi, jnp.maximum))
        mm_ref[16:24, :] += _fold8(jnp.where(s_hi >= 0.0, 1.0, 0.0), jnp.add)
        mm_ref[24:32, :] += _fold8(jnp.where(s_hi > 0.0, 1.0, 0.0), jnp.add)

    def full_tile(j, c):
        score_tile(j, False)
        return c

    lax.fori_loop(0, nfull, full_tile, 0)
    score_tile(nfull, True)
    mn = jnp.min(mm_ref[0:8, :], axis=0, keepdims=True)
    mx = jnp.max(mm_ref[8:16, :], axis=0, keepdims=True)
    c_nonneg = jnp.sum(mm_ref[16:24, :], axis=0, keepdims=True)
    c_pos = jnp.sum(mm_ref[24:32, :], axis=0, keepdims=True)

    def count_keys(hit_fn):
        def body(j, acc):
            ks = pl.multiple_of(j * kt, kt)
            return acc + _fold8(hit_fn(sc_ref[pl.ds(ks, kt), :], ks), jnp.add)
        acc = lax.fori_loop(0, nkt, body, jnp.zeros((8, qb), F32))
        return jnp.sum(acc, axis=0, keepdims=True)

    def count_ge(th):
        return count_keys(lambda s, ks: jnp.where(s >= th, 1.0, 0.0))

    above = c_pos >= kk
    below = c_nonneg < kk
    live = above | below
    zero = jnp.zeros((1, qb), F32)
    lo0 = jnp.where(below, mn, zero)
    cnt_lo0 = jnp.where(below, (qpos + 1).astype(F32), c_nonneg)
    hi0 = jnp.where(above, mx + jnp.abs(mx) * 1e-6 + 1e-30, zero)
    cnt_hi0 = jnp.where(above, zero, jnp.where(below, c_nonneg, c_pos))

    def midpoint(lo, hi):
        return lo + 0.5 * (hi - lo)

    def unresolved(lo, hi, cnt_lo):
        mid = midpoint(lo, hi)
        return jnp.max((live & (cnt_lo != kk) & (mid > lo) & (mid < hi)).astype(jnp.int32)) > 0

    def bis_cond(c):
        it, lo, hi, cnt_lo, cnt_hi = c
        return (it < 320) & unresolved(lo, hi, cnt_lo)

    def bis_body(c):
        it, lo, hi, cnt_lo, cnt_hi = c
        mid = midpoint(lo, hi)
        cnt = count_ge(mid)
        up = live & (cnt >= kk)
        dn = live & (cnt < kk)
        return (it + 1, jnp.where(up, mid, lo), jnp.where(dn, mid, hi),
                jnp.where(up, cnt, cnt_lo), jnp.where(dn, cnt, cnt_hi))

    _, lo, hi, cnt_lo, cnt_hi = lax.while_loop(bis_cond, bis_body, (jnp.int32(0), lo0, hi0, cnt_lo0, cnt_hi0))

    tied = cnt_lo != kk
    need = kk - cnt_hi

    def resolve_ties(cs):
        lo_c, need_c, tied_c = lo[:, cs], need[:, cs], tied[:, cs]

        def tie_keys(j):
            ks = pl.multiple_of(j * kt, kt)
            rows = pl.ds(ks, kt)
            return rows, sc_ref[rows, cs], ks + lax.broadcasted_iota(jnp.int32, (kt, LANES), 0)

        def count_tie(jm):
            def body(j, acc):
                _, s, kpos = tie_keys(j)
                return acc + _fold8(jnp.where((s == lo_c) & (kpos <= jm), 1.0, 0.0), jnp.add)
            acc = lax.fori_loop(0, nkt, body, jnp.zeros((8, LANES), F32))
            return jnp.sum(acc, axis=0, keepdims=True)

        def tie_body(_, c):
            jlo, jhi = c
            jm = (jlo + jhi) // 2
            ok = count_tie(jm) >= need_c
            return jnp.where(ok, jlo, jm), jnp.where(ok, jm, jhi)

        steps = int(np.ceil(np.log2(seq))) + 1
        _, jhi = lax.fori_loop(0, steps, tie_body,
                               (jnp.full((1, LANES), -1, jnp.int32), jnp.full((1, LANES), seq - 1, jnp.int32)))
        jmax = jnp.where(tied_c, jhi, seq)

        def strike(j, c):
            rows, s, kpos = tie_keys(j)
            sc_ref[rows, cs] = jnp.where((s == lo_c) & (kpos > jmax), -jnp.inf, s)
            return c

        lax.fori_loop(0, nkt, strike, 0)

    for c in range(qb // LANES):
        cs = slice(c * LANES, (c + 1) * LANES)
        pl.when(jnp.max(tied[:, cs].astype(jnp.int32)) > 0)(functools.partial(resolve_ties, cs))

    def logits(j, h, bias):
        ks = pl.multiple_of(j * kt, kt)
        sl = slice(h * ATT_DIM, (h + 1) * ATT_DIM)
        return _dot_nt(ak_ref[pl.ds(ks, kt), sl], aq_ref[:, sl]) + bias

    def select_bias(j):
        ks = pl.multiple_of(j * kt, kt)
        return jnp.where(sc_ref[pl.ds(ks, kt), :] >= lo, 0.0, NEG_BIG)

    def values(j, h):
        return avt_ref[0, j, h * V_ROWS:(h + 1) * V_ROWS, :]

    kmax = jnp.max(knt_ref[...], axis=1, keepdims=True)
    shift = [qnt_ref[h:h + 1, :] * kmax[ATT_HEADS + h:ATT_HEADS + h + 1, :] * NORM_SLACK for h in range(ATT_HEADS)]
    for h in range(ATT_HEADS):
        acc_refs[h][...] = jnp.zeros_like(acc_refs[h])

    def attn_tile_fast(j, c):
        bias = select_bias(j)
        for h in range(ATT_HEADS):
            p = jnp.exp2(logits(j, h, bias) - shift[h]).astype(BF16)
            acc_refs[h][...] += _dot(values(j, h), p)
        return c

    lax.fori_loop(0, nkt, attn_tile_fast, 0)
    denom_min = functools.reduce(jnp.minimum, [acc_refs[h][ATT_DIM:ATT_DIM + 1, :] for h in range(ATT_HEADS)])

    @pl.when(jnp.min(denom_min) < DENOM_FLOOR)
    def _():
        for h in range(ATT_HEADS):
            acc_refs[h][...] = jnp.zeros_like(acc_refs[h])
            m_refs[h][...] = jnp.full(m_refs[h].shape, NEG_BIG, F32)

        def attn_tile(j, c):
            bias = select_bias(j)
            for h in range(ATT_HEADS):
                acc_h, m_h = acc_refs[h], m_refs[h]
                lg = logits(j, h, bias)
                m_old = m_h[0:1, :]
                m_new = jnp.maximum(m_old, jnp.max(_fold8(lg, jnp.maximum), axis=0, keepdims=True))
                p = jnp.exp2(lg - m_new).astype(BF16)
                acc_h[...] = jnp.exp2(m_old - m_new) * acc_h[...] + _dot(values(j, h), p)
                m_h[0:1, :] = m_new
            return c

        lax.fori_loop(0, nkt, attn_tile, 0)

    for h in range(ATT_HEADS):
        o_t = acc_refs[h][0:ATT_DIM, :] / acc_refs[h][ATT_DIM:ATT_DIM + 1, :]
        o_ref[:, h * ATT_DIM:(h + 1) * ATT_DIM] = o_t.T.astype(BF16)


def _dsa(iq, iwt, ik, aq, ak, avt, nrm, B, T, qb, kt):
    M = ak.shape[0]
    nq = T // qb
    assert kt % qb == 0 and T % kt == 0 and avt.shape[1:] == (T // kt, ATT_HEADS * V_ROWS, kt)
    topk = min(TOPK_MAX, T // 4)
    qrow = lambda b, i: (b * nq + i, 0)
    brow = lambda b, i: (b, 0)
    kern = functools.partial(_dsa_kernel, qb=qb, kt=kt, topk=topk, seq=T)
    return pl.pallas_call(
        kern,
        grid=(B, nq),
        in_specs=[pl.BlockSpec((IDX_HEADS, qb, IDX_DIM), lambda b, i: (0, b * nq + i, 0)),
                  pl.BlockSpec((IDX_HEADS, qb), lambda b, i: (0, b * nq + i)),
                  _resident((T, IDX_DIM), brow),
                  pl.BlockSpec((qb, GROUP), qrow),
                  _resident((T, GROUP), brow),
                  _resident((1,) + avt.shape[1:], lambda b, i: (b, 0, 0, 0)),
                  pl.BlockSpec((2 * ATT_HEADS, qb), lambda b, i: (0, b * nq + i)),
                  _resident((2 * ATT_HEADS, T), lambda b, i: (0, b))],
        out_specs=pl.BlockSpec((qb, GROUP), qrow),
        out_shape=jax.ShapeDtypeStruct((M, GROUP), BF16),
        scratch_shapes=[pltpu.VMEM((T, qb), F32),
                        pltpu.VMEM((32, qb), F32),
                        ] + [pltpu.VMEM((V_ROWS, qb), F32)] * ATT_HEADS
                        + [pltpu.VMEM((8, qb), F32)] * ATT_HEADS,
        compiler_params=_cparams(("parallel", "arbitrary")),
        name="dsa",
    )(iq, iwt, ik, aq, ak, avt, nrm, nrm)


def _rms(a, gain):
    return a * lax.rsqrt(jnp.mean(a * a, axis=-1, keepdims=True) + EPS) * gain


def _out_ffn_kernel(ohg_ref, oatt_ref, x_ref, mod_ref, gains_ref, wo_ref, wi_ref, wf_ref, o_ref, acc_ref,
                    *, d_ff, fc):
    mod = mod_ref[0]
    gains = gains_ref[...]
    hw = ohg_ref.shape[1]
    y = _dot(ohg_ref[...], wo_ref[0:hw, :]) + _dot(oatt_ref[...], wo_ref[hw:, :])
    x1 = x_ref[...] + mod[2:3, :] * _rms(y, gains[0:1, :])
    h2 = (_rms(x1, gains[1:2, :]) * (1.0 + mod[4:5, :]) + mod[3:4, :]).astype(BF16)
    for c in range(d_ff // fc):
        gate = _dot(h2, wi_ref[:, c * fc:(c + 1) * fc])
        up = _dot(h2, wi_ref[:, d_ff + c * fc:d_ff + (c + 1) * fc])
        part = _dot((_silu(gate) * up).astype(BF16), wf_ref[c * fc:(c + 1) * fc, :])
        if c == 0:
            acc_ref[...] = part
        else:
            acc_ref[...] += part
    o_ref[...] = x1 + mod[5:6, :] * _rms(acc_ref[...], gains[2:3, :])


def _out_ffn(ohg, oatt, x2, mod3, gains, wo, wi, wf, T, tm):
    M, D = x2.shape
    nt = T // tm
    d_ff = wf.shape[0]
    row = lambda i: (i, 0)
    const = lambda i: (0, 0)
    kern = functools.partial(_out_ffn_kernel, d_ff=d_ff, fc=256)
    return pl.pallas_call(
        kern,
        grid=(M // tm,),
        in_specs=[pl.BlockSpec((tm, GROUP), row),
                  pl.BlockSpec((tm, GROUP), row),
                  pl.BlockSpec((tm, D), row),
                  pl.BlockSpec((1, N_MOD, D), lambda i: (i // nt, 0, 0)),
                  pl.BlockSpec((3, D), const),
                  _resident(wo.shape, const),
                  _resident(wi.shape, const),
                  _resident(wf.shape, const)],
        out_specs=pl.BlockSpec((tm, D), row),
        out_shape=jax.ShapeDtypeStruct((M, D), F32),
        scratch_shapes=[pltpu.VMEM((tm, D), F32)],
        compiler_params=_cparams(("parallel",)),
        name="out_ffn",
    )(ohg, oatt, x2, mod3, gains, wo, wi, wf)


def kernel(x, c, positions, w_ada, b_ada, norm_pre_mix, norm_post_mix, norm_pre_ffn, norm_post_ffn, w_in,
           hgrn_lower_bound, hgrn_out_norm, idx_k_norm_w, idx_k_norm_b, w_out, w_ffn_in, w_ffn_out):
    B, T, D = x.shape
    depth = w_ada.shape[0]
    assert depth == 1 and hgrn_lower_bound.shape[0] == 2
    assert T % 512 == 0 and D % LANES == 0
    tm = 512

    mod3 = _adaln(c, w_ada[0], b_ada[0]).reshape(B, N_MOD, D)
    tab = _rope_table(positions)

    main = 8 * GROUP
    w = w_in[0]
    w_main = w[:, :main].astype(BF16)
    w_tail = jnp.concatenate([jnp.pad(w[:, main:main + IDX_DIM], ((0, 0), (0, LANES - IDX_DIM))),
                              jnp.pad(w[:, main + IDX_DIM:], ((0, 0), (0, LANES - IDX_HEADS)))], axis=1).astype(BF16)
    lnw = jnp.pad(idx_k_norm_w[0], (0, LANES - IDX_DIM)).reshape(1, LANES)
    lnb = jnp.pad(idx_k_norm_b[0], (0, LANES - IDX_DIM)).reshape(1, LANES)

    x2 = x.reshape(B * T, D)
    hq, lf, hk, hv, hg, aq, ak, avt, iq, ik, iwt, nrm = _in_proj(
        x2, mod3, norm_pre_mix, w_main, w_tail, hgrn_lower_bound, lnw, lnb, tab, T, tm)

    o_hg = _hgrn(hq, lf, hk, hv, hg, hgrn_out_norm, B, T, 256)
    o_att = _dsa(iq, iwt, ik, aq, ak, avt, nrm, B, T, 512, tm)

    gains = jnp.concatenate([norm_post_mix, norm_pre_ffn, norm_post_ffn], axis=0)
    out = _out_ffn(o_hg, o_att, x2, mod3, gains, w_out[0].astype(BF16), w_ffn_in[0].astype(BF16),
                   w_ffn_out[0].astype(BF16), T, tm)
    return out.reshape(B, T, D)
```

```python
import functools

import numpy as np
import jax
import jax.numpy as jnp
from jax import lax
from jax.experimental import pallas as pl
from jax.experimental.pallas import tpu as pltpu

F32 = jnp.float32
BF16 = jnp.bfloat16

HG_HEADS = 4
HG_DIM = 128
ATT_HEADS = 4
ATT_DIM = 128
IDX_HEADS = 8
IDX_DIM = 64
TOPK_MAX = 256
ROPE_THETA = 500000.0
ROPE_FRACTION = 4
N_MOD = 6
EPS = 1e-6

GROUP = 512
LANES = 128
VMEM_LIMIT = 56 * 1024 * 1024

HG_CHUNK = 128
HG_LEVELS = (64, 32, 16, 8, 4, 2, 1)

NEG_BIG = -1e30
DENOM_FLOOR = 2.0 ** -64
NORM_SLACK = 1.01
COARSE_PASSES = 10
FINE_PASSES = 6
Q_SCALE = ATT_DIM ** -0.5 * 1.4426950408889634
V_ROWS = ATT_DIM + 16


def _cparams(sem):
    return pltpu.CompilerParams(dimension_semantics=sem, vmem_limit_bytes=VMEM_LIMIT)


def _resident(shape, index_map):
    return pl.BlockSpec(shape, index_map, pipeline_mode=pl.Buffered(1))


def _split_bf16(a):
    hi = a.astype(BF16)
    lo = (a - hi.astype(F32)).astype(BF16)
    return hi, lo


def _dot(a, b):
    return jnp.dot(a, b, preferred_element_type=F32)


def _dot_nt(a, b):
    return lax.dot_general(a, b, (((1,), (1,)), ((), ())), preferred_element_type=F32)


def _silu(a):
    return a * jax.nn.sigmoid(a)


def _adaln_kernel(c_ref, w_ref, b_ref, o_ref):
    a = _silu(c_ref[...])
    a_hi, a_lo = _split_bf16(a)
    w_hi, w_lo = _split_bf16(w_ref[...])
    acc = _dot(a_hi, w_hi) + (_dot(a_hi, w_lo) + _dot(a_lo, w_hi))
    o_ref[...] = acc + b_ref[...]


def _adaln(c, w, b):
    B, D = c.shape
    N = w.shape[1]
    return pl.pallas_call(
        _adaln_kernel,
        grid=(N // D,),
        in_specs=[pl.BlockSpec((B, D), lambda j: (0, 0)),
                  pl.BlockSpec((D, D), lambda j: (0, j)),
                  pl.BlockSpec((1, D), lambda j: (0, j))],
        out_specs=pl.BlockSpec((B, D), lambda j: (0, j)),
        out_shape=jax.ShapeDtypeStruct((B, N), F32),
        compiler_params=_cparams(("arbitrary",)),
        name="adaln",
    )(c, w, b.reshape(1, N))


def _rope_table_kernel(pos_ref, fa_ref, fb_ref, o_ref):
    pos = pos_ref[0].astype(F32)
    ang_a = fa_ref[...] * pos
    ang_b = fb_ref[...] * pos
    ha = fa_ref.shape[0]
    hb = fb_ref.shape[0]
    o_ref[0, 0:ha, :] = jnp.cos(ang_a)
    o_ref[0, ha:2 * ha, :] = jnp.sin(ang_a)
    o_ref[0, 2 * ha:2 * ha + hb, :] = jnp.cos(ang_b)
    o_ref[0, 2 * ha + hb:2 * ha + 2 * hb, :] = jnp.sin(ang_b)
    o_ref[0, 2 * ha + 2 * hb:, :] = jnp.zeros((LANES - 2 * ha - 2 * hb, pos.shape[1]), F32)


ROPE_HALF_A = ATT_DIM // ROPE_FRACTION // 2
ROPE_HALF_B = IDX_DIM // ROPE_FRACTION // 2


def _rope_table(positions):
    B, T = positions.shape
    ha, hb = ROPE_HALF_A, ROPE_HALF_B
    fa = (ROPE_THETA ** (-jnp.arange(ha, dtype=F32) / ha)).reshape(ha, 1)
    fb = (ROPE_THETA ** (-jnp.arange(hb, dtype=F32) / hb)).reshape(hb, 1)
    return pl.pallas_call(
        _rope_table_kernel,
        grid=(B,),
        in_specs=[pl.BlockSpec((1, 1, T), lambda b: (b, 0, 0)),
                  pl.BlockSpec((ha, 1), lambda b: (0, 0)),
                  pl.BlockSpec((hb, 1), lambda b: (0, 0))],
        out_specs=pl.BlockSpec((1, LANES, T), lambda b: (b, 0, 0)),
        out_shape=jax.ShapeDtypeStruct((B, LANES, T), F32),
        compiler_params=_cparams(("arbitrary",)),
        name="rope_table",
    )(positions.reshape(B, 1, T), fa, fb)


def _rope_patterns(tab):
    ha, hb = ROPE_HALF_A, ROPE_HALF_B
    lane = lax.broadcasted_iota(jnp.int32, tab.shape, 1)
    shifted = lambda sh: pltpu.roll(tab, sh % LANES, axis=1)
    cos_a = jnp.where(lane < ha, tab, jnp.where(lane < 2 * ha, shifted(ha), 1.0))
    sin_a = jnp.where(lane < ha, -shifted(-ha), jnp.where(lane < 2 * ha, tab, 0.0))
    l64 = lane & (IDX_DIM - 1)
    first = lane < IDX_DIM
    cb0, sb0 = 2 * ha, 2 * ha + hb
    cos_b = jnp.where(l64 < hb, jnp.where(first, shifted(-cb0), shifted(IDX_DIM - cb0)),
                      jnp.where(l64 < 2 * hb, jnp.where(first, shifted(hb - cb0), shifted(IDX_DIM + hb - cb0)), 1.0))
    sin_b = jnp.where(l64 < hb, -jnp.where(first, shifted(-sb0), shifted(IDX_DIM - sb0)),
                      jnp.where(l64 < 2 * hb, jnp.where(first, shifted(hb - sb0), shifted(IDX_DIM + hb - sb0)), 0.0))
    return cos_a, sin_a, cos_b, sin_b


def _rope(xb, cos, sin, half, period):
    lane = lax.broadcasted_iota(jnp.int32, xb.shape, 1)
    fwd = pltpu.roll(xb, LANES - half, axis=1)
    bwd = pltpu.roll(xb, half, axis=1)
    partner = jnp.where((lane & (period - 1)) < half, fwd, bwd)
    return xb * cos + partner * sin


def _in_proj_kernel(x_ref, mod_ref, gain_ref, w_ref, wt_ref, lbp_ref, lnw_ref, lnb_ref, tab_ref,
                    hq_ref, lf_ref, hk_ref, hv_ref, hg_ref,
                    aq_ref, ak_ref, avt_ref, iq_ref, ik_ref, iwt_ref, nrm_ref):
    x = x_ref[...]
    ms = jnp.mean(x * x, axis=-1, keepdims=True)
    mod = mod_ref[0]
    h = x * lax.rsqrt(ms + EPS) * gain_ref[...]
    h = h * (1.0 + mod[1:2, :]) + mod[0:1, :]
    hb = h.astype(BF16)

    def proj(g, width=GROUP):
        return _dot(hb, w_ref[:, g * GROUP:g * GROUP + width])

    hq_ref[...] = (_silu(proj(0)) * (HG_DIM ** -0.5)).astype(BF16)
    a = lbp_ref[...]
    amax = jnp.max(a, axis=0, keepdims=True)
    e = jnp.exp(a - amax)
    lb = e[0:1, :] / jnp.sum(e, axis=0, keepdims=True)
    fr = proj(1)
    f = lb + (1.0 - lb) * jax.nn.sigmoid(fr)
    lf_ref[...] = jnp.log(f).astype(BF16)
    hk_ref[...] = ((1.0 - lb) * jax.nn.sigmoid(-fr)).astype(BF16)
    hv_ref[...] = proj(2).astype(BF16)
    hg_ref[...] = _silu(proj(3)).astype(BF16)

    cosa, sina, cosb, sinb = _rope_patterns(tab_ref[0].T)
    half_a = ROPE_HALF_A
    q = proj(4)
    k = proj(5)
    lane_a = lax.broadcasted_iota(jnp.int32, cosa.shape, 1)
    norms = jnp.zeros(cosa.shape, F32)
    for hh in range(ATT_HEADS):
        sl = slice(hh * ATT_DIM, (hh + 1) * ATT_DIM)
        qh = (_rope(q[:, sl], cosa, sina, half_a, ATT_DIM) * Q_SCALE).astype(BF16)
        kh = _rope(k[:, sl], cosa, sina, half_a, ATT_DIM).astype(BF16)
        aq_ref[:, sl] = qh
        ak_ref[:, sl] = kh
        for slot, a in ((hh, qh), (ATT_HEADS + hh, kh)):
            norms = jnp.where(lane_a == slot, jnp.sqrt(_dot(a * a, jnp.ones((ATT_DIM, LANES), BF16))), norms)
    nrm_ref[...] = norms.T[0:2 * ATT_HEADS, :]
    v = proj(6)
    for hh in range(ATT_HEADS):
        r0 = hh * V_ROWS
        avt_ref[0, 0, r0:r0 + ATT_DIM, :] = v[:, hh * ATT_DIM:(hh + 1) * ATT_DIM].T.astype(BF16)
        avt_ref[0, 0, r0 + ATT_DIM:r0 + V_ROWS, :] = jnp.ones((V_ROWS - ATT_DIM, v.shape[0]), BF16)

    half_b = ROPE_HALF_B
    qi = proj(7)
    for cc in range(GROUP // LANES):
        qr = _rope(qi[:, cc * LANES:(cc + 1) * LANES], cosb, sinb, half_b, IDX_DIM).astype(BF16)
        for hh in range(LANES // IDX_DIM):
            iq_ref[cc * (LANES // IDX_DIM) + hh] = qr[:, hh * IDX_DIM:(hh + 1) * IDX_DIM]
    kw = _dot(hb, wt_ref[...])
    ki = kw[:, 0:LANES]
    lane = lax.broadcasted_iota(jnp.int32, ki.shape, 1)
    real = lane < IDX_DIM
    mu = jnp.sum(ki, axis=-1, keepdims=True) * (1.0 / IDX_DIM)
    d = jnp.where(real, ki - mu, 0.0)
    var = jnp.sum(d * d, axis=-1, keepdims=True) * (1.0 / IDX_DIM)
    kn = d * lax.rsqrt(var + EPS) * lnw_ref[...] + lnb_ref[...]
    kn = _rope(kn, cosb, sinb, half_b, IDX_DIM)
    ik_ref[...] = kn[:, 0:IDX_DIM].astype(BF16)
    iwt_ref[...] = kw[:, LANES:2 * LANES].T[0:IDX_HEADS, :] * (IDX_HEADS ** -0.5 * IDX_DIM ** -0.5)


def _in_proj(x2, mod3, gain, w_main, w_tail, lbp, lnw, lnb, tab, T, tm):
    M, D = x2.shape
    nt = T // tm
    row = lambda i: (i, 0)
    const = lambda i: (0, 0)
    bf16o = jax.ShapeDtypeStruct((M, GROUP), BF16)
    grp_spec = pl.BlockSpec((tm, GROUP), row)
    return pl.pallas_call(
        _in_proj_kernel,
        grid=(M // tm,),
        in_specs=[pl.BlockSpec((tm, D), row),
                  pl.BlockSpec((1, N_MOD, D), lambda i: (i // nt, 0, 0)),
                  pl.BlockSpec((1, D), const),
                  _resident(w_main.shape, const),
                  _resident(w_tail.shape, const),
                  pl.BlockSpec(lbp.shape, const),
                  pl.BlockSpec((1, LANES), const),
                  pl.BlockSpec((1, LANES), const),
                  pl.BlockSpec((1, LANES, tm), lambda i: (i // nt, 0, i % nt))],
        out_specs=[grp_spec] * 7 + [pl.BlockSpec((1, 1, ATT_HEADS * V_ROWS, tm), lambda i: (i // nt, i % nt, 0, 0)),
                                    pl.BlockSpec((IDX_HEADS, tm, IDX_DIM), lambda i: (0, i, 0)),
                                    pl.BlockSpec((tm, IDX_DIM), row),
                                    pl.BlockSpec((IDX_HEADS, tm), lambda i: (0, i)),
                                    pl.BlockSpec((2 * ATT_HEADS, tm), lambda i: (0, i))],
        out_shape=[bf16o] * 7 + [jax.ShapeDtypeStruct((M // T, nt, ATT_HEADS * V_ROWS, tm), BF16),
                                 jax.ShapeDtypeStruct((IDX_HEADS, M, IDX_DIM), BF16),
                                 jax.ShapeDtypeStruct((M, IDX_DIM), BF16),
                                 jax.ShapeDtypeStruct((IDX_HEADS, M), F32),
                                 jax.ShapeDtypeStruct((2 * ATT_HEADS, M), F32)],
        compiler_params=_cparams(("parallel",)),
        name="in_proj",
    )(x2, mod3, gain, w_main, w_tail, lbp, lnw, lnb, tab)


def _hgrn_consts():
    C = HG_CHUNK
    t = np.arange(C)
    tri = (t[None, :] <= t[:, None]).astype(np.float32)
    blocks = [tri]
    for m in HG_LEVELS:
        split = (t // (2 * m)) * (2 * m) + m - 1
        blocks.append(tri - tri[split])
    mall = np.concatenate(blocks, axis=0)
    x = t[:, None] ^ t[None, :]
    lvl = np.full((C, C), len(HG_LEVELS) + 1, np.int32)
    for li, m in enumerate(HG_LEVELS):
        lvl[(t[:, None] > t[None, :]) & (x >= m) & (x < 2 * m)] = li
    lvl[t[:, None] == t[None, :]] = len(HG_LEVELS)
    return jnp.asarray(mall, BF16), jnp.asarray(lvl)


def _hgrn_kernel(hq_ref, lf_ref, hk_ref, hv_ref, hg_ref, mall_ref, lvl_ref, onorm_ref, o_ref, st_ref):
    C = HG_CHUNK
    nl = len(HG_LEVELS)

    @pl.when(pl.program_id(1) == 0)
    def _():
        st_ref[...] = jnp.zeros_like(st_ref)

    lvl = lvl_ref[...]
    row = lax.broadcasted_iota(jnp.int32, (C, HG_DIM), 0)
    mall = mall_ref[...]

    for ci in range(hq_ref.shape[0] // C):
        rs = slice(ci * C, (ci + 1) * C)
        for h in range(HG_HEADS):
            cs = slice(h * HG_DIM, (h + 1) * HG_DIM)
            q, k, v = hq_ref[rs, cs], hk_ref[rs, cs], hv_ref[rs, cs]
            dall = _dot(mall, lf_ref[rs, cs])
            g = dall[0:C]

            a = jnp.zeros((C, C), F32)
            for li, m in enumerate(HG_LEVELS):
                dm = dall[(li + 1) * C:(li + 2) * C]
                xm = (jnp.where((row & m) != 0, q, k) * jnp.exp(-jnp.abs(dm))).astype(BF16)
                a = jnp.where(lvl == li, _dot_nt(xm, xm), a)
            a = jnp.where(lvl == nl, _dot_nt(q, k), a)

            st = st_ref[h]
            o = _dot_nt((q * jnp.exp(g)).astype(BF16), st.astype(BF16)) + _dot(a.astype(BF16), v)
            g_last = g[C - 1:C, :]
            kd = (k * jnp.exp(g_last - g)).astype(BF16)
            st_ref[h] = jnp.exp(g_last) * st + _dot(v.astype(F32).T.astype(BF16), kd)

            o = o * lax.rsqrt(jnp.mean(o * o, axis=-1, keepdims=True) + EPS)
            o_ref[rs, cs] = (o * onorm_ref[:, cs] * hg_ref[rs, cs]).astype(BF16)


def _hgrn(hq, lf, hk, hv, hg, onorm, B, T, ct):
    M = hq.shape[0]
    nct = T // ct
    mall, lvl = _hgrn_consts()
    blk = pl.BlockSpec((ct, GROUP), lambda b, c: (b * nct + c, 0))
    const = lambda b, c: (0, 0)
    return pl.pallas_call(
        _hgrn_kernel,
        grid=(B, nct),
        in_specs=[blk] * 5 + [pl.BlockSpec(mall.shape, const), pl.BlockSpec(lvl.shape, const),
                              pl.BlockSpec((1, GROUP), const)],
        out_specs=blk,
        out_shape=jax.ShapeDtypeStruct((M, GROUP), BF16),
        scratch_shapes=[pltpu.VMEM((HG_HEADS, HG_DIM, HG_DIM), F32)],
        compiler_params=_cparams(("parallel", "arbitrary")),
        name="hgrn2",
    )(hq, lf, hk, hv, hg, mall, lvl, onorm)


def _fold8(a, op, rows=8):
    chains = [None] * 4
    for r in range(a.shape[0] // rows):
        part = a[r * rows:(r + 1) * rows, :]
        c = r % len(chains)
        chains[c] = part if chains[c] is None else op(chains[c], part)
    return op(op(chains[0], chains[1]), op(chains[2], chains[3]))


def _floor_bf16(a):
    r = a.astype(BF16).astype(F32)
    below = (r - jnp.abs(r) * (5.0 / 1024.0)).astype(BF16)
    return jnp.where(r > a, below.astype(F32), r).astype(BF16)


def _dsa_kernel(iq_ref, iwt_ref, ik_ref, aq_ref, ak_ref, avt_ref, qnt_ref, knt_ref, o_ref,
                sc_ref, scb_ref, mm_ref, *head_refs, qb, kt, topk, seq):
    acc_refs, m_refs = head_refs[:ATT_HEADS], head_refs[ATT_HEADS:]
    i = pl.program_id(1)
    q0 = i * qb
    nfull = q0 // kt
    nkt = nfull + 1
    qpos = q0 + lax.broadcasted_iota(jnp.int32, (1, qb), 1)
    kk = jnp.minimum(qpos + 1, topk).astype(F32)

    mm_ref[0:8, :] = jnp.full((8, qb), jnp.inf, F32)
    mm_ref[8:16, :] = jnp.full((8, qb), -jnp.inf, F32)
    mm_ref[16:32, :] = jnp.zeros((16, qb), F32)

    def score_tile(j, diag):
        ks = pl.multiple_of(j * kt, kt)
        ki = ik_ref[pl.ds(ks, kt), :]
        s = None
        for h in range(IDX_HEADS):
            t = jnp.maximum(_dot_nt(ki, iq_ref[h]), 0.0) * iwt_ref[h:h + 1, :]
            s = t if s is None else s + t
        if diag:
            causal = ks + lax.broadcasted_iota(jnp.int32, (kt, qb), 0) <= qpos
            s_hi = jnp.where(causal, s, -jnp.inf)
            s_lo = jnp.where(causal, s, jnp.inf)
        else:
            s_hi = s_lo = s
        sc_ref[pl.ds(ks, kt), :] = s_hi
        scb_ref[pl.ds(ks, kt), :] = _floor_bf16(s_hi)
        mm_ref[0:8, :] = jnp.minimum(mm_ref[0:8, :], _fold8(s_lo, jnp.minimum))
        mm_ref[8:16, :] = jnp.maximum(mm_ref[8:16, :], _fold8(s_hi, jnp.maximum))
        mm_ref[16:24, :] += _fold8(jnp.where(s_hi >= 0.0, 1.0, 0.0), jnp.add)
        mm_ref[24:32, :] += _fold8(jnp.where(s_hi > 0.0, 1.0, 0.0), jnp.add)

    def full_tile(j, c):
        score_tile(j, False)
        return c

    lax.fori_loop(0, nfull, full_tile, 0)
    score_tile(nfull, True)
    mn = jnp.min(mm_ref[0:8, :], axis=0, keepdims=True)
    mx = jnp.max(mm_ref[8:16, :], axis=0, keepdims=True)
    c_nonneg = jnp.sum(mm_ref[16:24, :], axis=0, keepdims=True)
    c_pos = jnp.sum(mm_ref[24:32, :], axis=0, keepdims=True)

    def count_keys(hit_fn):
        def body(j, acc):
            ks = pl.multiple_of(j * kt, kt)
            return acc + _fold8(hit_fn(sc_ref[pl.ds(ks, kt), :], ks), jnp.add)
        acc = lax.fori_loop(0, nkt, body, jnp.zeros((8, qb), F32))
        return jnp.sum(acc, axis=0, keepdims=True)

    def count_ge(th):
        return count_keys(lambda s, ks: jnp.where(s >= th, 1.0, 0.0))

    above = c_pos >= kk
    below = c_nonneg < kk
    live = above | below
    zero = jnp.zeros((1, qb), F32)
    lo0 = jnp.where(below, mn, zero)
    cnt_lo0 = jnp.where(below, (qpos + 1).astype(F32), c_nonneg)
    hi0 = jnp.where(above, mx + jnp.abs(mx) * 1e-6 + 1e-30, zero)
    cnt_hi0 = jnp.where(above, zero, jnp.where(below, c_nonneg, c_pos))

    def midpoint(lo, hi):
        return lo + 0.5 * (hi - lo)

    def unresolved(lo, hi, cnt_lo):
        mid = midpoint(lo, hi)
        return jnp.max((live & (cnt_lo != kk) & (mid > lo) & (mid < hi)).astype(jnp.int32)) > 0

    assert kt // 16 <= 256

    def count_ge_bf16(th):
        def body(j, acc):
            ks = pl.multiple_of(j * kt, kt)
            hit = jnp.where(scb_ref[pl.ds(ks, kt), :] >= th, jnp.ones((), BF16), jnp.zeros((), BF16))
            part = _fold8(hit, jnp.add, rows=16).astype(F32)
            return acc + part[0:8, :] + part[8:16, :]
        acc = lax.fori_loop(0, nkt, body, jnp.zeros((8, qb), F32))
        return jnp.sum(acc, axis=0, keepdims=True)

    def coarse_body(_, c):
        lo, hi, cnt_lo, cnt_hi = c
        th = midpoint(lo, hi).astype(BF16)
        mid = th.astype(F32)
        inside = live & (mid > lo) & (mid < hi)
        cnt = count_ge_bf16(th)
        up = inside & (cnt >= kk)
        dn = inside & (cnt < kk)
        return (jnp.where(up, mid, lo), jnp.where(dn, mid, hi),
                jnp.where(up, cnt, cnt_lo), jnp.where(dn, cnt, cnt_hi))

    lo1, hi1, cnt_lo1, cnt_hi1 = lax.fori_loop(0, COARSE_PASSES, coarse_body, (lo0, hi0, cnt_lo0, cnt_hi0))

    def fine_pass(c):
        lo, hi, cnt_lo, cnt_hi = c
        mid = midpoint(lo, hi)
        cnt = count_ge(mid)
        up = live & (cnt >= kk)
        dn = live & (cnt < kk)
        return (jnp.where(up, mid, lo), jnp.where(dn, mid, hi), jnp.where(up, cnt, cnt_lo), jnp.where(dn, cnt, cnt_hi))

    c2 = lax.fori_loop(0, FINE_PASSES, lambda _, c: fine_pass(c), (lo1, hi1, cnt_lo1, cnt_hi1))

    def bis_cond(c):
        it, (lo, hi, cnt_lo, _) = c
        return (it < 320) & unresolved(lo, hi, cnt_lo)

    _, (lo, hi, cnt_lo, cnt_hi) = lax.while_loop(bis_cond, lambda c: (c[0] + 1, fine_pass(c[1])), (jnp.int32(0), c2))

    tied = cnt_lo != kk
    need = kk - cnt_hi

    def resolve_ties(cs):
        lo_c, need_c, tied_c = lo[:, cs], need[:, cs], tied[:, cs]

        def tie_keys(j):
            ks = pl.multiple_of(j * kt, kt)
            rows = pl.ds(ks, kt)
            return rows, sc_ref[rows, cs], ks + lax.broadcasted_iota(jnp.int32, (kt, LANES), 0)

        def count_tie(jm):
            def body(j, acc):
                _, s, kpos = tie_keys(j)
                return acc + _fold8(jnp.where((s == lo_c) & (kpos <= jm), 1.0, 0.0), jnp.add)
            acc = lax.fori_loop(0, nkt, body, jnp.zeros((8, LANES), F32))
            return jnp.sum(acc, axis=0, keepdims=True)

        def tie_body(_, c):
            jlo, jhi = c
            jm = (jlo + jhi) // 2
            ok = count_tie(jm) >= need_c
            return jnp.where(ok, jlo, jm), jnp.where(ok, jm, jhi)

        steps = int(np.ceil(np.log2(seq))) + 1
        _, jhi = lax.fori_loop(0, steps, tie_body,
                               (jnp.full((1, LANES), -1, jnp.int32), jnp.full((1, LANES), seq - 1, jnp.int32)))
        jmax = jnp.where(tied_c, jhi, seq)

        def strike(j, c):
            rows, s, kpos = tie_keys(j)
            sc_ref[rows, cs] = jnp.where((s == lo_c) & (kpos > jmax), -jnp.inf, s)
            return c

        lax.fori_loop(0, nkt, strike, 0)

    for c in range(qb // LANES):
        cs = slice(c * LANES, (c + 1) * LANES)
        pl.when(jnp.max(tied[:, cs].astype(jnp.int32)) > 0)(functools.partial(resolve_ties, cs))

    def logits(j, h, bias):
        ks = pl.multiple_of(j * kt, kt)
        sl = slice(h * ATT_DIM, (h + 1) * ATT_DIM)
        return _dot_nt(ak_ref[pl.ds(ks, kt), sl], aq_ref[:, sl]) + bias

    def select_bias(j):
        ks = pl.multiple_of(j * kt, kt)
        return jnp.where(sc_ref[pl.ds(ks, kt), :] >= lo, 0.0, NEG_BIG)

    def values(j, h):
        return avt_ref[0, j, h * V_ROWS:(h + 1) * V_ROWS, :]

    kmax = jnp.max(knt_ref[...], axis=1, keepdims=True)
    shift = [qnt_ref[h:h + 1, :] * kmax[ATT_HEADS + h:ATT_HEADS + h + 1, :] * NORM_SLACK for h in range(ATT_HEADS)]
    for h in range(ATT_HEADS):
        acc_refs[h][...] = jnp.zeros_like(acc_refs[h])

    def attn_tile_fast(j, c):
        bias = select_bias(j)
        for h in range(ATT_HEADS):
            p = jnp.exp2(logits(j, h, bias) - shift[h]).astype(BF16)
            acc_refs[h][...] += _dot(values(j, h), p)
        return c

    lax.fori_loop(0, nkt, attn_tile_fast, 0)
    denom_min = functools.reduce(jnp.minimum, [acc_refs[h][ATT_DIM:ATT_DIM + 1, :] for h in range(ATT_HEADS)])

    @pl.when(jnp.min(denom_min) < DENOM_FLOOR)
    def _():
        for h in range(ATT_HEADS):
            acc_refs[h][...] = jnp.zeros_like(acc_refs[h])
            m_refs[h][...] = jnp.full(m_refs[h].shape, NEG_BIG, F32)

        def attn_tile(j, c):
            bias = select_bias(j)
            for h in range(ATT_HEADS):
                acc_h, m_h = acc_refs[h], m_refs[h]
                lg = logits(j, h, bias)
                m_old = m_h[0:1, :]
                m_new = jnp.maximum(m_old, jnp.max(_fold8(lg, jnp.maximum), axis=0, keepdims=True))
                p = jnp.exp2(lg - m_new).astype(BF16)
                acc_h[...] = jnp.exp2(m_old - m_new) * acc_h[...] + _dot(values(j, h), p)
                m_h[0:1, :] = m_new
            return c

        lax.fori_loop(0, nkt, attn_tile, 0)

    for h in range(ATT_HEADS):
        o_t = acc_refs[h][0:ATT_DIM, :] / acc_refs[h][ATT_DIM:ATT_DIM + 1, :]
        o_ref[:, h * ATT_DIM:(h + 1) * ATT_DIM] = o_t.T.astype(BF16)


def _dsa(iq, iwt, ik, aq, ak, avt, nrm, B, T, qb, kt):
    M = ak.shape[0]
    nq = T // qb
    assert kt % qb == 0 and T % kt == 0 and avt.shape[1:] == (T // kt, ATT_HEADS * V_ROWS, kt)
    topk = min(TOPK_MAX, T // 4)
    qrow = lambda b, i: (b * nq + i, 0)
    brow = lambda b, i: (b, 0)
    kern = functools.partial(_dsa_kernel, qb=qb, kt=kt, topk=topk, seq=T)
    return pl.pallas_call(
        kern,
        grid=(B, nq),
        in_specs=[pl.BlockSpec((IDX_HEADS, qb, IDX_DIM), lambda b, i: (0, b * nq + i, 0)),
                  pl.BlockSpec((IDX_HEADS, qb), lambda b, i: (0, b * nq + i)),
                  _resident((T, IDX_DIM), brow),
                  pl.BlockSpec((qb, GROUP), qrow),
                  _resident((T, GROUP), brow),
                  _resident((1,) + avt.shape[1:], lambda b, i: (b, 0, 0, 0)),
                  pl.BlockSpec((2 * ATT_HEADS, qb), lambda b, i: (0, b * nq + i)),
                  _resident((2 * ATT_HEADS, T), lambda b, i: (0, b))],
        out_specs=pl.BlockSpec((qb, GROUP), qrow),
        out_shape=jax.ShapeDtypeStruct((M, GROUP), BF16),
        scratch_shapes=[pltpu.VMEM((T, qb), F32),
                        pltpu.VMEM((T, qb), BF16),
                        pltpu.VMEM((32, qb), F32),
                        ] + [pltpu.VMEM((V_ROWS, qb), F32)] * ATT_HEADS
                        + [pltpu.VMEM((8, qb), F32)] * ATT_HEADS,
        compiler_params=_cparams(("parallel", "arbitrary")),
        name="dsa",
    )(iq, iwt, ik, aq, ak, avt, nrm, nrm)


def _rms(a, gain):
    return a * lax.rsqrt(jnp.mean(a * a, axis=-1, keepdims=True) + EPS) * gain


def _out_ffn_kernel(ohg_ref, oatt_ref, x_ref, mod_ref, gains_ref, wo_ref, wi_ref, wf_ref, o_ref, acc_ref,
                    *, d_ff, fc):
    mod = mod_ref[0]
    gains = gains_ref[...]
    hw = ohg_ref.shape[1]
    y = _dot(ohg_ref[...], wo_ref[0:hw, :]) + _dot(oatt_ref[...], wo_ref[hw:, :])
    x1 = x_ref[...] + mod[2:3, :] * _rms(y, gains[0:1, :])
    h2 = (_rms(x1, gains[1:2, :]) * (1.0 + mod[4:5, :]) + mod[3:4, :]).astype(BF16)
    for c in range(d_ff // fc):
        gate = _dot(h2, wi_ref[:, c * fc:(c + 1) * fc])
        up = _dot(h2, wi_ref[:, d_ff + c * fc:d_ff + (c + 1) * fc])
        part = _dot((_silu(gate) * up).astype(BF16), wf_ref[c * fc:(c + 1) * fc, :])
        if c == 0:
            acc_ref[...] = part
        else:
            acc_ref[...] += part
    o_ref[...] = x1 + mod[5:6, :] * _rms(acc_ref[...], gains[2:3, :])


def _out_ffn(ohg, oatt, x2, mod3, gains, wo, wi, wf, T, tm):
    M, D = x2.shape
    nt = T // tm
    d_ff = wf.shape[0]
    row = lambda i: (i, 0)
    const = lambda i: (0, 0)
    kern = functools.partial(_out_ffn_kernel, d_ff=d_ff, fc=256)
    return pl.pallas_call(
        kern,
        grid=(M // tm,),
        in_specs=[pl.BlockSpec((tm, GROUP), row),
                  pl.BlockSpec((tm, GROUP), row),
                  pl.BlockSpec((tm, D), row),
                  pl.BlockSpec((1, N_MOD, D), lambda i: (i // nt, 0, 0)),
                  pl.BlockSpec((3, D), const),
                  _resident(wo.shape, const),
                  _resident(wi.shape, const),
                  _resident(wf.shape, const)],
        out_specs=pl.BlockSpec((tm, D), row),
        out_shape=jax.ShapeDtypeStruct((M, D), F32),
        scratch_shapes=[pltpu.VMEM((tm, D), F32)],
        compiler_params=_cparams(("parallel",)),
        name="out_ffn",
    )(ohg, oatt, x2, mod3, gains, wo, wi, wf)


def kernel(x, c, positions, w_ada, b_ada, norm_pre_mix, norm_post_mix, norm_pre_ffn, norm_post_ffn, w_in,
           hgrn_lower_bound, hgrn_out_norm, idx_k_norm_w, idx_k_norm_b, w_out, w_ffn_in, w_ffn_out):
    B, T, D = x.shape
    depth = w_ada.shape[0]
    assert depth == 1 and hgrn_lower_bound.shape[0] == 2
    assert T % 512 == 0 and D % LANES == 0
    tm = 512

    mod3 = _adaln(c, w_ada[0], b_ada[0]).reshape(B, N_MOD, D)
    tab = _rope_table(positions)

    main = 8 * GROUP
    w = w_in[0]
    w_main = w[:, :main].astype(BF16)
    w_tail = jnp.concatenate([jnp.pad(w[:, main:main + IDX_DIM], ((0, 0), (0, LANES - IDX_DIM))),
                              jnp.pad(w[:, main + IDX_DIM:], ((0, 0), (0, LANES - IDX_HEADS)))], axis=1).astype(BF16)
    lnw = jnp.pad(idx_k_norm_w[0], (0, LANES - IDX_DIM)).reshape(1, LANES)
    lnb = jnp.pad(idx_k_norm_b[0], (0, LANES - IDX_DIM)).reshape(1, LANES)

    x2 = x.reshape(B * T, D)
    hq, lf, hk, hv, hg, aq, ak, avt, iq, ik, iwt, nrm = _in_proj(
        x2, mod3, norm_pre_mix, w_main, w_tail, hgrn_lower_bound, lnw, lnb, tab, T, tm)

    o_hg = _hgrn(hq, lf, hk, hv, hg, hgrn_out_norm, B, T, 256)
    o_att = _dsa(iq, iwt, ik, aq, ak, avt, nrm, B, T, 512, tm)

    gains = jnp.concatenate([norm_post_mix, norm_pre_ffn, norm_post_ffn], axis=0)
    out = _out_ffn(o_hg, o_att, x2, mod3, gains, w_out[0].astype(BF16), w_ffn_in[0].astype(BF16),
                   w_ffn_out[0].astype(BF16), T, tm)
    return out.reshape(B, T, D)
```

```python
import functools

import numpy as np
import jax
import jax.numpy as jnp
from jax import lax
from jax.experimental import pallas as pl
from jax.experimental.pallas import tpu as pltpu

F32 = jnp.float32
BF16 = jnp.bfloat16

HG_HEADS = 4
HG_DIM = 128
ATT_HEADS = 4
ATT_DIM = 128
IDX_HEADS = 8
IDX_DIM = 64
TOPK_MAX = 256
ROPE_THETA = 500000.0
ROPE_FRACTION = 4
N_MOD = 6
EPS = 1e-6

GROUP = 512
LANES = 128
VMEM_LIMIT = 56 * 1024 * 1024

HG_CHUNK = 128
HG_LEVELS = (64, 32, 16, 8, 4, 2, 1)

NEG_BIG = -1e30
DENOM_FLOOR = 2.0 ** -64
NORM_SLACK = 1.01
COARSE_PASSES = 10
FINE_PASSES = 6
Q_SCALE = ATT_DIM ** -0.5 * 1.4426950408889634
V_ROWS = ATT_DIM + 16


def _cparams(sem):
    return pltpu.CompilerParams(dimension_semantics=sem, vmem_limit_bytes=VMEM_LIMIT)


def _resident(shape, index_map):
    return pl.BlockSpec(shape, index_map, pipeline_mode=pl.Buffered(1))


def _split_bf16(a):
    hi = a.astype(BF16)
    lo = (a - hi.astype(F32)).astype(BF16)
    return hi, lo


def _dot(a, b):
    return jnp.dot(a, b, preferred_element_type=F32)


def _dot_nt(a, b):
    return lax.dot_general(a, b, (((1,), (1,)), ((), ())), preferred_element_type=F32)


def _silu(a):
    return a * jax.nn.sigmoid(a)


def _adaln_kernel(c_ref, w_ref, b_ref, o_ref):
    a = _silu(c_ref[...])
    a_hi, a_lo = _split_bf16(a)
    w_hi, w_lo = _split_bf16(w_ref[...])
    acc = _dot(a_hi, w_hi) + (_dot(a_hi, w_lo) + _dot(a_lo, w_hi))
    o_ref[...] = acc + b_ref[...]


def _adaln(c, w, b):
    B, D = c.shape
    N = w.shape[1]
    return pl.pallas_call(
        _adaln_kernel,
        grid=(N // D,),
        in_specs=[pl.BlockSpec((B, D), lambda j: (0, 0)),
                  pl.BlockSpec((D, D), lambda j: (0, j)),
                  pl.BlockSpec((1, D), lambda j: (0, j))],
        out_specs=pl.BlockSpec((B, D), lambda j: (0, j)),
        out_shape=jax.ShapeDtypeStruct((B, N), F32),
        compiler_params=_cparams(("arbitrary",)),
        name="adaln",
    )(c, w, b.reshape(1, N))


def _rope_table_kernel(pos_ref, fa_ref, fb_ref, o_ref):
    pos = pos_ref[0].astype(F32)
    ang_a = fa_ref[...] * pos
    ang_b = fb_ref[...] * pos
    ha = fa_ref.shape[0]
    hb = fb_ref.shape[0]
    o_ref[0, 0:ha, :] = jnp.cos(ang_a)
    o_ref[0, ha:2 * ha, :] = jnp.sin(ang_a)
    o_ref[0, 2 * ha:2 * ha + hb, :] = jnp.cos(ang_b)
    o_ref[0, 2 * ha + hb:2 * ha + 2 * hb, :] = jnp.sin(ang_b)
    o_ref[0, 2 * ha + 2 * hb:, :] = jnp.zeros((LANES - 2 * ha - 2 * hb, pos.shape[1]), F32)


ROPE_HALF_A = ATT_DIM // ROPE_FRACTION // 2
ROPE_HALF_B = IDX_DIM // ROPE_FRACTION // 2


def _rope_table(positions):
    B, T = positions.shape
    ha, hb = ROPE_HALF_A, ROPE_HALF_B
    fa = (ROPE_THETA ** (-jnp.arange(ha, dtype=F32) / ha)).reshape(ha, 1)
    fb = (ROPE_THETA ** (-jnp.arange(hb, dtype=F32) / hb)).reshape(hb, 1)
    return pl.pallas_call(
        _rope_table_kernel,
        grid=(B,),
        in_specs=[pl.BlockSpec((1, 1, T), lambda b: (b, 0, 0)),
                  pl.BlockSpec((ha, 1), lambda b: (0, 0)),
                  pl.BlockSpec((hb, 1), lambda b: (0, 0))],
        out_specs=pl.BlockSpec((1, LANES, T), lambda b: (b, 0, 0)),
        out_shape=jax.ShapeDtypeStruct((B, LANES, T), F32),
        compiler_params=_cparams(("arbitrary",)),
        name="rope_table",
    )(positions.reshape(B, 1, T), fa, fb)


def _rope_patterns(tab):
    ha, hb = ROPE_HALF_A, ROPE_HALF_B
    lane = lax.broadcasted_iota(jnp.int32, tab.shape, 1)
    shifted = lambda sh: pltpu.roll(tab, sh % LANES, axis=1)
    cos_a = jnp.where(lane < ha, tab, jnp.where(lane < 2 * ha, shifted(ha), 1.0))
    sin_a = jnp.where(lane < ha, -shifted(-ha), jnp.where(lane < 2 * ha, tab, 0.0))
    l64 = lane & (IDX_DIM - 1)
    first = lane < IDX_DIM
    cb0, sb0 = 2 * ha, 2 * ha + hb
    cos_b = jnp.where(l64 < hb, jnp.where(first, shifted(-cb0), shifted(IDX_DIM - cb0)),
                      jnp.where(l64 < 2 * hb, jnp.where(first, shifted(hb - cb0), shifted(IDX_DIM + hb - cb0)), 1.0))
    sin_b = jnp.where(l64 < hb, -jnp.where(first, shifted(-sb0), shifted(IDX_DIM - sb0)),
                      jnp.where(l64 < 2 * hb, jnp.where(first, shifted(hb - sb0), shifted(IDX_DIM + hb - sb0)), 0.0))
    return cos_a, sin_a, cos_b, sin_b


def _rope(xb, cos, sin, half, period):
    lane = lax.broadcasted_iota(jnp.int32, xb.shape, 1)
    fwd = pltpu.roll(xb, LANES - half, axis=1)
    bwd = pltpu.roll(xb, half, axis=1)
    partner = jnp.where((lane & (period - 1)) < half, fwd, bwd)
    return xb * cos + partner * sin


def _in_proj_kernel(x_ref, mod_ref, gain_ref, w_ref, wt_ref, lbp_ref, lnw_ref, lnb_ref, tab_ref,
                    hq_ref, lf_ref, hk_ref, hv_ref, hg_ref,
                    aq_ref, ak_ref, avt_ref, iq_ref, ik_ref, iwt_ref, nrm_ref):
    x = x_ref[...]
    ms = jnp.mean(x * x, axis=-1, keepdims=True)
    mod = mod_ref[0]
    h = x * lax.rsqrt(ms + EPS) * gain_ref[...]
    h = h * (1.0 + mod[1:2, :]) + mod[0:1, :]
    hb = h.astype(BF16)

    def proj(g, width=GROUP):
        return _dot(hb, w_ref[:, g * GROUP:g * GROUP + width])

    hq_ref[...] = (_silu(proj(0)) * (HG_DIM ** -0.5)).astype(BF16)
    a = lbp_ref[...]
    amax = jnp.max(a, axis=0, keepdims=True)
    e = jnp.exp(a - amax)
    lb = e[0:1, :] / jnp.sum(e, axis=0, keepdims=True)
    fr = proj(1)
    f = lb + (1.0 - lb) * jax.nn.sigmoid(fr)
    lf_ref[...] = jnp.log2(f).astype(BF16)
    hk_ref[...] = ((1.0 - lb) * jax.nn.sigmoid(-fr)).astype(BF16)
    hv_ref[...] = proj(2).astype(BF16)
    hg_ref[...] = _silu(proj(3)).astype(BF16)

    cosa, sina, cosb, sinb = _rope_patterns(tab_ref[0].T)
    half_a = ROPE_HALF_A
    q = proj(4)
    k = proj(5)
    lane_a = lax.broadcasted_iota(jnp.int32, cosa.shape, 1)
    norms = jnp.zeros(cosa.shape, F32)
    for hh in range(ATT_HEADS):
        sl = slice(hh * ATT_DIM, (hh + 1) * ATT_DIM)
        qh = (_rope(q[:, sl], cosa, sina, half_a, ATT_DIM) * Q_SCALE).astype(BF16)
        kh = _rope(k[:, sl], cosa, sina, half_a, ATT_DIM).astype(BF16)
        aq_ref[:, sl] = qh
        ak_ref[:, sl] = kh
        for slot, a in ((hh, qh), (ATT_HEADS + hh, kh)):
            norms = jnp.where(lane_a == slot, jnp.sqrt(_dot(a * a, jnp.ones((ATT_DIM, LANES), BF16))), norms)
    nrm_ref[...] = norms.T[0:2 * ATT_HEADS, :]
    v = proj(6)
    for hh in range(ATT_HEADS):
        r0 = hh * V_ROWS
        avt_ref[0, 0, r0:r0 + ATT_DIM, :] = v[:, hh * ATT_DIM:(hh + 1) * ATT_DIM].T.astype(BF16)
        avt_ref[0, 0, r0 + ATT_DIM:r0 + V_ROWS, :] = jnp.ones((V_ROWS - ATT_DIM, v.shape[0]), BF16)

    half_b = ROPE_HALF_B
    qi = proj(7)
    for cc in range(GROUP // LANES):
        qr = _rope(qi[:, cc * LANES:(cc + 1) * LANES], cosb, sinb, half_b, IDX_DIM).astype(BF16)
        for hh in range(LANES // IDX_DIM):
            iq_ref[cc * (LANES // IDX_DIM) + hh] = qr[:, hh * IDX_DIM:(hh + 1) * IDX_DIM]
    kw = _dot(hb, wt_ref[...])
    ki = kw[:, 0:LANES]
    lane = lax.broadcasted_iota(jnp.int32, ki.shape, 1)
    real = lane < IDX_DIM
    mu = jnp.sum(ki, axis=-1, keepdims=True) * (1.0 / IDX_DIM)
    d = jnp.where(real, ki - mu, 0.0)
    var = jnp.sum(d * d, axis=-1, keepdims=True) * (1.0 / IDX_DIM)
    kn = d * lax.rsqrt(var + EPS) * lnw_ref[...] + lnb_ref[...]
    kn = _rope(kn, cosb, sinb, half_b, IDX_DIM)
    ik_ref[...] = kn[:, 0:IDX_DIM].astype(BF16)
    iwt_ref[...] = kw[:, LANES:2 * LANES].T[0:IDX_HEADS, :] * (IDX_HEADS ** -0.5 * IDX_DIM ** -0.5)


def _in_proj(x2, mod3, gain, w_main, w_tail, lbp, lnw, lnb, tab, T, tm):
    M, D = x2.shape
    nt = T // tm
    row = lambda i: (i, 0)
    const = lambda i: (0, 0)
    bf16o = jax.ShapeDtypeStruct((M, GROUP), BF16)
    grp_spec = pl.BlockSpec((tm, GROUP), row)
    return pl.pallas_call(
        _in_proj_kernel,
        grid=(M // tm,),
        in_specs=[pl.BlockSpec((tm, D), row),
                  pl.BlockSpec((1, N_MOD, D), lambda i: (i // nt, 0, 0)),
                  pl.BlockSpec((1, D), const),
                  _resident(w_main.shape, const),
                  _resident(w_tail.shape, const),
                  pl.BlockSpec(lbp.shape, const),
                  pl.BlockSpec((1, LANES), const),
                  pl.BlockSpec((1, LANES), const),
                  pl.BlockSpec((1, LANES, tm), lambda i: (i // nt, 0, i % nt))],
        out_specs=[grp_spec] * 7 + [pl.BlockSpec((1, 1, ATT_HEADS * V_ROWS, tm), lambda i: (i // nt, i % nt, 0, 0)),
                                    pl.BlockSpec((IDX_HEADS, tm, IDX_DIM), lambda i: (0, i, 0)),
                                    pl.BlockSpec((tm, IDX_DIM), row),
                                    pl.BlockSpec((IDX_HEADS, tm), lambda i: (0, i)),
                                    pl.BlockSpec((2 * ATT_HEADS, tm), lambda i: (0, i))],
        out_shape=[bf16o] * 7 + [jax.ShapeDtypeStruct((M // T, nt, ATT_HEADS * V_ROWS, tm), BF16),
                                 jax.ShapeDtypeStruct((IDX_HEADS, M, IDX_DIM), BF16),
                                 jax.ShapeDtypeStruct((M, IDX_DIM), BF16),
                                 jax.ShapeDtypeStruct((IDX_HEADS, M), F32),
                                 jax.ShapeDtypeStruct((2 * ATT_HEADS, M), F32)],
        compiler_params=_cparams(("parallel",)),
        name="in_proj",
    )(x2, mod3, gain, w_main, w_tail, lbp, lnw, lnb, tab)


def _hgrn_consts():
    C = HG_CHUNK
    t = np.arange(C)
    tri = (t[None, :] <= t[:, None]).astype(np.float32)
    blocks = [tri]
    for m in HG_LEVELS:
        if m % 8:
            split = (t // (2 * m)) * (2 * m) + m - 1
            upper = ((t & m) != 0)[:, None]
            blocks.append(np.where(upper, tri - tri[split], tri[split] - tri))
    mall = np.concatenate(blocks, axis=0)
    x = t[:, None] ^ t[None, :]
    lvl = np.full((C, C), len(HG_LEVELS) + 1, np.int32)
    for li, m in enumerate(HG_LEVELS):
        lvl[(t[:, None] > t[None, :]) & (x >= m) & (x < 2 * m)] = li
    lvl[t[:, None] == t[None, :]] = len(HG_LEVELS)
    return jnp.asarray(mall, BF16), jnp.asarray(lvl)


def _hgrn_kernel(hq_ref, lf_ref, hk_ref, hv_ref, hg_ref, mall_ref, lvl_ref, onorm_ref, o_ref, st_ref):
    C = HG_CHUNK
    nl = len(HG_LEVELS)

    @pl.when(pl.program_id(1) == 0)
    def _():
        st_ref[...] = jnp.zeros_like(st_ref)

    lvl = lvl_ref[...]
    row = lax.broadcasted_iota(jnp.int32, (C, HG_DIM), 0)
    mall = mall_ref[...]

    for ci in range(hq_ref.shape[0] // C):
        rs = slice(ci * C, (ci + 1) * C)
        for h in range(HG_HEADS):
            cs = slice(h * HG_DIM, (h + 1) * HG_DIM)
            q, k, v = hq_ref[rs, cs], hk_ref[rs, cs], hv_ref[rs, cs]
            qf, kf = q.astype(F32), k.astype(F32)
            dall = _dot(mall, lf_ref[rs, cs])
            g = dall[0:C]

            a = jnp.zeros((C, C), F32)
            fine = 0
            for li, m in enumerate(HG_LEVELS):
                if m % 8 == 0:
                    qk = jnp.concatenate([(qf if (b & 1) else kf)[b * m:(b + 1) * m] for b in range(C // m)], axis=0)
                    parts = []
                    for b in range(C // (2 * m)):
                        lo_rows = slice(2 * b * m, (2 * b + 1) * m)
                        hi_rows = slice((2 * b + 1) * m, (2 * b + 2) * m)
                        ref = jnp.broadcast_to(g[lo_rows.stop - 1:lo_rows.stop, :], (m, HG_DIM))
                        parts += [ref - g[lo_rows], g[hi_rows] - ref]
                    dm = jnp.concatenate(parts, axis=0)
                else:
                    fine += 1
                    qk = jnp.where((row & m) != 0, qf, kf)
                    dm = dall[fine * C:(fine + 1) * C]
                xm = (qk * jnp.exp2(dm)).astype(BF16)
                a = jnp.where(lvl == li, _dot_nt(xm, xm), a)
            a = jnp.where(lvl == nl, _dot_nt(q, k), a)

            st = st_ref[h]
            o = _dot_nt((qf * jnp.exp2(g)).astype(BF16), st.astype(BF16)) + _dot(a.astype(BF16), v)
            g_last = g[C - 1:C, :]
            kd = (kf * jnp.exp2(g_last - g)).astype(BF16)
            st_ref[h] = jnp.exp2(g_last) * st + _dot(v.astype(F32).T.astype(BF16), kd)

            o = o * lax.rsqrt(jnp.mean(o * o, axis=-1, keepdims=True) + EPS)
            o_ref[rs, cs] = (o * onorm_ref[:, cs] * hg_ref[rs, cs]).astype(BF16)


def _hgrn(hq, lf, hk, hv, hg, onorm, B, T, ct):
    M = hq.shape[0]
    nct = T // ct
    mall, lvl = _hgrn_consts()
    blk = pl.BlockSpec((ct, GROUP), lambda b, c: (b * nct + c, 0))
    const = lambda b, c: (0, 0)
    return pl.pallas_call(
        _hgrn_kernel,
        grid=(B, nct),
        in_specs=[blk] * 5 + [pl.BlockSpec(mall.shape, const), pl.BlockSpec(lvl.shape, const),
                              pl.BlockSpec((1, GROUP), const)],
        out_specs=blk,
        out_shape=jax.ShapeDtypeStruct((M, GROUP), BF16),
        scratch_shapes=[pltpu.VMEM((HG_HEADS, HG_DIM, HG_DIM), F32)],
        compiler_params=_cparams(("parallel", "arbitrary")),
        name="hgrn2",
    )(hq, lf, hk, hv, hg, mall, lvl, onorm)


def _fold8(a, op, rows=8):
    chains = [None] * 4
    for r in range(a.shape[0] // rows):
        part = a[r * rows:(r + 1) * rows, :]
        c = r % len(chains)
        chains[c] = part if chains[c] is None else op(chains[c], part)
    return op(op(chains[0], chains[1]), op(chains[2], chains[3]))


def _floor_bf16(a):
    r = a.astype(BF16).astype(F32)
    below = (r - jnp.abs(r) * (5.0 / 1024.0)).astype(BF16)
    return jnp.where(r > a, below.astype(F32), r).astype(BF16)


def _dsa_kernel(iq_ref, iwt_ref, ik_ref, aq_ref, ak_ref, avt_ref, qnt_ref, knt_ref, o_ref,
                sc_ref, scb_ref, mm_ref, *head_refs, qb, kt, topk, seq):
    acc_refs, m_refs = head_refs[:ATT_HEADS], head_refs[ATT_HEADS:]
    i = pl.program_id(1)
    q0 = i * qb
    nfull = q0 // kt
    nkt = nfull + 1
    qpos = q0 + lax.broadcasted_iota(jnp.int32, (1, qb), 1)
    kk = jnp.minimum(qpos + 1, topk).astype(F32)

    mm_ref[0:8, :] = jnp.full((8, qb), jnp.inf, F32)
    mm_ref[8:16, :] = jnp.full((8, qb), -jnp.inf, F32)
    mm_ref[16:32, :] = jnp.zeros((16, qb), F32)

    def score_tile(j, diag):
        ks = pl.multiple_of(j * kt, kt)
        ki = ik_ref[pl.ds(ks, kt), :]
        s = None
        for h in range(IDX_HEADS):
            t = jnp.maximum(_dot_nt(ki, iq_ref[h]), 0.0) * iwt_ref[h:h + 1, :]
            s = t if s is None else s + t
        if diag:
            causal = ks + lax.broadcasted_iota(jnp.int32, (kt, qb), 0) <= qpos
            s_hi = jnp.where(causal, s, -jnp.inf)
            s_lo = jnp.where(causal, s, jnp.inf)
        else:
            s_hi = s_lo = s
        sc_ref[pl.ds(ks, kt), :] = s_hi
        scb_ref[pl.ds(ks, kt), :] = _floor_bf16(s_hi)
        mm_ref[0:8, :] = jnp.minimum(mm_ref[0:8, :], _fold8(s_lo, jnp.minimum))
        mm_ref[8:16, :] = jnp.maximum(mm_ref[8:16, :], _fold8(s_hi, jnp.maximum))
        mm_ref[16:24, :] += _fold8(jnp.where(s_hi >= 0.0, 1.0, 0.0), jnp.add)
        mm_ref[24:32, :] += _fold8(jnp.where(s_hi > 0.0, 1.0, 0.0), jnp.add)

    def full_tile(j, c):
        score_tile(j, False)
        return c

    lax.fori_loop(0, nfull, full_tile, 0)
    score_tile(nfull, True)
    mn = jnp.min(mm_ref[0:8, :], axis=0, keepdims=True)
    mx = jnp.max(mm_ref[8:16, :], axis=0, keepdims=True)
    c_nonneg = jnp.sum(mm_ref[16:24, :], axis=0, keepdims=True)
    c_pos = jnp.sum(mm_ref[24:32, :], axis=0, keepdims=True)

    def count_keys(hit_fn):
        def body(j, acc):
            ks = pl.multiple_of(j * kt, kt)
            return acc + _fold8(hit_fn(sc_ref[pl.ds(ks, kt), :], ks), jnp.add)
        acc = lax.fori_loop(0, nkt, body, jnp.zeros((8, qb), F32))
        return jnp.sum(acc, axis=0, keepdims=True)

    def count_ge(th):
        return count_keys(lambda s, ks: jnp.where(s >= th, 1.0, 0.0))

    above = c_pos >= kk
    below = c_nonneg < kk
    live = above | below
    zero = jnp.zeros((1, qb), F32)
    lo0 = jnp.where(below, mn, zero)
    cnt_lo0 = jnp.where(below, (qpos + 1).astype(F32), c_nonneg)
    hi0 = jnp.where(above, mx + jnp.abs(mx) * 1e-6 + 1e-30, zero)
    cnt_hi0 = jnp.where(above, zero, jnp.where(below, c_nonneg, c_pos))

    def midpoint(lo, hi):
        return lo + 0.5 * (hi - lo)

    def unresolved(lo, hi, cnt_lo):
        mid = midpoint(lo, hi)
        return jnp.max((live & (cnt_lo != kk) & (mid > lo) & (mid < hi)).astype(jnp.int32)) > 0

    assert kt // 16 <= 256

    def count_ge_bf16(th):
        def body(j, acc):
            ks = pl.multiple_of(j * kt, kt)
            hit = jnp.where(scb_ref[pl.ds(ks, kt), :] >= th, jnp.ones((), BF16), jnp.zeros((), BF16))
            part = _fold8(hit, jnp.add, rows=16).astype(F32)
            return acc + part[0:8, :] + part[8:16, :]
        acc = lax.fori_loop(0, nkt, body, jnp.zeros((8, qb), F32))
        return jnp.sum(acc, axis=0, keepdims=True)

    def coarse_body(_, c):
        lo, hi, cnt_lo, cnt_hi = c
        th = midpoint(lo, hi).astype(BF16)
        mid = th.astype(F32)
        inside = live & (mid > lo) & (mid < hi)
        cnt = count_ge_bf16(th)
        up = inside & (cnt >= kk)
        dn = inside & (cnt < kk)
        return (jnp.where(up, mid, lo), jnp.where(dn, mid, hi),
                jnp.where(up, cnt, cnt_lo), jnp.where(dn, cnt, cnt_hi))

    lo1, hi1, cnt_lo1, cnt_hi1 = lax.fori_loop(0, COARSE_PASSES, coarse_body, (lo0, hi0, cnt_lo0, cnt_hi0))

    def fine_pass(c):
        lo, hi, cnt_lo, cnt_hi = c
        mid = midpoint(lo, hi)
        cnt = count_ge(mid)
        up = live & (cnt >= kk)
        dn = live & (cnt < kk)
        return (jnp.where(up, mid, lo), jnp.where(dn, mid, hi), jnp.where(up, cnt, cnt_lo), jnp.where(dn, cnt, cnt_hi))

    c2 = lax.fori_loop(0, FINE_PASSES, lambda _, c: fine_pass(c), (lo1, hi1, cnt_lo1, cnt_hi1))

    def bis_cond(c):
        it, (lo, hi, cnt_lo, _) = c
        return (it < 320) & unresolved(lo, hi, cnt_lo)

    _, (lo, hi, cnt_lo, cnt_hi) = lax.while_loop(bis_cond, lambda c: (c[0] + 1, fine_pass(c[1])), (jnp.int32(0), c2))

    tied = cnt_lo != kk
    need = kk - cnt_hi

    def resolve_ties(cs):
        lo_c, need_c, tied_c = lo[:, cs], need[:, cs], tied[:, cs]

        def tie_keys(j):
            ks = pl.multiple_of(j * kt, kt)
            rows = pl.ds(ks, kt)
            return rows, sc_ref[rows, cs], ks + lax.broadcasted_iota(jnp.int32, (kt, LANES), 0)

        def count_tie(jm):
            def body(j, acc):
                _, s, kpos = tie_keys(j)
                return acc + _fold8(jnp.where((s == lo_c) & (kpos <= jm), 1.0, 0.0), jnp.add)
            acc = lax.fori_loop(0, nkt, body, jnp.zeros((8, LANES), F32))
            return jnp.sum(acc, axis=0, keepdims=True)

        def tie_body(_, c):
            jlo, jhi = c
            jm = (jlo + jhi) // 2
            ok = count_tie(jm) >= need_c
            return jnp.where(ok, jlo, jm), jnp.where(ok, jm, jhi)

        steps = int(np.ceil(np.log2(seq))) + 1
        _, jhi = lax.fori_loop(0, steps, tie_body,
                               (jnp.full((1, LANES), -1, jnp.int32), jnp.full((1, LANES), seq - 1, jnp.int32)))
        jmax = jnp.where(tied_c, jhi, seq)

        def strike(j, c):
            rows, s, kpos = tie_keys(j)
            sc_ref[rows, cs] = jnp.where((s == lo_c) & (kpos > jmax), -jnp.inf, s)
            return c

        lax.fori_loop(0, nkt, strike, 0)

    for c in range(qb // LANES):
        cs = slice(c * LANES, (c + 1) * LANES)
        pl.when(jnp.max(tied[:, cs].astype(jnp.int32)) > 0)(functools.partial(resolve_ties, cs))

    def logits(j, h, bias):
        ks = pl.multiple_of(j * kt, kt)
        sl = slice(h * ATT_DIM, (h + 1) * ATT_DIM)
        return _dot_nt(ak_ref[pl.ds(ks, kt), sl], aq_ref[:, sl]) + bias

    def select_bias(j):
        ks = pl.multiple_of(j * kt, kt)
        return jnp.where(sc_ref[pl.ds(ks, kt), :] >= lo, 0.0, NEG_BIG)

    def values(j, h):
        return avt_ref[0, j, h * V_ROWS:(h + 1) * V_ROWS, :]

    kmax = jnp.max(knt_ref[...], axis=1, keepdims=True)
    shift = [qnt_ref[h:h + 1, :] * kmax[ATT_HEADS + h:ATT_HEADS + h + 1, :] * NORM_SLACK for h in range(ATT_HEADS)]
    for h in range(ATT_HEADS):
        acc_refs[h][...] = jnp.zeros_like(acc_refs[h])

    def attn_tile_fast(j, c):
        bias = select_bias(j)
        for h in range(ATT_HEADS):
            p = jnp.exp2(logits(j, h, bias) - shift[h]).astype(BF16)
            acc_refs[h][...] += _dot(values(j, h), p)
        return c

    lax.fori_loop(0, nkt, attn_tile_fast, 0)
    denom_min = functools.reduce(jnp.minimum, [acc_refs[h][ATT_DIM:ATT_DIM + 1, :] for h in range(ATT_HEADS)])

    @pl.when(jnp.min(denom_min) < DENOM_FLOOR)
    def _():
        for h in range(ATT_HEADS):
            acc_refs[h][...] = jnp.zeros_like(acc_refs[h])
            m_refs[h][...] = jnp.full(m_refs[h].shape, NEG_BIG, F32)

        def attn_tile(j, c):
            bias = select_bias(j)
            for h in range(ATT_HEADS):
                acc_h, m_h = acc_refs[h], m_refs[h]
                lg = logits(j, h, bias)
                m_old = m_h[0:1, :]
                m_new = jnp.maximum(m_old, jnp.max(_fold8(lg, jnp.maximum), axis=0, keepdims=True))
                p = jnp.exp2(lg - m_new).astype(BF16)
                acc_h[...] = jnp.exp2(m_old - m_new) * acc_h[...] + _dot(values(j, h), p)
                m_h[0:1, :] = m_new
            return c

        lax.fori_loop(0, nkt, attn_tile, 0)

    for h in range(ATT_HEADS):
        o_t = acc_refs[h][0:ATT_DIM, :] / acc_refs[h][ATT_DIM:ATT_DIM + 1, :]
        o_ref[:, h * ATT_DIM:(h + 1) * ATT_DIM] = o_t.T.astype(BF16)


def _dsa(iq, iwt, ik, aq, ak, avt, nrm, B, T, qb, kt):
    M = ak.shape[0]
    nq = T // qb
    assert kt % qb == 0 and T % kt == 0 and avt.shape[1:] == (T // kt, ATT_HEADS * V_ROWS, kt)
    topk = min(TOPK_MAX, T // 4)
    qrow = lambda b, i: (b * nq + i, 0)
    brow = lambda b, i: (b, 0)
    kern = functools.partial(_dsa_kernel, qb=qb, kt=kt, topk=topk, seq=T)
    return pl.pallas_call(
        kern,
        grid=(B, nq),
        in_specs=[pl.BlockSpec((IDX_HEADS, qb, IDX_DIM), lambda b, i: (0, b * nq + i, 0)),
                  pl.BlockSpec((IDX_HEADS, qb), lambda b, i: (0, b * nq + i)),
                  _resident((T, IDX_DIM), brow),
                  pl.BlockSpec((qb, GROUP), qrow),
                  _resident((T, GROUP), brow),
                  _resident((1,) + avt.shape[1:], lambda b, i: (b, 0, 0, 0)),
                  pl.BlockSpec((2 * ATT_HEADS, qb), lambda b, i: (0, b * nq + i)),
                  _resident((2 * ATT_HEADS, T), lambda b, i: (0, b))],
        out_specs=pl.BlockSpec((qb, GROUP), qrow),
        out_shape=jax.ShapeDtypeStruct((M, GROUP), BF16),
        scratch_shapes=[pltpu.VMEM((T, qb), F32),
                        pltpu.VMEM((T, qb), BF16),
                        pltpu.VMEM((32, qb), F32),
                        ] + [pltpu.VMEM((V_ROWS, qb), F32)] * ATT_HEADS
                        + [pltpu.VMEM((8, qb), F32)] * ATT_HEADS,
        compiler_params=_cparams(("parallel", "arbitrary")),
        name="dsa",
    )(iq, iwt, ik, aq, ak, avt, nrm, nrm)


def _rms(a, gain):
    return a * lax.rsqrt(jnp.mean(a * a, axis=-1, keepdims=True) + EPS) * gain


def _out_ffn_kernel(ohg_ref, oatt_ref, x_ref, mod_ref, gains_ref, wo_ref, wi_ref, wf_ref, o_ref, acc_ref,
                    *, d_ff, fc):
    mod = mod_ref[0]
    gains = gains_ref[...]
    hw = ohg_ref.shape[1]
    y = _dot(ohg_ref[...], wo_ref[0:hw, :]) + _dot(oatt_ref[...], wo_ref[hw:, :])
    x1 = x_ref[...] + mod[2:3, :] * _rms(y, gains[0:1, :])
    h2 = (_rms(x1, gains[1:2, :]) * (1.0 + mod[4:5, :]) + mod[3:4, :]).astype(BF16)
    for c in range(d_ff // fc):
        gate = _dot(h2, wi_ref[:, c * fc:(c + 1) * fc])
        up = _dot(h2, wi_ref[:, d_ff + c * fc:d_ff + (c + 1) * fc])
        part = _dot((_silu(gate) * up).astype(BF16), wf_ref[c * fc:(c + 1) * fc, :])
        if c == 0:
            acc_ref[...] = part
        else:
            acc_ref[...] += part
    o_ref[...] = x1 + mod[5:6, :] * _rms(acc_ref[...], gains[2:3, :])


def _out_ffn(ohg, oatt, x2, mod3, gains, wo, wi, wf, T, tm):
    M, D = x2.shape
    nt = T // tm
    d_ff = wf.shape[0]
    row = lambda i: (i, 0)
    const = lambda i: (0, 0)
    kern = functools.partial(_out_ffn_kernel, d_ff=d_ff, fc=256)
    return pl.pallas_call(
        kern,
        grid=(M // tm,),
        in_specs=[pl.BlockSpec((tm, GROUP), row),
                  pl.BlockSpec((tm, GROUP), row),
                  pl.BlockSpec((tm, D), row),
                  pl.BlockSpec((1, N_MOD, D), lambda i: (i // nt, 0, 0)),
                  pl.BlockSpec((3, D), const),
                  _resident(wo.shape, const),
                  _resident(wi.shape, const),
                  _resident(wf.shape, const)],
        out_specs=pl.BlockSpec((tm, D), row),
        out_shape=jax.ShapeDtypeStruct((M, D), F32),
        scratch_shapes=[pltpu.VMEM((tm, D), F32)],
        compiler_params=_cparams(("parallel",)),
        name="out_ffn",
    )(ohg, oatt, x2, mod3, gains, wo, wi, wf)


def kernel(x, c, positions, w_ada, b_ada, norm_pre_mix, norm_post_mix, norm_pre_ffn, norm_post_ffn, w_in,
           hgrn_lower_bound, hgrn_out_norm, idx_k_norm_w, idx_k_norm_b, w_out, w_ffn_in, w_ffn_out):
    B, T, D = x.shape
    depth = w_ada.shape[0]
    assert depth == 1 and hgrn_lower_bound.shape[0] == 2
    assert T % 512 == 0 and D % LANES == 0
    tm = 512

    mod3 = _adaln(c, w_ada[0], b_ada[0]).reshape(B, N_MOD, D)
    tab = _rope_table(positions)

    main = 8 * GROUP
    w = w_in[0]
    w_main = w[:, :main].astype(BF16)
    w_tail = jnp.concatenate([jnp.pad(w[:, main:main + IDX_DIM], ((0, 0), (0, LANES - IDX_DIM))),
                              jnp.pad(w[:, main + IDX_DIM:], ((0, 0), (0, LANES - IDX_HEADS)))], axis=1).astype(BF16)
    lnw = jnp.pad(idx_k_norm_w[0], (0, LANES - IDX_DIM)).reshape(1, LANES)
    lnb = jnp.pad(idx_k_norm_b[0], (0, LANES - IDX_DIM)).reshape(1, LANES)

    x2 = x.reshape(B * T, D)
    hq, lf, hk, hv, hg, aq, ak, avt, iq, ik, iwt, nrm = _in_proj(
        x2, mod3, norm_pre_mix, w_main, w_tail, hgrn_lower_bound, lnw, lnb, tab, T, tm)

    o_hg = _hgrn(hq, lf, hk, hv, hg, hgrn_out_norm, B, T, 512)
    o_att = _dsa(iq, iwt, ik, aq, ak, avt, nrm, B, T, 512, tm)

    gains = jnp.concatenate([norm_post_mix, norm_pre_ffn, norm_post_ffn], axis=0)
    out = _out_ffn(o_hg, o_att, x2, mod3, gains, w_out[0].astype(BF16), w_ffn_in[0].astype(BF16),
                   w_ffn_out[0].astype(BF16), T, tm)
    return out.reshape(B, T, D)
```

```python
import functools

import numpy as np
import jax
import jax.numpy as jnp
from jax import lax
from jax.experimental import pallas as pl
from jax.experimental.pallas import tpu as pltpu

F32 = jnp.float32
BF16 = jnp.bfloat16

HG_HEADS = 4
HG_DIM = 128
ATT_HEADS = 4
ATT_DIM = 128
IDX_HEADS = 8
IDX_DIM = 64
TOPK_MAX = 256
ROPE_THETA = 500000.0
ROPE_FRACTION = 4
N_MOD = 6
EPS = 1e-6

GROUP = 512
LANES = 128
VMEM_LIMIT = 56 * 1024 * 1024

HG_CHUNK = 128
HG_LEVELS = (64, 32, 16, 8, 4, 2, 1)

NEG_BIG = -1e30
DENOM_FLOOR = 2.0 ** -64
NORM_SLACK = 1.01
COARSE_PASSES = 10
FINE_PASSES = 10
COUNT_SLAB = 64
Q_SCALE = ATT_DIM ** -0.5 * 1.4426950408889634
V_ROWS = ATT_DIM + 16


def _cparams(sem):
    return pltpu.CompilerParams(dimension_semantics=sem, vmem_limit_bytes=VMEM_LIMIT)


def _resident(shape, index_map):
    return pl.BlockSpec(shape, index_map, pipeline_mode=pl.Buffered(1))


def _split_bf16(a):
    hi = a.astype(BF16)
    lo = (a - hi.astype(F32)).astype(BF16)
    return hi, lo


def _dot(a, b):
    return jnp.dot(a, b, preferred_element_type=F32)


def _dot_nt(a, b):
    return lax.dot_general(a, b, (((1,), (1,)), ((), ())), preferred_element_type=F32)


def _silu(a):
    return a * jax.nn.sigmoid(a)


def _adaln_kernel(c_ref, w_ref, b_ref, o_ref):
    a = _silu(c_ref[...])
    a_hi, a_lo = _split_bf16(a)
    w_hi, w_lo = _split_bf16(w_ref[...])
    acc = _dot(a_hi, w_hi) + (_dot(a_hi, w_lo) + _dot(a_lo, w_hi))
    o_ref[...] = acc + b_ref[...]


def _adaln(c, w, b):
    B, D = c.shape
    N = w.shape[1]
    return pl.pallas_call(
        _adaln_kernel,
        grid=(N // D,),
        in_specs=[pl.BlockSpec((B, D), lambda j: (0, 0)),
                  pl.BlockSpec((D, D), lambda j: (0, j)),
                  pl.BlockSpec((1, D), lambda j: (0, j))],
        out_specs=pl.BlockSpec((B, D), lambda j: (0, j)),
        out_shape=jax.ShapeDtypeStruct((B, N), F32),
        compiler_params=_cparams(("arbitrary",)),
        name="adaln",
    )(c, w, b.reshape(1, N))


def _rope_table_kernel(pos_ref, fa_ref, fb_ref, o_ref):
    pos = pos_ref[0].astype(F32)
    ang_a = fa_ref[...] * pos
    ang_b = fb_ref[...] * pos
    ha = fa_ref.shape[0]
    hb = fb_ref.shape[0]
    o_ref[0, 0:ha, :] = jnp.cos(ang_a)
    o_ref[0, ha:2 * ha, :] = jnp.sin(ang_a)
    o_ref[0, 2 * ha:2 * ha + hb, :] = jnp.cos(ang_b)
    o_ref[0, 2 * ha + hb:2 * ha + 2 * hb, :] = jnp.sin(ang_b)
    o_ref[0, 2 * ha + 2 * hb:, :] = jnp.zeros((LANES - 2 * ha - 2 * hb, pos.shape[1]), F32)


ROPE_HALF_A = ATT_DIM // ROPE_FRACTION // 2
ROPE_HALF_B = IDX_DIM // ROPE_FRACTION // 2


def _rope_table(positions):
    B, T = positions.shape
    ha, hb = ROPE_HALF_A, ROPE_HALF_B
    fa = (ROPE_THETA ** (-jnp.arange(ha, dtype=F32) / ha)).reshape(ha, 1)
    fb = (ROPE_THETA ** (-jnp.arange(hb, dtype=F32) / hb)).reshape(hb, 1)
    return pl.pallas_call(
        _rope_table_kernel,
        grid=(B,),
        in_specs=[pl.BlockSpec((1, 1, T), lambda b: (b, 0, 0)),
                  pl.BlockSpec((ha, 1), lambda b: (0, 0)),
                  pl.BlockSpec((hb, 1), lambda b: (0, 0))],
        out_specs=pl.BlockSpec((1, LANES, T), lambda b: (b, 0, 0)),
        out_shape=jax.ShapeDtypeStruct((B, LANES, T), F32),
        compiler_params=_cparams(("arbitrary",)),
        name="rope_table",
    )(positions.reshape(B, 1, T), fa, fb)


def _rope_patterns(tab):
    ha, hb = ROPE_HALF_A, ROPE_HALF_B
    lane = lax.broadcasted_iota(jnp.int32, tab.shape, 1)
    shifted = lambda sh: pltpu.roll(tab, sh % LANES, axis=1)
    cos_a = jnp.where(lane < ha, tab, jnp.where(lane < 2 * ha, shifted(ha), 1.0))
    sin_a = jnp.where(lane < ha, -shifted(-ha), jnp.where(lane < 2 * ha, tab, 0.0))
    l64 = lane & (IDX_DIM - 1)
    first = lane < IDX_DIM
    cb0, sb0 = 2 * ha, 2 * ha + hb
    cos_b = jnp.where(l64 < hb, jnp.where(first, shifted(-cb0), shifted(IDX_DIM - cb0)),
                      jnp.where(l64 < 2 * hb, jnp.where(first, shifted(hb - cb0), shifted(IDX_DIM + hb - cb0)), 1.0))
    sin_b = jnp.where(l64 < hb, -jnp.where(first, shifted(-sb0), shifted(IDX_DIM - sb0)),
                      jnp.where(l64 < 2 * hb, jnp.where(first, shifted(hb - sb0), shifted(IDX_DIM + hb - sb0)), 0.0))
    return cos_a, sin_a, cos_b, sin_b


def _rope(xb, cos, sin, half, period):
    lane = lax.broadcasted_iota(jnp.int32, xb.shape, 1)
    fwd = pltpu.roll(xb, LANES - half, axis=1)
    bwd = pltpu.roll(xb, half, axis=1)
    partner = jnp.where((lane & (period - 1)) < half, fwd, bwd)
    return xb * cos + partner * sin


def _in_proj_kernel(x_ref, mod_ref, gain_ref, w_ref, wt_ref, lbp_ref, lnw_ref, lnb_ref, tab_ref,
                    hq_ref, lf_ref, hk_ref, hv_ref, hg_ref,
                    aq_ref, ak_ref, avt_ref, iq_ref, ik_ref, iwt_ref, nrm_ref):
    x = x_ref[...]
    ms = jnp.mean(x * x, axis=-1, keepdims=True)
    mod = mod_ref[0]
    h = x * lax.rsqrt(ms + EPS) * gain_ref[...]
    h = h * (1.0 + mod[1:2, :]) + mod[0:1, :]
    hb = h.astype(BF16)

    def proj(g, width=GROUP):
        return _dot(hb, w_ref[:, g * GROUP:g * GROUP + width])

    hq_ref[...] = (_silu(proj(0)) * (HG_DIM ** -0.5)).astype(BF16)
    a = lbp_ref[...]
    amax = jnp.max(a, axis=0, keepdims=True)
    e = jnp.exp(a - amax)
    lb = e[0:1, :] / jnp.sum(e, axis=0, keepdims=True)
    fr = proj(1)
    f = lb + (1.0 - lb) * jax.nn.sigmoid(fr)
    lf_ref[...] = jnp.log2(f).astype(BF16)
    hk_ref[...] = ((1.0 - lb) * jax.nn.sigmoid(-fr)).astype(BF16)
    hv_ref[...] = proj(2).astype(BF16)
    hg_ref[...] = _silu(proj(3)).astype(BF16)

    cosa, sina, cosb, sinb = _rope_patterns(tab_ref[0].T)
    half_a = ROPE_HALF_A
    q = proj(4)
    k = proj(5)
    lane_a = lax.broadcasted_iota(jnp.int32, cosa.shape, 1)
    norms = jnp.zeros(cosa.shape, F32)
    for hh in range(ATT_HEADS):
        sl = slice(hh * ATT_DIM, (hh + 1) * ATT_DIM)
        qh = (_rope(q[:, sl], cosa, sina, half_a, ATT_DIM) * Q_SCALE).astype(BF16)
        kh = _rope(k[:, sl], cosa, sina, half_a, ATT_DIM).astype(BF16)
        aq_ref[:, sl] = qh
        ak_ref[:, sl] = kh
        for slot, a in ((hh, qh), (ATT_HEADS + hh, kh)):
            norms = jnp.where(lane_a == slot, jnp.sqrt(_dot(a * a, jnp.ones((ATT_DIM, LANES), BF16))), norms)
    nrm_ref[...] = norms.T[0:2 * ATT_HEADS, :]
    v = proj(6)
    for hh in range(ATT_HEADS):
        r0 = hh * V_ROWS
        avt_ref[0, 0, r0:r0 + ATT_DIM, :] = v[:, hh * ATT_DIM:(hh + 1) * ATT_DIM].T.astype(BF16)
        avt_ref[0, 0, r0 + ATT_DIM:r0 + V_ROWS, :] = jnp.ones((V_ROWS - ATT_DIM, v.shape[0]), BF16)

    half_b = ROPE_HALF_B
    qi = proj(7)
    for cc in range(GROUP // LANES):
        qr = _rope(qi[:, cc * LANES:(cc + 1) * LANES], cosb, sinb, half_b, IDX_DIM).astype(BF16)
        for hh in range(LANES // IDX_DIM):
            iq_ref[cc * (LANES // IDX_DIM) + hh] = qr[:, hh * IDX_DIM:(hh + 1) * IDX_DIM]
    kw = _dot(hb, wt_ref[...])
    ki = kw[:, 0:LANES]
    lane = lax.broadcasted_iota(jnp.int32, ki.shape, 1)
    real = lane < IDX_DIM
    mu = jnp.sum(ki, axis=-1, keepdims=True) * (1.0 / IDX_DIM)
    d = jnp.where(real, ki - mu, 0.0)
    var = jnp.sum(d * d, axis=-1, keepdims=True) * (1.0 / IDX_DIM)
    kn = d * lax.rsqrt(var + EPS) * lnw_ref[...] + lnb_ref[...]
    kn = _rope(kn, cosb, sinb, half_b, IDX_DIM)
    ik_ref[...] = kn[:, 0:IDX_DIM].astype(BF16)
    iwt_ref[...] = kw[:, LANES:2 * LANES].T[0:IDX_HEADS, :] * (IDX_HEADS ** -0.5 * IDX_DIM ** -0.5)


def _in_proj(x2, mod3, gain, w_main, w_tail, lbp, lnw, lnb, tab, T, tm):
    M, D = x2.shape
    nt = T // tm
    row = lambda i: (i, 0)
    const = lambda i: (0, 0)
    bf16o = jax.ShapeDtypeStruct((M, GROUP), BF16)
    grp_spec = pl.BlockSpec((tm, GROUP), row)
    return pl.pallas_call(
        _in_proj_kernel,
        grid=(M // tm,),
        in_specs=[pl.BlockSpec((tm, D), row),
                  pl.BlockSpec((1, N_MOD, D), lambda i: (i // nt, 0, 0)),
                  pl.BlockSpec((1, D), const),
                  _resident(w_main.shape, const),
                  _resident(w_tail.shape, const),
                  pl.BlockSpec(lbp.shape, const),
                  pl.BlockSpec((1, LANES), const),
                  pl.BlockSpec((1, LANES), const),
                  pl.BlockSpec((1, LANES, tm), lambda i: (i // nt, 0, i % nt))],
        out_specs=[grp_spec] * 7 + [pl.BlockSpec((1, 1, ATT_HEADS * V_ROWS, tm), lambda i: (i // nt, i % nt, 0, 0)),
                                    pl.BlockSpec((IDX_HEADS, tm, IDX_DIM), lambda i: (0, i, 0)),
                                    pl.BlockSpec((tm, IDX_DIM), row),
                                    pl.BlockSpec((IDX_HEADS, tm), lambda i: (0, i)),
                                    pl.BlockSpec((2 * ATT_HEADS, tm), lambda i: (0, i))],
        out_shape=[bf16o] * 7 + [jax.ShapeDtypeStruct((M // T, nt, ATT_HEADS * V_ROWS, tm), BF16),
                                 jax.ShapeDtypeStruct((IDX_HEADS, M, IDX_DIM), BF16),
                                 jax.ShapeDtypeStruct((M, IDX_DIM), BF16),
                                 jax.ShapeDtypeStruct((IDX_HEADS, M), F32),
                                 jax.ShapeDtypeStruct((2 * ATT_HEADS, M), F32)],
        compiler_params=_cparams(("parallel",)),
        name="in_proj",
    )(x2, mod3, gain, w_main, w_tail, lbp, lnw, lnb, tab)


def _hgrn_consts():
    C = HG_CHUNK
    t = np.arange(C)
    tri = (t[None, :] <= t[:, None]).astype(np.float32)
    blocks = [tri]
    for m in HG_LEVELS:
        if m % 8:
            split = (t // (2 * m)) * (2 * m) + m - 1
            upper = ((t & m) != 0)[:, None]
            blocks.append(np.where(upper, tri - tri[split], tri[split] - tri))
    mall = np.concatenate(blocks, axis=0)
    x = t[:, None] ^ t[None, :]
    lvl = np.full((C, C), len(HG_LEVELS) + 1, np.int32)
    for li, m in enumerate(HG_LEVELS):
        lvl[(t[:, None] > t[None, :]) & (x >= m) & (x < 2 * m)] = li
    lvl[t[:, None] == t[None, :]] = len(HG_LEVELS)
    return jnp.asarray(mall, BF16), jnp.asarray(lvl)


def _hgrn_kernel(hq_ref, lf_ref, hk_ref, hv_ref, hg_ref, mall_ref, lvl_ref, onorm_ref, o_ref, st_ref):
    C = HG_CHUNK
    nl = len(HG_LEVELS)

    @pl.when(pl.program_id(1) == 0)
    def _():
        st_ref[...] = jnp.zeros_like(st_ref)

    lvl = lvl_ref[...]
    row = lax.broadcasted_iota(jnp.int32, (C, HG_DIM), 0)
    mall = mall_ref[...]

    for ci in range(hq_ref.shape[0] // C):
        rs = slice(ci * C, (ci + 1) * C)
        for h in range(HG_HEADS):
            cs = slice(h * HG_DIM, (h + 1) * HG_DIM)
            q, k, v = hq_ref[rs, cs], hk_ref[rs, cs], hv_ref[rs, cs]
            qf, kf = q.astype(F32), k.astype(F32)
            dall = _dot(mall, lf_ref[rs, cs])
            g = dall[0:C]

            a = jnp.zeros((C, C), F32)
            fine = 0
            for li, m in enumerate(HG_LEVELS):
                if m % 8 == 0:
                    qk = jnp.concatenate([(qf if (b & 1) else kf)[b * m:(b + 1) * m] for b in range(C // m)], axis=0)
                    parts = []
                    for b in range(C // (2 * m)):
                        lo_rows = slice(2 * b * m, (2 * b + 1) * m)
                        hi_rows = slice((2 * b + 1) * m, (2 * b + 2) * m)
                        ref = jnp.broadcast_to(g[lo_rows.stop - 1:lo_rows.stop, :], (m, HG_DIM))
                        parts += [ref - g[lo_rows], g[hi_rows] - ref]
                    dm = jnp.concatenate(parts, axis=0)
                else:
                    fine += 1
                    qk = jnp.where((row & m) != 0, qf, kf)
                    dm = dall[fine * C:(fine + 1) * C]
                xm = (qk * jnp.exp2(dm)).astype(BF16)
                a = jnp.where(lvl == li, _dot_nt(xm, xm), a)
            a = jnp.where(lvl == nl, _dot_nt(q, k), a)

            st = st_ref[h]
            o = _dot_nt((qf * jnp.exp2(g)).astype(BF16), st.astype(BF16)) + _dot(a.astype(BF16), v)
            g_last = g[C - 1:C, :]
            kd = (kf * jnp.exp2(g_last - g)).astype(BF16)
            st_ref[h] = jnp.exp2(g_last) * st + _dot(v.astype(F32).T.astype(BF16), kd)

            o = o * lax.rsqrt(jnp.mean(o * o, axis=-1, keepdims=True) + EPS)
            o_ref[rs, cs] = (o * onorm_ref[:, cs] * hg_ref[rs, cs]).astype(BF16)


def _hgrn(hq, lf, hk, hv, hg, onorm, B, T, ct):
    M = hq.shape[0]
    nct = T // ct
    mall, lvl = _hgrn_consts()
    blk = pl.BlockSpec((ct, GROUP), lambda b, c: (b * nct + c, 0))
    const = lambda b, c: (0, 0)
    return pl.pallas_call(
        _hgrn_kernel,
        grid=(B, nct),
        in_specs=[blk] * 5 + [pl.BlockSpec(mall.shape, const), pl.BlockSpec(lvl.shape, const),
                              pl.BlockSpec((1, GROUP), const)],
        out_specs=blk,
        out_shape=jax.ShapeDtypeStruct((M, GROUP), BF16),
        scratch_shapes=[pltpu.VMEM((HG_HEADS, HG_DIM, HG_DIM), F32)],
        compiler_params=_cparams(("parallel", "arbitrary")),
        name="hgrn2",
    )(hq, lf, hk, hv, hg, mall, lvl, onorm)


def _fold8(a, op, rows=8):
    chains = [None] * 4
    for r in range(a.shape[0] // rows):
        part = a[r * rows:(r + 1) * rows, :]
        c = r % len(chains)
        chains[c] = part if chains[c] is None else op(chains[c], part)
    return op(op(chains[0], chains[1]), op(chains[2], chains[3]))


def _floor_bf16(a):
    r = a.astype(BF16).astype(F32)
    below = (r - jnp.abs(r) * (5.0 / 1024.0)).astype(BF16)
    return jnp.where(r > a, below.astype(F32), r).astype(BF16)


def _dsa_kernel(iq_ref, iwt_ref, ik_ref, aq_ref, ak_ref, avt_ref, qnt_ref, knt_ref, o_ref,
                sc_ref, scb_ref, mm_ref, *head_refs, qb, kt, topk, seq):
    acc_refs, m_refs = head_refs[:ATT_HEADS], head_refs[ATT_HEADS:]
    i = pl.program_id(1)
    q0 = i * qb
    nfull = q0 // kt
    nkt = nfull + 1
    qpos = q0 + lax.broadcasted_iota(jnp.int32, (1, qb), 1)
    kk = jnp.minimum(qpos + 1, topk).astype(F32)

    mm_ref[0:8, :] = jnp.full((8, qb), jnp.inf, F32)
    mm_ref[8:16, :] = jnp.full((8, qb), -jnp.inf, F32)
    mm_ref[16:32, :] = jnp.zeros((16, qb), F32)

    def score_tile(j, diag):
        ks = pl.multiple_of(j * kt, kt)
        ki = ik_ref[pl.ds(ks, kt), :]
        s = None
        for h in range(IDX_HEADS):
            t = jnp.maximum(_dot_nt(ki, iq_ref[h]), 0.0) * iwt_ref[h:h + 1, :]
            s = t if s is None else s + t
        if diag:
            causal = ks + lax.broadcasted_iota(jnp.int32, (kt, qb), 0) <= qpos
            s_hi = jnp.where(causal, s, -jnp.inf)
            s_lo = jnp.where(causal, s, jnp.inf)
        else:
            s_hi = s_lo = s
        sc_ref[pl.ds(ks, kt), :] = s_hi
        scb_ref[pl.ds(ks, kt), :] = _floor_bf16(s_hi)
        mm_ref[0:8, :] = jnp.minimum(mm_ref[0:8, :], _fold8(s_lo, jnp.minimum))
        mm_ref[8:16, :] = jnp.maximum(mm_ref[8:16, :], _fold8(s_hi, jnp.maximum))
        mm_ref[16:24, :] += _fold8(jnp.where(s_hi >= 0.0, 1.0, 0.0), jnp.add)
        mm_ref[24:32, :] += _fold8(jnp.where(s_hi > 0.0, 1.0, 0.0), jnp.add)

    def full_tile(j, c):
        score_tile(j, False)
        return c

    lax.fori_loop(0, nfull, full_tile, 0)
    score_tile(nfull, True)
    mn = jnp.min(mm_ref[0:8, :], axis=0, keepdims=True)
    mx = jnp.max(mm_ref[8:16, :], axis=0, keepdims=True)
    c_nonneg = jnp.sum(mm_ref[16:24, :], axis=0, keepdims=True)
    c_pos = jnp.sum(mm_ref[24:32, :], axis=0, keepdims=True)

    def tile_hits(ref, j, hit_fn, slab, rows):
        acc = None
        for r in range(kt // slab):
            ks = pl.multiple_of(j * kt + r * slab, slab)
            part = _fold8(hit_fn(ref[pl.ds(ks, slab), :]), jnp.add, rows=rows)
            acc = part if acc is None else acc + part
        return acc

    def count_ge(th):
        def body(j, acc):
            return acc + tile_hits(sc_ref, j, lambda s: jnp.where(s >= th, 1.0, 0.0), COUNT_SLAB, 8)
        acc = lax.fori_loop(0, nkt, body, jnp.zeros((8, qb), F32))
        return jnp.sum(acc, axis=0, keepdims=True)

    above = c_pos >= kk
    below = c_nonneg < kk
    live = above | below
    zero = jnp.zeros((1, qb), F32)
    lo0 = jnp.where(below, mn, zero)
    cnt_lo0 = jnp.where(below, (qpos + 1).astype(F32), c_nonneg)
    hi0 = jnp.where(above, mx + jnp.abs(mx) * 1e-6 + 1e-30, zero)
    cnt_hi0 = jnp.where(above, zero, jnp.where(below, c_nonneg, c_pos))

    def midpoint(lo, hi):
        return lo + 0.5 * (hi - lo)

    def unresolved(lo, hi, cnt_lo):
        mid = midpoint(lo, hi)
        return jnp.max((live & (cnt_lo != kk) & (mid > lo) & (mid < hi)).astype(jnp.int32)) > 0

    assert kt // 16 <= 256

    def count_ge_bf16(th):
        def body(j, acc):
            hit = lambda s: jnp.where(s >= th, jnp.ones((), BF16), jnp.zeros((), BF16))
            part = tile_hits(scb_ref, j, hit, 2 * COUNT_SLAB, 16).astype(F32)
            return acc + part[0:8, :] + part[8:16, :]
        acc = lax.fori_loop(0, nkt, body, jnp.zeros((8, qb), F32))
        return jnp.sum(acc, axis=0, keepdims=True)

    def coarse_body(_, c):
        lo, hi, cnt_lo, cnt_hi = c
        th = midpoint(lo, hi).astype(BF16)
        mid = th.astype(F32)
        inside = live & (mid > lo) & (mid < hi)
        cnt = count_ge_bf16(th)
        up = inside & (cnt >= kk)
        dn = inside & (cnt < kk)
        return (jnp.where(up, mid, lo), jnp.where(dn, mid, hi),
                jnp.where(up, cnt, cnt_lo), jnp.where(dn, cnt, cnt_hi))

    lo1, hi1, cnt_lo1, cnt_hi1 = lax.fori_loop(0, COARSE_PASSES, coarse_body, (lo0, hi0, cnt_lo0, cnt_hi0))

    def fine_pass(c):
        lo, hi, cnt_lo, cnt_hi = c
        mid = midpoint(lo, hi)
        cnt = count_ge(mid)
        up = live & (cnt >= kk)
        dn = live & (cnt < kk)
        return (jnp.where(up, mid, lo), jnp.where(dn, mid, hi), jnp.where(up, cnt, cnt_lo), jnp.where(dn, cnt, cnt_hi))

    c2 = lax.fori_loop(0, FINE_PASSES, lambda _, c: fine_pass(c), (lo1, hi1, cnt_lo1, cnt_hi1))

    def bis_cond(c):
        it, (lo, hi, cnt_lo, _) = c
        return (it < 320) & unresolved(lo, hi, cnt_lo)

    _, (lo, hi, cnt_lo, cnt_hi) = lax.while_loop(bis_cond, lambda c: (c[0] + 1, fine_pass(c[1])), (jnp.int32(0), c2))

    tied = cnt_lo != kk
    need = kk - cnt_hi

    def resolve_ties(cs):
        lo_c, need_c, tied_c = lo[:, cs], need[:, cs], tied[:, cs]

        def tie_keys(j):
            ks = pl.multiple_of(j * kt, kt)
            rows = pl.ds(ks, kt)
            return rows, sc_ref[rows, cs], ks + lax.broadcasted_iota(jnp.int32, (kt, LANES), 0)

        def count_tie(jm):
            def body(j, acc):
                _, s, kpos = tie_keys(j)
                return acc + _fold8(jnp.where((s == lo_c) & (kpos <= jm), 1.0, 0.0), jnp.add)
            acc = lax.fori_loop(0, nkt, body, jnp.zeros((8, LANES), F32))
            return jnp.sum(acc, axis=0, keepdims=True)

        def tie_body(_, c):
            jlo, jhi = c
            jm = (jlo + jhi) // 2
            ok = count_tie(jm) >= need_c
            return jnp.where(ok, jlo, jm), jnp.where(ok, jm, jhi)

        steps = int(np.ceil(np.log2(seq))) + 1
        _, jhi = lax.fori_loop(0, steps, tie_body,
                               (jnp.full((1, LANES), -1, jnp.int32), jnp.full((1, LANES), seq - 1, jnp.int32)))
        jmax = jnp.where(tied_c, jhi, seq)

        def strike(j, c):
            rows, s, kpos = tie_keys(j)
            sc_ref[rows, cs] = jnp.where((s == lo_c) & (kpos > jmax), -jnp.inf, s)
            return c

        lax.fori_loop(0, nkt, strike, 0)

    for c in range(qb // LANES):
        cs = slice(c * LANES, (c + 1) * LANES)
        pl.when(jnp.max(tied[:, cs].astype(jnp.int32)) > 0)(functools.partial(resolve_ties, cs))

    def logits(j, h, bias):
        ks = pl.multiple_of(j * kt, kt)
        sl = slice(h * ATT_DIM, (h + 1) * ATT_DIM)
        return _dot_nt(ak_ref[pl.ds(ks, kt), sl], aq_ref[:, sl]) + bias

    def select_bias(j):
        ks = pl.multiple_of(j * kt, kt)
        return jnp.where(sc_ref[pl.ds(ks, kt), :] >= lo, 0.0, NEG_BIG)

    def values(j, h):
        return avt_ref[0, j, h * V_ROWS:(h + 1) * V_ROWS, :]

    kmax = jnp.max(knt_ref[...], axis=1, keepdims=True)
    shift = [qnt_ref[h:h + 1, :] * kmax[ATT_HEADS + h:ATT_HEADS + h + 1, :] * NORM_SLACK for h in range(ATT_HEADS)]
    for h in range(ATT_HEADS):
        acc_refs[h][...] = jnp.zeros_like(acc_refs[h])

    def attn_tile_fast(j, c):
        bias = select_bias(j)
        for h in range(ATT_HEADS):
            p = jnp.exp2(logits(j, h, bias) - shift[h]).astype(BF16)
            acc_refs[h][...] += _dot(values(j, h), p)
        return c

    lax.fori_loop(0, nkt, attn_tile_fast, 0)
    denom_min = functools.reduce(jnp.minimum, [acc_refs[h][ATT_DIM:ATT_DIM + 1, :] for h in range(ATT_HEADS)])

    @pl.when(jnp.min(denom_min) < DENOM_FLOOR)
    def _():
        for h in range(ATT_HEADS):
            acc_refs[h][...] = jnp.zeros_like(acc_refs[h])
            m_refs[h][...] = jnp.full(m_refs[h].shape, NEG_BIG, F32)

        def attn_tile(j, c):
            bias = select_bias(j)
            for h in range(ATT_HEADS):
                acc_h, m_h = acc_refs[h], m_refs[h]
                lg = logits(j, h, bias)
                m_old = m_h[0:1, :]
                m_new = jnp.maximum(m_old, jnp.max(_fold8(lg, jnp.maximum), axis=0, keepdims=True))
                p = jnp.exp2(lg - m_new).astype(BF16)
                acc_h[...] = jnp.exp2(m_old - m_new) * acc_h[...] + _dot(values(j, h), p)
                m_h[0:1, :] = m_new
            return c

        lax.fori_loop(0, nkt, attn_tile, 0)

    for h in range(ATT_HEADS):
        o_t = acc_refs[h][0:ATT_DIM, :] / acc_refs[h][ATT_DIM:ATT_DIM + 1, :]
        o_ref[:, h * ATT_DIM:(h + 1) * ATT_DIM] = o_t.T.astype(BF16)


def _dsa(iq, iwt, ik, aq, ak, avt, nrm, B, T, qb, kt):
    M = ak.shape[0]
    nq = T // qb
    assert kt % qb == 0 and T % kt == 0 and avt.shape[1:] == (T // kt, ATT_HEADS * V_ROWS, kt)
    topk = min(TOPK_MAX, T // 4)
    qrow = lambda b, i: (b * nq + i, 0)
    brow = lambda b, i: (b, 0)
    kern = functools.partial(_dsa_kernel, qb=qb, kt=kt, topk=topk, seq=T)
    return pl.pallas_call(
        kern,
        grid=(B, nq),
        in_specs=[pl.BlockSpec((IDX_HEADS, qb, IDX_DIM), lambda b, i: (0, b * nq + i, 0)),
                  pl.BlockSpec((IDX_HEADS, qb), lambda b, i: (0, b * nq + i)),
                  _resident((T, IDX_DIM), brow),
                  pl.BlockSpec((qb, GROUP), qrow),
                  _resident((T, GROUP), brow),
                  _resident((1,) + avt.shape[1:], lambda b, i: (b, 0, 0, 0)),
                  pl.BlockSpec((2 * ATT_HEADS, qb), lambda b, i: (0, b * nq + i)),
                  _resident((2 * ATT_HEADS, T), lambda b, i: (0, b))],
        out_specs=pl.BlockSpec((qb, GROUP), qrow),
        out_shape=jax.ShapeDtypeStruct((M, GROUP), BF16),
        scratch_shapes=[pltpu.VMEM((T, qb), F32),
                        pltpu.VMEM((T, qb), BF16),
                        pltpu.VMEM((32, qb), F32),
                        ] + [pltpu.VMEM((V_ROWS, qb), F32)] * ATT_HEADS
                        + [pltpu.VMEM((8, qb), F32)] * ATT_HEADS,
        compiler_params=_cparams(("parallel", "arbitrary")),
        name="dsa",
    )(iq, iwt, ik, aq, ak, avt, nrm, nrm)


def _rms(a, gain):
    return a * lax.rsqrt(jnp.mean(a * a, axis=-1, keepdims=True) + EPS) * gain


def _out_ffn_kernel(ohg_ref, oatt_ref, x_ref, mod_ref, gains_ref, wo_ref, wi_ref, wf_ref, o_ref, acc_ref,
                    *, d_ff, fc):
    mod = mod_ref[0]
    gains = gains_ref[...]
    hw = ohg_ref.shape[1]
    y = _dot(ohg_ref[...], wo_ref[0:hw, :]) + _dot(oatt_ref[...], wo_ref[hw:, :])
    x1 = x_ref[...] + mod[2:3, :] * _rms(y, gains[0:1, :])
    h2 = (_rms(x1, gains[1:2, :]) * (1.0 + mod[4:5, :]) + mod[3:4, :]).astype(BF16)
    for c in range(d_ff // fc):
        gate = _dot(h2, wi_ref[:, c * fc:(c + 1) * fc])
        up = _dot(h2, wi_ref[:, d_ff + c * fc:d_ff + (c + 1) * fc])
        part = _dot((_silu(gate) * up).astype(BF16), wf_ref[c * fc:(c + 1) * fc, :])
        if c == 0:
            acc_ref[...] = part
        else:
            acc_ref[...] += part
    o_ref[...] = x1 + mod[5:6, :] * _rms(acc_ref[...], gains[2:3, :])


def _out_ffn(ohg, oatt, x2, mod3, gains, wo, wi, wf, T, tm):
    M, D = x2.shape
    nt = T // tm
    d_ff = wf.shape[0]
    row = lambda i: (i, 0)
    const = lambda i: (0, 0)
    kern = functools.partial(_out_ffn_kernel, d_ff=d_ff, fc=256)
    return pl.pallas_call(
        kern,
        grid=(M // tm,),
        in_specs=[pl.BlockSpec((tm, GROUP), row),
                  pl.BlockSpec((tm, GROUP), row),
                  pl.BlockSpec((tm, D), row),
                  pl.BlockSpec((1, N_MOD, D), lambda i: (i // nt, 0, 0)),
                  pl.BlockSpec((3, D), const),
                  _resident(wo.shape, const),
                  _resident(wi.shape, const),
                  _resident(wf.shape, const)],
        out_specs=pl.BlockSpec((tm, D), row),
        out_shape=jax.ShapeDtypeStruct((M, D), F32),
        scratch_shapes=[pltpu.VMEM((tm, D), F32)],
        compiler_params=_cparams(("parallel",)),
        name="out_ffn",
    )(ohg, oatt, x2, mod3, gains, wo, wi, wf)


def kernel(x, c, positions, w_ada, b_ada, norm_pre_mix, norm_post_mix, norm_pre_ffn, norm_post_ffn, w_in,
           hgrn_lower_bound, hgrn_out_norm, idx_k_norm_w, idx_k_norm_b, w_out, w_ffn_in, w_ffn_out):
    B, T, D = x.shape
    depth = w_ada.shape[0]
    assert depth == 1 and hgrn_lower_bound.shape[0] == 2
    assert T % 512 == 0 and D % LANES == 0
    tm = 512

    mod3 = _adaln(c, w_ada[0], b_ada[0]).reshape(B, N_MOD, D)
    tab = _rope_table(positions)

    main = 8 * GROUP
    w = w_in[0]
    w_main = w[:, :main].astype(BF16)
    w_tail = jnp.concatenate([jnp.pad(w[:, main:main + IDX_DIM], ((0, 0), (0, LANES - IDX_DIM))),
                              jnp.pad(w[:, main + IDX_DIM:], ((0, 0), (0, LANES - IDX_HEADS)))], axis=1).astype(BF16)
    lnw = jnp.pad(idx_k_norm_w[0], (0, LANES - IDX_DIM)).reshape(1, LANES)
    lnb = jnp.pad(idx_k_norm_b[0], (0, LANES - IDX_DIM)).reshape(1, LANES)

    x2 = x.reshape(B * T, D)
    hq, lf, hk, hv, hg, aq, ak, avt, iq, ik, iwt, nrm = _in_proj(
        x2, mod3, norm_pre_mix, w_main, w_tail, hgrn_lower_bound, lnw, lnb, tab, T, tm)

    o_hg = _hgrn(hq, lf, hk, hv, hg, hgrn_out_norm, B, T, 512)
    o_att = _dsa(iq, iwt, ik, aq, ak, avt, nrm, B, T, 512, tm)

    gains = jnp.concatenate([norm_post_mix, norm_pre_ffn, norm_post_ffn], axis=0)
    out = _out_ffn(o_hg, o_att, x2, mod3, gains, w_out[0].astype(BF16), w_ffn_in[0].astype(BF16),
                   w_ffn_out[0].astype(BF16), T, tm)
    return out.reshape(B, T, D)
```

```python
import functools

import numpy as np
import jax
import jax.numpy as jnp
from jax import lax
from jax.experimental import pallas as pl
from jax.experimental.pallas import tpu as pltpu

F32 = jnp.float32
BF16 = jnp.bfloat16

HG_HEADS = 4
HG_DIM = 128
ATT_HEADS = 4
ATT_DIM = 128
IDX_HEADS = 8
IDX_DIM = 64
TOPK_MAX = 256
ROPE_THETA = 500000.0
ROPE_FRACTION = 4
N_MOD = 6
EPS = 1e-6

GROUP = 512
LANES = 128
VMEM_LIMIT = 56 * 1024 * 1024

HG_CHUNK = 128
HG_LEVELS = (64, 32, 16, 8, 4, 2, 1)

NEG_BIG = -1e30
DENOM_FLOOR = 2.0 ** -64
NORM_SLACK = 1.01
COARSE_PASSES = 10
FINE_PASSES = 10
COUNT_SLAB = 64

ROW_TILE = 512
KEY_TILE = 512
QUERY_BLOCK = 512
SCORE_SLAB = KEY_TILE
FFN_CHUNK = 256
Q_SCALE = ATT_DIM ** -0.5 * 1.4426950408889634
V_ROWS = ATT_DIM + 16


def _cparams(sem):
    return pltpu.CompilerParams(dimension_semantics=sem, vmem_limit_bytes=VMEM_LIMIT)


def _resident(shape, index_map):
    return pl.BlockSpec(shape, index_map, pipeline_mode=pl.Buffered(1))


def _split_bf16(a):
    hi = a.astype(BF16)
    lo = (a - hi.astype(F32)).astype(BF16)
    return hi, lo


def _dot(a, b):
    return jnp.dot(a, b, preferred_element_type=F32)


def _dot_nt(a, b):
    return lax.dot_general(a, b, (((1,), (1,)), ((), ())), preferred_element_type=F32)


def _silu(a):
    return a * jax.nn.sigmoid(a)


def _adaln_kernel(c_ref, w_ref, b_ref, o_ref):
    a = _silu(c_ref[...])
    a_hi, a_lo = _split_bf16(a)
    w_hi, w_lo = _split_bf16(w_ref[...])
    acc = _dot(a_hi, w_hi) + (_dot(a_hi, w_lo) + _dot(a_lo, w_hi))
    o_ref[...] = acc + b_ref[...]


def _adaln(c, w, b):
    B, D = c.shape
    N = w.shape[1]
    return pl.pallas_call(
        _adaln_kernel,
        grid=(N // D,),
        in_specs=[pl.BlockSpec((B, D), lambda j: (0, 0)),
                  pl.BlockSpec((D, D), lambda j: (0, j)),
                  pl.BlockSpec((1, D), lambda j: (0, j))],
        out_specs=pl.BlockSpec((B, D), lambda j: (0, j)),
        out_shape=jax.ShapeDtypeStruct((B, N), F32),
        compiler_params=_cparams(("arbitrary",)),
        name="adaln",
    )(c, w, b.reshape(1, N))


def _rope_table_kernel(pos_ref, fa_ref, fb_ref, o_ref):
    pos = pos_ref[0].astype(F32)
    ang_a = fa_ref[...] * pos
    ang_b = fb_ref[...] * pos
    ha = fa_ref.shape[0]
    hb = fb_ref.shape[0]
    o_ref[0, 0:ha, :] = jnp.cos(ang_a)
    o_ref[0, ha:2 * ha, :] = jnp.sin(ang_a)
    o_ref[0, 2 * ha:2 * ha + hb, :] = jnp.cos(ang_b)
    o_ref[0, 2 * ha + hb:2 * ha + 2 * hb, :] = jnp.sin(ang_b)
    o_ref[0, 2 * ha + 2 * hb:, :] = jnp.zeros((LANES - 2 * ha - 2 * hb, pos.shape[1]), F32)


ROPE_HALF_A = ATT_DIM // ROPE_FRACTION // 2
ROPE_HALF_B = IDX_DIM // ROPE_FRACTION // 2


def _rope_table(positions):
    B, T = positions.shape
    ha, hb = ROPE_HALF_A, ROPE_HALF_B
    fa = (ROPE_THETA ** (-jnp.arange(ha, dtype=F32) / ha)).reshape(ha, 1)
    fb = (ROPE_THETA ** (-jnp.arange(hb, dtype=F32) / hb)).reshape(hb, 1)
    return pl.pallas_call(
        _rope_table_kernel,
        grid=(B,),
        in_specs=[pl.BlockSpec((1, 1, T), lambda b: (b, 0, 0)),
                  pl.BlockSpec((ha, 1), lambda b: (0, 0)),
                  pl.BlockSpec((hb, 1), lambda b: (0, 0))],
        out_specs=pl.BlockSpec((1, LANES, T), lambda b: (b, 0, 0)),
        out_shape=jax.ShapeDtypeStruct((B, LANES, T), F32),
        compiler_params=_cparams(("arbitrary",)),
        name="rope_table",
    )(positions.reshape(B, 1, T), fa, fb)


def _rope_patterns(tab):
    ha, hb = ROPE_HALF_A, ROPE_HALF_B
    lane = lax.broadcasted_iota(jnp.int32, tab.shape, 1)
    shifted = lambda sh: pltpu.roll(tab, sh % LANES, axis=1)
    cos_a = jnp.where(lane < ha, tab, jnp.where(lane < 2 * ha, shifted(ha), 1.0))
    sin_a = jnp.where(lane < ha, -shifted(-ha), jnp.where(lane < 2 * ha, tab, 0.0))
    l64 = lane & (IDX_DIM - 1)
    first = lane < IDX_DIM
    cb0, sb0 = 2 * ha, 2 * ha + hb
    cos_b = jnp.where(l64 < hb, jnp.where(first, shifted(-cb0), shifted(IDX_DIM - cb0)),
                      jnp.where(l64 < 2 * hb, jnp.where(first, shifted(hb - cb0), shifted(IDX_DIM + hb - cb0)), 1.0))
    sin_b = jnp.where(l64 < hb, -jnp.where(first, shifted(-sb0), shifted(IDX_DIM - sb0)),
                      jnp.where(l64 < 2 * hb, jnp.where(first, shifted(hb - sb0), shifted(IDX_DIM + hb - sb0)), 0.0))
    return cos_a, sin_a, cos_b, sin_b


def _rope(xb, cos, sin, half, period):
    lane = lax.broadcasted_iota(jnp.int32, xb.shape, 1)
    fwd = pltpu.roll(xb, LANES - half, axis=1)
    bwd = pltpu.roll(xb, half, axis=1)
    partner = jnp.where((lane & (period - 1)) < half, fwd, bwd)
    return xb * cos + partner * sin


def _in_proj_kernel(x_ref, mod_ref, gain_ref, w_ref, wt_ref, lbp_ref, lnw_ref, lnb_ref, tab_ref,
                    hq_ref, lf_ref, hk_ref, hv_ref, hg_ref,
                    aq_ref, ak_ref, avt_ref, iq_ref, ik_ref, iwt_ref, nrm_ref):
    x = x_ref[...]
    ms = jnp.mean(x * x, axis=-1, keepdims=True)
    mod = mod_ref[0]
    h = x * lax.rsqrt(ms + EPS) * gain_ref[...]
    h = h * (1.0 + mod[1:2, :]) + mod[0:1, :]
    hb = h.astype(BF16)

    def proj(g, width=GROUP):
        return _dot(hb, w_ref[:, g * GROUP:g * GROUP + width])

    hq_ref[...] = (_silu(proj(0)) * (HG_DIM ** -0.5)).astype(BF16)
    a = lbp_ref[...]
    amax = jnp.max(a, axis=0, keepdims=True)
    e = jnp.exp(a - amax)
    lb = e[0:1, :] / jnp.sum(e, axis=0, keepdims=True)
    fr = proj(1)
    f = lb + (1.0 - lb) * jax.nn.sigmoid(fr)
    lf_ref[...] = jnp.log2(f).astype(BF16)
    hk_ref[...] = ((1.0 - lb) * jax.nn.sigmoid(-fr)).astype(BF16)
    hv_ref[...] = proj(2).astype(BF16)
    hg_ref[...] = _silu(proj(3)).astype(BF16)

    cosa, sina, cosb, sinb = _rope_patterns(tab_ref[0].T)
    half_a = ROPE_HALF_A
    q = proj(4)
    k = proj(5)
    lane_a = lax.broadcasted_iota(jnp.int32, cosa.shape, 1)
    norms = jnp.zeros(cosa.shape, F32)
    for hh in range(ATT_HEADS):
        sl = slice(hh * ATT_DIM, (hh + 1) * ATT_DIM)
        qh = (_rope(q[:, sl], cosa, sina, half_a, ATT_DIM) * Q_SCALE).astype(BF16)
        kh = _rope(k[:, sl], cosa, sina, half_a, ATT_DIM).astype(BF16)
        aq_ref[:, sl] = qh
        ak_ref[:, sl] = kh
        for slot, a in ((hh, qh), (ATT_HEADS + hh, kh)):
            norms = jnp.where(lane_a == slot, jnp.sqrt(_dot(a * a, jnp.ones((ATT_DIM, LANES), BF16))), norms)
    nrm_ref[...] = norms.T[0:2 * ATT_HEADS, :]
    v = proj(6)
    for hh in range(ATT_HEADS):
        r0 = hh * V_ROWS
        avt_ref[0, 0, r0:r0 + ATT_DIM, :] = v[:, hh * ATT_DIM:(hh + 1) * ATT_DIM].T.astype(BF16)
        avt_ref[0, 0, r0 + ATT_DIM:r0 + V_ROWS, :] = jnp.ones((V_ROWS - ATT_DIM, v.shape[0]), BF16)

    half_b = ROPE_HALF_B
    qi = proj(7)
    for cc in range(GROUP // LANES):
        qr = _rope(qi[:, cc * LANES:(cc + 1) * LANES], cosb, sinb, half_b, IDX_DIM).astype(BF16)
        for hh in range(LANES // IDX_DIM):
            iq_ref[cc * (LANES // IDX_DIM) + hh] = qr[:, hh * IDX_DIM:(hh + 1) * IDX_DIM]
    kw = _dot(hb, wt_ref[...])
    ki = kw[:, 0:LANES]
    lane = lax.broadcasted_iota(jnp.int32, ki.shape, 1)
    real = lane < IDX_DIM
    mu = jnp.sum(ki, axis=-1, keepdims=True) * (1.0 / IDX_DIM)
    d = jnp.where(real, ki - mu, 0.0)
    var = jnp.sum(d * d, axis=-1, keepdims=True) * (1.0 / IDX_DIM)
    kn = d * lax.rsqrt(var + EPS) * lnw_ref[...] + lnb_ref[...]
    kn = _rope(kn, cosb, sinb, half_b, IDX_DIM)
    ik_ref[...] = kn[:, 0:IDX_DIM].astype(BF16)
    iwt_ref[...] = kw[:, LANES:2 * LANES].T[0:IDX_HEADS, :] * (IDX_HEADS ** -0.5 * IDX_DIM ** -0.5)


def _in_proj(x2, mod3, gain, w_main, w_tail, lbp, lnw, lnb, tab, T, tm, kt):
    M, D = x2.shape
    nt = T // tm
    row = lambda i: (i, 0)
    const = lambda i: (0, 0)
    bf16o = jax.ShapeDtypeStruct((M, GROUP), BF16)
    grp_spec = pl.BlockSpec((tm, GROUP), row)
    return pl.pallas_call(
        _in_proj_kernel,
        grid=(M // tm,),
        in_specs=[pl.BlockSpec((tm, D), row),
                  pl.BlockSpec((1, N_MOD, D), lambda i: (i // nt, 0, 0)),
                  pl.BlockSpec((1, D), const),
                  _resident(w_main.shape, const),
                  _resident(w_tail.shape, const),
                  pl.BlockSpec(lbp.shape, const),
                  pl.BlockSpec((1, LANES), const),
                  pl.BlockSpec((1, LANES), const),
                  pl.BlockSpec((1, LANES, tm), lambda i: (i // nt, 0, i % nt))],
        out_specs=[grp_spec] * 7 + [pl.BlockSpec((1, 1, ATT_HEADS * V_ROWS, tm),
                                                 lambda i: (i // nt, (i % nt) // (kt // tm), 0, (i % nt) % (kt // tm))),
                                    pl.BlockSpec((IDX_HEADS, tm, IDX_DIM), lambda i: (0, i, 0)),
                                    pl.BlockSpec((tm, IDX_DIM), row),
                                    pl.BlockSpec((IDX_HEADS, tm), lambda i: (0, i)),
                                    pl.BlockSpec((2 * ATT_HEADS, tm), lambda i: (0, i))],
        out_shape=[bf16o] * 7 + [jax.ShapeDtypeStruct((M // T, T // kt, ATT_HEADS * V_ROWS, kt), BF16),
                                 jax.ShapeDtypeStruct((IDX_HEADS, M, IDX_DIM), BF16),
                                 jax.ShapeDtypeStruct((M, IDX_DIM), BF16),
                                 jax.ShapeDtypeStruct((IDX_HEADS, M), F32),
                                 jax.ShapeDtypeStruct((2 * ATT_HEADS, M), F32)],
        compiler_params=_cparams(("parallel",)),
        name="in_proj",
    )(x2, mod3, gain, w_main, w_tail, lbp, lnw, lnb, tab)


def _hgrn_consts():
    C = HG_CHUNK
    t = np.arange(C)
    tri = (t[None, :] <= t[:, None]).astype(np.float32)
    blocks = [tri]
    for m in HG_LEVELS:
        if m % 8:
            split = (t // (2 * m)) * (2 * m) + m - 1
            upper = ((t & m) != 0)[:, None]
            blocks.append(np.where(upper, tri - tri[split], tri[split] - tri))
    mall = np.concatenate(blocks, axis=0)
    x = t[:, None] ^ t[None, :]
    lvl = np.full((C, C), len(HG_LEVELS) + 1, np.int32)
    for li, m in enumerate(HG_LEVELS):
        lvl[(t[:, None] > t[None, :]) & (x >= m) & (x < 2 * m)] = li
    lvl[t[:, None] == t[None, :]] = len(HG_LEVELS)
    return jnp.asarray(mall, BF16), jnp.asarray(lvl)


def _hgrn_kernel(hq_ref, lf_ref, hk_ref, hv_ref, hg_ref, mall_ref, lvl_ref, onorm_ref, o_ref, st_ref):
    C = HG_CHUNK
    nl = len(HG_LEVELS)

    @pl.when(pl.program_id(1) == 0)
    def _():
        st_ref[...] = jnp.zeros_like(st_ref)

    lvl = lvl_ref[...]
    row = lax.broadcasted_iota(jnp.int32, (C, HG_DIM), 0)
    mall = mall_ref[...]

    for ci in range(hq_ref.shape[0] // C):
        rs = slice(ci * C, (ci + 1) * C)
        for h in range(HG_HEADS):
            cs = slice(h * HG_DIM, (h + 1) * HG_DIM)
            q, k, v = hq_ref[rs, cs], hk_ref[rs, cs], hv_ref[rs, cs]
            qf, kf = q.astype(F32), k.astype(F32)
            dall = _dot(mall, lf_ref[rs, cs])
            g = dall[0:C]

            a = jnp.zeros((C, C), F32)
            fine = 0
            for li, m in enumerate(HG_LEVELS):
                if m % 8 == 0:
                    qk = jnp.concatenate([(qf if (b & 1) else kf)[b * m:(b + 1) * m] for b in range(C // m)], axis=0)
                    parts = []
                    for b in range(C // (2 * m)):
                        lo_rows = slice(2 * b * m, (2 * b + 1) * m)
                        hi_rows = slice((2 * b + 1) * m, (2 * b + 2) * m)
                        ref = jnp.broadcast_to(g[lo_rows.stop - 1:lo_rows.stop, :], (m, HG_DIM))
                        parts += [ref - g[lo_rows], g[hi_rows] - ref]
                    dm = jnp.concatenate(parts, axis=0)
                else:
                    fine += 1
                    qk = jnp.where((row & m) != 0, qf, kf)
                    dm = dall[fine * C:(fine + 1) * C]
                xm = (qk * jnp.exp2(dm)).astype(BF16)
                a = jnp.where(lvl == li, _dot_nt(xm, xm), a)
            a = jnp.where(lvl == nl, _dot_nt(q, k), a)

            st = st_ref[h]
            o = _dot_nt((qf * jnp.exp2(g)).astype(BF16), st.astype(BF16)) + _dot(a.astype(BF16), v)
            g_last = g[C - 1:C, :]
            kd = (kf * jnp.exp2(g_last - g)).astype(BF16)
            st_ref[h] = jnp.exp2(g_last) * st + _dot(v.astype(F32).T.astype(BF16), kd)

            o = o * lax.rsqrt(jnp.mean(o * o, axis=-1, keepdims=True) + EPS)
            o_ref[rs, cs] = (o * onorm_ref[:, cs] * hg_ref[rs, cs]).astype(BF16)


def _hgrn(hq, lf, hk, hv, hg, onorm, B, T, ct):
    M = hq.shape[0]
    nct = T // ct
    mall, lvl = _hgrn_consts()
    blk = pl.BlockSpec((ct, GROUP), lambda b, c: (b * nct + c, 0))
    const = lambda b, c: (0, 0)
    return pl.pallas_call(
        _hgrn_kernel,
        grid=(B, nct),
        in_specs=[blk] * 5 + [pl.BlockSpec(mall.shape, const), pl.BlockSpec(lvl.shape, const),
                              pl.BlockSpec((1, GROUP), const)],
        out_specs=blk,
        out_shape=jax.ShapeDtypeStruct((M, GROUP), BF16),
        scratch_shapes=[pltpu.VMEM((HG_HEADS, HG_DIM, HG_DIM), F32)],
        compiler_params=_cparams(("parallel", "arbitrary")),
        name="hgrn2",
    )(hq, lf, hk, hv, hg, mall, lvl, onorm)


def _fold8(a, op, rows=8):
    chains = [None] * 4
    for r in range(a.shape[0] // rows):
        part = a[r * rows:(r + 1) * rows, :]
        c = r % len(chains)
        chains[c] = part if chains[c] is None else op(chains[c], part)
    return op(op(chains[0], chains[1]), op(chains[2], chains[3]))


def _floor_bf16(a):
    r = a.astype(BF16).astype(F32)
    below = (r - jnp.abs(r) * (5.0 / 1024.0)).astype(BF16)
    return jnp.where(r > a, below.astype(F32), r).astype(BF16)


def _dsa_kernel(iq_ref, iwt_ref, ik_ref, aq_ref, ak_ref, avt_ref, qnt_ref, knt_ref, o_ref,
                sc_ref, scb_ref, mm_ref, *head_refs, qb, kt, topk, seq):
    acc_refs, m_refs = head_refs[:ATT_HEADS], head_refs[ATT_HEADS:]
    i = pl.program_id(1)
    q0 = i * qb
    nfull = q0 // kt
    nkt = nfull + 1
    qpos = q0 + lax.broadcasted_iota(jnp.int32, (1, qb), 1)
    kk = jnp.minimum(qpos + 1, topk).astype(F32)

    mm_ref[0:8, :] = jnp.full((8, qb), jnp.inf, F32)
    mm_ref[8:16, :] = jnp.full((8, qb), -jnp.inf, F32)
    mm_ref[16:32, :] = jnp.zeros((16, qb), F32)

    def score_tile(j, diag):
        for r in range(kt // SCORE_SLAB):
            ks = pl.multiple_of(j * kt + r * SCORE_SLAB, SCORE_SLAB)
            rows = pl.ds(ks, SCORE_SLAB)
            ki = ik_ref[rows, :]
            s = None
            for h in range(IDX_HEADS):
                t = jnp.maximum(_dot_nt(ki, iq_ref[h]), 0.0) * iwt_ref[h:h + 1, :]
                s = t if s is None else s + t
            if diag:
                causal = ks + lax.broadcasted_iota(jnp.int32, (SCORE_SLAB, qb), 0) <= qpos
                s_hi = jnp.where(causal, s, -jnp.inf)
                s_lo = jnp.where(causal, s, jnp.inf)
            else:
                s_hi = s_lo = s
            sc_ref[rows, :] = s_hi
            scb_ref[rows, :] = _floor_bf16(s_hi)
            mm_ref[0:8, :] = jnp.minimum(mm_ref[0:8, :], _fold8(s_lo, jnp.minimum))
            mm_ref[8:16, :] = jnp.maximum(mm_ref[8:16, :], _fold8(s_hi, jnp.maximum))
            mm_ref[16:24, :] += _fold8(jnp.where(s_hi >= 0.0, 1.0, 0.0), jnp.add)
            mm_ref[24:32, :] += _fold8(jnp.where(s_hi > 0.0, 1.0, 0.0), jnp.add)

    def full_tile(j, c):
        score_tile(j, False)
        return c

    lax.fori_loop(0, nfull, full_tile, 0)
    score_tile(nfull, True)
    mn = jnp.min(mm_ref[0:8, :], axis=0, keepdims=True)
    mx = jnp.max(mm_ref[8:16, :], axis=0, keepdims=True)
    c_nonneg = jnp.sum(mm_ref[16:24, :], axis=0, keepdims=True)
    c_pos = jnp.sum(mm_ref[24:32, :], axis=0, keepdims=True)

    def tile_hits(ref, j, hit_fn, slab, rows):
        acc = None
        for r in range(kt // slab):
            ks = pl.multiple_of(j * kt + r * slab, slab)
            part = _fold8(hit_fn(ref[pl.ds(ks, slab), :]), jnp.add, rows=rows)
            acc = part if acc is None else acc + part
        return acc

    def count_ge(th):
        def body(j, acc):
            return acc + tile_hits(sc_ref, j, lambda s: jnp.where(s >= th, 1.0, 0.0), COUNT_SLAB, 8)
        acc = lax.fori_loop(0, nkt, body, jnp.zeros((8, qb), F32))
        return jnp.sum(acc, axis=0, keepdims=True)

    above = c_pos >= kk
    below = c_nonneg < kk
    live = above | below
    zero = jnp.zeros((1, qb), F32)
    lo0 = jnp.where(below, mn, zero)
    cnt_lo0 = jnp.where(below, (qpos + 1).astype(F32), c_nonneg)
    hi0 = jnp.where(above, mx + jnp.abs(mx) * 1e-6 + 1e-30, zero)
    cnt_hi0 = jnp.where(above, zero, jnp.where(below, c_nonneg, c_pos))

    def midpoint(lo, hi):
        return lo + 0.5 * (hi - lo)

    def unresolved(lo, hi, cnt_lo):
        mid = midpoint(lo, hi)
        return jnp.max((live & (cnt_lo != kk) & (mid > lo) & (mid < hi)).astype(jnp.int32)) > 0

    assert kt // 16 <= 256

    def count_ge_bf16(th):
        def body(j, acc):
            hit = lambda s: jnp.where(s >= th, jnp.ones((), BF16), jnp.zeros((), BF16))
            part = tile_hits(scb_ref, j, hit, 2 * COUNT_SLAB, 16).astype(F32)
            return acc + part[0:8, :] + part[8:16, :]
        acc = lax.fori_loop(0, nkt, body, jnp.zeros((8, qb), F32))
        return jnp.sum(acc, axis=0, keepdims=True)

    def coarse_body(_, c):
        lo, hi, cnt_lo, cnt_hi = c
        th = midpoint(lo, hi).astype(BF16)
        mid = th.astype(F32)
        inside = live & (mid > lo) & (mid < hi)
        cnt = count_ge_bf16(th)
        up = inside & (cnt >= kk)
        dn = inside & (cnt < kk)
        return (jnp.where(up, mid, lo), jnp.where(dn, mid, hi),
                jnp.where(up, cnt, cnt_lo), jnp.where(dn, cnt, cnt_hi))

    lo1, hi1, cnt_lo1, cnt_hi1 = lax.fori_loop(0, COARSE_PASSES, coarse_body, (lo0, hi0, cnt_lo0, cnt_hi0))

    def fine_pass(c):
        lo, hi, cnt_lo, cnt_hi = c
        mid = midpoint(lo, hi)
        cnt = count_ge(mid)
        up = live & (cnt >= kk)
        dn = live & (cnt < kk)
        return (jnp.where(up, mid, lo), jnp.where(dn, mid, hi), jnp.where(up, cnt, cnt_lo), jnp.where(dn, cnt, cnt_hi))

    c2 = lax.fori_loop(0, FINE_PASSES, lambda _, c: fine_pass(c), (lo1, hi1, cnt_lo1, cnt_hi1))

    def bis_cond(c):
        it, (lo, hi, cnt_lo, _) = c
        return (it < 320) & unresolved(lo, hi, cnt_lo)

    _, (lo, hi, cnt_lo, cnt_hi) = lax.while_loop(bis_cond, lambda c: (c[0] + 1, fine_pass(c[1])), (jnp.int32(0), c2))

    tied = cnt_lo != kk
    need = kk - cnt_hi

    def resolve_ties(cs):
        lo_c, need_c, tied_c = lo[:, cs], need[:, cs], tied[:, cs]

        def tie_keys(j):
            ks = pl.multiple_of(j * kt, kt)
            rows = pl.ds(ks, kt)
            return rows, sc_ref[rows, cs], ks + lax.broadcasted_iota(jnp.int32, (kt, LANES), 0)

        def count_tie(jm):
            def body(j, acc):
                _, s, kpos = tie_keys(j)
                return acc + _fold8(jnp.where((s == lo_c) & (kpos <= jm), 1.0, 0.0), jnp.add)
            acc = lax.fori_loop(0, nkt, body, jnp.zeros((8, LANES), F32))
            return jnp.sum(acc, axis=0, keepdims=True)

        def tie_body(_, c):
            jlo, jhi = c
            jm = (jlo + jhi) // 2
            ok = count_tie(jm) >= need_c
            return jnp.where(ok, jlo, jm), jnp.where(ok, jm, jhi)

        steps = int(np.ceil(np.log2(seq))) + 1
        _, jhi = lax.fori_loop(0, steps, tie_body,
                               (jnp.full((1, LANES), -1, jnp.int32), jnp.full((1, LANES), seq - 1, jnp.int32)))
        jmax = jnp.where(tied_c, jhi, seq)

        def strike(j, c):
            rows, s, kpos = tie_keys(j)
            sc_ref[rows, cs] = jnp.where((s == lo_c) & (kpos > jmax), -jnp.inf, s)
            return c

        lax.fori_loop(0, nkt, strike, 0)

    for c in range(qb // LANES):
        cs = slice(c * LANES, (c + 1) * LANES)
        pl.when(jnp.max(tied[:, cs].astype(jnp.int32)) > 0)(functools.partial(resolve_ties, cs))

    def logits(j, h, bias):
        ks = pl.multiple_of(j * kt, kt)
        sl = slice(h * ATT_DIM, (h + 1) * ATT_DIM)
        return _dot_nt(ak_ref[pl.ds(ks, kt), sl], aq_ref[:, sl]) + bias

    def select_bias(j):
        ks = pl.multiple_of(j * kt, kt)
        return jnp.where(sc_ref[pl.ds(ks, kt), :] >= lo, 0.0, NEG_BIG)

    def values(j, h):
        return avt_ref[0, j, h * V_ROWS:(h + 1) * V_ROWS, :]

    kmax = jnp.max(knt_ref[...], axis=1, keepdims=True)
    shift = [qnt_ref[h:h + 1, :] * kmax[ATT_HEADS + h:ATT_HEADS + h + 1, :] * NORM_SLACK for h in range(ATT_HEADS)]
    for h in range(ATT_HEADS):
        acc_refs[h][...] = jnp.zeros_like(acc_refs[h])

    def attn_tile_fast(j, c):
        bias = select_bias(j)
        for h in range(ATT_HEADS):
            p = jnp.exp2(logits(j, h, bias) - shift[h]).astype(BF16)
            acc_refs[h][...] += _dot(values(j, h), p)
        return c

    lax.fori_loop(0, nkt, attn_tile_fast, 0)
    denom_min = functools.reduce(jnp.minimum, [acc_refs[h][ATT_DIM:ATT_DIM + 1, :] for h in range(ATT_HEADS)])

    @pl.when(jnp.min(denom_min) < DENOM_FLOOR)
    def _():
        for h in range(ATT_HEADS):
            acc_refs[h][...] = jnp.zeros_like(acc_refs[h])
            m_refs[h][...] = jnp.full(m_refs[h].shape, NEG_BIG, F32)

        def attn_tile(j, c):
            bias = select_bias(j)
            for h in range(ATT_HEADS):
                acc_h, m_h = acc_refs[h], m_refs[h]
                lg = logits(j, h, bias)
                m_old = m_h[0:1, :]
                m_new = jnp.maximum(m_old, jnp.max(_fold8(lg, jnp.maximum), axis=0, keepdims=True))
                p = jnp.exp2(lg - m_new).astype(BF16)
                acc_h[...] = jnp.exp2(m_old - m_new) * acc_h[...] + _dot(values(j, h), p)
                m_h[0:1, :] = m_new
            return c

        lax.fori_loop(0, nkt, attn_tile, 0)

    for h in range(ATT_HEADS):
        o_t = acc_refs[h][0:ATT_DIM, :] / acc_refs[h][ATT_DIM:ATT_DIM + 1, :]
        o_ref[:, h * ATT_DIM:(h + 1) * ATT_DIM] = o_t.T.astype(BF16)


def _dsa(iq, iwt, ik, aq, ak, avt, nrm, B, T, qb, kt):
    M = ak.shape[0]
    nq = T // qb
    assert kt % qb == 0 and T % kt == 0 and avt.shape[1:] == (T // kt, ATT_HEADS * V_ROWS, kt)
    topk = min(TOPK_MAX, T // 4)
    qrow = lambda b, i: (b * nq + i, 0)
    brow = lambda b, i: (b, 0)
    kern = functools.partial(_dsa_kernel, qb=qb, kt=kt, topk=topk, seq=T)
    return pl.pallas_call(
        kern,
        grid=(B, nq),
        in_specs=[pl.BlockSpec((IDX_HEADS, qb, IDX_DIM), lambda b, i: (0, b * nq + i, 0)),
                  pl.BlockSpec((IDX_HEADS, qb), lambda b, i: (0, b * nq + i)),
                  _resident((T, IDX_DIM), brow),
                  pl.BlockSpec((qb, GROUP), qrow),
                  _resident((T, GROUP), brow),
                  _resident((1,) + avt.shape[1:], lambda b, i: (b, 0, 0, 0)),
                  pl.BlockSpec((2 * ATT_HEADS, qb), lambda b, i: (0, b * nq + i)),
                  _resident((2 * ATT_HEADS, T), lambda b, i: (0, b))],
        out_specs=pl.BlockSpec((qb, GROUP), qrow),
        out_shape=jax.ShapeDtypeStruct((M, GROUP), BF16),
        scratch_shapes=[pltpu.VMEM((T, qb), F32),
                        pltpu.VMEM((T, qb), BF16),
                        pltpu.VMEM((32, qb), F32),
                        ] + [pltpu.VMEM((V_ROWS, qb), F32)] * ATT_HEADS
                        + [pltpu.VMEM((8, qb), F32)] * ATT_HEADS,
        compiler_params=_cparams(("parallel", "arbitrary")),
        name="dsa",
    )(iq, iwt, ik, aq, ak, avt, nrm, nrm)


def _rms(a, gain):
    return a * lax.rsqrt(jnp.mean(a * a, axis=-1, keepdims=True) + EPS) * gain


def _out_ffn_kernel(ohg_ref, oatt_ref, x_ref, mod_ref, gains_ref, wo_ref, wi_ref, wf_ref, o_ref, acc_ref,
                    *, d_ff, fc):
    mod = mod_ref[0]
    gains = gains_ref[...]
    hw = ohg_ref.shape[1]
    y = _dot(ohg_ref[...], wo_ref[0:hw, :]) + _dot(oatt_ref[...], wo_ref[hw:, :])
    x1 = x_ref[...] + mod[2:3, :] * _rms(y, gains[0:1, :])
    h2 = (_rms(x1, gains[1:2, :]) * (1.0 + mod[4:5, :]) + mod[3:4, :]).astype(BF16)
    for c in range(d_ff // fc):
        gate = _dot(h2, wi_ref[:, c * fc:(c + 1) * fc])
        up = _dot(h2, wi_ref[:, d_ff + c * fc:d_ff + (c + 1) * fc])
        part = _dot((_silu(gate) * up).astype(BF16), wf_ref[c * fc:(c + 1) * fc, :])
        if c == 0:
            acc_ref[...] = part
        else:
            acc_ref[...] += part
    o_ref[...] = x1 + mod[5:6, :] * _rms(acc_ref[...], gains[2:3, :])


def _out_ffn(ohg, oatt, x2, mod3, gains, wo, wi, wf, T, tm):
    M, D = x2.shape
    nt = T // tm
    d_ff = wf.shape[0]
    row = lambda i: (i, 0)
    const = lambda i: (0, 0)
    assert d_ff % FFN_CHUNK == 0
    kern = functools.partial(_out_ffn_kernel, d_ff=d_ff, fc=FFN_CHUNK)
    return pl.pallas_call(
        kern,
        grid=(M // tm,),
        in_specs=[pl.BlockSpec((tm, GROUP), row),
                  pl.BlockSpec((tm, GROUP), row),
                  pl.BlockSpec((tm, D), row),
                  pl.BlockSpec((1, N_MOD, D), lambda i: (i // nt, 0, 0)),
                  pl.BlockSpec((3, D), const),
                  _resident(wo.shape, const),
                  _resident(wi.shape, const),
                  _resident(wf.shape, const)],
        out_specs=pl.BlockSpec((tm, D), row),
        out_shape=jax.ShapeDtypeStruct((M, D), F32),
        scratch_shapes=[pltpu.VMEM((tm, D), F32)],
        compiler_params=_cparams(("parallel",)),
        name="out_ffn",
    )(ohg, oatt, x2, mod3, gains, wo, wi, wf)


def kernel(x, c, positions, w_ada, b_ada, norm_pre_mix, norm_post_mix, norm_pre_ffn, norm_post_ffn, w_in,
           hgrn_lower_bound, hgrn_out_norm, idx_k_norm_w, idx_k_norm_b, w_out, w_ffn_in, w_ffn_out):
    B, T, D = x.shape
    depth = w_ada.shape[0]
    assert depth == 1 and hgrn_lower_bound.shape[0] == 2
    tm = ROW_TILE
    assert T % tm == 0 and T % KEY_TILE == 0 and T % QUERY_BLOCK == 0 and D % LANES == 0 and tm == KEY_TILE

    mod3 = _adaln(c, w_ada[0], b_ada[0]).reshape(B, N_MOD, D)
    tab = _rope_table(positions)

    main = 8 * GROUP
    w = w_in[0]
    w_main = w[:, :main].astype(BF16)
    w_tail = jnp.concatenate([jnp.pad(w[:, main:main + IDX_DIM], ((0, 0), (0, LANES - IDX_DIM))),
                              jnp.pad(w[:, main + IDX_DIM:], ((0, 0), (0, LANES - IDX_HEADS)))], axis=1).astype(BF16)
    lnw = jnp.pad(idx_k_norm_w[0], (0, LANES - IDX_DIM)).reshape(1, LANES)
    lnb = jnp.pad(idx_k_norm_b[0], (0, LANES - IDX_DIM)).reshape(1, LANES)

    x2 = x.reshape(B * T, D)
    hq, lf, hk, hv, hg, aq, ak, avt, iq, ik, iwt, nrm = _in_proj(
        x2, mod3, norm_pre_mix, w_main, w_tail, hgrn_lower_bound, lnw, lnb, tab, T, tm, tm)

    o_hg = _hgrn(hq, lf, hk, hv, hg, hgrn_out_norm, B, T, tm)
    o_att = _dsa(iq, iwt, ik, aq, ak, avt, nrm, B, T, QUERY_BLOCK, KEY_TILE)

    gains = jnp.concatenate([norm_post_mix, norm_pre_ffn, norm_post_ffn], axis=0)
    out = _out_ffn(o_hg, o_att, x2, mod3, gains, w_out[0].astype(BF16), w_ffn_in[0].astype(BF16),
                   w_ffn_out[0].astype(BF16), T, tm)
    return out.reshape(B, T, D)
```

```python
import functools

import numpy as np
import jax
import jax.numpy as jnp
from jax import lax
from jax.experimental import pallas as pl
from jax.experimental.pallas import tpu as pltpu

F32 = jnp.float32
BF16 = jnp.bfloat16

HG_HEADS = 4
HG_DIM = 128
ATT_HEADS = 4
ATT_DIM = 128
IDX_HEADS = 8
IDX_DIM = 64
TOPK_MAX = 256
ROPE_THETA = 500000.0
ROPE_FRACTION = 4
N_MOD = 6
EPS = 1e-6

GROUP = 512
LANES = 128
VMEM_LIMIT = 56 * 1024 * 1024

HG_CHUNK = 128
HG_LEVELS = (64, 32, 16, 8, 4, 2, 1)

NEG_BIG = -1e30
DENOM_FLOOR = 2.0 ** -64
NORM_SLACK = 1.01
COARSE_PASSES = 10
FINE_PASSES = 10
COUNT_SLAB = 64

ROW_TILE = 512
KEY_TILE = 512
QUERY_BLOCK = 512
SCORE_SLAB = KEY_TILE
FFN_CHUNK = 256
Q_SCALE = ATT_DIM ** -0.5 * 1.4426950408889634
V_ROWS = ATT_DIM + 16


def _cparams(sem):
    return pltpu.CompilerParams(dimension_semantics=sem, vmem_limit_bytes=VMEM_LIMIT)


def _resident(shape, index_map):
    return pl.BlockSpec(shape, index_map, pipeline_mode=pl.Buffered(1))


def _split_bf16(a):
    hi = a.astype(BF16)
    lo = (a - hi.astype(F32)).astype(BF16)
    return hi, lo


def _dot(a, b):
    return jnp.dot(a, b, preferred_element_type=F32)


def _dot_nt(a, b):
    return lax.dot_general(a, b, (((1,), (1,)), ((), ())), preferred_element_type=F32)


def _silu(a):
    return a * jax.nn.sigmoid(a)


def _adaln_kernel(c_ref, w_ref, b_ref, o_ref):
    a = _silu(c_ref[...])
    a_hi, a_lo = _split_bf16(a)
    w_hi, w_lo = _split_bf16(w_ref[0])
    acc = _dot(a_hi, w_hi) + (_dot(a_hi, w_lo) + _dot(a_lo, w_hi))
    o_ref[...] = acc + b_ref[...]


def _adaln(c, w, b):
    B, D = c.shape
    N = w.shape[2]
    return pl.pallas_call(
        _adaln_kernel,
        grid=(N // D,),
        in_specs=[pl.BlockSpec((B, D), lambda j: (0, 0)),
                  pl.BlockSpec((1, D, D), lambda j: (0, 0, j)),
                  pl.BlockSpec((1, D), lambda j: (0, j))],
        out_specs=pl.BlockSpec((B, D), lambda j: (0, j)),
        out_shape=jax.ShapeDtypeStruct((B, N), F32),
        compiler_params=_cparams(("arbitrary",)),
        name="adaln",
    )(c, w, b)


def _rope_table_kernel(pos_ref, fa_ref, fb_ref, o_ref):
    pos = pos_ref[0].astype(F32)
    ang_a = fa_ref[...] * pos
    ang_b = fb_ref[...] * pos
    ha = fa_ref.shape[0]
    hb = fb_ref.shape[0]
    o_ref[0, 0:ha, :] = jnp.cos(ang_a)
    o_ref[0, ha:2 * ha, :] = jnp.sin(ang_a)
    o_ref[0, 2 * ha:2 * ha + hb, :] = jnp.cos(ang_b)
    o_ref[0, 2 * ha + hb:2 * ha + 2 * hb, :] = jnp.sin(ang_b)
    o_ref[0, 2 * ha + 2 * hb:, :] = jnp.zeros((LANES - 2 * ha - 2 * hb, pos.shape[1]), F32)


ROPE_HALF_A = ATT_DIM // ROPE_FRACTION // 2
ROPE_HALF_B = IDX_DIM // ROPE_FRACTION // 2


def _rope_table(positions):
    B, T = positions.shape
    ha, hb = ROPE_HALF_A, ROPE_HALF_B
    fa = (ROPE_THETA ** (-jnp.arange(ha, dtype=F32) / ha)).reshape(ha, 1)
    fb = (ROPE_THETA ** (-jnp.arange(hb, dtype=F32) / hb)).reshape(hb, 1)
    return pl.pallas_call(
        _rope_table_kernel,
        grid=(B,),
        in_specs=[pl.BlockSpec((1, 1, T), lambda b: (b, 0, 0)),
                  pl.BlockSpec((ha, 1), lambda b: (0, 0)),
                  pl.BlockSpec((hb, 1), lambda b: (0, 0))],
        out_specs=pl.BlockSpec((1, LANES, T), lambda b: (b, 0, 0)),
        out_shape=jax.ShapeDtypeStruct((B, LANES, T), F32),
        compiler_params=_cparams(("arbitrary",)),
        name="rope_table",
    )(positions.reshape(B, 1, T), fa, fb)


def _rope_patterns(tab):
    ha, hb = ROPE_HALF_A, ROPE_HALF_B
    lane = lax.broadcasted_iota(jnp.int32, tab.shape, 1)
    shifted = lambda sh: pltpu.roll(tab, sh % LANES, axis=1)
    cos_a = jnp.where(lane < ha, tab, jnp.where(lane < 2 * ha, shifted(ha), 1.0))
    sin_a = jnp.where(lane < ha, -shifted(-ha), jnp.where(lane < 2 * ha, tab, 0.0))
    l64 = lane & (IDX_DIM - 1)
    first = lane < IDX_DIM
    cb0, sb0 = 2 * ha, 2 * ha + hb
    cos_b = jnp.where(l64 < hb, jnp.where(first, shifted(-cb0), shifted(IDX_DIM - cb0)),
                      jnp.where(l64 < 2 * hb, jnp.where(first, shifted(hb - cb0), shifted(IDX_DIM + hb - cb0)), 1.0))
    sin_b = jnp.where(l64 < hb, -jnp.where(first, shifted(-sb0), shifted(IDX_DIM - sb0)),
                      jnp.where(l64 < 2 * hb, jnp.where(first, shifted(hb - sb0), shifted(IDX_DIM + hb - sb0)), 0.0))
    return cos_a, sin_a, cos_b, sin_b


def _rope(xb, cos, sin, half, period):
    lane = lax.broadcasted_iota(jnp.int32, xb.shape, 1)
    fwd = pltpu.roll(xb, LANES - half, axis=1)
    bwd = pltpu.roll(xb, half, axis=1)
    partner = jnp.where((lane & (period - 1)) < half, fwd, bwd)
    return xb * cos + partner * sin


def _in_proj_kernel(x_ref, mod_ref, gain_ref, w_ref, wt_ref, lbp_ref, lnw_ref, lnb_ref, tab_ref,
                    hq_ref, lf_ref, hk_ref, hv_ref, hg_ref,
                    aq_ref, ak_ref, avt_ref, iq_ref, ik_ref, iwt_ref, nrm_ref):
    x = x_ref[...]
    ms = jnp.mean(x * x, axis=-1, keepdims=True)
    mod = mod_ref[0]
    h = x * lax.rsqrt(ms + EPS) * gain_ref[...]
    h = h * (1.0 + mod[1:2, :]) + mod[0:1, :]
    hb = h.astype(BF16)

    def proj(g, width=GROUP):
        return _dot(hb, w_ref[:, g * GROUP:g * GROUP + width])

    hq_ref[...] = (_silu(proj(0)) * (HG_DIM ** -0.5)).astype(BF16)
    a = lbp_ref[...]
    amax = jnp.max(a, axis=0, keepdims=True)
    e = jnp.exp(a - amax)
    lb = e[0:1, :] / jnp.sum(e, axis=0, keepdims=True)
    fr = proj(1)
    f = lb + (1.0 - lb) * jax.nn.sigmoid(fr)
    lf_ref[...] = jnp.log2(f).astype(BF16)
    hk_ref[...] = ((1.0 - lb) * jax.nn.sigmoid(-fr)).astype(BF16)
    hv_ref[...] = proj(2).astype(BF16)
    hg_ref[...] = _silu(proj(3)).astype(BF16)

    cosa, sina, cosb, sinb = _rope_patterns(tab_ref[0].T)
    half_a = ROPE_HALF_A
    q = proj(4)
    k = proj(5)
    lane_a = lax.broadcasted_iota(jnp.int32, cosa.shape, 1)
    norms = jnp.zeros(cosa.shape, F32)
    for hh in range(ATT_HEADS):
        sl = slice(hh * ATT_DIM, (hh + 1) * ATT_DIM)
        qh = (_rope(q[:, sl], cosa, sina, half_a, ATT_DIM) * Q_SCALE).astype(BF16)
        kh = _rope(k[:, sl], cosa, sina, half_a, ATT_DIM).astype(BF16)
        aq_ref[:, sl] = qh
        ak_ref[:, sl] = kh
        for slot, a in ((hh, qh), (ATT_HEADS + hh, kh)):
            norms = jnp.where(lane_a == slot, jnp.sqrt(_dot(a * a, jnp.ones((ATT_DIM, LANES), BF16))), norms)
    nrm_ref[...] = norms.T[0:2 * ATT_HEADS, :]
    v = proj(6)
    for hh in range(ATT_HEADS):
        r0 = hh * V_ROWS
        avt_ref[0, 0, r0:r0 + ATT_DIM, :] = v[:, hh * ATT_DIM:(hh + 1) * ATT_DIM].T.astype(BF16)
        avt_ref[0, 0, r0 + ATT_DIM:r0 + V_ROWS, :] = jnp.ones((V_ROWS - ATT_DIM, v.shape[0]), BF16)

    half_b = ROPE_HALF_B
    qi = proj(7)
    for cc in range(GROUP // LANES):
        qr = _rope(qi[:, cc * LANES:(cc + 1) * LANES], cosb, sinb, half_b, IDX_DIM).astype(BF16)
        for hh in range(LANES // IDX_DIM):
            iq_ref[cc * (LANES // IDX_DIM) + hh] = qr[:, hh * IDX_DIM:(hh + 1) * IDX_DIM]
    kw = _dot(hb, wt_ref[...])
    ki = kw[:, 0:LANES]
    lane = lax.broadcasted_iota(jnp.int32, ki.shape, 1)
    real = lane < IDX_DIM
    mu = jnp.sum(ki, axis=-1, keepdims=True) * (1.0 / IDX_DIM)
    d = jnp.where(real, ki - mu, 0.0)
    var = jnp.sum(d * d, axis=-1, keepdims=True) * (1.0 / IDX_DIM)
    kn = d * lax.rsqrt(var + EPS) * lnw_ref[...] + lnb_ref[...]
    kn = _rope(kn, cosb, sinb, half_b, IDX_DIM)
    ik_ref[...] = kn[:, 0:IDX_DIM].astype(BF16)
    iwt_ref[...] = kw[:, LANES:2 * LANES].T[0:IDX_HEADS, :] * (IDX_HEADS ** -0.5 * IDX_DIM ** -0.5)


def _in_proj(x2, mod3, gain, w_main, w_tail, lbp, lnw, lnb, tab, T, tm, kt):
    M, D = x2.shape
    nt = T // tm
    row = lambda i: (i, 0)
    const = lambda i: (0, 0)
    bf16o = jax.ShapeDtypeStruct((M, GROUP), BF16)
    grp_spec = pl.BlockSpec((tm, GROUP), row)
    return pl.pallas_call(
        _in_proj_kernel,
        grid=(M // tm,),
        in_specs=[pl.BlockSpec((tm, D), row),
                  pl.BlockSpec((1, N_MOD, D), lambda i: (i // nt, 0, 0)),
                  pl.BlockSpec((1, D), const),
                  _resident(w_main.shape, const),
                  _resident(w_tail.shape, const),
                  pl.BlockSpec(lbp.shape, const),
                  pl.BlockSpec((1, LANES), const),
                  pl.BlockSpec((1, LANES), const),
                  pl.BlockSpec((1, LANES, tm), lambda i: (i // nt, 0, i % nt))],
        out_specs=[grp_spec] * 7 + [pl.BlockSpec((1, 1, ATT_HEADS * V_ROWS, tm),
                                                 lambda i: (i // nt, (i % nt) // (kt // tm), 0, (i % nt) % (kt // tm))),
                                    pl.BlockSpec((IDX_HEADS, tm, IDX_DIM), lambda i: (0, i, 0)),
                                    pl.BlockSpec((tm, IDX_DIM), row),
                                    pl.BlockSpec((IDX_HEADS, tm), lambda i: (0, i)),
                                    pl.BlockSpec((2 * ATT_HEADS, tm), lambda i: (0, i))],
        out_shape=[bf16o] * 7 + [jax.ShapeDtypeStruct((M // T, T // kt, ATT_HEADS * V_ROWS, kt), BF16),
                                 jax.ShapeDtypeStruct((IDX_HEADS, M, IDX_DIM), BF16),
                                 jax.ShapeDtypeStruct((M, IDX_DIM), BF16),
                                 jax.ShapeDtypeStruct((IDX_HEADS, M), F32),
                                 jax.ShapeDtypeStruct((2 * ATT_HEADS, M), F32)],
        compiler_params=_cparams(("parallel",)),
        name="in_proj",
    )(x2, mod3, gain, w_main, w_tail, lbp, lnw, lnb, tab)


def _hgrn_consts():
    C = HG_CHUNK
    t = np.arange(C)
    tri = (t[None, :] <= t[:, None]).astype(np.float32)
    blocks = [tri]
    for m in HG_LEVELS:
        if m % 8:
            split = (t // (2 * m)) * (2 * m) + m - 1
            upper = ((t & m) != 0)[:, None]
            blocks.append(np.where(upper, tri - tri[split], tri[split] - tri))
    mall = np.concatenate(blocks, axis=0)
    x = t[:, None] ^ t[None, :]
    lvl = np.full((C, C), len(HG_LEVELS) + 1, np.int32)
    for li, m in enumerate(HG_LEVELS):
        lvl[(t[:, None] > t[None, :]) & (x >= m) & (x < 2 * m)] = li
    lvl[t[:, None] == t[None, :]] = len(HG_LEVELS)
    return jnp.asarray(mall, BF16), jnp.asarray(lvl)


def _hgrn_kernel(hq_ref, lf_ref, hk_ref, hv_ref, hg_ref, mall_ref, lvl_ref, onorm_ref, o_ref, st_ref):
    C = HG_CHUNK
    nl = len(HG_LEVELS)

    @pl.when(pl.program_id(1) == 0)
    def _():
        st_ref[...] = jnp.zeros_like(st_ref)

    lvl = lvl_ref[...]
    row = lax.broadcasted_iota(jnp.int32, (C, HG_DIM), 0)
    mall = mall_ref[...]

    for ci in range(hq_ref.shape[0] // C):
        rs = slice(ci * C, (ci + 1) * C)
        dheads = _dot(mall, lf_ref[rs, :])
        for h in range(HG_HEADS):
            cs = slice(h * HG_DIM, (h + 1) * HG_DIM)
            q, k, v = hq_ref[rs, cs], hk_ref[rs, cs], hv_ref[rs, cs]
            qf, kf = q.astype(F32), k.astype(F32)
            dall = dheads[:, cs]
            g = dall[0:C]

            a = jnp.zeros((C, C), F32)
            fine = 0
            for li, m in enumerate(HG_LEVELS):
                if m % 8 == 0:
                    qk = jnp.concatenate([(qf if (b & 1) else kf)[b * m:(b + 1) * m] for b in range(C // m)], axis=0)
                    parts = []
                    for b in range(C // (2 * m)):
                        lo_rows = slice(2 * b * m, (2 * b + 1) * m)
                        hi_rows = slice((2 * b + 1) * m, (2 * b + 2) * m)
                        ref = jnp.broadcast_to(g[lo_rows.stop - 1:lo_rows.stop, :], (m, HG_DIM))
                        parts += [ref - g[lo_rows], g[hi_rows] - ref]
                    dm = jnp.concatenate(parts, axis=0)
                else:
                    fine += 1
                    qk = jnp.where((row & m) != 0, qf, kf)
                    dm = dall[fine * C:(fine + 1) * C]
                xm = (qk * jnp.exp2(dm)).astype(BF16)
                a = jnp.where(lvl == li, _dot_nt(xm, xm), a)
            a = jnp.where(lvl == nl, _dot_nt(q, k), a)

            st = st_ref[h]
            o = _dot_nt((qf * jnp.exp2(g)).astype(BF16), st.astype(BF16)) + _dot(a.astype(BF16), v)
            g_last = g[C - 1:C, :]
            kd = (kf * jnp.exp2(g_last - g)).astype(BF16)
            st_ref[h] = jnp.exp2(g_last) * st + _dot(v.astype(F32).T.astype(BF16), kd)

            o = o * lax.rsqrt(jnp.mean(o * o, axis=-1, keepdims=True) + EPS)
            o_ref[rs, cs] = (o * onorm_ref[:, cs] * hg_ref[rs, cs]).astype(BF16)


def _hgrn(hq, lf, hk, hv, hg, onorm, B, T, ct):
    M = hq.shape[0]
    nct = T // ct
    mall, lvl = _hgrn_consts()
    blk = pl.BlockSpec((ct, GROUP), lambda b, c: (b * nct + c, 0))
    const = lambda b, c: (0, 0)
    return pl.pallas_call(
        _hgrn_kernel,
        grid=(B, nct),
        in_specs=[blk] * 5 + [pl.BlockSpec(mall.shape, const), pl.BlockSpec(lvl.shape, const),
                              pl.BlockSpec((1, GROUP), const)],
        out_specs=blk,
        out_shape=jax.ShapeDtypeStruct((M, GROUP), BF16),
        scratch_shapes=[pltpu.VMEM((HG_HEADS, HG_DIM, HG_DIM), F32)],
        compiler_params=_cparams(("parallel", "arbitrary")),
        name="hgrn2",
    )(hq, lf, hk, hv, hg, mall, lvl, onorm)


def _fold8(a, op, rows=8):
    chains = [None] * 4
    for r in range(a.shape[0] // rows):
        part = a[r * rows:(r + 1) * rows, :]
        c = r % len(chains)
        chains[c] = part if chains[c] is None else op(chains[c], part)
    return op(op(chains[0], chains[1]), op(chains[2], chains[3]))


def _floor_bf16(a):
    r = a.astype(BF16).astype(F32)
    below = (r - jnp.abs(r) * (5.0 / 1024.0)).astype(BF16)
    return jnp.where(r > a, below.astype(F32), r).astype(BF16)


def _dsa_kernel(iq_ref, iwt_ref, ik_ref, aq_ref, ak_ref, avt_ref, qnt_ref, knt_ref, o_ref,
                sc_ref, scb_ref, mm_ref, *head_refs, qb, kt, topk, seq):
    acc_refs, m_refs = head_refs[:ATT_HEADS], head_refs[ATT_HEADS:]
    i = pl.program_id(1)
    q0 = i * qb
    nfull = q0 // kt
    nkt = nfull + 1
    qpos = q0 + lax.broadcasted_iota(jnp.int32, (1, qb), 1)
    kk = jnp.minimum(qpos + 1, topk).astype(F32)

    mm_ref[0:8, :] = jnp.full((8, qb), jnp.inf, F32)
    mm_ref[8:16, :] = jnp.full((8, qb), -jnp.inf, F32)
    mm_ref[16:32, :] = jnp.zeros((16, qb), F32)

    def score_tile(j, diag):
        for r in range(kt // SCORE_SLAB):
            ks = pl.multiple_of(j * kt + r * SCORE_SLAB, SCORE_SLAB)
            rows = pl.ds(ks, SCORE_SLAB)
            ki = ik_ref[rows, :]
            s = None
            for h in range(IDX_HEADS):
                t = jnp.maximum(_dot_nt(ki, iq_ref[h]), 0.0) * iwt_ref[h:h + 1, :]
                s = t if s is None else s + t
            if diag:
                causal = ks + lax.broadcasted_iota(jnp.int32, (SCORE_SLAB, qb), 0) <= qpos
                s_hi = jnp.where(causal, s, -jnp.inf)
                s_lo = jnp.where(causal, s, jnp.inf)
            else:
                s_hi = s_lo = s
            sc_ref[rows, :] = s_hi
            scb_ref[rows, :] = _floor_bf16(s_hi)
            mm_ref[0:8, :] = jnp.minimum(mm_ref[0:8, :], _fold8(s_lo, jnp.minimum))
            mm_ref[8:16, :] = jnp.maximum(mm_ref[8:16, :], _fold8(s_hi, jnp.maximum))
            mm_ref[16:24, :] += _fold8(jnp.where(s_hi >= 0.0, 1.0, 0.0), jnp.add)
            mm_ref[24:32, :] += _fold8(jnp.where(s_hi > 0.0, 1.0, 0.0), jnp.add)

    def full_tile(j, c):
        score_tile(j, False)
        return c

    lax.fori_loop(0, nfull, full_tile, 0)
    score_tile(nfull, True)
    mn = jnp.min(mm_ref[0:8, :], axis=0, keepdims=True)
    mx = jnp.max(mm_ref[8:16, :], axis=0, keepdims=True)
    c_nonneg = jnp.sum(mm_ref[16:24, :], axis=0, keepdims=True)
    c_pos = jnp.sum(mm_ref[24:32, :], axis=0, keepdims=True)

    def tile_hits(ref, j, hit_fn, slab, rows):
        acc = None
        for r in range(kt // slab):
            ks = pl.multiple_of(j * kt + r * slab, slab)
            part = _fold8(hit_fn(ref[pl.ds(ks, slab), :]), jnp.add, rows=rows)
            acc = part if acc is None else acc + part
        return acc

    def count_ge(th):
        def body(j, acc):
            return acc + tile_hits(sc_ref, j, lambda s: jnp.where(s >= th, 1.0, 0.0), COUNT_SLAB, 8)
        acc = lax.fori_loop(0, nkt, body, jnp.zeros((8, qb), F32))
        return jnp.sum(acc, axis=0, keepdims=True)

    above = c_pos >= kk
    below = c_nonneg < kk
    live = above | below
    zero = jnp.zeros((1, qb), F32)
    lo0 = jnp.where(below, mn, zero)
    cnt_lo0 = jnp.where(below, (qpos + 1).astype(F32), c_nonneg)
    hi0 = jnp.where(above, mx + jnp.abs(mx) * 1e-6 + 1e-30, zero)
    cnt_hi0 = jnp.where(above, zero, jnp.where(below, c_nonneg, c_pos))

    def midpoint(lo, hi):
        return lo + 0.5 * (hi - lo)

    def unresolved(lo, hi, cnt_lo):
        mid = midpoint(lo, hi)
        return jnp.max((live & (cnt_lo != kk) & (mid > lo) & (mid < hi)).astype(jnp.int32)) > 0

    assert kt // 16 <= 256

    def count_ge_bf16(th):
        def body(j, acc):
            hit = lambda s: jnp.where(s >= th, jnp.ones((), BF16), jnp.zeros((), BF16))
            part = tile_hits(scb_ref, j, hit, 2 * COUNT_SLAB, 16).astype(F32)
            return acc + part[0:8, :] + part[8:16, :]
        acc = lax.fori_loop(0, nkt, body, jnp.zeros((8, qb), F32))
        return jnp.sum(acc, axis=0, keepdims=True)

    def coarse_body(_, c):
        lo, hi, cnt_lo, cnt_hi = c
        th = midpoint(lo, hi).astype(BF16)
        mid = th.astype(F32)
        inside = live & (mid > lo) & (mid < hi)
        cnt = count_ge_bf16(th)
        up = inside & (cnt >= kk)
        dn = inside & (cnt < kk)
        return (jnp.where(up, mid, lo), jnp.where(dn, mid, hi),
                jnp.where(up, cnt, cnt_lo), jnp.where(dn, cnt, cnt_hi))

    lo1, hi1, cnt_lo1, cnt_hi1 = lax.fori_loop(0, COARSE_PASSES, coarse_body, (lo0, hi0, cnt_lo0, cnt_hi0))

    def fine_pass(c):
        lo, hi, cnt_lo, cnt_hi = c
        mid = midpoint(lo, hi)
        cnt = count_ge(mid)
        up = live & (cnt >= kk)
        dn = live & (cnt < kk)
        return (jnp.where(up, mid, lo), jnp.where(dn, mid, hi), jnp.where(up, cnt, cnt_lo), jnp.where(dn, cnt, cnt_hi))

    c2 = lax.fori_loop(0, FINE_PASSES, lambda _, c: fine_pass(c), (lo1, hi1, cnt_lo1, cnt_hi1))

    def bis_cond(c):
        it, (lo, hi, cnt_lo, _) = c
        return (it < 320) & unresolved(lo, hi, cnt_lo)

    _, (lo, hi, cnt_lo, cnt_hi) = lax.while_loop(bis_cond, lambda c: (c[0] + 1, fine_pass(c[1])), (jnp.int32(0), c2))

    tied = cnt_lo != kk
    need = kk - cnt_hi

    def resolve_ties(cs):
        lo_c, need_c, tied_c = lo[:, cs], need[:, cs], tied[:, cs]

        def tie_keys(j):
            ks = pl.multiple_of(j * kt, kt)
            rows = pl.ds(ks, kt)
            return rows, sc_ref[rows, cs], ks + lax.broadcasted_iota(jnp.int32, (kt, LANES), 0)

        def count_tie(jm):
            def body(j, acc):
                _, s, kpos = tie_keys(j)
                return acc + _fold8(jnp.where((s == lo_c) & (kpos <= jm), 1.0, 0.0), jnp.add)
            acc = lax.fori_loop(0, nkt, body, jnp.zeros((8, LANES), F32))
            return jnp.sum(acc, axis=0, keepdims=True)

        def tie_body(_, c):
            jlo, jhi = c
            jm = (jlo + jhi) // 2
            ok = count_tie(jm) >= need_c
            return jnp.where(ok, jlo, jm), jnp.where(ok, jm, jhi)

        steps = int(np.ceil(np.log2(seq))) + 1
        _, jhi = lax.fori_loop(0, steps, tie_body,
                               (jnp.full((1, LANES), -1, jnp.int32), jnp.full((1, LANES), seq - 1, jnp.int32)))
        jmax = jnp.where(tied_c, jhi, seq)

        def strike(j, c):
            rows, s, kpos = tie_keys(j)
            sc_ref[rows, cs] = jnp.where((s == lo_c) & (kpos > jmax), -jnp.inf, s)
            return c

        lax.fori_loop(0, nkt, strike, 0)

    for c in range(qb // LANES):
        cs = slice(c * LANES, (c + 1) * LANES)
        pl.when(jnp.max(tied[:, cs].astype(jnp.int32)) > 0)(functools.partial(resolve_ties, cs))

    def logits(j, h, bias):
        ks = pl.multiple_of(j * kt, kt)
        sl = slice(h * ATT_DIM, (h + 1) * ATT_DIM)
        return _dot_nt(ak_ref[pl.ds(ks, kt), sl], aq_ref[:, sl]) + bias

    def select_bias(j):
        ks = pl.multiple_of(j * kt, kt)
        return jnp.where(sc_ref[pl.ds(ks, kt), :] >= lo, 0.0, NEG_BIG)

    def values(j, h):
        return avt_ref[0, j, h * V_ROWS:(h + 1) * V_ROWS, :]

    kmax = jnp.max(knt_ref[...], axis=1, keepdims=True)
    shift = [qnt_ref[h:h + 1, :] * kmax[ATT_HEADS + h:ATT_HEADS + h + 1, :] * NORM_SLACK for h in range(ATT_HEADS)]
    for h in range(ATT_HEADS):
        acc_refs[h][...] = jnp.zeros_like(acc_refs[h])

    def attn_tile_fast(j, c):
        bias = select_bias(j)
        for h in range(ATT_HEADS):
            p = jnp.exp2(logits(j, h, bias) - shift[h]).astype(BF16)
            acc_refs[h][...] += _dot(values(j, h), p)
        return c

    lax.fori_loop(0, nkt, attn_tile_fast, 0)
    denom_min = functools.reduce(jnp.minimum, [acc_refs[h][ATT_DIM:ATT_DIM + 1, :] for h in range(ATT_HEADS)])

    @pl.when(jnp.min(denom_min) < DENOM_FLOOR)
    def _():
        for h in range(ATT_HEADS):
            acc_refs[h][...] = jnp.zeros_like(acc_refs[h])
            m_refs[h][...] = jnp.full(m_refs[h].shape, NEG_BIG, F32)

        def attn_tile(j, c):
            bias = select_bias(j)
            for h in range(ATT_HEADS):
                acc_h, m_h = acc_refs[h], m_refs[h]
                lg = logits(j, h, bias)
                m_old = m_h[0:1, :]
                m_new = jnp.maximum(m_old, jnp.max(_fold8(lg, jnp.maximum), axis=0, keepdims=True))
                p = jnp.exp2(lg - m_new).astype(BF16)
                acc_h[...] = jnp.exp2(m_old - m_new) * acc_h[...] + _dot(values(j, h), p)
                m_h[0:1, :] = m_new
            return c

        lax.fori_loop(0, nkt, attn_tile, 0)

    for h in range(ATT_HEADS):
        o_t = acc_refs[h][0:ATT_DIM, :] / acc_refs[h][ATT_DIM:ATT_DIM + 1, :]
        o_ref[:, h * ATT_DIM:(h + 1) * ATT_DIM] = o_t.T.astype(BF16)


def _dsa(iq, iwt, ik, aq, ak, avt, nrm, B, T, qb, kt):
    M = ak.shape[0]
    nq = T // qb
    assert kt % qb == 0 and T % kt == 0 and avt.shape[1:] == (T // kt, ATT_HEADS * V_ROWS, kt)
    topk = min(TOPK_MAX, T // 4)
    qrow = lambda b, i: (b * nq + i, 0)
    brow = lambda b, i: (b, 0)
    kern = functools.partial(_dsa_kernel, qb=qb, kt=kt, topk=topk, seq=T)
    return pl.pallas_call(
        kern,
        grid=(B, nq),
        in_specs=[pl.BlockSpec((IDX_HEADS, qb, IDX_DIM), lambda b, i: (0, b * nq + i, 0)),
                  pl.BlockSpec((IDX_HEADS, qb), lambda b, i: (0, b * nq + i)),
                  _resident((T, IDX_DIM), brow),
                  pl.BlockSpec((qb, GROUP), qrow),
                  _resident((T, GROUP), brow),
                  _resident((1,) + avt.shape[1:], lambda b, i: (b, 0, 0, 0)),
                  pl.BlockSpec((2 * ATT_HEADS, qb), lambda b, i: (0, b * nq + i)),
                  _resident((2 * ATT_HEADS, T), lambda b, i: (0, b))],
        out_specs=pl.BlockSpec((qb, GROUP), qrow),
        out_shape=jax.ShapeDtypeStruct((M, GROUP), BF16),
        scratch_shapes=[pltpu.VMEM((T, qb), F32),
                        pltpu.VMEM((T, qb), BF16),
                        pltpu.VMEM((32, qb), F32),
                        ] + [pltpu.VMEM((V_ROWS, qb), F32)] * ATT_HEADS
                        + [pltpu.VMEM((8, qb), F32)] * ATT_HEADS,
        compiler_params=_cparams(("parallel", "arbitrary")),
        name="dsa",
    )(iq, iwt, ik, aq, ak, avt, nrm, nrm)


def _rms(a, gain):
    return a * lax.rsqrt(jnp.mean(a * a, axis=-1, keepdims=True) + EPS) * gain


def _out_ffn_kernel(ohg_ref, oatt_ref, x_ref, mod_ref, gains_ref, wo_ref, wi_ref, wf_ref, o_ref, acc_ref,
                    *, d_ff, fc):
    mod = mod_ref[0]
    gains = gains_ref[...]
    hw = ohg_ref.shape[1]
    y = _dot(ohg_ref[...], wo_ref[0:hw, :]) + _dot(oatt_ref[...], wo_ref[hw:, :])
    x1 = x_ref[...] + mod[2:3, :] * _rms(y, gains[0:1, :])
    h2 = (_rms(x1, gains[1:2, :]) * (1.0 + mod[4:5, :]) + mod[3:4, :]).astype(BF16)
    for c in range(d_ff // fc):
        gate = _dot(h2, wi_ref[:, c * fc:(c + 1) * fc])
        up = _dot(h2, wi_ref[:, d_ff + c * fc:d_ff + (c + 1) * fc])
        part = _dot((_silu(gate) * up).astype(BF16), wf_ref[c * fc:(c + 1) * fc, :])
        if c == 0:
            acc_ref[...] = part
        else:
            acc_ref[...] += part
    o_ref[...] = x1 + mod[5:6, :] * _rms(acc_ref[...], gains[2:3, :])


def _out_ffn(ohg, oatt, x2, mod3, gains, wo, wi, wf, T, tm):
    M, D = x2.shape
    nt = T // tm
    d_ff = wf.shape[0]
    row = lambda i: (i, 0)
    const = lambda i: (0, 0)
    assert d_ff % FFN_CHUNK == 0
    kern = functools.partial(_out_ffn_kernel, d_ff=d_ff, fc=FFN_CHUNK)
    return pl.pallas_call(
        kern,
        grid=(M // tm,),
        in_specs=[pl.BlockSpec((tm, GROUP), row),
                  pl.BlockSpec((tm, GROUP), row),
                  pl.BlockSpec((tm, D), row),
                  pl.BlockSpec((1, N_MOD, D), lambda i: (i // nt, 0, 0)),
                  pl.BlockSpec((3, D), const),
                  _resident(wo.shape, const),
                  _resident(wi.shape, const),
                  _resident(wf.shape, const)],
        out_specs=pl.BlockSpec((tm, D), row),
        out_shape=jax.ShapeDtypeStruct((M, D), F32),
        scratch_shapes=[pltpu.VMEM((tm, D), F32)],
        compiler_params=_cparams(("parallel",)),
        name="out_ffn",
    )(ohg, oatt, x2, mod3, gains, wo, wi, wf)


def kernel(x, c, positions, w_ada, b_ada, norm_pre_mix, norm_post_mix, norm_pre_ffn, norm_post_ffn, w_in,
           hgrn_lower_bound, hgrn_out_norm, idx_k_norm_w, idx_k_norm_b, w_out, w_ffn_in, w_ffn_out):
    B, T, D = x.shape
    depth = w_ada.shape[0]
    assert depth == 1 and hgrn_lower_bound.shape[0] == 2
    tm = ROW_TILE
    assert T % tm == 0 and T % KEY_TILE == 0 and T % QUERY_BLOCK == 0 and D % LANES == 0 and tm == KEY_TILE

    mod3 = _adaln(c, w_ada, b_ada).reshape(B, N_MOD, D)
    tab = _rope_table(positions)

    main = 8 * GROUP
    w = w_in.reshape(w_in.shape[1:])
    w_main = w[:, :main].astype(BF16)
    w_tail = jnp.concatenate([jnp.pad(w[:, main:main + IDX_DIM], ((0, 0), (0, LANES - IDX_DIM))),
                              jnp.pad(w[:, main + IDX_DIM:], ((0, 0), (0, LANES - IDX_HEADS)))], axis=1).astype(BF16)
    lnw = jnp.pad(idx_k_norm_w[0], (0, LANES - IDX_DIM)).reshape(1, LANES)
    lnb = jnp.pad(idx_k_norm_b[0], (0, LANES - IDX_DIM)).reshape(1, LANES)

    x2 = x.reshape(B * T, D)
    hq, lf, hk, hv, hg, aq, ak, avt, iq, ik, iwt, nrm = _in_proj(
        x2, mod3, norm_pre_mix, w_main, w_tail, hgrn_lower_bound, lnw, lnb, tab, T, tm, tm)

    o_hg = _hgrn(hq, lf, hk, hv, hg, hgrn_out_norm, B, T, tm)
    o_att = _dsa(iq, iwt, ik, aq, ak, avt, nrm, B, T, QUERY_BLOCK, KEY_TILE)

    gains = jnp.concatenate([norm_post_mix, norm_pre_ffn, norm_post_ffn], axis=0)
    out = _out_ffn(o_hg, o_att, x2, mod3, gains, w_out.reshape(w_out.shape[1:]).astype(BF16),
                   w_ffn_in.reshape(w_ffn_in.shape[1:]).astype(BF16),
                   w_ffn_out.reshape(w_ffn_out.shape[1:]).astype(BF16), T, tm)
    return out.reshape(B, T, D)
```

```python
import functools

import numpy as np
import jax
import jax.numpy as jnp
from jax import lax
from jax.experimental import pallas as pl
from jax.experimental.pallas import tpu as pltpu

F32 = jnp.float32
BF16 = jnp.bfloat16

HG_HEADS = 4
HG_DIM = 128
ATT_HEADS = 4
ATT_DIM = 128
IDX_HEADS = 8
IDX_DIM = 64
TOPK_MAX = 256
ROPE_THETA = 500000.0
ROPE_FRACTION = 4
N_MOD = 6
EPS = 1e-6

GROUP = 512
LANES = 128
VMEM_LIMIT = 56 * 1024 * 1024

HG_CHUNK = 128
HG_LEVELS = (64, 32, 16, 8, 4, 2, 1)

NEG_BIG = -1e30
DENOM_FLOOR = 2.0 ** -64
NORM_SLACK = 1.01
COARSE_PASSES = 10
FINE_PASSES = 10
COUNT_SLAB = 64
TILE_GROUP = 4

ROW_TILE = 512
KEY_TILE = 512
QUERY_BLOCK = 512
SCORE_SLAB = KEY_TILE
FFN_CHUNK = 256
Q_SCALE = ATT_DIM ** -0.5 * 1.4426950408889634
V_ROWS = ATT_DIM + 16


def _cparams(sem):
    return pltpu.CompilerParams(dimension_semantics=sem, vmem_limit_bytes=VMEM_LIMIT)


def _resident(shape, index_map):
    return pl.BlockSpec(shape, index_map, pipeline_mode=pl.Buffered(1))


def _split_bf16(a):
    hi = a.astype(BF16)
    lo = (a - hi.astype(F32)).astype(BF16)
    return hi, lo


def _dot(a, b):
    return jnp.dot(a, b, preferred_element_type=F32)


def _dot_nt(a, b):
    return lax.dot_general(a, b, (((1,), (1,)), ((), ())), preferred_element_type=F32)


def _silu(a):
    return a * jax.nn.sigmoid(a)


def _adaln_kernel(c_ref, w_ref, b_ref, o_ref):
    a = _silu(c_ref[...])
    a_hi, a_lo = _split_bf16(a)
    w_hi, w_lo = _split_bf16(w_ref[0])
    acc = _dot(a_hi, w_hi) + (_dot(a_hi, w_lo) + _dot(a_lo, w_hi))
    o_ref[...] = acc + b_ref[...]


def _adaln(c, w, b):
    B, D = c.shape
    N = w.shape[2]
    return pl.pallas_call(
        _adaln_kernel,
        grid=(N // D,),
        in_specs=[pl.BlockSpec((B, D), lambda j: (0, 0)),
                  pl.BlockSpec((1, D, D), lambda j: (0, 0, j)),
                  pl.BlockSpec((1, D), lambda j: (0, j))],
        out_specs=pl.BlockSpec((B, D), lambda j: (0, j)),
        out_shape=jax.ShapeDtypeStruct((B, N), F32),
        compiler_params=_cparams(("arbitrary",)),
        name="adaln",
    )(c, w, b)


def _rope_table_kernel(pos_ref, fa_ref, fb_ref, o_ref):
    pos = pos_ref[0].astype(F32)
    ang_a = fa_ref[...] * pos
    ang_b = fb_ref[...] * pos
    ha = fa_ref.shape[0]
    hb = fb_ref.shape[0]
    o_ref[0, 0:ha, :] = jnp.cos(ang_a)
    o_ref[0, ha:2 * ha, :] = jnp.sin(ang_a)
    o_ref[0, 2 * ha:2 * ha + hb, :] = jnp.cos(ang_b)
    o_ref[0, 2 * ha + hb:2 * ha + 2 * hb, :] = jnp.sin(ang_b)
    o_ref[0, 2 * ha + 2 * hb:, :] = jnp.zeros((LANES - 2 * ha - 2 * hb, pos.shape[1]), F32)


ROPE_HALF_A = ATT_DIM // ROPE_FRACTION // 2
ROPE_HALF_B = IDX_DIM // ROPE_FRACTION // 2


def _rope_table(positions):
    B, T = positions.shape
    ha, hb = ROPE_HALF_A, ROPE_HALF_B
    fa = (ROPE_THETA ** (-jnp.arange(ha, dtype=F32) / ha)).reshape(ha, 1)
    fb = (ROPE_THETA ** (-jnp.arange(hb, dtype=F32) / hb)).reshape(hb, 1)
    return pl.pallas_call(
        _rope_table_kernel,
        grid=(B,),
        in_specs=[pl.BlockSpec((1, 1, T), lambda b: (b, 0, 0)),
                  pl.BlockSpec((ha, 1), lambda b: (0, 0)),
                  pl.BlockSpec((hb, 1), lambda b: (0, 0))],
        out_specs=pl.BlockSpec((1, LANES, T), lambda b: (b, 0, 0)),
        out_shape=jax.ShapeDtypeStruct((B, LANES, T), F32),
        compiler_params=_cparams(("arbitrary",)),
        name="rope_table",
    )(positions.reshape(B, 1, T), fa, fb)


def _rope_patterns(tab):
    ha, hb = ROPE_HALF_A, ROPE_HALF_B
    lane = lax.broadcasted_iota(jnp.int32, tab.shape, 1)
    shifted = lambda sh: pltpu.roll(tab, sh % LANES, axis=1)
    cos_a = jnp.where(lane < ha, tab, jnp.where(lane < 2 * ha, shifted(ha), 1.0))
    sin_a = jnp.where(lane < ha, -shifted(-ha), jnp.where(lane < 2 * ha, tab, 0.0))
    l64 = lane & (IDX_DIM - 1)
    first = lane < IDX_DIM
    cb0, sb0 = 2 * ha, 2 * ha + hb
    cos_b = jnp.where(l64 < hb, jnp.where(first, shifted(-cb0), shifted(IDX_DIM - cb0)),
                      jnp.where(l64 < 2 * hb, jnp.where(first, shifted(hb - cb0), shifted(IDX_DIM + hb - cb0)), 1.0))
    sin_b = jnp.where(l64 < hb, -jnp.where(first, shifted(-sb0), shifted(IDX_DIM - sb0)),
                      jnp.where(l64 < 2 * hb, jnp.where(first, shifted(hb - sb0), shifted(IDX_DIM + hb - sb0)), 0.0))
    return cos_a, sin_a, cos_b, sin_b


def _rope(xb, cos, sin, half, period):
    lane = lax.broadcasted_iota(jnp.int32, xb.shape, 1)
    fwd = pltpu.roll(xb, LANES - half, axis=1)
    bwd = pltpu.roll(xb, half, axis=1)
    partner = jnp.where((lane & (period - 1)) < half, fwd, bwd)
    return xb * cos + partner * sin


def _in_proj_kernel(x_ref, mod_ref, gain_ref, w_ref, wt_ref, lbp_ref, lnw_ref, lnb_ref, tab_ref,
                    hq_ref, lf_ref, hk_ref, hv_ref, hg_ref,
                    aq_ref, ak_ref, avt_ref, iq_ref, ik_ref, iwt_ref, nrm_ref):
    x = x_ref[...]
    ms = jnp.mean(x * x, axis=-1, keepdims=True)
    mod = mod_ref[0]
    h = x * lax.rsqrt(ms + EPS) * gain_ref[...]
    h = h * (1.0 + mod[1:2, :]) + mod[0:1, :]
    hb = h.astype(BF16)

    def proj(g, width=GROUP):
        return _dot(hb, w_ref[:, g * GROUP:g * GROUP + width])

    hq_ref[...] = (_silu(proj(0)) * (HG_DIM ** -0.5)).astype(BF16)
    a = lbp_ref[...]
    amax = jnp.max(a, axis=0, keepdims=True)
    e = jnp.exp(a - amax)
    lb = e[0:1, :] / jnp.sum(e, axis=0, keepdims=True)
    fr = proj(1)
    f = lb + (1.0 - lb) * jax.nn.sigmoid(fr)
    lf_ref[...] = jnp.log2(f).astype(BF16)
    hk_ref[...] = ((1.0 - lb) * jax.nn.sigmoid(-fr)).astype(BF16)
    hv_ref[...] = proj(2).astype(BF16)
    hg_ref[...] = _silu(proj(3)).astype(BF16)

    cosa, sina, cosb, sinb = _rope_patterns(tab_ref[0].T)
    half_a = ROPE_HALF_A
    q = proj(4)
    k = proj(5)
    lane_a = lax.broadcasted_iota(jnp.int32, cosa.shape, 1)
    norms = jnp.zeros(cosa.shape, F32)
    for hh in range(ATT_HEADS):
        sl = slice(hh * ATT_DIM, (hh + 1) * ATT_DIM)
        qh = (_rope(q[:, sl], cosa, sina, half_a, ATT_DIM) * Q_SCALE).astype(BF16)
        kh = _rope(k[:, sl], cosa, sina, half_a, ATT_DIM).astype(BF16)
        aq_ref[:, sl] = qh
        ak_ref[:, sl] = kh
        for slot, a in ((hh, qh), (ATT_HEADS + hh, kh)):
            norms = jnp.where(lane_a == slot, jnp.sqrt(_dot(a * a, jnp.ones((ATT_DIM, LANES), BF16))), norms)
    nrm_ref[...] = norms.T[0:2 * ATT_HEADS, :]
    v = proj(6)
    for hh in range(ATT_HEADS):
        r0 = hh * V_ROWS
        avt_ref[0, 0, r0:r0 + ATT_DIM, :] = v[:, hh * ATT_DIM:(hh + 1) * ATT_DIM].T.astype(BF16)
        avt_ref[0, 0, r0 + ATT_DIM:r0 + V_ROWS, :] = jnp.ones((V_ROWS - ATT_DIM, v.shape[0]), BF16)

    half_b = ROPE_HALF_B
    qi = proj(7)
    for cc in range(GROUP // LANES):
        qr = _rope(qi[:, cc * LANES:(cc + 1) * LANES], cosb, sinb, half_b, IDX_DIM).astype(BF16)
        for hh in range(LANES // IDX_DIM):
            iq_ref[cc * (LANES // IDX_DIM) + hh] = qr[:, hh * IDX_DIM:(hh + 1) * IDX_DIM]
    kw = _dot(hb, wt_ref[...])
    ki = kw[:, 0:LANES]
    lane = lax.broadcasted_iota(jnp.int32, ki.shape, 1)
    real = lane < IDX_DIM
    mu = jnp.sum(ki, axis=-1, keepdims=True) * (1.0 / IDX_DIM)
    d = jnp.where(real, ki - mu, 0.0)
    var = jnp.sum(d * d, axis=-1, keepdims=True) * (1.0 / IDX_DIM)
    kn = d * lax.rsqrt(var + EPS) * lnw_ref[...] + lnb_ref[...]
    kn = _rope(kn, cosb, sinb, half_b, IDX_DIM)
    ik_ref[...] = kn[:, 0:IDX_DIM].astype(BF16)
    iwt_ref[...] = kw[:, LANES:2 * LANES].T[0:IDX_HEADS, :] * (IDX_HEADS ** -0.5 * IDX_DIM ** -0.5)


def _in_proj(x2, mod3, gain, w_main, w_tail, lbp, lnw, lnb, tab, T, tm, kt):
    M, D = x2.shape
    nt = T // tm
    row = lambda i: (i, 0)
    const = lambda i: (0, 0)
    bf16o = jax.ShapeDtypeStruct((M, GROUP), BF16)
    grp_spec = pl.BlockSpec((tm, GROUP), row)
    return pl.pallas_call(
        _in_proj_kernel,
        grid=(M // tm,),
        in_specs=[pl.BlockSpec((tm, D), row),
                  pl.BlockSpec((1, N_MOD, D), lambda i: (i // nt, 0, 0)),
                  pl.BlockSpec((1, D), const),
                  _resident(w_main.shape, const),
                  _resident(w_tail.shape, const),
                  pl.BlockSpec(lbp.shape, const),
                  pl.BlockSpec((1, LANES), const),
                  pl.BlockSpec((1, LANES), const),
                  pl.BlockSpec((1, LANES, tm), lambda i: (i // nt, 0, i % nt))],
        out_specs=[grp_spec] * 7 + [pl.BlockSpec((1, 1, ATT_HEADS * V_ROWS, tm),
                                                 lambda i: (i // nt, (i % nt) // (kt // tm), 0, (i % nt) % (kt // tm))),
                                    pl.BlockSpec((IDX_HEADS, tm, IDX_DIM), lambda i: (0, i, 0)),
                                    pl.BlockSpec((tm, IDX_DIM), row),
                                    pl.BlockSpec((IDX_HEADS, tm), lambda i: (0, i)),
                                    pl.BlockSpec((2 * ATT_HEADS, tm), lambda i: (0, i))],
        out_shape=[bf16o] * 7 + [jax.ShapeDtypeStruct((M // T, T // kt, ATT_HEADS * V_ROWS, kt), BF16),
                                 jax.ShapeDtypeStruct((IDX_HEADS, M, IDX_DIM), BF16),
                                 jax.ShapeDtypeStruct((M, IDX_DIM), BF16),
                                 jax.ShapeDtypeStruct((IDX_HEADS, M), F32),
                                 jax.ShapeDtypeStruct((2 * ATT_HEADS, M), F32)],
        compiler_params=_cparams(("parallel",)),
        name="in_proj",
    )(x2, mod3, gain, w_main, w_tail, lbp, lnw, lnb, tab)


def _hgrn_consts():
    C = HG_CHUNK
    t = np.arange(C)
    tri = (t[None, :] <= t[:, None]).astype(np.float32)
    blocks = [tri]
    for m in HG_LEVELS:
        if m % 8:
            split = (t // (2 * m)) * (2 * m) + m - 1
            upper = ((t & m) != 0)[:, None]
            blocks.append(np.where(upper, tri - tri[split], tri[split] - tri))
    mall = np.concatenate(blocks, axis=0)
    x = t[:, None] ^ t[None, :]
    lvl = np.full((C, C), len(HG_LEVELS) + 1, np.int32)
    for li, m in enumerate(HG_LEVELS):
        lvl[(t[:, None] > t[None, :]) & (x >= m) & (x < 2 * m)] = li
    lvl[t[:, None] == t[None, :]] = len(HG_LEVELS)
    return jnp.asarray(mall, BF16), jnp.asarray(lvl)


def _hgrn_kernel(hq_ref, lf_ref, hk_ref, hv_ref, hg_ref, mall_ref, lvl_ref, onorm_ref, o_ref, st_ref):
    C = HG_CHUNK
    nl = len(HG_LEVELS)

    @pl.when(pl.program_id(1) == 0)
    def _():
        st_ref[...] = jnp.zeros_like(st_ref)

    lvl = lvl_ref[...]
    row = lax.broadcasted_iota(jnp.int32, (C, HG_DIM), 0)
    mall = mall_ref[...]

    for ci in range(hq_ref.shape[0] // C):
        rs = slice(ci * C, (ci + 1) * C)
        dheads = _dot(mall, lf_ref[rs, :])
        for h in range(HG_HEADS):
            cs = slice(h * HG_DIM, (h + 1) * HG_DIM)
            q, k, v = hq_ref[rs, cs], hk_ref[rs, cs], hv_ref[rs, cs]
            qf, kf = q.astype(F32), k.astype(F32)
            dall = dheads[:, cs]
            g = dall[0:C]

            a = jnp.zeros((C, C), F32)
            fine = 0
            for li, m in enumerate(HG_LEVELS):
                if m % 8 == 0:
                    qk = jnp.concatenate([(qf if (b & 1) else kf)[b * m:(b + 1) * m] for b in range(C // m)], axis=0)
                    parts = []
                    for b in range(C // (2 * m)):
                        lo_rows = slice(2 * b * m, (2 * b + 1) * m)
                        hi_rows = slice((2 * b + 1) * m, (2 * b + 2) * m)
                        ref = jnp.broadcast_to(g[lo_rows.stop - 1:lo_rows.stop, :], (m, HG_DIM))
                        parts += [ref - g[lo_rows], g[hi_rows] - ref]
                    dm = jnp.concatenate(parts, axis=0)
                else:
                    fine += 1
                    qk = jnp.where((row & m) != 0, qf, kf)
                    dm = dall[fine * C:(fine + 1) * C]
                xm = (qk * jnp.exp2(dm)).astype(BF16)
                a = jnp.where(lvl == li, _dot_nt(xm, xm), a)
            a = jnp.where(lvl == nl, _dot_nt(q, k), a)

            st = st_ref[h]
            o = _dot_nt((qf * jnp.exp2(g)).astype(BF16), st.astype(BF16)) + _dot(a.astype(BF16), v)
            g_last = g[C - 1:C, :]
            kd = (kf * jnp.exp2(g_last - g)).astype(BF16)
            st_ref[h] = jnp.exp2(g_last) * st + _dot(v.astype(F32).T.astype(BF16), kd)

            o = o * lax.rsqrt(jnp.mean(o * o, axis=-1, keepdims=True) + EPS)
            o_ref[rs, cs] = (o * onorm_ref[:, cs] * hg_ref[rs, cs]).astype(BF16)


def _hgrn(hq, lf, hk, hv, hg, onorm, B, T, ct):
    M = hq.shape[0]
    nct = T // ct
    mall, lvl = _hgrn_consts()
    blk = pl.BlockSpec((ct, GROUP), lambda b, c: (b * nct + c, 0))
    const = lambda b, c: (0, 0)
    return pl.pallas_call(
        _hgrn_kernel,
        grid=(B, nct),
        in_specs=[blk] * 5 + [pl.BlockSpec(mall.shape, const), pl.BlockSpec(lvl.shape, const),
                              pl.BlockSpec((1, GROUP), const)],
        out_specs=blk,
        out_shape=jax.ShapeDtypeStruct((M, GROUP), BF16),
        scratch_shapes=[pltpu.VMEM((HG_HEADS, HG_DIM, HG_DIM), F32)],
        compiler_params=_cparams(("parallel", "arbitrary")),
        name="hgrn2",
    )(hq, lf, hk, hv, hg, mall, lvl, onorm)


def _fold8(a, op, rows=8):
    chains = [None] * 4
    for r in range(a.shape[0] // rows):
        part = a[r * rows:(r + 1) * rows, :]
        c = r % len(chains)
        chains[c] = part if chains[c] is None else op(chains[c], part)
    return op(op(chains[0], chains[1]), op(chains[2], chains[3]))


def _floor_bf16(a):
    r = a.astype(BF16).astype(F32)
    below = (r - jnp.abs(r) * (5.0 / 1024.0)).astype(BF16)
    return jnp.where(r > a, below.astype(F32), r).astype(BF16)


def _dsa_kernel(iq_ref, iwt_ref, ik_ref, aq_ref, ak_ref, avt_ref, qnt_ref, knt_ref, o_ref,
                sc_ref, scb_ref, mm_ref, *head_refs, qb, kt, topk, seq):
    acc_refs, m_refs = head_refs[:ATT_HEADS], head_refs[ATT_HEADS:]
    i = pl.program_id(1)
    q0 = i * qb
    nfull = q0 // kt
    nkt = nfull + 1
    qpos = q0 + lax.broadcasted_iota(jnp.int32, (1, qb), 1)
    kk = jnp.minimum(qpos + 1, topk).astype(F32)

    mm_ref[0:8, :] = jnp.full((8, qb), jnp.inf, F32)
    mm_ref[8:16, :] = jnp.full((8, qb), -jnp.inf, F32)
    mm_ref[16:32, :] = jnp.zeros((16, qb), F32)

    def score_tile(j, diag):
        for r in range(kt // SCORE_SLAB):
            ks = pl.multiple_of(j * kt + r * SCORE_SLAB, SCORE_SLAB)
            rows = pl.ds(ks, SCORE_SLAB)
            ki = ik_ref[rows, :]
            s = None
            for h in range(IDX_HEADS):
                t = jnp.maximum(_dot_nt(ki, iq_ref[h]), 0.0) * iwt_ref[h:h + 1, :]
                s = t if s is None else s + t
            if diag:
                causal = ks + lax.broadcasted_iota(jnp.int32, (SCORE_SLAB, qb), 0) <= qpos
                s_hi = jnp.where(causal, s, -jnp.inf)
                s_lo = jnp.where(causal, s, jnp.inf)
            else:
                s_hi = s_lo = s
            sc_ref[rows, :] = s_hi
            scb_ref[rows, :] = _floor_bf16(s_hi)
            mm_ref[0:8, :] = jnp.minimum(mm_ref[0:8, :], _fold8(s_lo, jnp.minimum))
            mm_ref[8:16, :] = jnp.maximum(mm_ref[8:16, :], _fold8(s_hi, jnp.maximum))
            mm_ref[16:24, :] += _fold8(jnp.where(s_hi >= 0.0, 1.0, 0.0), jnp.add)
            mm_ref[24:32, :] += _fold8(jnp.where(s_hi > 0.0, 1.0, 0.0), jnp.add)

    def full_tile(j, c):
        score_tile(j, False)
        return c

    lax.fori_loop(0, nfull, full_tile, 0)
    score_tile(nfull, True)
    mn = jnp.min(mm_ref[0:8, :], axis=0, keepdims=True)
    mx = jnp.max(mm_ref[8:16, :], axis=0, keepdims=True)
    c_nonneg = jnp.sum(mm_ref[16:24, :], axis=0, keepdims=True)
    c_pos = jnp.sum(mm_ref[24:32, :], axis=0, keepdims=True)

    def tile_hits(ref, j, hit_fn, slab, rows, cols=slice(None)):
        acc = None
        for r in range(kt // slab):
            ks = pl.multiple_of(j * kt + r * slab, slab)
            part = _fold8(hit_fn(ref[pl.ds(ks, slab), cols]), jnp.add, rows=rows)
            acc = part if acc is None else acc + part
        return acc

    def over_tiles(body, init):
        def group(jg, acc):
            for u in range(TILE_GROUP):
                acc = body(jg * TILE_GROUP + u, acc)
            return acc
        ngroups = nkt // TILE_GROUP
        return lax.fori_loop(ngroups * TILE_GROUP, nkt, body, lax.fori_loop(0, ngroups, group, init))

    def count_ge(th):
        def body(j, acc):
            return acc + tile_hits(sc_ref, j, lambda s: jnp.where(s >= th, 1.0, 0.0), COUNT_SLAB, 8)
        return jnp.sum(over_tiles(body, jnp.zeros((8, qb), F32)), axis=0, keepdims=True)

    above = c_pos >= kk
    below = c_nonneg < kk
    live = above | below
    zero = jnp.zeros((1, qb), F32)
    lo0 = jnp.where(below, mn, zero)
    cnt_lo0 = jnp.where(below, (qpos + 1).astype(F32), c_nonneg)
    hi0 = jnp.where(above, mx + jnp.abs(mx) * 1e-6 + 1e-30, zero)
    cnt_hi0 = jnp.where(above, zero, jnp.where(below, c_nonneg, c_pos))

    def midpoint(lo, hi):
        return lo + 0.5 * (hi - lo)

    assert kt // 16 <= 256

    def count_ge_bf16(th):
        def body(j, acc):
            hit = lambda s: jnp.where(s >= th, jnp.ones((), BF16), jnp.zeros((), BF16))
            part = tile_hits(scb_ref, j, hit, 2 * COUNT_SLAB, 16).astype(F32)
            return acc + part[0:8, :] + part[8:16, :]
        return jnp.sum(over_tiles(body, jnp.zeros((8, qb), F32)), axis=0, keepdims=True)

    def coarse_body(_, c):
        lo, hi, cnt_lo, cnt_hi = c
        th = midpoint(lo, hi).astype(BF16)
        mid = th.astype(F32)
        inside = live & (mid > lo) & (mid < hi)
        cnt = count_ge_bf16(th)
        up = inside & (cnt >= kk)
        dn = inside & (cnt < kk)
        return (jnp.where(up, mid, lo), jnp.where(dn, mid, hi),
                jnp.where(up, cnt, cnt_lo), jnp.where(dn, cnt, cnt_hi))

    lo1, hi1, cnt_lo1, cnt_hi1 = lax.fori_loop(0, COARSE_PASSES, coarse_body, (lo0, hi0, cnt_lo0, cnt_hi0))

    def fine_pass(c):
        lo, hi, cnt_lo, cnt_hi = c
        mid = midpoint(lo, hi)
        cnt = count_ge(mid)
        up = live & (cnt >= kk)
        dn = live & (cnt < kk)
        return (jnp.where(up, mid, lo), jnp.where(dn, mid, hi), jnp.where(up, cnt, cnt_lo), jnp.where(dn, cnt, cnt_hi))

    c2 = lax.fori_loop(0, FINE_PASSES, lambda _, c: fine_pass(c), (lo1, hi1, cnt_lo1, cnt_hi1))

    def finish_chunk(cs):
        kk_c, live_c = kk[:, cs], live[:, cs]

        def cond(c):
            it, (lo_c, hi_c, cnt_c, _) = c
            mid = midpoint(lo_c, hi_c)
            open_ = live_c & (cnt_c != kk_c) & (mid > lo_c) & (mid < hi_c)
            return (it < 320) & (jnp.max(open_.astype(jnp.int32)) > 0)

        def body(c):
            it, (lo_c, hi_c, cnt_lo_c, cnt_hi_c) = c
            mid = midpoint(lo_c, hi_c)
            tile = lambda j, acc: acc + tile_hits(sc_ref, j, lambda s: jnp.where(s >= mid, 1.0, 0.0), COUNT_SLAB, 8, cs)
            cnt = jnp.sum(over_tiles(tile, jnp.zeros((8, LANES), F32)), axis=0, keepdims=True)
            up = live_c & (cnt >= kk_c)
            dn = live_c & (cnt < kk_c)
            return it + 1, (jnp.where(up, mid, lo_c), jnp.where(dn, mid, hi_c),
                            jnp.where(up, cnt, cnt_lo_c), jnp.where(dn, cnt, cnt_hi_c))

        return lax.while_loop(cond, body, (jnp.int32(0), tuple(a[:, cs] for a in c2)))[1]

    chunks = [slice(c * LANES, (c + 1) * LANES) for c in range(qb // LANES)]
    lo, hi, cnt_lo, cnt_hi = (jnp.concatenate(parts, axis=1) for parts in zip(*[finish_chunk(cs) for cs in chunks]))

    tied = cnt_lo != kk
    need = kk - cnt_hi

    def resolve_ties(cs):
        lo_c, need_c = lo[:, cs], need[:, cs]
        tri = jnp.where(lax.broadcasted_iota(jnp.int32, (kt, kt), 0) >= lax.broadcasted_iota(jnp.int32, (kt, kt), 1),
                        1.0, 0.0).astype(BF16)

        def strike(j, seen):
            rows = pl.ds(pl.multiple_of(j * kt, kt), kt)
            s = sc_ref[rows, cs]
            eq = s == lo_c
            rank = seen + _dot(tri, jnp.where(eq, 1.0, 0.0).astype(BF16))
            sc_ref[rows, cs] = jnp.where(eq & (rank > need_c), -jnp.inf, s)
            return rank[kt - 1:kt, :]

        lax.fori_loop(0, nkt, strike, jnp.zeros((1, LANES), F32))

    for cs in chunks:
        pl.when(jnp.max(tied[:, cs].astype(jnp.int32)) > 0)(functools.partial(resolve_ties, cs))

    def logits(j, h, bias):
        ks = pl.multiple_of(j * kt, kt)
        sl = slice(h * ATT_DIM, (h + 1) * ATT_DIM)
        return _dot_nt(ak_ref[pl.ds(ks, kt), sl], aq_ref[:, sl]) + bias

    def select_bias(j):
        ks = pl.multiple_of(j * kt, kt)
        return jnp.where(sc_ref[pl.ds(ks, kt), :] >= lo, 0.0, NEG_BIG)

    def values(j, h):
        return avt_ref[0, j, h * V_ROWS:(h + 1) * V_ROWS, :]

    kmax = jnp.max(knt_ref[...], axis=1, keepdims=True)
    shift = [qnt_ref[h:h + 1, :] * kmax[ATT_HEADS + h:ATT_HEADS + h + 1, :] * NORM_SLACK for h in range(ATT_HEADS)]
    for h in range(ATT_HEADS):
        acc_refs[h][...] = jnp.zeros_like(acc_refs[h])

    def attn_tile_fast(j, c):
        bias = select_bias(j)
        for h in range(ATT_HEADS):
            p = jnp.exp2(logits(j, h, bias) - shift[h]).astype(BF16)
            acc_refs[h][...] += _dot(values(j, h), p)
        return c

    lax.fori_loop(0, nkt, attn_tile_fast, 0)
    denom_min = functools.reduce(jnp.minimum, [acc_refs[h][ATT_DIM:ATT_DIM + 1, :] for h in range(ATT_HEADS)])

    @pl.when(jnp.min(denom_min) < DENOM_FLOOR)
    def _():
        for h in range(ATT_HEADS):
            acc_refs[h][...] = jnp.zeros_like(acc_refs[h])
            m_refs[h][...] = jnp.full(m_refs[h].shape, NEG_BIG, F32)

        def attn_tile(j, c):
            bias = select_bias(j)
            for h in range(ATT_HEADS):
                acc_h, m_h = acc_refs[h], m_refs[h]
                lg = logits(j, h, bias)
                m_old = m_h[0:1, :]
                m_new = jnp.maximum(m_old, jnp.max(_fold8(lg, jnp.maximum), axis=0, keepdims=True))
                p = jnp.exp2(lg - m_new).astype(BF16)
                acc_h[...] = jnp.exp2(m_old - m_new) * acc_h[...] + _dot(values(j, h), p)
                m_h[0:1, :] = m_new
            return c

        lax.fori_loop(0, nkt, attn_tile, 0)

    for h in range(ATT_HEADS):
        o_t = acc_refs[h][0:ATT_DIM, :] / acc_refs[h][ATT_DIM:ATT_DIM + 1, :]
        o_ref[:, h * ATT_DIM:(h + 1) * ATT_DIM] = o_t.T.astype(BF16)


def _dsa(iq, iwt, ik, aq, ak, avt, nrm, B, T, qb, kt):
    M = ak.shape[0]
    nq = T // qb
    assert kt % qb == 0 and T % kt == 0 and avt.shape[1:] == (T // kt, ATT_HEADS * V_ROWS, kt)
    topk = min(TOPK_MAX, T // 4)
    qrow = lambda b, i: (b * nq + i, 0)
    brow = lambda b, i: (b, 0)
    kern = functools.partial(_dsa_kernel, qb=qb, kt=kt, topk=topk, seq=T)
    return pl.pallas_call(
        kern,
        grid=(B, nq),
        in_specs=[pl.BlockSpec((IDX_HEADS, qb, IDX_DIM), lambda b, i: (0, b * nq + i, 0)),
                  pl.BlockSpec((IDX_HEADS, qb), lambda b, i: (0, b * nq + i)),
                  _resident((T, IDX_DIM), brow),
                  pl.BlockSpec((qb, GROUP), qrow),
                  _resident((T, GROUP), brow),
                  _resident((1,) + avt.shape[1:], lambda b, i: (b, 0, 0, 0)),
                  pl.BlockSpec((2 * ATT_HEADS, qb), lambda b, i: (0, b * nq + i)),
                  _resident((2 * ATT_HEADS, T), lambda b, i: (0, b))],
        out_specs=pl.BlockSpec((qb, GROUP), qrow),
        out_shape=jax.ShapeDtypeStruct((M, GROUP), BF16),
        scratch_shapes=[pltpu.VMEM((T, qb), F32),
                        pltpu.VMEM((T, qb), BF16),
                        pltpu.VMEM((32, qb), F32),
                        ] + [pltpu.VMEM((V_ROWS, qb), F32)] * ATT_HEADS
                        + [pltpu.VMEM((8, qb), F32)] * ATT_HEADS,
        compiler_params=_cparams(("parallel", "arbitrary")),
        name="dsa",
    )(iq, iwt, ik, aq, ak, avt, nrm, nrm)


def _rms(a, gain):
    return a * lax.rsqrt(jnp.mean(a * a, axis=-1, keepdims=True) + EPS) * gain


def _out_ffn_kernel(ohg_ref, oatt_ref, x_ref, mod_ref, gains_ref, wo_ref, wi_ref, wf_ref, o_ref, acc_ref,
                    *, d_ff, fc):
    mod = mod_ref[0]
    gains = gains_ref[...]
    hw = ohg_ref.shape[1]
    y = _dot(ohg_ref[...], wo_ref[0:hw, :]) + _dot(oatt_ref[...], wo_ref[hw:, :])
    x1 = x_ref[...] + mod[2:3, :] * _rms(y, gains[0:1, :])
    h2 = (_rms(x1, gains[1:2, :]) * (1.0 + mod[4:5, :]) + mod[3:4, :]).astype(BF16)
    for c in range(d_ff // fc):
        gate = _dot(h2, wi_ref[:, c * fc:(c + 1) * fc])
        up = _dot(h2, wi_ref[:, d_ff + c * fc:d_ff + (c + 1) * fc])
        part = _dot((_silu(gate) * up).astype(BF16), wf_ref[c * fc:(c + 1) * fc, :])
        if c == 0:
            acc_ref[...] = part
        else:
            acc_ref[...] += part
    o_ref[...] = x1 + mod[5:6, :] * _rms(acc_ref[...], gains[2:3, :])


def _out_ffn(ohg, oatt, x2, mod3, gains, wo, wi, wf, T, tm):
    M, D = x2.shape
    nt = T // tm
    d_ff = wf.shape[0]
    row = lambda i: (i, 0)
    const = lambda i: (0, 0)
    assert d_ff % FFN_CHUNK == 0
    kern = functools.partial(_out_ffn_kernel, d_ff=d_ff, fc=FFN_CHUNK)
    return pl.pallas_call(
        kern,
        grid=(M // tm,),
        in_specs=[pl.BlockSpec((tm, GROUP), row),
                  pl.BlockSpec((tm, GROUP), row),
                  pl.BlockSpec((tm, D), row),
                  pl.BlockSpec((1, N_MOD, D), lambda i: (i // nt, 0, 0)),
                  pl.BlockSpec((3, D), const),
                  _resident(wo.shape, const),
                  _resident(wi.shape, const),
                  _resident(wf.shape, const)],
        out_specs=pl.BlockSpec((tm, D), row),
        out_shape=jax.ShapeDtypeStruct((M, D), F32),
        scratch_shapes=[pltpu.VMEM((tm, D), F32)],
        compiler_params=_cparams(("parallel",)),
        name="out_ffn",
    )(ohg, oatt, x2, mod3, gains, wo, wi, wf)


def kernel(x, c, positions, w_ada, b_ada, norm_pre_mix, norm_post_mix, norm_pre_ffn, norm_post_ffn, w_in,
           hgrn_lower_bound, hgrn_out_norm, idx_k_norm_w, idx_k_norm_b, w_out, w_ffn_in, w_ffn_out):
    B, T, D = x.shape
    depth = w_ada.shape[0]
    assert depth == 1 and hgrn_lower_bound.shape[0] == 2
    tm = ROW_TILE
    assert T % tm == 0 and T % KEY_TILE == 0 and T % QUERY_BLOCK == 0 and D % LANES == 0 and tm == KEY_TILE

    mod3 = _adaln(c, w_ada, b_ada).reshape(B, N_MOD, D)
    tab = _rope_table(positions)

    main = 8 * GROUP
    w = w_in.reshape(w_in.shape[1:])
    w_main = w[:, :main].astype(BF16)
    w_tail = jnp.concatenate([jnp.pad(w[:, main:main + IDX_DIM], ((0, 0), (0, LANES - IDX_DIM))),
                              jnp.pad(w[:, main + IDX_DIM:], ((0, 0), (0, LANES - IDX_HEADS)))], axis=1).astype(BF16)
    lnw = jnp.pad(idx_k_norm_w[0], (0, LANES - IDX_DIM)).reshape(1, LANES)
    lnb = jnp.pad(idx_k_norm_b[0], (0, LANES - IDX_DIM)).reshape(1, LANES)

    x2 = x.reshape(B * T, D)
    hq, lf, hk, hv, hg, aq, ak, avt, iq, ik, iwt, nrm = _in_proj(
        x2, mod3, norm_pre_mix, w_main, w_tail, hgrn_lower_bound, lnw, lnb, tab, T, tm, tm)

    o_hg = _hgrn(hq, lf, hk, hv, hg, hgrn_out_norm, B, T, tm)
    o_att = _dsa(iq, iwt, ik, aq, ak, avt, nrm, B, T, QUERY_BLOCK, KEY_TILE)

    gains = jnp.concatenate([norm_post_mix, norm_pre_ffn, norm_post_ffn], axis=0)
    out = _out_ffn(o_hg, o_att, x2, mod3, gains, w_out.reshape(w_out.shape[1:]).astype(BF16),
                   w_ffn_in.reshape(w_ffn_in.shape[1:]).astype(BF16),
                   w_ffn_out.reshape(w_ffn_out.shape[1:]).astype(BF16), T, tm)
    return out.reshape(B, T, D)
```

```python
import functools

import numpy as np
import jax
import jax.numpy as jnp
from jax import lax
from jax.experimental import pallas as pl
from jax.experimental.pallas import tpu as pltpu

F32 = jnp.float32
BF16 = jnp.bfloat16

HG_HEADS = 4
HG_DIM = 128
ATT_HEADS = 4
ATT_DIM = 128
IDX_HEADS = 8
IDX_DIM = 64
TOPK_MAX = 256
ROPE_THETA = 500000.0
ROPE_FRACTION = 4
N_MOD = 6
EPS = 1e-6

GROUP = 512
LANES = 128
VMEM_LIMIT = 56 * 1024 * 1024

HG_CHUNK = 128
HG_LEVELS = (64, 32, 16, 8, 4, 2, 1)

NEG_BIG = -1e30
DENOM_FLOOR = 2.0 ** -64
NORM_SLACK = 1.01
COARSE_PASSES = 10
FINE_PASSES = 10
COUNT_SLAB = 64
TILE_GROUP = 4
ATT_GROUP = 2

ROW_TILE = 512
KEY_TILE = 512
QUERY_BLOCK = 512
SCORE_GROUP = 1
FFN_CHUNK = 256
Q_SCALE = ATT_DIM ** -0.5 * 1.4426950408889634
V_ROWS = ATT_DIM + 16


def _cparams(sem):
    return pltpu.CompilerParams(dimension_semantics=sem, vmem_limit_bytes=VMEM_LIMIT)


def _resident(shape, index_map):
    return pl.BlockSpec(shape, index_map, pipeline_mode=pl.Buffered(1))


def _split_bf16(a):
    hi = a.astype(BF16)
    lo = (a - hi.astype(F32)).astype(BF16)
    return hi, lo


def _dot(a, b):
    return jnp.dot(a, b, preferred_element_type=F32)


def _dot_nt(a, b):
    return lax.dot_general(a, b, (((1,), (1,)), ((), ())), preferred_element_type=F32)


def _silu(a):
    return a * jax.nn.sigmoid(a)


def _adaln_kernel(c_ref, w_ref, b_ref, o_ref):
    a = _silu(c_ref[...])
    a_hi, a_lo = _split_bf16(a)
    w_hi, w_lo = _split_bf16(w_ref[0])
    acc = _dot(a_hi, w_hi) + (_dot(a_hi, w_lo) + _dot(a_lo, w_hi))
    o_ref[...] = acc + b_ref[...]


def _adaln(c, w, b):
    B, D = c.shape
    N = w.shape[2]
    return pl.pallas_call(
        _adaln_kernel,
        grid=(N // D,),
        in_specs=[pl.BlockSpec((B, D), lambda j: (0, 0)),
                  pl.BlockSpec((1, D, D), lambda j: (0, 0, j)),
                  pl.BlockSpec((1, D), lambda j: (0, j))],
        out_specs=pl.BlockSpec((B, D), lambda j: (0, j)),
        out_shape=jax.ShapeDtypeStruct((B, N), F32),
        compiler_params=_cparams(("arbitrary",)),
        name="adaln",
    )(c, w, b)


def _rope_table_kernel(pos_ref, fa_ref, fb_ref, o_ref):
    pos = pos_ref[0].astype(F32)
    ang_a = fa_ref[...] * pos
    ang_b = fb_ref[...] * pos
    ha = fa_ref.shape[0]
    hb = fb_ref.shape[0]
    o_ref[0, 0:ha, :] = jnp.cos(ang_a)
    o_ref[0, ha:2 * ha, :] = jnp.sin(ang_a)
    o_ref[0, 2 * ha:2 * ha + hb, :] = jnp.cos(ang_b)
    o_ref[0, 2 * ha + hb:2 * ha + 2 * hb, :] = jnp.sin(ang_b)
    o_ref[0, 2 * ha + 2 * hb:, :] = jnp.zeros((LANES - 2 * ha - 2 * hb, pos.shape[1]), F32)


ROPE_HALF_A = ATT_DIM // ROPE_FRACTION // 2
ROPE_HALF_B = IDX_DIM // ROPE_FRACTION // 2


def _rope_table(positions):
    B, T = positions.shape
    ha, hb = ROPE_HALF_A, ROPE_HALF_B
    fa = (ROPE_THETA ** (-jnp.arange(ha, dtype=F32) / ha)).reshape(ha, 1)
    fb = (ROPE_THETA ** (-jnp.arange(hb, dtype=F32) / hb)).reshape(hb, 1)
    return pl.pallas_call(
        _rope_table_kernel,
        grid=(B,),
        in_specs=[pl.BlockSpec((1, 1, T), lambda b: (b, 0, 0)),
                  pl.BlockSpec((ha, 1), lambda b: (0, 0)),
                  pl.BlockSpec((hb, 1), lambda b: (0, 0))],
        out_specs=pl.BlockSpec((1, LANES, T), lambda b: (b, 0, 0)),
        out_shape=jax.ShapeDtypeStruct((B, LANES, T), F32),
        compiler_params=_cparams(("arbitrary",)),
        name="rope_table",
    )(positions.reshape(B, 1, T), fa, fb)


def _rope_patterns(tab):
    ha, hb = ROPE_HALF_A, ROPE_HALF_B
    lane = lax.broadcasted_iota(jnp.int32, tab.shape, 1)
    shifted = lambda sh: pltpu.roll(tab, sh % LANES, axis=1)
    cos_a = jnp.where(lane < ha, tab, jnp.where(lane < 2 * ha, shifted(ha), 1.0))
    sin_a = jnp.where(lane < ha, -shifted(-ha), jnp.where(lane < 2 * ha, tab, 0.0))
    l64 = lane & (IDX_DIM - 1)
    first = lane < IDX_DIM
    cb0, sb0 = 2 * ha, 2 * ha + hb
    cos_b = jnp.where(l64 < hb, jnp.where(first, shifted(-cb0), shifted(IDX_DIM - cb0)),
                      jnp.where(l64 < 2 * hb, jnp.where(first, shifted(hb - cb0), shifted(IDX_DIM + hb - cb0)), 1.0))
    sin_b = jnp.where(l64 < hb, -jnp.where(first, shifted(-sb0), shifted(IDX_DIM - sb0)),
                      jnp.where(l64 < 2 * hb, jnp.where(first, shifted(hb - sb0), shifted(IDX_DIM + hb - sb0)), 0.0))
    return cos_a, sin_a, cos_b, sin_b


def _rope(xb, cos, sin, half, period):
    lane = lax.broadcasted_iota(jnp.int32, xb.shape, 1)
    fwd = pltpu.roll(xb, LANES - half, axis=1)
    bwd = pltpu.roll(xb, half, axis=1)
    partner = jnp.where((lane & (period - 1)) < half, fwd, bwd)
    return xb * cos + partner * sin


def _in_proj_kernel(x_ref, mod_ref, gain_ref, w_ref, wt_ref, lbp_ref, lnw_ref, lnb_ref, tab_ref,
                    hq_ref, lf_ref, hk_ref, hv_ref, hg_ref,
                    aq_ref, ak_ref, avt_ref, iq_ref, ik_ref, iwt_ref, nrm_ref):
    x = x_ref[...]
    ms = jnp.mean(x * x, axis=-1, keepdims=True)
    mod = mod_ref[0]
    h = x * lax.rsqrt(ms + EPS) * gain_ref[...]
    h = h * (1.0 + mod[1:2, :]) + mod[0:1, :]
    hb = h.astype(BF16)

    def proj(g, width=GROUP):
        return _dot(hb, w_ref[:, g * GROUP:g * GROUP + width])

    hq_ref[...] = (_silu(proj(0)) * (HG_DIM ** -0.5)).astype(BF16)
    a = lbp_ref[...]
    amax = jnp.max(a, axis=0, keepdims=True)
    e = jnp.exp(a - amax)
    lb = e[0:1, :] / jnp.sum(e, axis=0, keepdims=True)
    fr = proj(1)
    f = lb + (1.0 - lb) * jax.nn.sigmoid(fr)
    lf_ref[...] = jnp.log2(f).astype(BF16)
    hk_ref[...] = ((1.0 - lb) * jax.nn.sigmoid(-fr)).astype(BF16)
    hv_ref[...] = proj(2).astype(BF16)
    hg_ref[...] = _silu(proj(3)).astype(BF16)

    cosa, sina, cosb, sinb = _rope_patterns(tab_ref[0].T)
    half_a = ROPE_HALF_A
    q = proj(4)
    k = proj(5)
    lane_a = lax.broadcasted_iota(jnp.int32, cosa.shape, 1)
    norms = jnp.zeros(cosa.shape, F32)
    for hh in range(ATT_HEADS):
        sl = slice(hh * ATT_DIM, (hh + 1) * ATT_DIM)
        qh = (_rope(q[:, sl], cosa, sina, half_a, ATT_DIM) * Q_SCALE).astype(BF16)
        kh = _rope(k[:, sl], cosa, sina, half_a, ATT_DIM).astype(BF16)
        aq_ref[:, sl] = qh
        ak_ref[:, sl] = kh
        for slot, a in ((hh, qh), (ATT_HEADS + hh, kh)):
            norms = jnp.where(lane_a == slot, jnp.sqrt(_dot(a * a, jnp.ones((ATT_DIM, LANES), BF16))), norms)
    nrm_ref[...] = norms.T[0:2 * ATT_HEADS, :]
    v = proj(6)
    for hh in range(ATT_HEADS):
        r0 = hh * V_ROWS
        avt_ref[0, 0, r0:r0 + ATT_DIM, :] = v[:, hh * ATT_DIM:(hh + 1) * ATT_DIM].T.astype(BF16)
        avt_ref[0, 0, r0 + ATT_DIM:r0 + V_ROWS, :] = jnp.ones((V_ROWS - ATT_DIM, v.shape[0]), BF16)

    half_b = ROPE_HALF_B
    qi = proj(7)
    for cc in range(GROUP // LANES):
        qr = _rope(qi[:, cc * LANES:(cc + 1) * LANES], cosb, sinb, half_b, IDX_DIM).astype(BF16)
        for hh in range(LANES // IDX_DIM):
            iq_ref[cc * (LANES // IDX_DIM) + hh] = qr[:, hh * IDX_DIM:(hh + 1) * IDX_DIM]
    kw = _dot(hb, wt_ref[...])
    ki = kw[:, 0:LANES]
    lane = lax.broadcasted_iota(jnp.int32, ki.shape, 1)
    real = lane < IDX_DIM
    mu = jnp.sum(ki, axis=-1, keepdims=True) * (1.0 / IDX_DIM)
    d = jnp.where(real, ki - mu, 0.0)
    var = jnp.sum(d * d, axis=-1, keepdims=True) * (1.0 / IDX_DIM)
    kn = d * lax.rsqrt(var + EPS) * lnw_ref[...] + lnb_ref[...]
    kn = _rope(kn, cosb, sinb, half_b, IDX_DIM)
    ik_ref[...] = kn[:, 0:IDX_DIM].astype(BF16)
    iwt_ref[...] = kw[:, LANES:2 * LANES].T[0:IDX_HEADS, :] * (IDX_HEADS ** -0.5 * IDX_DIM ** -0.5)


def _in_proj(x2, mod3, gain, w_main, w_tail, lbp, lnw, lnb, tab, T, tm, kt):
    M, D = x2.shape
    nt = T // tm
    row = lambda i: (i, 0)
    const = lambda i: (0, 0)
    bf16o = jax.ShapeDtypeStruct((M, GROUP), BF16)
    grp_spec = pl.BlockSpec((tm, GROUP), row)
    return pl.pallas_call(
        _in_proj_kernel,
        grid=(M // tm,),
        in_specs=[pl.BlockSpec((tm, D), row),
                  pl.BlockSpec((1, N_MOD, D), lambda i: (i // nt, 0, 0)),
                  pl.BlockSpec((1, D), const),
                  _resident(w_main.shape, const),
                  _resident(w_tail.shape, const),
                  pl.BlockSpec(lbp.shape, const),
                  pl.BlockSpec((1, LANES), const),
                  pl.BlockSpec((1, LANES), const),
                  pl.BlockSpec((1, LANES, tm), lambda i: (i // nt, 0, i % nt))],
        out_specs=[grp_spec] * 7 + [pl.BlockSpec((1, 1, ATT_HEADS * V_ROWS, tm),
                                                 lambda i: (i // nt, (i % nt) // (kt // tm), 0, (i % nt) % (kt // tm))),
                                    pl.BlockSpec((IDX_HEADS, tm, IDX_DIM), lambda i: (0, i, 0)),
                                    pl.BlockSpec((tm, IDX_DIM), row),
                                    pl.BlockSpec((IDX_HEADS, tm), lambda i: (0, i)),
                                    pl.BlockSpec((2 * ATT_HEADS, tm), lambda i: (0, i))],
        out_shape=[bf16o] * 7 + [jax.ShapeDtypeStruct((M // T, T // kt, ATT_HEADS * V_ROWS, kt), BF16),
                                 jax.ShapeDtypeStruct((IDX_HEADS, M, IDX_DIM), BF16),
                                 jax.ShapeDtypeStruct((M, IDX_DIM), BF16),
                                 jax.ShapeDtypeStruct((IDX_HEADS, M), F32),
                                 jax.ShapeDtypeStruct((2 * ATT_HEADS, M), F32)],
        compiler_params=_cparams(("parallel",)),
        name="in_proj",
    )(x2, mod3, gain, w_main, w_tail, lbp, lnw, lnb, tab)


def _hgrn_consts():
    C = HG_CHUNK
    t = np.arange(C)
    tri = (t[None, :] <= t[:, None]).astype(np.float32)
    blocks = [tri]
    for m in HG_LEVELS:
        if m % 8:
            split = (t // (2 * m)) * (2 * m) + m - 1
            upper = ((t & m) != 0)[:, None]
            blocks.append(np.where(upper, tri - tri[split], tri[split] - tri))
    mall = np.concatenate(blocks, axis=0)
    x = t[:, None] ^ t[None, :]
    lvl = np.full((C, C), len(HG_LEVELS) + 1, np.int32)
    for li, m in enumerate(HG_LEVELS):
        lvl[(t[:, None] > t[None, :]) & (x >= m) & (x < 2 * m)] = li
    lvl[t[:, None] == t[None, :]] = len(HG_LEVELS)
    return jnp.asarray(mall, BF16), jnp.asarray(lvl)


def _hgrn_kernel(hq_ref, lf_ref, hk_ref, hv_ref, hg_ref, mall_ref, lvl_ref, onorm_ref, o_ref, st_ref):
    C = HG_CHUNK
    nl = len(HG_LEVELS)

    @pl.when(pl.program_id(1) == 0)
    def _():
        st_ref[...] = jnp.zeros_like(st_ref)

    lvl = lvl_ref[...]
    row = lax.broadcasted_iota(jnp.int32, (C, HG_DIM), 0)
    mall = mall_ref[...]

    for ci in range(hq_ref.shape[0] // C):
        rs = slice(ci * C, (ci + 1) * C)
        dheads = _dot(mall, lf_ref[rs, :])
        for h in range(HG_HEADS):
            cs = slice(h * HG_DIM, (h + 1) * HG_DIM)
            q, k, v = hq_ref[rs, cs], hk_ref[rs, cs], hv_ref[rs, cs]
            qf, kf = q.astype(F32), k.astype(F32)
            dall = dheads[:, cs]
            g = dall[0:C]

            a = jnp.zeros((C, C), F32)
            fine = 0
            for li, m in enumerate(HG_LEVELS):
                if m % 8 == 0:
                    qk = jnp.concatenate([(qf if (b & 1) else kf)[b * m:(b + 1) * m] for b in range(C // m)], axis=0)
                    parts = []
                    for b in range(C // (2 * m)):
                        lo_rows = slice(2 * b * m, (2 * b + 1) * m)
                        hi_rows = slice((2 * b + 1) * m, (2 * b + 2) * m)
                        ref = jnp.broadcast_to(g[lo_rows.stop - 1:lo_rows.stop, :], (m, HG_DIM))
                        parts += [ref - g[lo_rows], g[hi_rows] - ref]
                    dm = jnp.concatenate(parts, axis=0)
                else:
                    fine += 1
                    qk = jnp.where((row & m) != 0, qf, kf)
                    dm = dall[fine * C:(fine + 1) * C]
                xm = (qk * jnp.exp2(dm)).astype(BF16)
                a = jnp.where(lvl == li, _dot_nt(xm, xm), a)
            a = jnp.where(lvl == nl, _dot_nt(q, k), a)

            st = st_ref[h]
            o = _dot_nt((qf * jnp.exp2(g)).astype(BF16), st.astype(BF16)) + _dot(a.astype(BF16), v)
            g_last = g[C - 1:C, :]
            kd = (kf * jnp.exp2(g_last - g)).astype(BF16)
            st_ref[h] = jnp.exp2(g_last) * st + _dot(v.astype(F32).T.astype(BF16), kd)

            o = o * lax.rsqrt(jnp.mean(o * o, axis=-1, keepdims=True) + EPS)
            o_ref[rs, cs] = (o * onorm_ref[:, cs] * hg_ref[rs, cs]).astype(BF16)


def _hgrn(hq, lf, hk, hv, hg, onorm, B, T, ct):
    M = hq.shape[0]
    nct = T // ct
    mall, lvl = _hgrn_consts()
    blk = pl.BlockSpec((ct, GROUP), lambda b, c: (b * nct + c, 0))
    const = lambda b, c: (0, 0)
    return pl.pallas_call(
        _hgrn_kernel,
        grid=(B, nct),
        in_specs=[blk] * 5 + [pl.BlockSpec(mall.shape, const), pl.BlockSpec(lvl.shape, const),
                              pl.BlockSpec((1, GROUP), const)],
        out_specs=blk,
        out_shape=jax.ShapeDtypeStruct((M, GROUP), BF16),
        scratch_shapes=[pltpu.VMEM((HG_HEADS, HG_DIM, HG_DIM), F32)],
        compiler_params=_cparams(("parallel", "arbitrary")),
        name="hgrn2",
    )(hq, lf, hk, hv, hg, mall, lvl, onorm)


def _fold8(a, op, rows=8):
    chains = [None] * 4
    for r in range(a.shape[0] // rows):
        part = a[r * rows:(r + 1) * rows, :]
        c = r % len(chains)
        chains[c] = part if chains[c] is None else op(chains[c], part)
    return op(op(chains[0], chains[1]), op(chains[2], chains[3]))


def _floor_bf16(a):
    r = a.astype(BF16).astype(F32)
    below = (r - jnp.abs(r) * (5.0 / 1024.0)).astype(BF16)
    return jnp.where(r > a, below.astype(F32), r).astype(BF16)


def _dsa_kernel(iq_ref, iwt_ref, ik_ref, aq_ref, ak_ref, avt_ref, qnt_ref, knt_ref, o_ref,
                sc_ref, scb_ref, mm_ref, *head_refs, qb, kt, topk, seq):
    acc_refs, m_refs = head_refs[:ATT_HEADS], head_refs[ATT_HEADS:]
    i = pl.program_id(1)
    q0 = i * qb
    nfull = q0 // kt
    nkt = nfull + 1
    qpos = q0 + lax.broadcasted_iota(jnp.int32, (1, qb), 1)
    kk = jnp.minimum(qpos + 1, topk).astype(F32)

    mm_ref[0:8, :] = jnp.full((8, qb), jnp.inf, F32)
    mm_ref[8:16, :] = jnp.full((8, qb), -jnp.inf, F32)
    mm_ref[16:32, :] = jnp.zeros((16, qb), F32)

    def score_keys(j0, ntile, diag):
        nrow = ntile * kt
        ks = pl.multiple_of(j0 * kt, kt)
        rows = pl.ds(ks, nrow)
        ki = ik_ref[rows, :]
        s = None
        for h in range(IDX_HEADS):
            t = jnp.maximum(_dot_nt(ki, iq_ref[h]), 0.0) * iwt_ref[h:h + 1, :]
            s = t if s is None else s + t
        if diag:
            causal = ks + lax.broadcasted_iota(jnp.int32, (nrow, qb), 0) <= qpos
            s_hi = jnp.where(causal, s, -jnp.inf)
            s_lo = jnp.where(causal, s, jnp.inf)
        else:
            s_hi = s_lo = s
        sc_ref[rows, :] = s_hi
        scb_ref[rows, :] = _floor_bf16(s_hi)
        mm_ref[0:8, :] = jnp.minimum(mm_ref[0:8, :], _fold8(s_lo, jnp.minimum))
        mm_ref[8:16, :] = jnp.maximum(mm_ref[8:16, :], _fold8(s_hi, jnp.maximum))
        mm_ref[16:24, :] += _fold8(jnp.where(s_hi >= 0.0, 1.0, 0.0), jnp.add)
        mm_ref[24:32, :] += _fold8(jnp.where(s_hi > 0.0, 1.0, 0.0), jnp.add)

    def score_group(jg, c):
        score_keys(SCORE_GROUP * jg, SCORE_GROUP, False)
        return c

    lax.fori_loop(0, nfull // SCORE_GROUP, score_group, 0)
    done = (nfull // SCORE_GROUP) * SCORE_GROUP
    size = SCORE_GROUP // 2
    while size:
        pl.when(((nfull - done) & size) != 0)(functools.partial(score_keys, done, size, False))
        done = done + ((nfull - done) & size)
        size //= 2
    score_keys(nfull, 1, True)
    mn = jnp.min(mm_ref[0:8, :], axis=0, keepdims=True)
    mx = jnp.max(mm_ref[8:16, :], axis=0, keepdims=True)
    c_nonneg = jnp.sum(mm_ref[16:24, :], axis=0, keepdims=True)
    c_pos = jnp.sum(mm_ref[24:32, :], axis=0, keepdims=True)

    def tile_hits(ref, j, hit_fn, slab, rows, cols=slice(None)):
        acc = None
        for r in range(kt // slab):
            ks = pl.multiple_of(j * kt + r * slab, slab)
            part = _fold8(hit_fn(ref[pl.ds(ks, slab), cols]), jnp.add, rows=rows)
            acc = part if acc is None else acc + part
        return acc

    def over_tiles(body, init):
        def group(jg, acc):
            for u in range(TILE_GROUP):
                acc = body(jg * TILE_GROUP + u, acc)
            return acc
        ngroups = nkt // TILE_GROUP
        return lax.fori_loop(ngroups * TILE_GROUP, nkt, body, lax.fori_loop(0, ngroups, group, init))

    def count_ge(th):
        def body(j, acc):
            return acc + tile_hits(sc_ref, j, lambda s: jnp.where(s >= th, 1.0, 0.0), COUNT_SLAB, 8)
        return jnp.sum(over_tiles(body, jnp.zeros((8, qb), F32)), axis=0, keepdims=True)

    above = c_pos >= kk
    below = c_nonneg < kk
    live = above | below
    zero = jnp.zeros((1, qb), F32)
    lo0 = jnp.where(below, mn, zero)
    cnt_lo0 = jnp.where(below, (qpos + 1).astype(F32), c_nonneg)
    hi0 = jnp.where(above, mx + jnp.abs(mx) * 1e-6 + 1e-30, zero)
    cnt_hi0 = jnp.where(above, zero, jnp.where(below, c_nonneg, c_pos))

    def midpoint(lo, hi):
        return lo + 0.5 * (hi - lo)

    assert kt // 16 <= 256

    def count_ge_bf16(th):
        def body(j, acc):
            hit = lambda s: jnp.where(s >= th, jnp.ones((), BF16), jnp.zeros((), BF16))
            part = tile_hits(scb_ref, j, hit, 2 * COUNT_SLAB, 16).astype(F32)
            return acc + part[0:8, :] + part[8:16, :]
        return jnp.sum(over_tiles(body, jnp.zeros((8, qb), F32)), axis=0, keepdims=True)

    def coarse_body(_, c):
        lo, hi, cnt_lo, cnt_hi = c
        th = midpoint(lo, hi).astype(BF16)
        mid = th.astype(F32)
        inside = live & (mid > lo) & (mid < hi)
        cnt = count_ge_bf16(th)
        up = inside & (cnt >= kk)
        dn = inside & (cnt < kk)
        return (jnp.where(up, mid, lo), jnp.where(dn, mid, hi),
                jnp.where(up, cnt, cnt_lo), jnp.where(dn, cnt, cnt_hi))

    lo1, hi1, cnt_lo1, cnt_hi1 = lax.fori_loop(0, COARSE_PASSES, coarse_body, (lo0, hi0, cnt_lo0, cnt_hi0))

    def fine_pass(c):
        lo, hi, cnt_lo, cnt_hi = c
        mid = midpoint(lo, hi)
        cnt = count_ge(mid)
        up = live & (cnt >= kk)
        dn = live & (cnt < kk)
        return (jnp.where(up, mid, lo), jnp.where(dn, mid, hi), jnp.where(up, cnt, cnt_lo), jnp.where(dn, cnt, cnt_hi))

    c2 = lax.fori_loop(0, FINE_PASSES, lambda _, c: fine_pass(c), (lo1, hi1, cnt_lo1, cnt_hi1))

    def finish_chunk(cs):
        kk_c, live_c = kk[:, cs], live[:, cs]

        def cond(c):
            it, (lo_c, hi_c, cnt_c, _) = c
            mid = midpoint(lo_c, hi_c)
            open_ = live_c & (cnt_c != kk_c) & (mid > lo_c) & (mid < hi_c)
            return (it < 320) & (jnp.max(open_.astype(jnp.int32)) > 0)

        def body(c):
            it, (lo_c, hi_c, cnt_lo_c, cnt_hi_c) = c
            mid = midpoint(lo_c, hi_c)
            tile = lambda j, acc: acc + tile_hits(sc_ref, j, lambda s: jnp.where(s >= mid, 1.0, 0.0), COUNT_SLAB, 8, cs)
            cnt = jnp.sum(over_tiles(tile, jnp.zeros((8, LANES), F32)), axis=0, keepdims=True)
            up = live_c & (cnt >= kk_c)
            dn = live_c & (cnt < kk_c)
            return it + 1, (jnp.where(up, mid, lo_c), jnp.where(dn, mid, hi_c),
                            jnp.where(up, cnt, cnt_lo_c), jnp.where(dn, cnt, cnt_hi_c))

        return lax.while_loop(cond, body, (jnp.int32(0), tuple(a[:, cs] for a in c2)))[1]

    chunks = [slice(c * LANES, (c + 1) * LANES) for c in range(qb // LANES)]
    lo, hi, cnt_lo, cnt_hi = (jnp.concatenate(parts, axis=1) for parts in zip(*[finish_chunk(cs) for cs in chunks]))

    tied = cnt_lo != kk
    need = kk - cnt_hi

    def resolve_ties(cs):
        lo_c, need_c = lo[:, cs], need[:, cs]
        tri = jnp.where(lax.broadcasted_iota(jnp.int32, (kt, kt), 0) >= lax.broadcasted_iota(jnp.int32, (kt, kt), 1),
                        1.0, 0.0).astype(BF16)

        def strike(j, seen):
            rows = pl.ds(pl.multiple_of(j * kt, kt), kt)
            s = sc_ref[rows, cs]
            eq = s == lo_c
            rank = seen + _dot(tri, jnp.where(eq, 1.0, 0.0).astype(BF16))
            sc_ref[rows, cs] = jnp.where(eq & (rank > need_c), -jnp.inf, s)
            return rank[kt - 1:kt, :]

        lax.fori_loop(0, nkt, strike, jnp.zeros((1, LANES), F32))

    for cs in chunks:
        pl.when(jnp.max(tied[:, cs].astype(jnp.int32)) > 0)(functools.partial(resolve_ties, cs))

    def logits(j, h, bias):
        ks = pl.multiple_of(j * kt, kt)
        sl = slice(h * ATT_DIM, (h + 1) * ATT_DIM)
        return _dot_nt(ak_ref[pl.ds(ks, kt), sl], aq_ref[:, sl]) + bias

    def select_bias(j):
        ks = pl.multiple_of(j * kt, kt)
        return jnp.where(sc_ref[pl.ds(ks, kt), :] >= lo, 0.0, NEG_BIG)

    def values(j, h):
        return avt_ref[0, j, h * V_ROWS:(h + 1) * V_ROWS, :]

    kmax = jnp.max(knt_ref[...], axis=1, keepdims=True)
    shift = [qnt_ref[h:h + 1, :] * kmax[ATT_HEADS + h:ATT_HEADS + h + 1, :] * NORM_SLACK for h in range(ATT_HEADS)]
    for h in range(ATT_HEADS):
        acc_refs[h][...] = jnp.zeros_like(acc_refs[h])

    def attn_keys_fast(j0, ntile):
        rows = pl.ds(pl.multiple_of(j0 * kt, kt), ntile * kt)
        keep = jnp.where(sc_ref[rows, :] >= lo, 1.0, 0.0).astype(BF16)
        for h in range(ATT_HEADS):
            sl = slice(h * ATT_DIM, (h + 1) * ATT_DIM)
            p = jnp.exp2(_dot_nt(ak_ref[rows, sl], aq_ref[:, sl]) - shift[h]).astype(BF16) * keep
            vals = [values(j0 + u, h) for u in range(ntile)]
            acc_refs[h][...] += _dot(vals[0] if ntile == 1 else jnp.concatenate(vals, axis=1), p)

    def attn_group(jg, c):
        attn_keys_fast(ATT_GROUP * jg, ATT_GROUP)
        return c

    lax.fori_loop(0, nkt // ATT_GROUP, attn_group, 0)
    done = (nkt // ATT_GROUP) * ATT_GROUP
    size = ATT_GROUP // 2
    while size:
        pl.when(((nkt - done) & size) != 0)(functools.partial(attn_keys_fast, done, size))
        done = done + ((nkt - done) & size)
        size //= 2
    denom_min = functools.reduce(jnp.minimum, [acc_refs[h][ATT_DIM:ATT_DIM + 1, :] for h in range(ATT_HEADS)])

    @pl.when(jnp.min(denom_min) < DENOM_FLOOR)
    def _():
        for h in range(ATT_HEADS):
            acc_refs[h][...] = jnp.zeros_like(acc_refs[h])
            m_refs[h][...] = jnp.full(m_refs[h].shape, NEG_BIG, F32)

        def attn_tile(j, c):
            bias = select_bias(j)
            for h in range(ATT_HEADS):
                acc_h, m_h = acc_refs[h], m_refs[h]
                lg = logits(j, h, bias)
                m_old = m_h[0:1, :]
                m_new = jnp.maximum(m_old, jnp.max(_fold8(lg, jnp.maximum), axis=0, keepdims=True))
                p = jnp.exp2(lg - m_new).astype(BF16)
                acc_h[...] = jnp.exp2(m_old - m_new) * acc_h[...] + _dot(values(j, h), p)
                m_h[0:1, :] = m_new
            return c

        lax.fori_loop(0, nkt, attn_tile, 0)

    for h in range(ATT_HEADS):
        o_t = acc_refs[h][0:ATT_DIM, :] / acc_refs[h][ATT_DIM:ATT_DIM + 1, :]
        o_ref[:, h * ATT_DIM:(h + 1) * ATT_DIM] = o_t.T.astype(BF16)


def _dsa(iq, iwt, ik, aq, ak, avt, nrm, B, T, qb, kt):
    M = ak.shape[0]
    nq = T // qb
    assert kt % qb == 0 and T % kt == 0 and avt.shape[1:] == (T // kt, ATT_HEADS * V_ROWS, kt)
    topk = min(TOPK_MAX, T // 4)
    qrow = lambda b, i: (b * nq + i, 0)
    brow = lambda b, i: (b, 0)
    kern = functools.partial(_dsa_kernel, qb=qb, kt=kt, topk=topk, seq=T)
    return pl.pallas_call(
        kern,
        grid=(B, nq),
        in_specs=[pl.BlockSpec((IDX_HEADS, qb, IDX_DIM), lambda b, i: (0, b * nq + i, 0)),
                  pl.BlockSpec((IDX_HEADS, qb), lambda b, i: (0, b * nq + i)),
                  _resident((T, IDX_DIM), brow),
                  pl.BlockSpec((qb, GROUP), qrow),
                  _resident((T, GROUP), brow),
                  _resident((1,) + avt.shape[1:], lambda b, i: (b, 0, 0, 0)),
                  pl.BlockSpec((2 * ATT_HEADS, qb), lambda b, i: (0, b * nq + i)),
                  _resident((2 * ATT_HEADS, T), lambda b, i: (0, b))],
        out_specs=pl.BlockSpec((qb, GROUP), qrow),
        out_shape=jax.ShapeDtypeStruct((M, GROUP), BF16),
        scratch_shapes=[pltpu.VMEM((T, qb), F32),
                        pltpu.VMEM((T, qb), BF16),
                        pltpu.VMEM((32, qb), F32),
                        ] + [pltpu.VMEM((V_ROWS, qb), F32)] * ATT_HEADS
                        + [pltpu.VMEM((8, qb), F32)] * ATT_HEADS,
        compiler_params=_cparams(("parallel", "arbitrary")),
        name="dsa",
    )(iq, iwt, ik, aq, ak, avt, nrm, nrm)


def _rms(a, gain):
    return a * lax.rsqrt(jnp.mean(a * a, axis=-1, keepdims=True) + EPS) * gain


def _out_ffn_kernel(ohg_ref, oatt_ref, x_ref, mod_ref, gains_ref, wo_ref, wi_ref, wf_ref, o_ref, acc_ref,
                    *, d_ff, fc):
    mod = mod_ref[0]
    gains = gains_ref[...]
    hw = ohg_ref.shape[1]
    y = _dot(ohg_ref[...], wo_ref[0:hw, :]) + _dot(oatt_ref[...], wo_ref[hw:, :])
    x1 = x_ref[...] + mod[2:3, :] * _rms(y, gains[0:1, :])
    h2 = (_rms(x1, gains[1:2, :]) * (1.0 + mod[4:5, :]) + mod[3:4, :]).astype(BF16)
    for c in range(d_ff // fc):
        gate = _dot(h2, wi_ref[:, c * fc:(c + 1) * fc])
        up = _dot(h2, wi_ref[:, d_ff + c * fc:d_ff + (c + 1) * fc])
        part = _dot((_silu(gate) * up).astype(BF16), wf_ref[c * fc:(c + 1) * fc, :])
        if c == 0:
            acc_ref[...] = part
        else:
            acc_ref[...] += part
    o_ref[...] = x1 + mod[5:6, :] * _rms(acc_ref[...], gains[2:3, :])


def _out_ffn(ohg, oatt, x2, mod3, gains, wo, wi, wf, T, tm):
    M, D = x2.shape
    nt = T // tm
    d_ff = wf.shape[0]
    row = lambda i: (i, 0)
    const = lambda i: (0, 0)
    assert d_ff % FFN_CHUNK == 0
    kern = functools.partial(_out_ffn_kernel, d_ff=d_ff, fc=FFN_CHUNK)
    return pl.pallas_call(
        kern,
        grid=(M // tm,),
        in_specs=[pl.BlockSpec((tm, GROUP), row),
                  pl.BlockSpec((tm, GROUP), row),
                  pl.BlockSpec((tm, D), row),
                  pl.BlockSpec((1, N_MOD, D), lambda i: (i // nt, 0, 0)),
                  pl.BlockSpec((3, D), const),
                  _resident(wo.shape, const),
                  _resident(wi.shape, const),
                  _resident(wf.shape, const)],
        out_specs=pl.BlockSpec((tm, D), row),
        out_shape=jax.ShapeDtypeStruct((M, D), F32),
        scratch_shapes=[pltpu.VMEM((tm, D), F32)],
        compiler_params=_cparams(("parallel",)),
        name="out_ffn",
    )(ohg, oatt, x2, mod3, gains, wo, wi, wf)


def kernel(x, c, positions, w_ada, b_ada, norm_pre_mix, norm_post_mix, norm_pre_ffn, norm_post_ffn, w_in,
           hgrn_lower_bound, hgrn_out_norm, idx_k_norm_w, idx_k_norm_b, w_out, w_ffn_in, w_ffn_out):
    B, T, D = x.shape
    depth = w_ada.shape[0]
    assert depth == 1 and hgrn_lower_bound.shape[0] == 2
    tm = ROW_TILE
    assert T % tm == 0 and T % KEY_TILE == 0 and T % QUERY_BLOCK == 0 and D % LANES == 0 and tm == KEY_TILE

    mod3 = _adaln(c, w_ada, b_ada).reshape(B, N_MOD, D)
    tab = _rope_table(positions)

    main = 8 * GROUP
    w = w_in.reshape(w_in.shape[1:])
    w_main = w[:, :main].astype(BF16)
    w_tail = jnp.concatenate([jnp.pad(w[:, main:main + IDX_DIM], ((0, 0), (0, LANES - IDX_DIM))),
                              jnp.pad(w[:, main + IDX_DIM:], ((0, 0), (0, LANES - IDX_HEADS)))], axis=1).astype(BF16)
    lnw = jnp.pad(idx_k_norm_w[0], (0, LANES - IDX_DIM)).reshape(1, LANES)
    lnb = jnp.pad(idx_k_norm_b[0], (0, LANES - IDX_DIM)).reshape(1, LANES)

    x2 = x.reshape(B * T, D)
    hq, lf, hk, hv, hg, aq, ak, avt, iq, ik, iwt, nrm = _in_proj(
        x2, mod3, norm_pre_mix, w_main, w_tail, hgrn_lower_bound, lnw, lnb, tab, T, tm, tm)

    o_hg = _hgrn(hq, lf, hk, hv, hg, hgrn_out_norm, B, T, tm)
    o_att = _dsa(iq, iwt, ik, aq, ak, avt, nrm, B, T, QUERY_BLOCK, KEY_TILE)

    gains = jnp.concatenate([norm_post_mix, norm_pre_ffn, norm_post_ffn], axis=0)
    out = _out_ffn(o_hg, o_att, x2, mod3, gains, w_out.reshape(w_out.shape[1:]).astype(BF16),
                   w_ffn_in.reshape(w_ffn_in.shape[1:]).astype(BF16),
                   w_ffn_out.reshape(w_ffn_out.shape[1:]).astype(BF16), T, tm)
    return out.reshape(B, T, D)
```

```python
import functools

import numpy as np
import jax
import jax.numpy as jnp
from jax import lax
from jax.experimental import pallas as pl
from jax.experimental.pallas import tpu as pltpu

F32 = jnp.float32
BF16 = jnp.bfloat16

HG_HEADS = 4
HG_DIM = 128
ATT_HEADS = 4
ATT_DIM = 128
IDX_HEADS = 8
IDX_DIM = 64
TOPK_MAX = 256
ROPE_THETA = 500000.0
ROPE_FRACTION = 4
N_MOD = 6
EPS = 1e-6

GROUP = 512
LANES = 128
VMEM_LIMIT = 56 * 1024 * 1024

HG_CHUNK = 128
HG_LEVELS = (64, 32, 16, 8, 4, 2, 1)

NEG_BIG = -1e30
DENOM_FLOOR = 2.0 ** -64
NORM_SLACK = 1.01
COARSE_PASSES = 10
FINE_PASSES = 10
COUNT_SLAB = 64
TILE_GROUP = 4
ATT_GROUP = 2

ROW_TILE = 512
KEY_TILE = 512
QUERY_BLOCK = 512
SCORE_GROUP = 1
FFN_CHUNK = 256
PROJ_GROUPS = 8
Q_SCALE = ATT_DIM ** -0.5 * 1.4426950408889634
V_ROWS = ATT_DIM + 16


def _cparams(sem):
    return pltpu.CompilerParams(dimension_semantics=sem, vmem_limit_bytes=VMEM_LIMIT)


def _resident(shape, index_map):
    return pl.BlockSpec(shape, index_map, pipeline_mode=pl.Buffered(1))


def _split_bf16(a):
    hi = a.astype(BF16)
    lo = (a - hi.astype(F32)).astype(BF16)
    return hi, lo


def _dot(a, b):
    return jnp.dot(a, b, preferred_element_type=F32)


def _dot_nt(a, b):
    return lax.dot_general(a, b, (((1,), (1,)), ((), ())), preferred_element_type=F32)


def _silu(a):
    return a * jax.nn.sigmoid(a)


def _adaln_kernel(c_ref, w_ref, b_ref, o_ref):
    a = _silu(c_ref[...])
    a_hi, a_lo = _split_bf16(a)
    w_hi, w_lo = _split_bf16(w_ref[0])
    acc = _dot(a_hi, w_hi) + (_dot(a_hi, w_lo) + _dot(a_lo, w_hi))
    o_ref[...] = acc + b_ref[...]


def _adaln(c, w, b):
    B, D = c.shape
    N = w.shape[2]
    return pl.pallas_call(
        _adaln_kernel,
        grid=(N // D,),
        in_specs=[pl.BlockSpec((B, D), lambda j: (0, 0)),
                  pl.BlockSpec((1, D, D), lambda j: (0, 0, j)),
                  pl.BlockSpec((1, D), lambda j: (0, j))],
        out_specs=pl.BlockSpec((B, D), lambda j: (0, j)),
        out_shape=jax.ShapeDtypeStruct((B, N), F32),
        compiler_params=_cparams(("arbitrary",)),
        name="adaln",
    )(c, w, b)


def _rope_table_kernel(pos_ref, fa_ref, fb_ref, o_ref):
    pos = pos_ref[0].astype(F32)
    ang_a = fa_ref[...] * pos
    ang_b = fb_ref[...] * pos
    ha = fa_ref.shape[0]
    hb = fb_ref.shape[0]
    o_ref[0, 0:ha, :] = jnp.cos(ang_a)
    o_ref[0, ha:2 * ha, :] = jnp.sin(ang_a)
    o_ref[0, 2 * ha:2 * ha + hb, :] = jnp.cos(ang_b)
    o_ref[0, 2 * ha + hb:2 * ha + 2 * hb, :] = jnp.sin(ang_b)
    o_ref[0, 2 * ha + 2 * hb:, :] = jnp.zeros((LANES - 2 * ha - 2 * hb, pos.shape[1]), F32)


ROPE_HALF_A = ATT_DIM // ROPE_FRACTION // 2
ROPE_HALF_B = IDX_DIM // ROPE_FRACTION // 2


def _rope_table(positions):
    B, T = positions.shape
    ha, hb = ROPE_HALF_A, ROPE_HALF_B
    fa = (ROPE_THETA ** (-jnp.arange(ha, dtype=F32) / ha)).reshape(ha, 1)
    fb = (ROPE_THETA ** (-jnp.arange(hb, dtype=F32) / hb)).reshape(hb, 1)
    return pl.pallas_call(
        _rope_table_kernel,
        grid=(B,),
        in_specs=[pl.BlockSpec((1, 1, T), lambda b: (b, 0, 0)),
                  pl.BlockSpec((ha, 1), lambda b: (0, 0)),
                  pl.BlockSpec((hb, 1), lambda b: (0, 0))],
        out_specs=pl.BlockSpec((1, LANES, T), lambda b: (b, 0, 0)),
        out_shape=jax.ShapeDtypeStruct((B, LANES, T), F32),
        compiler_params=_cparams(("arbitrary",)),
        name="rope_table",
    )(positions.reshape(B, 1, T), fa, fb)


def _rope_patterns(tab):
    ha, hb = ROPE_HALF_A, ROPE_HALF_B
    lane = lax.broadcasted_iota(jnp.int32, tab.shape, 1)
    shifted = lambda sh: pltpu.roll(tab, sh % LANES, axis=1)
    cos_a = jnp.where(lane < ha, tab, jnp.where(lane < 2 * ha, shifted(ha), 1.0))
    sin_a = jnp.where(lane < ha, -shifted(-ha), jnp.where(lane < 2 * ha, tab, 0.0))
    l64 = lane & (IDX_DIM - 1)
    first = lane < IDX_DIM
    cb0, sb0 = 2 * ha, 2 * ha + hb
    cos_b = jnp.where(l64 < hb, jnp.where(first, shifted(-cb0), shifted(IDX_DIM - cb0)),
                      jnp.where(l64 < 2 * hb, jnp.where(first, shifted(hb - cb0), shifted(IDX_DIM + hb - cb0)), 1.0))
    sin_b = jnp.where(l64 < hb, -jnp.where(first, shifted(-sb0), shifted(IDX_DIM - sb0)),
                      jnp.where(l64 < 2 * hb, jnp.where(first, shifted(hb - sb0), shifted(IDX_DIM + hb - sb0)), 0.0))
    return cos_a, sin_a, cos_b, sin_b


def _rope(xb, cos, sin, half, period):
    lane = lax.broadcasted_iota(jnp.int32, xb.shape, 1)
    fwd = pltpu.roll(xb, LANES - half, axis=1)
    bwd = pltpu.roll(xb, half, axis=1)
    partner = jnp.where((lane & (period - 1)) < half, fwd, bwd)
    return xb * cos + partner * sin


def _in_proj_kernel(x_ref, mod_ref, gain_ref, w_ref, wt_ref, lbp_ref, lnw_ref, lnb_ref, tab_ref,
                    hq_ref, lf_ref, hk_ref, hv_ref, hg_ref,
                    aq_ref, ak_ref, avt_ref, iq_ref, ik_ref, iwt_ref, nrm_ref):
    x = x_ref[...]
    ms = jnp.mean(x * x, axis=-1, keepdims=True)
    mod = mod_ref[0]
    h = x * lax.rsqrt(ms + EPS) * gain_ref[...]
    h = h * (1.0 + mod[1:2, :]) + mod[0:1, :]
    hb = h.astype(BF16)

    wide = {}

    def proj(g):
        base = g - g % PROJ_GROUPS
        if base not in wide:
            wide[base] = _dot(hb, w_ref[:, base * GROUP:(base + PROJ_GROUPS) * GROUP])
        return wide[base][:, (g - base) * GROUP:(g - base + 1) * GROUP]

    hq_ref[...] = (_silu(proj(0)) * (HG_DIM ** -0.5)).astype(BF16)
    a = lbp_ref[...]
    amax = jnp.max(a, axis=0, keepdims=True)
    e = jnp.exp(a - amax)
    lb = e[0:1, :] / jnp.sum(e, axis=0, keepdims=True)
    fr = proj(1)
    f = lb + (1.0 - lb) * jax.nn.sigmoid(fr)
    lf_ref[...] = jnp.log2(f).astype(BF16)
    hk_ref[...] = ((1.0 - lb) * jax.nn.sigmoid(-fr)).astype(BF16)
    hv_ref[...] = proj(2).astype(BF16)
    hg_ref[...] = _silu(proj(3)).astype(BF16)

    cosa, sina, cosb, sinb = _rope_patterns(tab_ref[0].T)
    half_a = ROPE_HALF_A
    q = proj(4)
    k = proj(5)
    lane_a = lax.broadcasted_iota(jnp.int32, cosa.shape, 1)
    norms = jnp.zeros(cosa.shape, F32)
    for hh in range(ATT_HEADS):
        sl = slice(hh * ATT_DIM, (hh + 1) * ATT_DIM)
        qh = (_rope(q[:, sl], cosa, sina, half_a, ATT_DIM) * Q_SCALE).astype(BF16)
        kh = _rope(k[:, sl], cosa, sina, half_a, ATT_DIM).astype(BF16)
        aq_ref[:, sl] = qh
        ak_ref[:, sl] = kh
        for slot, a in ((hh, qh), (ATT_HEADS + hh, kh)):
            norms = jnp.where(lane_a == slot, jnp.sqrt(_dot(a * a, jnp.ones((ATT_DIM, LANES), BF16))), norms)
    nrm_ref[...] = norms.T[0:2 * ATT_HEADS, :]
    v = proj(6)
    for hh in range(ATT_HEADS):
        r0 = hh * V_ROWS
        avt_ref[0, 0, r0:r0 + ATT_DIM, :] = v[:, hh * ATT_DIM:(hh + 1) * ATT_DIM].T.astype(BF16)
        avt_ref[0, 0, r0 + ATT_DIM:r0 + V_ROWS, :] = jnp.ones((V_ROWS - ATT_DIM, v.shape[0]), BF16)

    half_b = ROPE_HALF_B
    qi = proj(7)
    for cc in range(GROUP // LANES):
        qr = _rope(qi[:, cc * LANES:(cc + 1) * LANES], cosb, sinb, half_b, IDX_DIM).astype(BF16)
        for hh in range(LANES // IDX_DIM):
            iq_ref[cc * (LANES // IDX_DIM) + hh] = qr[:, hh * IDX_DIM:(hh + 1) * IDX_DIM]
    kw = _dot(hb, wt_ref[...])
    ki = kw[:, 0:LANES]
    lane = lax.broadcasted_iota(jnp.int32, ki.shape, 1)
    real = lane < IDX_DIM
    mu = jnp.sum(ki, axis=-1, keepdims=True) * (1.0 / IDX_DIM)
    d = jnp.where(real, ki - mu, 0.0)
    var = jnp.sum(d * d, axis=-1, keepdims=True) * (1.0 / IDX_DIM)
    kn = d * lax.rsqrt(var + EPS) * lnw_ref[...] + lnb_ref[...]
    kn = _rope(kn, cosb, sinb, half_b, IDX_DIM)
    ik_ref[...] = kn[:, 0:IDX_DIM].astype(BF16)
    iwt_ref[...] = kw[:, LANES:2 * LANES].T[0:IDX_HEADS, :] * (IDX_HEADS ** -0.5 * IDX_DIM ** -0.5)


def _in_proj(x2, mod3, gain, w_main, w_tail, lbp, lnw, lnb, tab, T, tm, kt):
    M, D = x2.shape
    nt = T // tm
    row = lambda i: (i, 0)
    const = lambda i: (0, 0)
    bf16o = jax.ShapeDtypeStruct((M, GROUP), BF16)
    grp_spec = pl.BlockSpec((tm, GROUP), row)
    return pl.pallas_call(
        _in_proj_kernel,
        grid=(M // tm,),
        in_specs=[pl.BlockSpec((tm, D), row),
                  pl.BlockSpec((1, N_MOD, D), lambda i: (i // nt, 0, 0)),
                  pl.BlockSpec((1, D), const),
                  _resident(w_main.shape, const),
                  _resident(w_tail.shape, const),
                  pl.BlockSpec(lbp.shape, const),
                  pl.BlockSpec((1, LANES), const),
                  pl.BlockSpec((1, LANES), const),
                  pl.BlockSpec((1, LANES, tm), lambda i: (i // nt, 0, i % nt))],
        out_specs=[grp_spec] * 7 + [pl.BlockSpec((1, 1, ATT_HEADS * V_ROWS, tm),
                                                 lambda i: (i // nt, (i % nt) // (kt // tm), 0, (i % nt) % (kt // tm))),
                                    pl.BlockSpec((IDX_HEADS, tm, IDX_DIM), lambda i: (0, i, 0)),
                                    pl.BlockSpec((tm, IDX_DIM), row),
                                    pl.BlockSpec((IDX_HEADS, tm), lambda i: (0, i)),
                                    pl.BlockSpec((2 * ATT_HEADS, tm), lambda i: (0, i))],
        out_shape=[bf16o] * 7 + [jax.ShapeDtypeStruct((M // T, T // kt, ATT_HEADS * V_ROWS, kt), BF16),
                                 jax.ShapeDtypeStruct((IDX_HEADS, M, IDX_DIM), BF16),
                                 jax.ShapeDtypeStruct((M, IDX_DIM), BF16),
                                 jax.ShapeDtypeStruct((IDX_HEADS, M), F32),
                                 jax.ShapeDtypeStruct((2 * ATT_HEADS, M), F32)],
        compiler_params=_cparams(("parallel",)),
        name="in_proj",
    )(x2, mod3, gain, w_main, w_tail, lbp, lnw, lnb, tab)


def _hgrn_consts():
    C = HG_CHUNK
    t = np.arange(C)
    tri = (t[None, :] <= t[:, None]).astype(np.float32)
    blocks = [tri]
    for m in HG_LEVELS:
        if m % 8:
            split = (t // (2 * m)) * (2 * m) + m - 1
            upper = ((t & m) != 0)[:, None]
            blocks.append(np.where(upper, tri - tri[split], tri[split] - tri))
    mall = np.concatenate(blocks, axis=0)
    x = t[:, None] ^ t[None, :]
    lvl = np.full((C, C), len(HG_LEVELS) + 1, np.int32)
    for li, m in enumerate(HG_LEVELS):
        lvl[(t[:, None] > t[None, :]) & (x >= m) & (x < 2 * m)] = li
    lvl[t[:, None] == t[None, :]] = len(HG_LEVELS)
    return jnp.asarray(mall, BF16), jnp.asarray(lvl)


def _hgrn_kernel(hq_ref, lf_ref, hk_ref, hv_ref, hg_ref, mall_ref, lvl_ref, onorm_ref, o_ref, st_ref):
    C = HG_CHUNK
    nl = len(HG_LEVELS)

    @pl.when(pl.program_id(1) == 0)
    def _():
        st_ref[...] = jnp.zeros_like(st_ref)

    lvl = lvl_ref[...]
    row = lax.broadcasted_iota(jnp.int32, (C, HG_DIM), 0)
    mall = mall_ref[...]

    for ci in range(hq_ref.shape[0] // C):
        rs = slice(ci * C, (ci + 1) * C)
        dheads = _dot(mall, lf_ref[rs, :])
        for h in range(HG_HEADS):
            cs = slice(h * HG_DIM, (h + 1) * HG_DIM)
            q, k, v = hq_ref[rs, cs], hk_ref[rs, cs], hv_ref[rs, cs]
            qf, kf = q.astype(F32), k.astype(F32)
            dall = dheads[:, cs]
            g = dall[0:C]

            a = jnp.zeros((C, C), F32)
            fine = 0
            for li, m in enumerate(HG_LEVELS):
                if m % 8 == 0:
                    qk = jnp.concatenate([(qf if (b & 1) else kf)[b * m:(b + 1) * m] for b in range(C // m)], axis=0)
                    parts = []
                    for b in range(C // (2 * m)):
                        lo_rows = slice(2 * b * m, (2 * b + 1) * m)
                        hi_rows = slice((2 * b + 1) * m, (2 * b + 2) * m)
                        ref = jnp.broadcast_to(g[lo_rows.stop - 1:lo_rows.stop, :], (m, HG_DIM))
                        parts += [ref - g[lo_rows], g[hi_rows] - ref]
                    dm = jnp.concatenate(parts, axis=0)
                else:
                    fine += 1
                    qk = jnp.where((row & m) != 0, qf, kf)
                    dm = dall[fine * C:(fine + 1) * C]
                xm = (qk * jnp.exp2(dm)).astype(BF16)
                a = jnp.where(lvl == li, _dot_nt(xm, xm), a)
            a = jnp.where(lvl == nl, _dot_nt(q, k), a)

            st = st_ref[h]
            o = _dot_nt((qf * jnp.exp2(g)).astype(BF16), st.astype(BF16)) + _dot(a.astype(BF16), v)
            g_last = g[C - 1:C, :]
            kd = (kf * jnp.exp2(g_last - g)).astype(BF16)
            st_ref[h] = jnp.exp2(g_last) * st + _dot(v.astype(F32).T.astype(BF16), kd)

            o = o * lax.rsqrt(jnp.mean(o * o, axis=-1, keepdims=True) + EPS)
            o_ref[rs, cs] = (o * onorm_ref[:, cs] * hg_ref[rs, cs]).astype(BF16)


def _hgrn(hq, lf, hk, hv, hg, onorm, B, T, ct):
    M = hq.shape[0]
    nct = T // ct
    mall, lvl = _hgrn_consts()
    blk = pl.BlockSpec((ct, GROUP), lambda b, c: (b * nct + c, 0))
    const = lambda b, c: (0, 0)
    return pl.pallas_call(
        _hgrn_kernel,
        grid=(B, nct),
        in_specs=[blk] * 5 + [pl.BlockSpec(mall.shape, const), pl.BlockSpec(lvl.shape, const),
                              pl.BlockSpec((1, GROUP), const)],
        out_specs=blk,
        out_shape=jax.ShapeDtypeStruct((M, GROUP), BF16),
        scratch_shapes=[pltpu.VMEM((HG_HEADS, HG_DIM, HG_DIM), F32)],
        compiler_params=_cparams(("parallel", "arbitrary")),
        name="hgrn2",
    )(hq, lf, hk, hv, hg, mall, lvl, onorm)


def _fold8(a, op, rows=8):
    chains = [None] * 4
    for r in range(a.shape[0] // rows):
        part = a[r * rows:(r + 1) * rows, :]
        c = r % len(chains)
        chains[c] = part if chains[c] is None else op(chains[c], part)
    return op(op(chains[0], chains[1]), op(chains[2], chains[3]))


def _floor_bf16(a):
    r = a.astype(BF16).astype(F32)
    below = (r - jnp.abs(r) * (5.0 / 1024.0)).astype(BF16)
    return jnp.where(r > a, below.astype(F32), r).astype(BF16)


def _dsa_kernel(iq_ref, iwt_ref, ik_ref, aq_ref, ak_ref, avt_ref, qnt_ref, knt_ref, o_ref,
                sc_ref, scb_ref, mm_ref, *head_refs, qb, kt, topk, seq):
    acc_refs, m_refs = head_refs[:ATT_HEADS], head_refs[ATT_HEADS:]
    i = pl.program_id(1)
    q0 = i * qb
    nfull = q0 // kt
    nkt = nfull + 1
    qpos = q0 + lax.broadcasted_iota(jnp.int32, (1, qb), 1)
    kk = jnp.minimum(qpos + 1, topk).astype(F32)

    mm_ref[0:8, :] = jnp.full((8, qb), jnp.inf, F32)
    mm_ref[8:16, :] = jnp.full((8, qb), -jnp.inf, F32)
    mm_ref[16:32, :] = jnp.zeros((16, qb), F32)

    def score_keys(j0, ntile, diag):
        nrow = ntile * kt
        ks = pl.multiple_of(j0 * kt, kt)
        rows = pl.ds(ks, nrow)
        ki = ik_ref[rows, :]
        s = None
        for h in range(IDX_HEADS):
            t = jnp.maximum(_dot_nt(ki, iq_ref[h]), 0.0) * iwt_ref[h:h + 1, :]
            s = t if s is None else s + t
        if diag:
            causal = ks + lax.broadcasted_iota(jnp.int32, (nrow, qb), 0) <= qpos
            s_hi = jnp.where(causal, s, -jnp.inf)
            s_lo = jnp.where(causal, s, jnp.inf)
        else:
            s_hi = s_lo = s
        sc_ref[rows, :] = s_hi
        scb_ref[rows, :] = _floor_bf16(s_hi)
        mm_ref[0:8, :] = jnp.minimum(mm_ref[0:8, :], _fold8(s_lo, jnp.minimum))
        mm_ref[8:16, :] = jnp.maximum(mm_ref[8:16, :], _fold8(s_hi, jnp.maximum))
        mm_ref[16:24, :] += _fold8(jnp.where(s_hi >= 0.0, 1.0, 0.0), jnp.add)
        mm_ref[24:32, :] += _fold8(jnp.where(s_hi > 0.0, 1.0, 0.0), jnp.add)

    def score_group(jg, c):
        score_keys(SCORE_GROUP * jg, SCORE_GROUP, False)
        return c

    lax.fori_loop(0, nfull // SCORE_GROUP, score_group, 0)
    done = (nfull // SCORE_GROUP) * SCORE_GROUP
    size = SCORE_GROUP // 2
    while size:
        pl.when(((nfull - done) & size) != 0)(functools.partial(score_keys, done, size, False))
        done = done + ((nfull - done) & size)
        size //= 2
    score_keys(nfull, 1, True)
    mn = jnp.min(mm_ref[0:8, :], axis=0, keepdims=True)
    mx = jnp.max(mm_ref[8:16, :], axis=0, keepdims=True)
    c_nonneg = jnp.sum(mm_ref[16:24, :], axis=0, keepdims=True)
    c_pos = jnp.sum(mm_ref[24:32, :], axis=0, keepdims=True)

    def tile_hits(ref, j, hit_fn, slab, rows, cols=slice(None)):
        acc = None
        for r in range(kt // slab):
            ks = pl.multiple_of(j * kt + r * slab, slab)
            part = _fold8(hit_fn(ref[pl.ds(ks, slab), cols]), jnp.add, rows=rows)
            acc = part if acc is None else acc + part
        return acc

    def over_tiles(body, init):
        def group(jg, acc):
            for u in range(TILE_GROUP):
                acc = body(jg * TILE_GROUP + u, acc)
            return acc
        ngroups = nkt // TILE_GROUP
        return lax.fori_loop(ngroups * TILE_GROUP, nkt, body, lax.fori_loop(0, ngroups, group, init))

    def count_ge(th):
        def body(j, acc):
            return acc + tile_hits(sc_ref, j, lambda s: jnp.where(s >= th, 1.0, 0.0), COUNT_SLAB, 8)
        return jnp.sum(over_tiles(body, jnp.zeros((8, qb), F32)), axis=0, keepdims=True)

    above = c_pos >= kk
    below = c_nonneg < kk
    live = above | below
    zero = jnp.zeros((1, qb), F32)
    lo0 = jnp.where(below, mn, zero)
    cnt_lo0 = jnp.where(below, (qpos + 1).astype(F32), c_nonneg)
    hi0 = jnp.where(above, mx + jnp.abs(mx) * 1e-6 + 1e-30, zero)
    cnt_hi0 = jnp.where(above, zero, jnp.where(below, c_nonneg, c_pos))

    def midpoint(lo, hi):
        return lo + 0.5 * (hi - lo)

    assert kt // 16 <= 256

    def count_ge_bf16(th):
        def body(j, acc):
            hit = lambda s: jnp.where(s >= th, jnp.ones((), BF16), jnp.zeros((), BF16))
            part = tile_hits(scb_ref, j, hit, 2 * COUNT_SLAB, 16).astype(F32)
            return acc + part[0:8, :] + part[8:16, :]
        return jnp.sum(over_tiles(body, jnp.zeros((8, qb), F32)), axis=0, keepdims=True)

    def coarse_body(_, c):
        lo, hi, cnt_lo, cnt_hi = c
        th = midpoint(lo, hi).astype(BF16)
        mid = th.astype(F32)
        inside = live & (mid > lo) & (mid < hi)
        cnt = count_ge_bf16(th)
        up = inside & (cnt >= kk)
        dn = inside & (cnt < kk)
        return (jnp.where(up, mid, lo), jnp.where(dn, mid, hi),
                jnp.where(up, cnt, cnt_lo), jnp.where(dn, cnt, cnt_hi))

    lo1, hi1, cnt_lo1, cnt_hi1 = lax.fori_loop(0, COARSE_PASSES, coarse_body, (lo0, hi0, cnt_lo0, cnt_hi0))

    def fine_pass(c):
        lo, hi, cnt_lo, cnt_hi = c
        mid = midpoint(lo, hi)
        cnt = count_ge(mid)
        up = live & (cnt >= kk)
        dn = live & (cnt < kk)
        return (jnp.where(up, mid, lo), jnp.where(dn, mid, hi), jnp.where(up, cnt, cnt_lo), jnp.where(dn, cnt, cnt_hi))

    c2 = lax.fori_loop(0, FINE_PASSES, lambda _, c: fine_pass(c), (lo1, hi1, cnt_lo1, cnt_hi1))

    def finish_chunk(cs):
        kk_c, live_c = kk[:, cs], live[:, cs]

        def cond(c):
            it, (lo_c, hi_c, cnt_c, _) = c
            mid = midpoint(lo_c, hi_c)
            open_ = live_c & (cnt_c != kk_c) & (mid > lo_c) & (mid < hi_c)
            return (it < 320) & (jnp.max(open_.astype(jnp.int32)) > 0)

        def body(c):
            it, (lo_c, hi_c, cnt_lo_c, cnt_hi_c) = c
            mid = midpoint(lo_c, hi_c)
            tile = lambda j, acc: acc + tile_hits(sc_ref, j, lambda s: jnp.where(s >= mid, 1.0, 0.0), COUNT_SLAB, 8, cs)
            cnt = jnp.sum(over_tiles(tile, jnp.zeros((8, LANES), F32)), axis=0, keepdims=True)
            up = live_c & (cnt >= kk_c)
            dn = live_c & (cnt < kk_c)
            return it + 1, (jnp.where(up, mid, lo_c), jnp.where(dn, mid, hi_c),
                            jnp.where(up, cnt, cnt_lo_c), jnp.where(dn, cnt, cnt_hi_c))

        return lax.while_loop(cond, body, (jnp.int32(0), tuple(a[:, cs] for a in c2)))[1]

    chunks = [slice(c * LANES, (c + 1) * LANES) for c in range(qb // LANES)]
    lo, hi, cnt_lo, cnt_hi = (jnp.concatenate(parts, axis=1) for parts in zip(*[finish_chunk(cs) for cs in chunks]))

    tied = cnt_lo != kk
    need = kk - cnt_hi

    def resolve_ties(cs):
        lo_c, need_c = lo[:, cs], need[:, cs]
        tri = jnp.where(lax.broadcasted_iota(jnp.int32, (kt, kt), 0) >= lax.broadcasted_iota(jnp.int32, (kt, kt), 1),
                        1.0, 0.0).astype(BF16)

        def strike(j, seen):
            rows = pl.ds(pl.multiple_of(j * kt, kt), kt)
            s = sc_ref[rows, cs]
            eq = s == lo_c
            rank = seen + _dot(tri, jnp.where(eq, 1.0, 0.0).astype(BF16))
            sc_ref[rows, cs] = jnp.where(eq & (rank > need_c), -jnp.inf, s)
            return rank[kt - 1:kt, :]

        lax.fori_loop(0, nkt, strike, jnp.zeros((1, LANES), F32))

    for cs in chunks:
        pl.when(jnp.max(tied[:, cs].astype(jnp.int32)) > 0)(functools.partial(resolve_ties, cs))

    def logits(j, h, bias):
        ks = pl.multiple_of(j * kt, kt)
        sl = slice(h * ATT_DIM, (h + 1) * ATT_DIM)
        return _dot_nt(ak_ref[pl.ds(ks, kt), sl], aq_ref[:, sl]) + bias

    def select_bias(j):
        ks = pl.multiple_of(j * kt, kt)
        return jnp.where(sc_ref[pl.ds(ks, kt), :] >= lo, 0.0, NEG_BIG)

    def values(j, h):
        return avt_ref[0, j, h * V_ROWS:(h + 1) * V_ROWS, :]

    kmax = jnp.max(knt_ref[...], axis=1, keepdims=True)
    shift = [qnt_ref[h:h + 1, :] * kmax[ATT_HEADS + h:ATT_HEADS + h + 1, :] * NORM_SLACK for h in range(ATT_HEADS)]
    for h in range(ATT_HEADS):
        acc_refs[h][...] = jnp.zeros_like(acc_refs[h])

    def attn_keys_fast(j0, ntile):
        rows = pl.ds(pl.multiple_of(j0 * kt, kt), ntile * kt)
        keep = jnp.where(sc_ref[rows, :] >= lo, 1.0, 0.0).astype(BF16)
        for h in range(ATT_HEADS):
            sl = slice(h * ATT_DIM, (h + 1) * ATT_DIM)
            p = jnp.exp2(_dot_nt(ak_ref[rows, sl], aq_ref[:, sl]) - shift[h]).astype(BF16) * keep
            vals = [values(j0 + u, h) for u in range(ntile)]
            acc_refs[h][...] += _dot(vals[0] if ntile == 1 else jnp.concatenate(vals, axis=1), p)

    def attn_group(jg, c):
        attn_keys_fast(ATT_GROUP * jg, ATT_GROUP)
        return c

    lax.fori_loop(0, nkt // ATT_GROUP, attn_group, 0)
    done = (nkt // ATT_GROUP) * ATT_GROUP
    size = ATT_GROUP // 2
    while size:
        pl.when(((nkt - done) & size) != 0)(functools.partial(attn_keys_fast, done, size))
        done = done + ((nkt - done) & size)
        size //= 2
    denom_min = functools.reduce(jnp.minimum, [acc_refs[h][ATT_DIM:ATT_DIM + 1, :] for h in range(ATT_HEADS)])

    @pl.when(jnp.min(denom_min) < DENOM_FLOOR)
    def _():
        for h in range(ATT_HEADS):
            acc_refs[h][...] = jnp.zeros_like(acc_refs[h])
            m_refs[h][...] = jnp.full(m_refs[h].shape, NEG_BIG, F32)

        def attn_tile(j, c):
            bias = select_bias(j)
            for h in range(ATT_HEADS):
                acc_h, m_h = acc_refs[h], m_refs[h]
                lg = logits(j, h, bias)
                m_old = m_h[0:1, :]
                m_new = jnp.maximum(m_old, jnp.max(_fold8(lg, jnp.maximum), axis=0, keepdims=True))
                p = jnp.exp2(lg - m_new).astype(BF16)
                acc_h[...] = jnp.exp2(m_old - m_new) * acc_h[...] + _dot(values(j, h), p)
                m_h[0:1, :] = m_new
            return c

        lax.fori_loop(0, nkt, attn_tile, 0)

    for h in range(ATT_HEADS):
        o_t = acc_refs[h][0:ATT_DIM, :] / acc_refs[h][ATT_DIM:ATT_DIM + 1, :]
        o_ref[:, h * ATT_DIM:(h + 1) * ATT_DIM] = o_t.T.astype(BF16)


def _dsa(iq, iwt, ik, aq, ak, avt, nrm, B, T, qb, kt):
    M = ak.shape[0]
    nq = T // qb
    assert kt % qb == 0 and T % kt == 0 and avt.shape[1:] == (T // kt, ATT_HEADS * V_ROWS, kt)
    topk = min(TOPK_MAX, T // 4)
    qrow = lambda b, i: (b * nq + i, 0)
    brow = lambda b, i: (b, 0)
    kern = functools.partial(_dsa_kernel, qb=qb, kt=kt, topk=topk, seq=T)
    return pl.pallas_call(
        kern,
        grid=(B, nq),
        in_specs=[pl.BlockSpec((IDX_HEADS, qb, IDX_DIM), lambda b, i: (0, b * nq + i, 0)),
                  pl.BlockSpec((IDX_HEADS, qb), lambda b, i: (0, b * nq + i)),
                  _resident((T, IDX_DIM), brow),
                  pl.BlockSpec((qb, GROUP), qrow),
                  _resident((T, GROUP), brow),
                  _resident((1,) + avt.shape[1:], lambda b, i: (b, 0, 0, 0)),
                  pl.BlockSpec((2 * ATT_HEADS, qb), lambda b, i: (0, b * nq + i)),
                  _resident((2 * ATT_HEADS, T), lambda b, i: (0, b))],
        out_specs=pl.BlockSpec((qb, GROUP), qrow),
        out_shape=jax.ShapeDtypeStruct((M, GROUP), BF16),
        scratch_shapes=[pltpu.VMEM((T, qb), F32),
                        pltpu.VMEM((T, qb), BF16),
                        pltpu.VMEM((32, qb), F32),
                        ] + [pltpu.VMEM((V_ROWS, qb), F32)] * ATT_HEADS
                        + [pltpu.VMEM((8, qb), F32)] * ATT_HEADS,
        compiler_params=_cparams(("parallel", "arbitrary")),
        name="dsa",
    )(iq, iwt, ik, aq, ak, avt, nrm, nrm)


def _rms(a, gain):
    return a * lax.rsqrt(jnp.mean(a * a, axis=-1, keepdims=True) + EPS) * gain


def _out_ffn_kernel(ohg_ref, oatt_ref, x_ref, mod_ref, gains_ref, wo_ref, wi_ref, wf_ref, o_ref, acc_ref,
                    *, d_ff, fc):
    mod = mod_ref[0]
    gains = gains_ref[...]
    y = _dot(jnp.concatenate([ohg_ref[...], oatt_ref[...]], axis=1), wo_ref[...])
    x1 = x_ref[...] + mod[2:3, :] * _rms(y, gains[0:1, :])
    h2 = (_rms(x1, gains[1:2, :]) * (1.0 + mod[4:5, :]) + mod[3:4, :]).astype(BF16)
    for c in range(d_ff // fc):
        gate = _dot(h2, wi_ref[:, c * fc:(c + 1) * fc])
        up = _dot(h2, wi_ref[:, d_ff + c * fc:d_ff + (c + 1) * fc])
        part = _dot((_silu(gate) * up).astype(BF16), wf_ref[c * fc:(c + 1) * fc, :])
        if c == 0:
            acc_ref[...] = part
        else:
            acc_ref[...] += part
    o_ref[...] = x1 + mod[5:6, :] * _rms(acc_ref[...], gains[2:3, :])


def _out_ffn(ohg, oatt, x2, mod3, gains, wo, wi, wf, T, tm):
    M, D = x2.shape
    nt = T // tm
    d_ff = wf.shape[0]
    row = lambda i: (i, 0)
    const = lambda i: (0, 0)
    assert d_ff % FFN_CHUNK == 0
    kern = functools.partial(_out_ffn_kernel, d_ff=d_ff, fc=FFN_CHUNK)
    return pl.pallas_call(
        kern,
        grid=(M // tm,),
        in_specs=[pl.BlockSpec((tm, GROUP), row),
                  pl.BlockSpec((tm, GROUP), row),
                  pl.BlockSpec((tm, D), row),
                  pl.BlockSpec((1, N_MOD, D), lambda i: (i // nt, 0, 0)),
                  pl.BlockSpec((3, D), const),
                  _resident(wo.shape, const),
                  _resident(wi.shape, const),
                  _resident(wf.shape, const)],
        out_specs=pl.BlockSpec((tm, D), row),
        out_shape=jax.ShapeDtypeStruct((M, D), F32),
        scratch_shapes=[pltpu.VMEM((tm, D), F32)],
        compiler_params=_cparams(("parallel",)),
        name="out_ffn",
    )(ohg, oatt, x2, mod3, gains, wo, wi, wf)


def kernel(x, c, positions, w_ada, b_ada, norm_pre_mix, norm_post_mix, norm_pre_ffn, norm_post_ffn, w_in,
           hgrn_lower_bound, hgrn_out_norm, idx_k_norm_w, idx_k_norm_b, w_out, w_ffn_in, w_ffn_out):
    B, T, D = x.shape
    depth = w_ada.shape[0]
    assert depth == 1 and hgrn_lower_bound.shape[0] == 2
    tm = ROW_TILE
    assert T % tm == 0 and T % KEY_TILE == 0 and T % QUERY_BLOCK == 0 and D % LANES == 0 and tm == KEY_TILE

    mod3 = _adaln(c, w_ada, b_ada).reshape(B, N_MOD, D)
    tab = _rope_table(positions)

    main = 8 * GROUP
    w = w_in.reshape(w_in.shape[1:])
    w_main = w[:, :main].astype(BF16)
    w_tail = jnp.concatenate([jnp.pad(w[:, main:main + IDX_DIM], ((0, 0), (0, LANES - IDX_DIM))),
                              jnp.pad(w[:, main + IDX_DIM:], ((0, 0), (0, LANES - IDX_HEADS)))], axis=1).astype(BF16)
    lnw = jnp.pad(idx_k_norm_w[0], (0, LANES - IDX_DIM)).reshape(1, LANES)
    lnb = jnp.pad(idx_k_norm_b[0], (0, LANES - IDX_DIM)).reshape(1, LANES)

    x2 = x.reshape(B * T, D)
    hq, lf, hk, hv, hg, aq, ak, avt, iq, ik, iwt, nrm = _in_proj(
        x2, mod3, norm_pre_mix, w_main, w_tail, hgrn_lower_bound, lnw, lnb, tab, T, tm, tm)

    o_hg = _hgrn(hq, lf, hk, hv, hg, hgrn_out_norm, B, T, tm)
    o_att = _dsa(iq, iwt, ik, aq, ak, avt, nrm, B, T, QUERY_BLOCK, KEY_TILE)

    gains = jnp.concatenate([norm_post_mix, norm_pre_ffn, norm_post_ffn], axis=0)
    out = _out_ffn(o_hg, o_att, x2, mod3, gains, w_out.reshape(w_out.shape[1:]).astype(BF16),
                   w_ffn_in.reshape(w_ffn_in.shape[1:]).astype(BF16),
                   w_ffn_out.reshape(w_ffn_out.shape[1:]).astype(BF16), T, tm)
    return out.reshape(B, T, D)
```

```python
import functools

import numpy as np
import jax
import jax.numpy as jnp
from jax import lax
from jax.experimental import pallas as pl
from jax.experimental.pallas import tpu as pltpu

F32 = jnp.float32
BF16 = jnp.bfloat16

HG_HEADS = 4
HG_DIM = 128
ATT_HEADS = 4
ATT_DIM = 128
IDX_HEADS = 8
IDX_DIM = 64
TOPK_MAX = 256
ROPE_THETA = 500000.0
ROPE_FRACTION = 4
N_MOD = 6
EPS = 1e-6

GROUP = 512
LANES = 128
VMEM_LIMIT = 56 * 1024 * 1024

HG_CHUNK = 128
HG_LEVELS = (64, 32, 16, 8, 4, 2, 1)

NEG_BIG = -1e30
DENOM_FLOOR = 2.0 ** -64
NORM_SLACK = 1.01
COARSE_PASSES = 10
FINE_PASSES = 10
COUNT_SLAB = 64
TILE_GROUP = 4
ATT_GROUP = 2

ROW_TILE = 512
KEY_TILE = 512
QUERY_BLOCK = 512
SCORE_GROUP = 1
FFN_CHUNK = 256
PROJ_GROUPS = 8
Q_SCALE = ATT_DIM ** -0.5 * 1.4426950408889634
V_ROWS = ATT_DIM + 16


def _cparams(sem):
    return pltpu.CompilerParams(dimension_semantics=sem, vmem_limit_bytes=VMEM_LIMIT)


def _resident(shape, index_map):
    return pl.BlockSpec(shape, index_map, pipeline_mode=pl.Buffered(1))


def _split_bf16(a):
    hi = a.astype(BF16)
    lo = (a - hi.astype(F32)).astype(BF16)
    return hi, lo


def _dot(a, b):
    return jnp.dot(a, b, preferred_element_type=F32)


def _dot_nt(a, b):
    return lax.dot_general(a, b, (((1,), (1,)), ((), ())), preferred_element_type=F32)


def _silu(a):
    return a * jax.nn.sigmoid(a)


def _adaln_kernel(c_ref, w_ref, b_ref, o_ref):
    a = _silu(c_ref[...])
    a_hi, a_lo = _split_bf16(a)
    w_hi, w_lo = _split_bf16(w_ref[0])
    acc = _dot(a_hi, w_hi) + (_dot(a_hi, w_lo) + _dot(a_lo, w_hi))
    o_ref[...] = acc + b_ref[...]


def _adaln(c, w, b):
    B, D = c.shape
    N = w.shape[2]
    return pl.pallas_call(
        _adaln_kernel,
        grid=(N // D,),
        in_specs=[pl.BlockSpec((B, D), lambda j: (0, 0)),
                  pl.BlockSpec((1, D, D), lambda j: (0, 0, j)),
                  pl.BlockSpec((1, D), lambda j: (0, j))],
        out_specs=pl.BlockSpec((B, D), lambda j: (0, j)),
        out_shape=jax.ShapeDtypeStruct((B, N), F32),
        compiler_params=_cparams(("arbitrary",)),
        name="adaln",
    )(c, w, b)


def _rope_table_kernel(pos_ref, fa_ref, fb_ref, o_ref):
    pos = pos_ref[0].astype(F32)
    ang_a = fa_ref[...] * pos
    ang_b = fb_ref[...] * pos
    ha = fa_ref.shape[0]
    hb = fb_ref.shape[0]
    o_ref[0, 0:ha, :] = jnp.cos(ang_a)
    o_ref[0, ha:2 * ha, :] = jnp.sin(ang_a)
    o_ref[0, 2 * ha:2 * ha + hb, :] = jnp.cos(ang_b)
    o_ref[0, 2 * ha + hb:2 * ha + 2 * hb, :] = jnp.sin(ang_b)
    o_ref[0, 2 * ha + 2 * hb:, :] = jnp.zeros((LANES - 2 * ha - 2 * hb, pos.shape[1]), F32)


ROPE_HALF_A = ATT_DIM // ROPE_FRACTION // 2
ROPE_HALF_B = IDX_DIM // ROPE_FRACTION // 2


def _rope_table(positions):
    B, T = positions.shape
    ha, hb = ROPE_HALF_A, ROPE_HALF_B
    fa = (ROPE_THETA ** (-jnp.arange(ha, dtype=F32) / ha)).reshape(ha, 1)
    fb = (ROPE_THETA ** (-jnp.arange(hb, dtype=F32) / hb)).reshape(hb, 1)
    return pl.pallas_call(
        _rope_table_kernel,
        grid=(B,),
        in_specs=[pl.BlockSpec((1, 1, T), lambda b: (b, 0, 0)),
                  pl.BlockSpec((ha, 1), lambda b: (0, 0)),
                  pl.BlockSpec((hb, 1), lambda b: (0, 0))],
        out_specs=pl.BlockSpec((1, LANES, T), lambda b: (b, 0, 0)),
        out_shape=jax.ShapeDtypeStruct((B, LANES, T), F32),
        compiler_params=_cparams(("arbitrary",)),
        name="rope_table",
    )(positions.reshape(B, 1, T), fa, fb)


def _rope_patterns(tab):
    ha, hb = ROPE_HALF_A, ROPE_HALF_B
    lane = lax.broadcasted_iota(jnp.int32, tab.shape, 1)
    shifted = lambda sh: pltpu.roll(tab, sh % LANES, axis=1)
    cos_a = jnp.where(lane < ha, tab, jnp.where(lane < 2 * ha, shifted(ha), 1.0))
    sin_a = jnp.where(lane < ha, -shifted(-ha), jnp.where(lane < 2 * ha, tab, 0.0))
    l64 = lane & (IDX_DIM - 1)
    first = lane < IDX_DIM
    cb0, sb0 = 2 * ha, 2 * ha + hb
    cos_b = jnp.where(l64 < hb, jnp.where(first, shifted(-cb0), shifted(IDX_DIM - cb0)),
                      jnp.where(l64 < 2 * hb, jnp.where(first, shifted(hb - cb0), shifted(IDX_DIM + hb - cb0)), 1.0))
    sin_b = jnp.where(l64 < hb, -jnp.where(first, shifted(-sb0), shifted(IDX_DIM - sb0)),
                      jnp.where(l64 < 2 * hb, jnp.where(first, shifted(hb - sb0), shifted(IDX_DIM + hb - sb0)), 0.0))
    return cos_a, sin_a, cos_b, sin_b


def _rope(xb, cos, sin, half, period):
    lane = lax.broadcasted_iota(jnp.int32, xb.shape, 1)
    fwd = pltpu.roll(xb, LANES - half, axis=1)
    bwd = pltpu.roll(xb, half, axis=1)
    partner = jnp.where((lane & (period - 1)) < half, fwd, bwd)
    return xb * cos + partner * sin


def _in_proj_kernel(x_ref, mod_ref, gain_ref, w_ref, wt_ref, lbp_ref, lnw_ref, lnb_ref, tab_ref,
                    hq_ref, lf_ref, hk_ref, hv_ref, hg_ref,
                    aq_ref, ak_ref, avt_ref, iq_ref, ik_ref, iwt_ref, nrm_ref):
    x = x_ref[...]
    ms = jnp.mean(x * x, axis=-1, keepdims=True)
    mod = mod_ref[0]
    h = x * lax.rsqrt(ms + EPS) * gain_ref[...]
    h = h * (1.0 + mod[1:2, :]) + mod[0:1, :]
    hb = h.astype(BF16)

    wide = {}

    def proj(g):
        base = g - g % PROJ_GROUPS
        if base not in wide:
            wide[base] = _dot(hb, w_ref[:, base * GROUP:(base + PROJ_GROUPS) * GROUP])
        return wide[base][:, (g - base) * GROUP:(g - base + 1) * GROUP]

    hq_ref[...] = (_silu(proj(0)) * (HG_DIM ** -0.5)).astype(BF16)
    a = lbp_ref[...]
    amax = jnp.max(a, axis=0, keepdims=True)
    e = jnp.exp(a - amax)
    lb = e[0:1, :] / jnp.sum(e, axis=0, keepdims=True)
    fr = proj(1)
    f = lb + (1.0 - lb) * jax.nn.sigmoid(fr)
    lf_ref[...] = jnp.log2(f).astype(BF16)
    hk_ref[...] = ((1.0 - lb) * jax.nn.sigmoid(-fr)).astype(BF16)
    hv_ref[...] = proj(2).astype(BF16)
    hg_ref[...] = _silu(proj(3)).astype(BF16)

    cosa, sina, cosb, sinb = _rope_patterns(tab_ref[0].T)
    half_a = ROPE_HALF_A
    q = proj(4)
    k = proj(5)
    lane_a = lax.broadcasted_iota(jnp.int32, cosa.shape, 1)
    norms = jnp.zeros(cosa.shape, F32)
    for hh in range(ATT_HEADS):
        sl = slice(hh * ATT_DIM, (hh + 1) * ATT_DIM)
        qh = (_rope(q[:, sl], cosa, sina, half_a, ATT_DIM) * Q_SCALE).astype(BF16)
        kh = _rope(k[:, sl], cosa, sina, half_a, ATT_DIM).astype(BF16)
        aq_ref[:, sl] = qh
        ak_ref[:, sl] = kh
        for slot, a in ((hh, qh), (ATT_HEADS + hh, kh)):
            norms = jnp.where(lane_a == slot, jnp.sqrt(_dot(a * a, jnp.ones((ATT_DIM, LANES), BF16))), norms)
    nrm_ref[...] = norms.T[0:2 * ATT_HEADS, :]
    v = proj(6)
    for hh in range(ATT_HEADS):
        r0 = hh * V_ROWS
        avt_ref[0, 0, r0:r0 + ATT_DIM, :] = v[:, hh * ATT_DIM:(hh + 1) * ATT_DIM].T.astype(BF16)
        avt_ref[0, 0, r0 + ATT_DIM:r0 + V_ROWS, :] = jnp.ones((V_ROWS - ATT_DIM, v.shape[0]), BF16)

    half_b = ROPE_HALF_B
    qi = proj(7)
    for cc in range(GROUP // LANES):
        qr = _rope(qi[:, cc * LANES:(cc + 1) * LANES], cosb, sinb, half_b, IDX_DIM).astype(BF16)
        for hh in range(LANES // IDX_DIM):
            iq_ref[cc * (LANES // IDX_DIM) + hh] = qr[:, hh * IDX_DIM:(hh + 1) * IDX_DIM]
    kw = _dot(hb, wt_ref[...])
    ki = kw[:, 0:LANES]
    lane = lax.broadcasted_iota(jnp.int32, ki.shape, 1)
    real = lane < IDX_DIM
    mu = jnp.sum(ki, axis=-1, keepdims=True) * (1.0 / IDX_DIM)
    d = jnp.where(real, ki - mu, 0.0)
    var = jnp.sum(d * d, axis=-1, keepdims=True) * (1.0 / IDX_DIM)
    kn = d * lax.rsqrt(var + EPS) * lnw_ref[...] + lnb_ref[...]
    kn = _rope(kn, cosb, sinb, half_b, IDX_DIM)
    ik_ref[...] = kn[:, 0:IDX_DIM].astype(BF16)
    iwt_ref[...] = kw[:, LANES:2 * LANES].T[0:IDX_HEADS, :] * (IDX_HEADS ** -0.5 * IDX_DIM ** -0.5)


def _in_proj(x2, mod3, gain, w_main, w_tail, lbp, lnw, lnb, tab, T, tm, kt):
    M, D = x2.shape
    nt = T // tm
    row = lambda i: (i, 0)
    const = lambda i: (0, 0)
    bf16o = jax.ShapeDtypeStruct((M, GROUP), BF16)
    grp_spec = pl.BlockSpec((tm, GROUP), row)
    return pl.pallas_call(
        _in_proj_kernel,
        grid=(M // tm,),
        in_specs=[pl.BlockSpec((tm, D), row),
                  pl.BlockSpec((1, N_MOD, D), lambda i: (i // nt, 0, 0)),
                  pl.BlockSpec((1, D), const),
                  _resident(w_main.shape, const),
                  _resident(w_tail.shape, const),
                  pl.BlockSpec(lbp.shape, const),
                  pl.BlockSpec((1, LANES), const),
                  pl.BlockSpec((1, LANES), const),
                  pl.BlockSpec((1, LANES, tm), lambda i: (i // nt, 0, i % nt))],
        out_specs=[grp_spec] * 7 + [pl.BlockSpec((1, 1, ATT_HEADS * V_ROWS, tm),
                                                 lambda i: (i // nt, (i % nt) // (kt // tm), 0, (i % nt) % (kt // tm))),
                                    pl.BlockSpec((IDX_HEADS, tm, IDX_DIM), lambda i: (0, i, 0)),
                                    pl.BlockSpec((tm, IDX_DIM), row),
                                    pl.BlockSpec((IDX_HEADS, tm), lambda i: (0, i)),
                                    pl.BlockSpec((2 * ATT_HEADS, tm), lambda i: (0, i))],
        out_shape=[bf16o] * 7 + [jax.ShapeDtypeStruct((M // T, T // kt, ATT_HEADS * V_ROWS, kt), BF16),
                                 jax.ShapeDtypeStruct((IDX_HEADS, M, IDX_DIM), BF16),
                                 jax.ShapeDtypeStruct((M, IDX_DIM), BF16),
                                 jax.ShapeDtypeStruct((IDX_HEADS, M), F32),
                                 jax.ShapeDtypeStruct((2 * ATT_HEADS, M), F32)],
        compiler_params=_cparams(("parallel",)),
        name="in_proj",
    )(x2, mod3, gain, w_main, w_tail, lbp, lnw, lnb, tab)


def _hgrn_consts():
    C = HG_CHUNK
    t = np.arange(C)
    tri = (t[None, :] <= t[:, None]).astype(np.float32)
    blocks = [tri]
    for m in HG_LEVELS:
        if m % 8:
            split = (t // (2 * m)) * (2 * m) + m - 1
            upper = ((t & m) != 0)[:, None]
            blocks.append(np.where(upper, tri - tri[split], tri[split] - tri))
    mall = np.concatenate(blocks, axis=0)
    x = t[:, None] ^ t[None, :]
    lvl = np.full((C, C), len(HG_LEVELS) + 1, np.int32)
    for li, m in enumerate(HG_LEVELS):
        lvl[(t[:, None] > t[None, :]) & (x >= m) & (x < 2 * m)] = li
    lvl[t[:, None] == t[None, :]] = len(HG_LEVELS)
    return jnp.asarray(mall, BF16), jnp.asarray(lvl)


def _hgrn_kernel(hq_ref, lf_ref, hk_ref, hv_ref, hg_ref, mall_ref, lvl_ref, onorm_ref, o_ref, st_ref):
    C = HG_CHUNK
    nl = len(HG_LEVELS)

    @pl.when(pl.program_id(1) == 0)
    def _():
        st_ref[...] = jnp.zeros_like(st_ref)

    lvl = lvl_ref[...]
    row = lax.broadcasted_iota(jnp.int32, (C, HG_DIM), 0)
    mall = mall_ref[...]

    for ci in range(hq_ref.shape[0] // C):
        rs = slice(ci * C, (ci + 1) * C)
        dheads = _dot(mall, lf_ref[rs, :])
        for h in range(HG_HEADS):
            cs = slice(h * HG_DIM, (h + 1) * HG_DIM)
            q, k, v = hq_ref[rs, cs], hk_ref[rs, cs], hv_ref[rs, cs]
            qf, kf = q.astype(F32), k.astype(F32)
            dall = dheads[:, cs]
            g = dall[0:C]

            a = jnp.zeros((C, C), F32)
            fine = 0
            for li, m in enumerate(HG_LEVELS):
                if m % 8 == 0:
                    qk = jnp.concatenate([(qf if (b & 1) else kf)[b * m:(b + 1) * m] for b in range(C // m)], axis=0)
                    parts = []
                    for b in range(C // (2 * m)):
                        lo_rows = slice(2 * b * m, (2 * b + 1) * m)
                        hi_rows = slice((2 * b + 1) * m, (2 * b + 2) * m)
                        ref = jnp.broadcast_to(g[lo_rows.stop - 1:lo_rows.stop, :], (m, HG_DIM))
                        parts += [ref - g[lo_rows], g[hi_rows] - ref]
                    dm = jnp.concatenate(parts, axis=0)
                else:
                    fine += 1
                    qk = jnp.where((row & m) != 0, qf, kf)
                    dm = dall[fine * C:(fine + 1) * C]
                xm = (qk * jnp.exp2(dm)).astype(BF16)
                a = jnp.where(lvl == li, _dot_nt(xm, xm), a)
            a = jnp.where(lvl == nl, _dot_nt(q, k), a)

            st = st_ref[h]
            o = _dot_nt((qf * jnp.exp2(g)).astype(BF16), st.astype(BF16)) + _dot(a.astype(BF16), v)
            g_last = g[C - 1:C, :]
            kd = (kf * jnp.exp2(g_last - g)).astype(BF16)
            st_ref[h] = jnp.exp2(g_last) * st + _dot(v.astype(F32).T.astype(BF16), kd)

            o = o * lax.rsqrt(jnp.mean(o * o, axis=-1, keepdims=True) + EPS)
            o_ref[rs, cs] = (o * onorm_ref[:, cs] * hg_ref[rs, cs]).astype(BF16)


def _hgrn(hq, lf, hk, hv, hg, onorm, B, T, ct):
    M = hq.shape[0]
    nct = T // ct
    mall, lvl = _hgrn_consts()
    blk = pl.BlockSpec((ct, GROUP), lambda b, c: (b * nct + c, 0))
    const = lambda b, c: (0, 0)
    return pl.pallas_call(
        _hgrn_kernel,
        grid=(B, nct),
        in_specs=[blk] * 5 + [pl.BlockSpec(mall.shape, const), pl.BlockSpec(lvl.shape, const),
                              pl.BlockSpec((1, GROUP), const)],
        out_specs=blk,
        out_shape=jax.ShapeDtypeStruct((M, GROUP), BF16),
        scratch_shapes=[pltpu.VMEM((HG_HEADS, HG_DIM, HG_DIM), F32)],
        compiler_params=_cparams(("parallel", "arbitrary")),
        name="hgrn2",
    )(hq, lf, hk, hv, hg, mall, lvl, onorm)


def _fold8(a, op, rows=8):
    chains = [None] * 4
    for r in range(a.shape[0] // rows):
        part = a[r * rows:(r + 1) * rows, :]
        c = r % len(chains)
        chains[c] = part if chains[c] is None else op(chains[c], part)
    return op(op(chains[0], chains[1]), op(chains[2], chains[3]))


def _floor_bf16(a):
    r = a.astype(BF16).astype(F32)
    below = (r - jnp.abs(r) * (5.0 / 1024.0)).astype(BF16)
    return jnp.where(r > a, below.astype(F32), r).astype(BF16)


def _dsa_kernel(iq_ref, iwt_ref, ik_ref, aq_ref, ak_ref, avt_ref, qnt_ref, knt_ref, o_ref,
                sc_ref, scb_ref, mm_ref, *head_refs, qb, kt, topk, seq):
    acc_refs, m_refs = head_refs[:ATT_HEADS], head_refs[ATT_HEADS:]
    i = pl.program_id(1)
    q0 = i * qb
    nfull = q0 // kt
    nkt = nfull + 1
    qpos = q0 + lax.broadcasted_iota(jnp.int32, (1, qb), 1)
    kk = jnp.minimum(qpos + 1, topk).astype(F32)

    mm_ref[0:8, :] = jnp.full((8, qb), jnp.inf, F32)
    mm_ref[8:16, :] = jnp.full((8, qb), -jnp.inf, F32)
    mm_ref[16:32, :] = jnp.zeros((16, qb), F32)

    def score_keys(j0, ntile, diag):
        nrow = ntile * kt
        ks = pl.multiple_of(j0 * kt, kt)
        rows = pl.ds(ks, nrow)
        ki = ik_ref[rows, :]
        s = None
        for h in range(IDX_HEADS):
            t = jnp.maximum(_dot_nt(ki, iq_ref[h]), 0.0) * iwt_ref[h:h + 1, :]
            s = t if s is None else s + t
        if diag:
            causal = ks + lax.broadcasted_iota(jnp.int32, (nrow, qb), 0) <= qpos
            s_hi = jnp.where(causal, s, -jnp.inf)
            s_lo = jnp.where(causal, s, jnp.inf)
        else:
            s_hi = s_lo = s
        sc_ref[rows, :] = s_hi
        fb = _floor_bf16(s_hi)
        scb_ref[rows, :] = fb
        fb_lo = _floor_bf16(s_lo) if diag else fb
        halves = lambda a: (a[0:8, :], a[8:16, :])
        lo_a, lo_b = halves(_fold8(fb_lo, jnp.minimum, rows=16).astype(F32))
        hi_a, hi_b = halves(_fold8(fb, jnp.maximum, rows=16).astype(F32))
        ge_a, ge_b = halves(_fold8(jnp.where(fb >= 0, jnp.ones((), BF16), jnp.zeros((), BF16)), jnp.add,
                                   rows=16).astype(F32))
        mm_ref[0:8, :] = jnp.minimum(mm_ref[0:8, :], jnp.minimum(lo_a, lo_b))
        mm_ref[8:16, :] = jnp.maximum(mm_ref[8:16, :], jnp.maximum(hi_a, hi_b))
        mm_ref[16:24, :] += ge_a + ge_b
        mm_ref[24:32, :] += _fold8(jnp.where(s_hi > 0.0, 1.0, 0.0), jnp.add)

    def score_group(jg, c):
        score_keys(SCORE_GROUP * jg, SCORE_GROUP, False)
        return c

    lax.fori_loop(0, nfull // SCORE_GROUP, score_group, 0)
    done = (nfull // SCORE_GROUP) * SCORE_GROUP
    size = SCORE_GROUP // 2
    while size:
        pl.when(((nfull - done) & size) != 0)(functools.partial(score_keys, done, size, False))
        done = done + ((nfull - done) & size)
        size //= 2
    score_keys(nfull, 1, True)
    mn = jnp.min(mm_ref[0:8, :], axis=0, keepdims=True)
    mx = jnp.max(mm_ref[8:16, :], axis=0, keepdims=True)
    c_nonneg = jnp.sum(mm_ref[16:24, :], axis=0, keepdims=True)
    c_pos = jnp.sum(mm_ref[24:32, :], axis=0, keepdims=True)

    def tile_hits(ref, j, hit_fn, slab, rows, cols=slice(None)):
        acc = None
        for r in range(kt // slab):
            ks = pl.multiple_of(j * kt + r * slab, slab)
            part = _fold8(hit_fn(ref[pl.ds(ks, slab), cols]), jnp.add, rows=rows)
            acc = part if acc is None else acc + part
        return acc

    def over_tiles(body, init):
        def group(jg, acc):
            for u in range(TILE_GROUP):
                acc = body(jg * TILE_GROUP + u, acc)
            return acc
        ngroups = nkt // TILE_GROUP
        return lax.fori_loop(ngroups * TILE_GROUP, nkt, body, lax.fori_loop(0, ngroups, group, init))

    def count_ge(th):
        def body(j, acc):
            return acc + tile_hits(sc_ref, j, lambda s: jnp.where(s >= th, 1.0, 0.0), COUNT_SLAB, 8)
        return jnp.sum(over_tiles(body, jnp.zeros((8, qb), F32)), axis=0, keepdims=True)

    above = c_pos >= kk
    below = c_nonneg < kk
    live = above | below
    zero = jnp.zeros((1, qb), F32)
    lo0 = jnp.where(below, mn, zero)
    cnt_lo0 = jnp.where(below, (qpos + 1).astype(F32), c_nonneg)
    hi0 = jnp.where(above, mx + jnp.abs(mx) * 2.0 ** -6 + 1e-30, zero)
    cnt_hi0 = jnp.where(above, zero, jnp.where(below, c_nonneg, c_pos))

    def midpoint(lo, hi):
        return lo + 0.5 * (hi - lo)

    assert kt // 16 <= 256

    def count_ge_bf16(th):
        def body(j, acc):
            hit = lambda s: jnp.where(s >= th, jnp.ones((), BF16), jnp.zeros((), BF16))
            part = tile_hits(scb_ref, j, hit, 2 * COUNT_SLAB, 16).astype(F32)
            return acc + part[0:8, :] + part[8:16, :]
        return jnp.sum(over_tiles(body, jnp.zeros((8, qb), F32)), axis=0, keepdims=True)

    def coarse_body(_, c):
        lo, hi, cnt_lo, cnt_hi = c
        th = midpoint(lo, hi).astype(BF16)
        mid = th.astype(F32)
        inside = live & (mid > lo) & (mid < hi)
        cnt = count_ge_bf16(th)
        up = inside & (cnt >= kk)
        dn = inside & (cnt < kk)
        return (jnp.where(up, mid, lo), jnp.where(dn, mid, hi),
                jnp.where(up, cnt, cnt_lo), jnp.where(dn, cnt, cnt_hi))

    lo1, hi1, cnt_lo1, cnt_hi1 = lax.fori_loop(0, COARSE_PASSES, coarse_body, (lo0, hi0, cnt_lo0, cnt_hi0))

    def fine_pass(c):
        lo, hi, cnt_lo, cnt_hi = c
        mid = midpoint(lo, hi)
        cnt = count_ge(mid)
        up = live & (cnt >= kk)
        dn = live & (cnt < kk)
        return (jnp.where(up, mid, lo), jnp.where(dn, mid, hi), jnp.where(up, cnt, cnt_lo), jnp.where(dn, cnt, cnt_hi))

    c2 = lax.fori_loop(0, FINE_PASSES, lambda _, c: fine_pass(c), (lo1, hi1, cnt_lo1, cnt_hi1))

    def finish_chunk(cs):
        kk_c, live_c = kk[:, cs], live[:, cs]

        def cond(c):
            it, (lo_c, hi_c, cnt_c, _) = c
            mid = midpoint(lo_c, hi_c)
            open_ = live_c & (cnt_c != kk_c) & (mid > lo_c) & (mid < hi_c)
            return (it < 320) & (jnp.max(open_.astype(jnp.int32)) > 0)

        def body(c):
            it, (lo_c, hi_c, cnt_lo_c, cnt_hi_c) = c
            mid = midpoint(lo_c, hi_c)
            tile = lambda j, acc: acc + tile_hits(sc_ref, j, lambda s: jnp.where(s >= mid, 1.0, 0.0), COUNT_SLAB, 8, cs)
            cnt = jnp.sum(over_tiles(tile, jnp.zeros((8, LANES), F32)), axis=0, keepdims=True)
            up = live_c & (cnt >= kk_c)
            dn = live_c & (cnt < kk_c)
            return it + 1, (jnp.where(up, mid, lo_c), jnp.where(dn, mid, hi_c),
                            jnp.where(up, cnt, cnt_lo_c), jnp.where(dn, cnt, cnt_hi_c))

        return lax.while_loop(cond, body, (jnp.int32(0), tuple(a[:, cs] for a in c2)))[1]

    chunks = [slice(c * LANES, (c + 1) * LANES) for c in range(qb // LANES)]
    lo, hi, cnt_lo, cnt_hi = (jnp.concatenate(parts, axis=1) for parts in zip(*[finish_chunk(cs) for cs in chunks]))

    tied = cnt_lo != kk
    need = kk - cnt_hi

    def resolve_ties(cs):
        lo_c, need_c = lo[:, cs], need[:, cs]
        tri = jnp.where(lax.broadcasted_iota(jnp.int32, (kt, kt), 0) >= lax.broadcasted_iota(jnp.int32, (kt, kt), 1),
                        1.0, 0.0).astype(BF16)

        def strike(j, seen):
            rows = pl.ds(pl.multiple_of(j * kt, kt), kt)
            s = sc_ref[rows, cs]
            eq = s == lo_c
            rank = seen + _dot(tri, jnp.where(eq, 1.0, 0.0).astype(BF16))
            sc_ref[rows, cs] = jnp.where(eq & (rank > need_c), -jnp.inf, s)
            return rank[kt - 1:kt, :]

        lax.fori_loop(0, nkt, strike, jnp.zeros((1, LANES), F32))

    for cs in chunks:
        pl.when(jnp.max(tied[:, cs].astype(jnp.int32)) > 0)(functools.partial(resolve_ties, cs))

    def logits(j, h, bias):
        ks = pl.multiple_of(j * kt, kt)
        sl = slice(h * ATT_DIM, (h + 1) * ATT_DIM)
        return _dot_nt(ak_ref[pl.ds(ks, kt), sl], aq_ref[:, sl]) + bias

    def select_bias(j):
        ks = pl.multiple_of(j * kt, kt)
        return jnp.where(sc_ref[pl.ds(ks, kt), :] >= lo, 0.0, NEG_BIG)

    def values(j, h):
        return avt_ref[0, j, h * V_ROWS:(h + 1) * V_ROWS, :]

    kmax = jnp.max(knt_ref[...], axis=1, keepdims=True)
    shift = [qnt_ref[h:h + 1, :] * kmax[ATT_HEADS + h:ATT_HEADS + h + 1, :] * NORM_SLACK for h in range(ATT_HEADS)]
    for h in range(ATT_HEADS):
        acc_refs[h][...] = jnp.zeros_like(acc_refs[h])

    def attn_keys_fast(j0, ntile):
        rows = pl.ds(pl.multiple_of(j0 * kt, kt), ntile * kt)
        keep = jnp.where(sc_ref[rows, :] >= lo, 1.0, 0.0).astype(BF16)
        for h in range(ATT_HEADS):
            sl = slice(h * ATT_DIM, (h + 1) * ATT_DIM)
            p = jnp.exp2(_dot_nt(ak_ref[rows, sl], aq_ref[:, sl]) - shift[h]).astype(BF16) * keep
            vals = [values(j0 + u, h) for u in range(ntile)]
            acc_refs[h][...] += _dot(vals[0] if ntile == 1 else jnp.concatenate(vals, axis=1), p)

    def attn_group(jg, c):
        attn_keys_fast(ATT_GROUP * jg, ATT_GROUP)
        return c

    lax.fori_loop(0, nkt // ATT_GROUP, attn_group, 0)
    done = (nkt // ATT_GROUP) * ATT_GROUP
    size = ATT_GROUP // 2
    while size:
        pl.when(((nkt - done) & size) != 0)(functools.partial(attn_keys_fast, done, size))
        done = done + ((nkt - done) & size)
        size //= 2
    denom_min = functools.reduce(jnp.minimum, [acc_refs[h][ATT_DIM:ATT_DIM + 1, :] for h in range(ATT_HEADS)])

    @pl.when(jnp.min(denom_min) < DENOM_FLOOR)
    def _():
        for h in range(ATT_HEADS):
            acc_refs[h][...] = jnp.zeros_like(acc_refs[h])
            m_refs[h][...] = jnp.full(m_refs[h].shape, NEG_BIG, F32)

        def attn_tile(j, c):
            bias = select_bias(j)
            for h in range(ATT_HEADS):
                acc_h, m_h = acc_refs[h], m_refs[h]
                lg = logits(j, h, bias)
                m_old = m_h[0:1, :]
                m_new = jnp.maximum(m_old, jnp.max(_fold8(lg, jnp.maximum), axis=0, keepdims=True))
                p = jnp.exp2(lg - m_new).astype(BF16)
                acc_h[...] = jnp.exp2(m_old - m_new) * acc_h[...] + _dot(values(j, h), p)
                m_h[0:1, :] = m_new
            return c

        lax.fori_loop(0, nkt, attn_tile, 0)

    for h in range(ATT_HEADS):
        o_t = acc_refs[h][0:ATT_DIM, :] / acc_refs[h][ATT_DIM:ATT_DIM + 1, :]
        o_ref[:, h * ATT_DIM:(h + 1) * ATT_DIM] = o_t.T.astype(BF16)


def _dsa(iq, iwt, ik, aq, ak, avt, nrm, B, T, qb, kt):
    M = ak.shape[0]
    nq = T // qb
    assert kt % qb == 0 and T % kt == 0 and avt.shape[1:] == (T // kt, ATT_HEADS * V_ROWS, kt)
    topk = min(TOPK_MAX, T // 4)
    qrow = lambda b, i: (b * nq + i, 0)
    brow = lambda b, i: (b, 0)
    kern = functools.partial(_dsa_kernel, qb=qb, kt=kt, topk=topk, seq=T)
    return pl.pallas_call(
        kern,
        grid=(B, nq),
        in_specs=[pl.BlockSpec((IDX_HEADS, qb, IDX_DIM), lambda b, i: (0, b * nq + i, 0)),
                  pl.BlockSpec((IDX_HEADS, qb), lambda b, i: (0, b * nq + i)),
                  _resident((T, IDX_DIM), brow),
                  pl.BlockSpec((qb, GROUP), qrow),
                  _resident((T, GROUP), brow),
                  _resident((1,) + avt.shape[1:], lambda b, i: (b, 0, 0, 0)),
                  pl.BlockSpec((2 * ATT_HEADS, qb), lambda b, i: (0, b * nq + i)),
                  _resident((2 * ATT_HEADS, T), lambda b, i: (0, b))],
        out_specs=pl.BlockSpec((qb, GROUP), qrow),
        out_shape=jax.ShapeDtypeStruct((M, GROUP), BF16),
        scratch_shapes=[pltpu.VMEM((T, qb), F32),
                        pltpu.VMEM((T, qb), BF16),
                        pltpu.VMEM((32, qb), F32),
                        ] + [pltpu.VMEM((V_ROWS, qb), F32)] * ATT_HEADS
                        + [pltpu.VMEM((8, qb), F32)] * ATT_HEADS,
        compiler_params=_cparams(("parallel", "arbitrary")),
        name="dsa",
    )(iq, iwt, ik, aq, ak, avt, nrm, nrm)


def _rms(a, gain):
    return a * lax.rsqrt(jnp.mean(a * a, axis=-1, keepdims=True) + EPS) * gain


def _out_ffn_kernel(ohg_ref, oatt_ref, x_ref, mod_ref, gains_ref, wo_ref, wi_ref, wf_ref, o_ref, acc_ref,
                    *, d_ff, fc):
    mod = mod_ref[0]
    gains = gains_ref[...]
    y = _dot(jnp.concatenate([ohg_ref[...], oatt_ref[...]], axis=1), wo_ref[...])
    x1 = x_ref[...] + mod[2:3, :] * _rms(y, gains[0:1, :])
    h2 = (_rms(x1, gains[1:2, :]) * (1.0 + mod[4:5, :]) + mod[3:4, :]).astype(BF16)
    for c in range(d_ff // fc):
        gate = _dot(h2, wi_ref[:, c * fc:(c + 1) * fc])
        up = _dot(h2, wi_ref[:, d_ff + c * fc:d_ff + (c + 1) * fc])
        part = _dot((_silu(gate) * up).astype(BF16), wf_ref[c * fc:(c + 1) * fc, :])
        if c == 0:
            acc_ref[...] = part
        else:
            acc_ref[...] += part
    o_ref[...] = x1 + mod[5:6, :] * _rms(acc_ref[...], gains[2:3, :])


def _out_ffn(ohg, oatt, x2, mod3, gains, wo, wi, wf, T, tm):
    M, D = x2.shape
    nt = T // tm
    d_ff = wf.shape[0]
    row = lambda i: (i, 0)
    const = lambda i: (0, 0)
    assert d_ff % FFN_CHUNK == 0
    kern = functools.partial(_out_ffn_kernel, d_ff=d_ff, fc=FFN_CHUNK)
    return pl.pallas_call(
        kern,
        grid=(M // tm,),
        in_specs=[pl.BlockSpec((tm, GROUP), row),
                  pl.BlockSpec((tm, GROUP), row),
                  pl.BlockSpec((tm, D), row),
                  pl.BlockSpec((1, N_MOD, D), lambda i: (i // nt, 0, 0)),
                  pl.BlockSpec((3, D), const),
                  _resident(wo.shape, const),
                  _resident(wi.shape, const),
                  _resident(wf.shape, const)],
        out_specs=pl.BlockSpec((tm, D), row),
        out_shape=jax.ShapeDtypeStruct((M, D), F32),
        scratch_shapes=[pltpu.VMEM((tm, D), F32)],
        compiler_params=_cparams(("parallel",)),
        name="out_ffn",
    )(ohg, oatt, x2, mod3, gains, wo, wi, wf)


def kernel(x, c, positions, w_ada, b_ada, norm_pre_mix, norm_post_mix, norm_pre_ffn, norm_post_ffn, w_in,
           hgrn_lower_bound, hgrn_out_norm, idx_k_norm_w, idx_k_norm_b, w_out, w_ffn_in, w_ffn_out):
    B, T, D = x.shape
    depth = w_ada.shape[0]
    assert depth == 1 and hgrn_lower_bound.shape[0] == 2
    tm = ROW_TILE
    assert T % tm == 0 and T % KEY_TILE == 0 and T % QUERY_BLOCK == 0 and D % LANES == 0 and tm == KEY_TILE

    mod3 = _adaln(c, w_ada, b_ada).reshape(B, N_MOD, D)
    tab = _rope_table(positions)

    main = 8 * GROUP
    w = w_in.reshape(w_in.shape[1:])
    w_main = w[:, :main].astype(BF16)
    w_tail = jnp.concatenate([jnp.pad(w[:, main:main + IDX_DIM], ((0, 0), (0, LANES - IDX_DIM))),
                              jnp.pad(w[:, main + IDX_DIM:], ((0, 0), (0, LANES - IDX_HEADS)))], axis=1).astype(BF16)
    lnw = jnp.pad(idx_k_norm_w[0], (0, LANES - IDX_DIM)).reshape(1, LANES)
    lnb = jnp.pad(idx_k_norm_b[0], (0, LANES - IDX_DIM)).reshape(1, LANES)

    x2 = x.reshape(B * T, D)
    hq, lf, hk, hv, hg, aq, ak, avt, iq, ik, iwt, nrm = _in_proj(
        x2, mod3, norm_pre_mix, w_main, w_tail, hgrn_lower_bound, lnw, lnb, tab, T, tm, tm)

    o_hg = _hgrn(hq, lf, hk, hv, hg, hgrn_out_norm, B, T, tm)
    o_att = _dsa(iq, iwt, ik, aq, ak, avt, nrm, B, T, QUERY_BLOCK, KEY_TILE)

    gains = jnp.concatenate([norm_post_mix, norm_pre_ffn, norm_post_ffn], axis=0)
    out = _out_ffn(o_hg, o_att, x2, mod3, gains, w_out.reshape(w_out.shape[1:]).astype(BF16),
                   w_ffn_in.reshape(w_ffn_in.shape[1:]).astype(BF16),
                   w_ffn_out.reshape(w_ffn_out.shape[1:]).astype(BF16), T, tm)
    return out.reshape(B, T, D)
```

```python
import functools

import numpy as np
import jax
import jax.numpy as jnp
from jax import lax
from jax.experimental import pallas as pl
from jax.experimental.pallas import tpu as pltpu

F32 = jnp.float32
BF16 = jnp.bfloat16

HG_HEADS = 4
HG_DIM = 128
ATT_HEADS = 4
ATT_DIM = 128
IDX_HEADS = 8
IDX_DIM = 64
TOPK_MAX = 256
ROPE_THETA = 500000.0
ROPE_FRACTION = 4
N_MOD = 6
EPS = 1e-6

GROUP = 512
LANES = 128
VMEM_LIMIT = 56 * 1024 * 1024

HG_CHUNK = 256
HG_LEVELS = (128, 64, 32, 16, 8, 4, 2, 1)

NEG_BIG = -1e30
DENOM_FLOOR = 2.0 ** -64
NORM_SLACK = 1.01
COARSE_PASSES = 10
FINE_PASSES = 10
COUNT_SLAB = 64
TILE_GROUP = 4
ATT_GROUP = 2

ROW_TILE = 512
KEY_TILE = 512
QUERY_BLOCK = 512
SCORE_GROUP = 1
FFN_CHUNK = 256
PROJ_GROUPS = 8
Q_SCALE = ATT_DIM ** -0.5 * 1.4426950408889634
V_ROWS = ATT_DIM + 16


def _cparams(sem):
    return pltpu.CompilerParams(dimension_semantics=sem, vmem_limit_bytes=VMEM_LIMIT)


def _resident(shape, index_map):
    return pl.BlockSpec(shape, index_map, pipeline_mode=pl.Buffered(1))


def _split_bf16(a):
    hi = a.astype(BF16)
    lo = (a - hi.astype(F32)).astype(BF16)
    return hi, lo


def _dot(a, b):
    return jnp.dot(a, b, preferred_element_type=F32)


def _dot_nt(a, b):
    return lax.dot_general(a, b, (((1,), (1,)), ((), ())), preferred_element_type=F32)


def _silu(a):
    return a * jax.nn.sigmoid(a)


def _adaln_kernel(c_ref, w_ref, b_ref, o_ref):
    a = _silu(c_ref[...])
    a_hi, a_lo = _split_bf16(a)
    w_hi, w_lo = _split_bf16(w_ref[0])
    acc = _dot(a_hi, w_hi) + (_dot(a_hi, w_lo) + _dot(a_lo, w_hi))
    o_ref[...] = acc + b_ref[...]


def _adaln(c, w, b):
    B, D = c.shape
    N = w.shape[2]
    return pl.pallas_call(
        _adaln_kernel,
        grid=(N // D,),
        in_specs=[pl.BlockSpec((B, D), lambda j: (0, 0)),
                  pl.BlockSpec((1, D, D), lambda j: (0, 0, j)),
                  pl.BlockSpec((1, D), lambda j: (0, j))],
        out_specs=pl.BlockSpec((B, D), lambda j: (0, j)),
        out_shape=jax.ShapeDtypeStruct((B, N), F32),
        compiler_params=_cparams(("arbitrary",)),
        name="adaln",
    )(c, w, b)


def _rope_table_kernel(pos_ref, fa_ref, fb_ref, o_ref):
    pos = pos_ref[0].astype(F32)
    ang_a = fa_ref[...] * pos
    ang_b = fb_ref[...] * pos
    ha = fa_ref.shape[0]
    hb = fb_ref.shape[0]
    o_ref[0, 0:ha, :] = jnp.cos(ang_a)
    o_ref[0, ha:2 * ha, :] = jnp.sin(ang_a)
    o_ref[0, 2 * ha:2 * ha + hb, :] = jnp.cos(ang_b)
    o_ref[0, 2 * ha + hb:2 * ha + 2 * hb, :] = jnp.sin(ang_b)
    o_ref[0, 2 * ha + 2 * hb:, :] = jnp.zeros((LANES - 2 * ha - 2 * hb, pos.shape[1]), F32)


ROPE_HALF_A = ATT_DIM // ROPE_FRACTION // 2
ROPE_HALF_B = IDX_DIM // ROPE_FRACTION // 2


def _rope_table(positions):
    B, T = positions.shape
    ha, hb = ROPE_HALF_A, ROPE_HALF_B
    fa = (ROPE_THETA ** (-jnp.arange(ha, dtype=F32) / ha)).reshape(ha, 1)
    fb = (ROPE_THETA ** (-jnp.arange(hb, dtype=F32) / hb)).reshape(hb, 1)
    return pl.pallas_call(
        _rope_table_kernel,
        grid=(B,),
        in_specs=[pl.BlockSpec((1, 1, T), lambda b: (b, 0, 0)),
                  pl.BlockSpec((ha, 1), lambda b: (0, 0)),
                  pl.BlockSpec((hb, 1), lambda b: (0, 0))],
        out_specs=pl.BlockSpec((1, LANES, T), lambda b: (b, 0, 0)),
        out_shape=jax.ShapeDtypeStruct((B, LANES, T), F32),
        compiler_params=_cparams(("arbitrary",)),
        name="rope_table",
    )(positions.reshape(B, 1, T), fa, fb)


def _rope_patterns(tab):
    ha, hb = ROPE_HALF_A, ROPE_HALF_B
    lane = lax.broadcasted_iota(jnp.int32, tab.shape, 1)
    shifted = lambda sh: pltpu.roll(tab, sh % LANES, axis=1)
    cos_a = jnp.where(lane < ha, tab, jnp.where(lane < 2 * ha, shifted(ha), 1.0))
    sin_a = jnp.where(lane < ha, -shifted(-ha), jnp.where(lane < 2 * ha, tab, 0.0))
    l64 = lane & (IDX_DIM - 1)
    first = lane < IDX_DIM
    cb0, sb0 = 2 * ha, 2 * ha + hb
    cos_b = jnp.where(l64 < hb, jnp.where(first, shifted(-cb0), shifted(IDX_DIM - cb0)),
                      jnp.where(l64 < 2 * hb, jnp.where(first, shifted(hb - cb0), shifted(IDX_DIM + hb - cb0)), 1.0))
    sin_b = jnp.where(l64 < hb, -jnp.where(first, shifted(-sb0), shifted(IDX_DIM - sb0)),
                      jnp.where(l64 < 2 * hb, jnp.where(first, shifted(hb - sb0), shifted(IDX_DIM + hb - sb0)), 0.0))
    return cos_a, sin_a, cos_b, sin_b


def _rope(xb, cos, sin, half, period):
    lane = lax.broadcasted_iota(jnp.int32, xb.shape, 1)
    fwd = pltpu.roll(xb, LANES - half, axis=1)
    bwd = pltpu.roll(xb, half, axis=1)
    partner = jnp.where((lane & (period - 1)) < half, fwd, bwd)
    return xb * cos + partner * sin


def _in_proj_kernel(x_ref, mod_ref, gain_ref, w_ref, wt_ref, lbp_ref, lnw_ref, lnb_ref, tab_ref,
                    hq_ref, lf_ref, hk_ref, hv_ref, hg_ref,
                    aq_ref, ak_ref, avt_ref, iq_ref, ik_ref, iwt_ref, nrm_ref):
    x = x_ref[...]
    ms = jnp.mean(x * x, axis=-1, keepdims=True)
    mod = mod_ref[0]
    h = x * lax.rsqrt(ms + EPS) * gain_ref[...]
    h = h * (1.0 + mod[1:2, :]) + mod[0:1, :]
    hb = h.astype(BF16)

    wide = {}

    def proj(g):
        base = g - g % PROJ_GROUPS
        if base not in wide:
            wide[base] = _dot(hb, w_ref[:, base * GROUP:(base + PROJ_GROUPS) * GROUP])
        return wide[base][:, (g - base) * GROUP:(g - base + 1) * GROUP]

    hq_ref[...] = (_silu(proj(0)) * (HG_DIM ** -0.5)).astype(BF16)
    a = lbp_ref[...]
    amax = jnp.max(a, axis=0, keepdims=True)
    e = jnp.exp(a - amax)
    lb = e[0:1, :] / jnp.sum(e, axis=0, keepdims=True)
    fr = proj(1)
    f = lb + (1.0 - lb) * jax.nn.sigmoid(fr)
    lf_ref[...] = jnp.log2(f).astype(BF16)
    hk_ref[...] = ((1.0 - lb) * jax.nn.sigmoid(-fr)).astype(BF16)
    hv_ref[...] = proj(2).astype(BF16)
    hg_ref[...] = _silu(proj(3)).astype(BF16)

    cosa, sina, cosb, sinb = _rope_patterns(tab_ref[0].T)
    half_a = ROPE_HALF_A
    q = proj(4)
    k = proj(5)
    lane_a = lax.broadcasted_iota(jnp.int32, cosa.shape, 1)
    norms = jnp.zeros(cosa.shape, F32)
    for hh in range(ATT_HEADS):
        sl = slice(hh * ATT_DIM, (hh + 1) * ATT_DIM)
        qh = (_rope(q[:, sl], cosa, sina, half_a, ATT_DIM) * Q_SCALE).astype(BF16)
        kh = _rope(k[:, sl], cosa, sina, half_a, ATT_DIM).astype(BF16)
        aq_ref[:, sl] = qh
        ak_ref[:, sl] = kh
        for slot, a in ((hh, qh), (ATT_HEADS + hh, kh)):
            norms = jnp.where(lane_a == slot, jnp.sqrt(_dot(a * a, jnp.ones((ATT_DIM, LANES), BF16))), norms)
    nrm_ref[...] = norms.T[0:2 * ATT_HEADS, :]
    v = proj(6)
    for hh in range(ATT_HEADS):
        r0 = hh * V_ROWS
        avt_ref[0, 0, r0:r0 + ATT_DIM, :] = v[:, hh * ATT_DIM:(hh + 1) * ATT_DIM].T.astype(BF16)
        avt_ref[0, 0, r0 + ATT_DIM:r0 + V_ROWS, :] = jnp.ones((V_ROWS - ATT_DIM, v.shape[0]), BF16)

    half_b = ROPE_HALF_B
    qi = proj(7)
    for cc in range(GROUP // LANES):
        qr = _rope(qi[:, cc * LANES:(cc + 1) * LANES], cosb, sinb, half_b, IDX_DIM).astype(BF16)
        for hh in range(LANES // IDX_DIM):
            iq_ref[cc * (LANES // IDX_DIM) + hh] = qr[:, hh * IDX_DIM:(hh + 1) * IDX_DIM]
    kw = _dot(hb, wt_ref[...])
    ki = kw[:, 0:LANES]
    lane = lax.broadcasted_iota(jnp.int32, ki.shape, 1)
    real = lane < IDX_DIM
    mu = jnp.sum(ki, axis=-1, keepdims=True) * (1.0 / IDX_DIM)
    d = jnp.where(real, ki - mu, 0.0)
    var = jnp.sum(d * d, axis=-1, keepdims=True) * (1.0 / IDX_DIM)
    kn = d * lax.rsqrt(var + EPS) * lnw_ref[...] + lnb_ref[...]
    kn = _rope(kn, cosb, sinb, half_b, IDX_DIM)
    ik_ref[...] = kn[:, 0:IDX_DIM].astype(BF16)
    iwt_ref[...] = kw[:, LANES:2 * LANES].T[0:IDX_HEADS, :] * (IDX_HEADS ** -0.5 * IDX_DIM ** -0.5)


def _in_proj(x2, mod3, gain, w_main, w_tail, lbp, lnw, lnb, tab, T, tm, kt):
    M, D = x2.shape
    nt = T // tm
    row = lambda i: (i, 0)
    const = lambda i: (0, 0)
    bf16o = jax.ShapeDtypeStruct((M, GROUP), BF16)
    grp_spec = pl.BlockSpec((tm, GROUP), row)
    return pl.pallas_call(
        _in_proj_kernel,
        grid=(M // tm,),
        in_specs=[pl.BlockSpec((tm, D), row),
                  pl.BlockSpec((1, N_MOD, D), lambda i: (i // nt, 0, 0)),
                  pl.BlockSpec((1, D), const),
                  _resident(w_main.shape, const),
                  _resident(w_tail.shape, const),
                  pl.BlockSpec(lbp.shape, const),
                  pl.BlockSpec((1, LANES), const),
                  pl.BlockSpec((1, LANES), const),
                  pl.BlockSpec((1, LANES, tm), lambda i: (i // nt, 0, i % nt))],
        out_specs=[grp_spec] * 7 + [pl.BlockSpec((1, 1, ATT_HEADS * V_ROWS, tm),
                                                 lambda i: (i // nt, (i % nt) // (kt // tm), 0, (i % nt) % (kt // tm))),
                                    pl.BlockSpec((IDX_HEADS, tm, IDX_DIM), lambda i: (0, i, 0)),
                                    pl.BlockSpec((tm, IDX_DIM), row),
                                    pl.BlockSpec((IDX_HEADS, tm), lambda i: (0, i)),
                                    pl.BlockSpec((2 * ATT_HEADS, tm), lambda i: (0, i))],
        out_shape=[bf16o] * 7 + [jax.ShapeDtypeStruct((M // T, T // kt, ATT_HEADS * V_ROWS, kt), BF16),
                                 jax.ShapeDtypeStruct((IDX_HEADS, M, IDX_DIM), BF16),
                                 jax.ShapeDtypeStruct((M, IDX_DIM), BF16),
                                 jax.ShapeDtypeStruct((IDX_HEADS, M), F32),
                                 jax.ShapeDtypeStruct((2 * ATT_HEADS, M), F32)],
        compiler_params=_cparams(("parallel",)),
        name="in_proj",
    )(x2, mod3, gain, w_main, w_tail, lbp, lnw, lnb, tab)


def _hgrn_consts():
    C = HG_CHUNK
    t = np.arange(C)
    tri = (t[None, :] <= t[:, None]).astype(np.float32)
    blocks = [tri]
    for m in HG_LEVELS:
        if m % 8:
            split = (t // (2 * m)) * (2 * m) + m - 1
            upper = ((t & m) != 0)[:, None]
            blocks.append(np.where(upper, tri - tri[split], tri[split] - tri))
    mall = np.concatenate(blocks, axis=0)
    x = t[:, None] ^ t[None, :]
    lvl = np.full((C, C), len(HG_LEVELS) + 1, np.int32)
    for li, m in enumerate(HG_LEVELS):
        lvl[(t[:, None] > t[None, :]) & (x >= m) & (x < 2 * m)] = li
    lvl[t[:, None] == t[None, :]] = len(HG_LEVELS)
    return jnp.asarray(mall, BF16), jnp.asarray(lvl)


def _hgrn_kernel(hq_ref, lf_ref, hk_ref, hv_ref, hg_ref, mall_ref, lvl_ref, onorm_ref, o_ref, st_ref):
    C = HG_CHUNK
    nl = len(HG_LEVELS)

    @pl.when(pl.program_id(1) == 0)
    def _():
        st_ref[...] = jnp.zeros_like(st_ref)

    lvl = lvl_ref[...]
    row = lax.broadcasted_iota(jnp.int32, (C, HG_DIM), 0)
    mall = mall_ref[...]

    for ci in range(hq_ref.shape[0] // C):
        rs = slice(ci * C, (ci + 1) * C)
        dheads = _dot(mall, lf_ref[rs, :])
        for h in range(HG_HEADS):
            cs = slice(h * HG_DIM, (h + 1) * HG_DIM)
            q, k, v = hq_ref[rs, cs], hk_ref[rs, cs], hv_ref[rs, cs]
            qf, kf = q.astype(F32), k.astype(F32)
            dall = dheads[:, cs]
            g = dall[0:C]

            a = jnp.zeros((C, C), F32)
            fine = 0
            for li, m in enumerate(HG_LEVELS):
                if m % 8 == 0:
                    qk = jnp.concatenate([(qf if (b & 1) else kf)[b * m:(b + 1) * m] for b in range(C // m)], axis=0)
                    parts = []
                    for b in range(C // (2 * m)):
                        lo_rows = slice(2 * b * m, (2 * b + 1) * m)
                        hi_rows = slice((2 * b + 1) * m, (2 * b + 2) * m)
                        ref = jnp.broadcast_to(g[lo_rows.stop - 1:lo_rows.stop, :], (m, HG_DIM))
                        parts += [ref - g[lo_rows], g[hi_rows] - ref]
                    dm = jnp.concatenate(parts, axis=0)
                else:
                    fine += 1
                    qk = jnp.where((row & m) != 0, qf, kf)
                    dm = dall[fine * C:(fine + 1) * C]
                xm = (qk * jnp.exp2(dm)).astype(BF16)
                a = jnp.where(lvl == li, _dot_nt(xm, xm), a)
            a = jnp.where(lvl == nl, _dot_nt(q, k), a)

            st = st_ref[h]
            o = _dot_nt((qf * jnp.exp2(g)).astype(BF16), st.astype(BF16)) + _dot(a.astype(BF16), v)
            g_last = g[C - 1:C, :]
            kd = (kf * jnp.exp2(g_last - g)).astype(BF16)
            st_ref[h] = jnp.exp2(g_last) * st + _dot(v.astype(F32).T.astype(BF16), kd)

            o = o * lax.rsqrt(jnp.mean(o * o, axis=-1, keepdims=True) + EPS)
            o_ref[rs, cs] = (o * onorm_ref[:, cs] * hg_ref[rs, cs]).astype(BF16)


def _hgrn(hq, lf, hk, hv, hg, onorm, B, T, ct):
    M = hq.shape[0]
    nct = T // ct
    mall, lvl = _hgrn_consts()
    blk = pl.BlockSpec((ct, GROUP), lambda b, c: (b * nct + c, 0))
    const = lambda b, c: (0, 0)
    return pl.pallas_call(
        _hgrn_kernel,
        grid=(B, nct),
        in_specs=[blk] * 5 + [pl.BlockSpec(mall.shape, const), pl.BlockSpec(lvl.shape, const),
                              pl.BlockSpec((1, GROUP), const)],
        out_specs=blk,
        out_shape=jax.ShapeDtypeStruct((M, GROUP), BF16),
        scratch_shapes=[pltpu.VMEM((HG_HEADS, HG_DIM, HG_DIM), F32)],
        compiler_params=_cparams(("parallel", "arbitrary")),
        name="hgrn2",
    )(hq, lf, hk, hv, hg, mall, lvl, onorm)


def _fold8(a, op, rows=8):
    chains = [None] * 4
    for r in range(a.shape[0] // rows):
        part = a[r * rows:(r + 1) * rows, :]
        c = r % len(chains)
        chains[c] = part if chains[c] is None else op(chains[c], part)
    return op(op(chains[0], chains[1]), op(chains[2], chains[3]))


def _floor_bf16(a):
    r = a.astype(BF16).astype(F32)
    below = (r - jnp.abs(r) * (5.0 / 1024.0)).astype(BF16)
    return jnp.where(r > a, below.astype(F32), r).astype(BF16)


def _dsa_kernel(iq_ref, iwt_ref, ik_ref, aq_ref, ak_ref, avt_ref, qnt_ref, knt_ref, o_ref,
                sc_ref, scb_ref, mm_ref, *head_refs, qb, kt, topk, seq):
    acc_refs, m_refs = head_refs[:ATT_HEADS], head_refs[ATT_HEADS:]
    i = pl.program_id(1)
    q0 = i * qb
    nfull = q0 // kt
    nkt = nfull + 1
    qpos = q0 + lax.broadcasted_iota(jnp.int32, (1, qb), 1)
    kk = jnp.minimum(qpos + 1, topk).astype(F32)

    mm_ref[0:8, :] = jnp.full((8, qb), jnp.inf, F32)
    mm_ref[8:16, :] = jnp.full((8, qb), -jnp.inf, F32)
    mm_ref[16:32, :] = jnp.zeros((16, qb), F32)

    def score_keys(j0, ntile, diag):
        nrow = ntile * kt
        ks = pl.multiple_of(j0 * kt, kt)
        rows = pl.ds(ks, nrow)
        ki = ik_ref[rows, :]
        s = None
        for h in range(IDX_HEADS):
            t = jnp.maximum(_dot_nt(ki, iq_ref[h]), 0.0) * iwt_ref[h:h + 1, :]
            s = t if s is None else s + t
        if diag:
            causal = ks + lax.broadcasted_iota(jnp.int32, (nrow, qb), 0) <= qpos
            s_hi = jnp.where(causal, s, -jnp.inf)
            s_lo = jnp.where(causal, s, jnp.inf)
        else:
            s_hi = s_lo = s
        sc_ref[rows, :] = s_hi
        fb = _floor_bf16(s_hi)
        scb_ref[rows, :] = fb
        fb_lo = _floor_bf16(s_lo) if diag else fb
        halves = lambda a: (a[0:8, :], a[8:16, :])
        lo_a, lo_b = halves(_fold8(fb_lo, jnp.minimum, rows=16).astype(F32))
        hi_a, hi_b = halves(_fold8(fb, jnp.maximum, rows=16).astype(F32))
        ge_a, ge_b = halves(_fold8(jnp.where(fb >= 0, jnp.ones((), BF16), jnp.zeros((), BF16)), jnp.add,
                                   rows=16).astype(F32))
        mm_ref[0:8, :] = jnp.minimum(mm_ref[0:8, :], jnp.minimum(lo_a, lo_b))
        mm_ref[8:16, :] = jnp.maximum(mm_ref[8:16, :], jnp.maximum(hi_a, hi_b))
        mm_ref[16:24, :] += ge_a + ge_b
        mm_ref[24:32, :] += _fold8(jnp.where(s_hi > 0.0, 1.0, 0.0), jnp.add)

    def score_group(jg, c):
        score_keys(SCORE_GROUP * jg, SCORE_GROUP, False)
        return c

    lax.fori_loop(0, nfull // SCORE_GROUP, score_group, 0)
    done = (nfull // SCORE_GROUP) * SCORE_GROUP
    size = SCORE_GROUP // 2
    while size:
        pl.when(((nfull - done) & size) != 0)(functools.partial(score_keys, done, size, False))
        done = done + ((nfull - done) & size)
        size //= 2
    score_keys(nfull, 1, True)
    mn = jnp.min(mm_ref[0:8, :], axis=0, keepdims=True)
    mx = jnp.max(mm_ref[8:16, :], axis=0, keepdims=True)
    c_nonneg = jnp.sum(mm_ref[16:24, :], axis=0, keepdims=True)
    c_pos = jnp.sum(mm_ref[24:32, :], axis=0, keepdims=True)

    def tile_hits(ref, j, hit_fn, slab, rows, cols=slice(None)):
        acc = None
        for r in range(kt // slab):
            ks = pl.multiple_of(j * kt + r * slab, slab)
            part = _fold8(hit_fn(ref[pl.ds(ks, slab), cols]), jnp.add, rows=rows)
            acc = part if acc is None else acc + part
        return acc

    def over_tiles(body, init):
        def group(jg, acc):
            for u in range(TILE_GROUP):
                acc = body(jg * TILE_GROUP + u, acc)
            return acc
        ngroups = nkt // TILE_GROUP
        return lax.fori_loop(ngroups * TILE_GROUP, nkt, body, lax.fori_loop(0, ngroups, group, init))

    def count_ge(th):
        def body(j, acc):
            return acc + tile_hits(sc_ref, j, lambda s: jnp.where(s >= th, 1.0, 0.0), COUNT_SLAB, 8)
        return jnp.sum(over_tiles(body, jnp.zeros((8, qb), F32)), axis=0, keepdims=True)

    above = c_pos >= kk
    below = c_nonneg < kk
    live = above | below
    zero = jnp.zeros((1, qb), F32)
    lo0 = jnp.where(below, mn, zero)
    cnt_lo0 = jnp.where(below, (qpos + 1).astype(F32), c_nonneg)
    hi0 = jnp.where(above, mx + jnp.abs(mx) * 2.0 ** -6 + 1e-30, zero)
    cnt_hi0 = jnp.where(above, zero, jnp.where(below, c_nonneg, c_pos))

    def midpoint(lo, hi):
        return lo + 0.5 * (hi - lo)

    assert kt // 16 <= 256

    def count_ge_bf16(th):
        def body(j, acc):
            hit = lambda s: jnp.where(s >= th, jnp.ones((), BF16), jnp.zeros((), BF16))
            part = tile_hits(scb_ref, j, hit, 2 * COUNT_SLAB, 16).astype(F32)
            return acc + part[0:8, :] + part[8:16, :]
        return jnp.sum(over_tiles(body, jnp.zeros((8, qb), F32)), axis=0, keepdims=True)

    def coarse_body(_, c):
        lo, hi, cnt_lo, cnt_hi = c
        th = midpoint(lo, hi).astype(BF16)
        mid = th.astype(F32)
        inside = live & (mid > lo) & (mid < hi)
        cnt = count_ge_bf16(th)
        up = inside & (cnt >= kk)
        dn = inside & (cnt < kk)
        return (jnp.where(up, mid, lo), jnp.where(dn, mid, hi),
                jnp.where(up, cnt, cnt_lo), jnp.where(dn, cnt, cnt_hi))

    lo1, hi1, cnt_lo1, cnt_hi1 = lax.fori_loop(0, COARSE_PASSES, coarse_body, (lo0, hi0, cnt_lo0, cnt_hi0))

    def fine_pass(c):
        lo, hi, cnt_lo, cnt_hi = c
        mid = midpoint(lo, hi)
        cnt = count_ge(mid)
        up = live & (cnt >= kk)
        dn = live & (cnt < kk)
        return (jnp.where(up, mid, lo), jnp.where(dn, mid, hi), jnp.where(up, cnt, cnt_lo), jnp.where(dn, cnt, cnt_hi))

    c2 = lax.fori_loop(0, FINE_PASSES, lambda _, c: fine_pass(c), (lo1, hi1, cnt_lo1, cnt_hi1))

    def finish_chunk(cs):
        kk_c, live_c = kk[:, cs], live[:, cs]

        def cond(c):
            it, (lo_c, hi_c, cnt_c, _) = c
            mid = midpoint(lo_c, hi_c)
            open_ = live_c & (cnt_c != kk_c) & (mid > lo_c) & (mid < hi_c)
            return (it < 320) & (jnp.max(open_.astype(jnp.int32)) > 0)

        def body(c):
            it, (lo_c, hi_c, cnt_lo_c, cnt_hi_c) = c
            mid = midpoint(lo_c, hi_c)
            tile = lambda j, acc: acc + tile_hits(sc_ref, j, lambda s: jnp.where(s >= mid, 1.0, 0.0), COUNT_SLAB, 8, cs)
            cnt = jnp.sum(over_tiles(tile, jnp.zeros((8, LANES), F32)), axis=0, keepdims=True)
            up = live_c & (cnt >= kk_c)
            dn = live_c & (cnt < kk_c)
            return it + 1, (jnp.where(up, mid, lo_c), jnp.where(dn, mid, hi_c),
                            jnp.where(up, cnt, cnt_lo_c), jnp.where(dn, cnt, cnt_hi_c))

        return lax.while_loop(cond, body, (jnp.int32(0), tuple(a[:, cs] for a in c2)))[1]

    chunks = [slice(c * LANES, (c + 1) * LANES) for c in range(qb // LANES)]
    lo, hi, cnt_lo, cnt_hi = (jnp.concatenate(parts, axis=1) for parts in zip(*[finish_chunk(cs) for cs in chunks]))

    tied = cnt_lo != kk
    need = kk - cnt_hi

    def resolve_ties(cs):
        lo_c, need_c = lo[:, cs], need[:, cs]
        tri = jnp.where(lax.broadcasted_iota(jnp.int32, (kt, kt), 0) >= lax.broadcasted_iota(jnp.int32, (kt, kt), 1),
                        1.0, 0.0).astype(BF16)

        def strike(j, seen):
            rows = pl.ds(pl.multiple_of(j * kt, kt), kt)
            s = sc_ref[rows, cs]
            eq = s == lo_c
            rank = seen + _dot(tri, jnp.where(eq, 1.0, 0.0).astype(BF16))
            sc_ref[rows, cs] = jnp.where(eq & (rank > need_c), -jnp.inf, s)
            return rank[kt - 1:kt, :]

        lax.fori_loop(0, nkt, strike, jnp.zeros((1, LANES), F32))

    for cs in chunks:
        pl.when(jnp.max(tied[:, cs].astype(jnp.int32)) > 0)(functools.partial(resolve_ties, cs))

    def logits(j, h, bias):
        ks = pl.multiple_of(j * kt, kt)
        sl = slice(h * ATT_DIM, (h + 1) * ATT_DIM)
        return _dot_nt(ak_ref[pl.ds(ks, kt), sl], aq_ref[:, sl]) + bias

    def select_bias(j):
        ks = pl.multiple_of(j * kt, kt)
        return jnp.where(sc_ref[pl.ds(ks, kt), :] >= lo, 0.0, NEG_BIG)

    def values(j, h):
        return avt_ref[0, j, h * V_ROWS:(h + 1) * V_ROWS, :]

    kmax = jnp.max(knt_ref[...], axis=1, keepdims=True)
    shift = [qnt_ref[h:h + 1, :] * kmax[ATT_HEADS + h:ATT_HEADS + h + 1, :] * NORM_SLACK for h in range(ATT_HEADS)]
    for h in range(ATT_HEADS):
        acc_refs[h][...] = jnp.zeros_like(acc_refs[h])

    def attn_keys_fast(j0, ntile):
        rows = pl.ds(pl.multiple_of(j0 * kt, kt), ntile * kt)
        keep = jnp.where(sc_ref[rows, :] >= lo, 1.0, 0.0).astype(BF16)
        for h in range(ATT_HEADS):
            sl = slice(h * ATT_DIM, (h + 1) * ATT_DIM)
            p = jnp.exp2(_dot_nt(ak_ref[rows, sl], aq_ref[:, sl]) - shift[h]).astype(BF16) * keep
            vals = [values(j0 + u, h) for u in range(ntile)]
            acc_refs[h][...] += _dot(vals[0] if ntile == 1 else jnp.concatenate(vals, axis=1), p)

    def attn_group(jg, c):
        attn_keys_fast(ATT_GROUP * jg, ATT_GROUP)
        return c

    lax.fori_loop(0, nkt // ATT_GROUP, attn_group, 0)
    done = (nkt // ATT_GROUP) * ATT_GROUP
    size = ATT_GROUP // 2
    while size:
        pl.when(((nkt - done) & size) != 0)(functools.partial(attn_keys_fast, done, size))
        done = done + ((nkt - done) & size)
        size //= 2
    denom_min = functools.reduce(jnp.minimum, [acc_refs[h][ATT_DIM:ATT_DIM + 1, :] for h in range(ATT_HEADS)])

    @pl.when(jnp.min(denom_min) < DENOM_FLOOR)
    def _():
        for h in range(ATT_HEADS):
            acc_refs[h][...] = jnp.zeros_like(acc_refs[h])
            m_refs[h][...] = jnp.full(m_refs[h].shape, NEG_BIG, F32)

        def attn_tile(j, c):
            bias = select_bias(j)
            for h in range(ATT_HEADS):
                acc_h, m_h = acc_refs[h], m_refs[h]
                lg = logits(j, h, bias)
                m_old = m_h[0:1, :]
                m_new = jnp.maximum(m_old, jnp.max(_fold8(lg, jnp.maximum), axis=0, keepdims=True))
                p = jnp.exp2(lg - m_new).astype(BF16)
                acc_h[...] = jnp.exp2(m_old - m_new) * acc_h[...] + _dot(values(j, h), p)
                m_h[0:1, :] = m_new
            return c

        lax.fori_loop(0, nkt, attn_tile, 0)

    for h in range(ATT_HEADS):
        o_t = acc_refs[h][0:ATT_DIM, :] / acc_refs[h][ATT_DIM:ATT_DIM + 1, :]
        o_ref[:, h * ATT_DIM:(h + 1) * ATT_DIM] = o_t.T.astype(BF16)


def _dsa(iq, iwt, ik, aq, ak, avt, nrm, B, T, qb, kt):
    M = ak.shape[0]
    nq = T // qb
    assert kt % qb == 0 and T % kt == 0 and avt.shape[1:] == (T // kt, ATT_HEADS * V_ROWS, kt)
    topk = min(TOPK_MAX, T // 4)
    qrow = lambda b, i: (b * nq + i, 0)
    brow = lambda b, i: (b, 0)
    kern = functools.partial(_dsa_kernel, qb=qb, kt=kt, topk=topk, seq=T)
    return pl.pallas_call(
        kern,
        grid=(B, nq),
        in_specs=[pl.BlockSpec((IDX_HEADS, qb, IDX_DIM), lambda b, i: (0, b * nq + i, 0)),
                  pl.BlockSpec((IDX_HEADS, qb), lambda b, i: (0, b * nq + i)),
                  _resident((T, IDX_DIM), brow),
                  pl.BlockSpec((qb, GROUP), qrow),
                  _resident((T, GROUP), brow),
                  _resident((1,) + avt.shape[1:], lambda b, i: (b, 0, 0, 0)),
                  pl.BlockSpec((2 * ATT_HEADS, qb), lambda b, i: (0, b * nq + i)),
                  _resident((2 * ATT_HEADS, T), lambda b, i: (0, b))],
        out_specs=pl.BlockSpec((qb, GROUP), qrow),
        out_shape=jax.ShapeDtypeStruct((M, GROUP), BF16),
        scratch_shapes=[pltpu.VMEM((T, qb), F32),
                        pltpu.VMEM((T, qb), BF16),
                        pltpu.VMEM((32, qb), F32),
                        ] + [pltpu.VMEM((V_ROWS, qb), F32)] * ATT_HEADS
                        + [pltpu.VMEM((8, qb), F32)] * ATT_HEADS,
        compiler_params=_cparams(("parallel", "arbitrary")),
        name="dsa",
    )(iq, iwt, ik, aq, ak, avt, nrm, nrm)


def _rms(a, gain):
    return a * lax.rsqrt(jnp.mean(a * a, axis=-1, keepdims=True) + EPS) * gain


def _out_ffn_kernel(ohg_ref, oatt_ref, x_ref, mod_ref, gains_ref, wo_ref, wi_ref, wf_ref, o_ref, acc_ref,
                    *, d_ff, fc):
    mod = mod_ref[0]
    gains = gains_ref[...]
    y = _dot(jnp.concatenate([ohg_ref[...], oatt_ref[...]], axis=1), wo_ref[...])
    x1 = x_ref[...] + mod[2:3, :] * _rms(y, gains[0:1, :])
    h2 = (_rms(x1, gains[1:2, :]) * (1.0 + mod[4:5, :]) + mod[3:4, :]).astype(BF16)
    for c in range(d_ff // fc):
        gate = _dot(h2, wi_ref[:, c * fc:(c + 1) * fc])
        up = _dot(h2, wi_ref[:, d_ff + c * fc:d_ff + (c + 1) * fc])
        part = _dot((_silu(gate) * up).astype(BF16), wf_ref[c * fc:(c + 1) * fc, :])
        if c == 0:
            acc_ref[...] = part
        else:
            acc_ref[...] += part
    o_ref[...] = x1 + mod[5:6, :] * _rms(acc_ref[...], gains[2:3, :])


def _out_ffn(ohg, oatt, x2, mod3, gains, wo, wi, wf, T, tm):
    M, D = x2.shape
    nt = T // tm
    d_ff = wf.shape[0]
    row = lambda i: (i, 0)
    const = lambda i: (0, 0)
    assert d_ff % FFN_CHUNK == 0
    kern = functools.partial(_out_ffn_kernel, d_ff=d_ff, fc=FFN_CHUNK)
    return pl.pallas_call(
        kern,
        grid=(M // tm,),
        in_specs=[pl.BlockSpec((tm, GROUP), row),
                  pl.BlockSpec((tm, GROUP), row),
                  pl.BlockSpec((tm, D), row),
                  pl.BlockSpec((1, N_MOD, D), lambda i: (i // nt, 0, 0)),
                  pl.BlockSpec((3, D), const),
                  _resident(wo.shape, const),
                  _resident(wi.shape, const),
                  _resident(wf.shape, const)],
        out_specs=pl.BlockSpec((tm, D), row),
        out_shape=jax.ShapeDtypeStruct((M, D), F32),
        scratch_shapes=[pltpu.VMEM((tm, D), F32)],
        compiler_params=_cparams(("parallel",)),
        name="out_ffn",
    )(ohg, oatt, x2, mod3, gains, wo, wi, wf)


def kernel(x, c, positions, w_ada, b_ada, norm_pre_mix, norm_post_mix, norm_pre_ffn, norm_post_ffn, w_in,
           hgrn_lower_bound, hgrn_out_norm, idx_k_norm_w, idx_k_norm_b, w_out, w_ffn_in, w_ffn_out):
    B, T, D = x.shape
    depth = w_ada.shape[0]
    assert depth == 1 and hgrn_lower_bound.shape[0] == 2
    tm = ROW_TILE
    assert T % tm == 0 and T % KEY_TILE == 0 and T % QUERY_BLOCK == 0 and D % LANES == 0 and tm == KEY_TILE

    mod3 = _adaln(c, w_ada, b_ada).reshape(B, N_MOD, D)
    tab = _rope_table(positions)

    main = 8 * GROUP
    w = w_in.reshape(w_in.shape[1:])
    w_main = w[:, :main].astype(BF16)
    w_tail = jnp.concatenate([jnp.pad(w[:, main:main + IDX_DIM], ((0, 0), (0, LANES - IDX_DIM))),
                              jnp.pad(w[:, main + IDX_DIM:], ((0, 0), (0, LANES - IDX_HEADS)))], axis=1).astype(BF16)
    lnw = jnp.pad(idx_k_norm_w[0], (0, LANES - IDX_DIM)).reshape(1, LANES)
    lnb = jnp.pad(idx_k_norm_b[0], (0, LANES - IDX_DIM)).reshape(1, LANES)

    x2 = x.reshape(B * T, D)
    hq, lf, hk, hv, hg, aq, ak, avt, iq, ik, iwt, nrm = _in_proj(
        x2, mod3, norm_pre_mix, w_main, w_tail, hgrn_lower_bound, lnw, lnb, tab, T, tm, tm)

    o_hg = _hgrn(hq, lf, hk, hv, hg, hgrn_out_norm, B, T, tm)
    o_att = _dsa(iq, iwt, ik, aq, ak, avt, nrm, B, T, QUERY_BLOCK, KEY_TILE)

    gains = jnp.concatenate([norm_post_mix, norm_pre_ffn, norm_post_ffn], axis=0)
    out = _out_ffn(o_hg, o_att, x2, mod3, gains, w_out.reshape(w_out.shape[1:]).astype(BF16),
                   w_ffn_in.reshape(w_ffn_in.shape[1:]).astype(BF16),
                   w_ffn_out.reshape(w_ffn_out.shape[1:]).astype(BF16), T, tm)
    return out.reshape(B, T, D)
```

```python
import functools

import numpy as np
import jax
import jax.numpy as jnp
from jax import lax
from jax.experimental import pallas as pl
from jax.experimental.pallas import tpu as pltpu

F32 = jnp.float32
BF16 = jnp.bfloat16

HG_HEADS = 4
HG_DIM = 128
ATT_HEADS = 4
ATT_DIM = 128
IDX_HEADS = 8
IDX_DIM = 64
TOPK_MAX = 256
ROPE_THETA = 500000.0
ROPE_FRACTION = 4
N_MOD = 6
EPS = 1e-6

GROUP = 512
LANES = 128
VMEM_LIMIT = 56 * 1024 * 1024

HG_CHUNK = 256
HG_LEVELS = (128, 64, 32, 16, 8, 4, 2, 1)

NEG_BIG = -1e30
DENOM_FLOOR = 2.0 ** -64
NORM_SLACK = 1.01
COARSE_PASSES = 10
FINE_PASSES = 10
COUNT_SLAB = 64
TILE_GROUP = 4
ATT_GROUP = 2

ROW_TILE = 512
KEY_TILE = 512
QUERY_BLOCK = 512
SCORE_GROUP = 1
FFN_CHUNK = 256
PROJ_GROUPS = 8
Q_SCALE = ATT_DIM ** -0.5 * 1.4426950408889634
V_ROWS = ATT_DIM + 16


def _cparams(sem):
    return pltpu.CompilerParams(dimension_semantics=sem, vmem_limit_bytes=VMEM_LIMIT)


def _resident(shape, index_map):
    return pl.BlockSpec(shape, index_map, pipeline_mode=pl.Buffered(1))


def _split_bf16(a):
    hi = a.astype(BF16)
    lo = (a - hi.astype(F32)).astype(BF16)
    return hi, lo


def _dot(a, b):
    return jnp.dot(a, b, preferred_element_type=F32)


def _dot_nt(a, b):
    return lax.dot_general(a, b, (((1,), (1,)), ((), ())), preferred_element_type=F32)


def _silu(a):
    return a * jax.nn.sigmoid(a)


def _adaln_kernel(c_ref, w_ref, b_ref, o_ref):
    a = _silu(c_ref[...])
    a_hi, a_lo = _split_bf16(a)
    w_hi, w_lo = _split_bf16(w_ref[0])
    acc = _dot(a_hi, w_hi) + (_dot(a_hi, w_lo) + _dot(a_lo, w_hi))
    o_ref[...] = acc + b_ref[...]


def _adaln(c, w, b):
    B, D = c.shape
    N = w.shape[2]
    return pl.pallas_call(
        _adaln_kernel,
        grid=(N // D,),
        in_specs=[pl.BlockSpec((B, D), lambda j: (0, 0)),
                  pl.BlockSpec((1, D, D), lambda j: (0, 0, j)),
                  pl.BlockSpec((1, D), lambda j: (0, j))],
        out_specs=pl.BlockSpec((B, D), lambda j: (0, j)),
        out_shape=jax.ShapeDtypeStruct((B, N), F32),
        compiler_params=_cparams(("arbitrary",)),
        name="adaln",
    )(c, w, b)


def _rope_table_kernel(pos_ref, fa_ref, fb_ref, o_ref):
    pos = pos_ref[0].astype(F32)
    ang_a = fa_ref[...] * pos
    ang_b = fb_ref[...] * pos
    ha = fa_ref.shape[0]
    hb = fb_ref.shape[0]
    o_ref[0, 0:ha, :] = jnp.cos(ang_a)
    o_ref[0, ha:2 * ha, :] = jnp.sin(ang_a)
    o_ref[0, 2 * ha:2 * ha + hb, :] = jnp.cos(ang_b)
    o_ref[0, 2 * ha + hb:2 * ha + 2 * hb, :] = jnp.sin(ang_b)
    o_ref[0, 2 * ha + 2 * hb:, :] = jnp.zeros((LANES - 2 * ha - 2 * hb, pos.shape[1]), F32)


ROPE_HALF_A = ATT_DIM // ROPE_FRACTION // 2
ROPE_HALF_B = IDX_DIM // ROPE_FRACTION // 2


def _rope_table(positions):
    B, T = positions.shape
    ha, hb = ROPE_HALF_A, ROPE_HALF_B
    fa = (ROPE_THETA ** (-jnp.arange(ha, dtype=F32) / ha)).reshape(ha, 1)
    fb = (ROPE_THETA ** (-jnp.arange(hb, dtype=F32) / hb)).reshape(hb, 1)
    return pl.pallas_call(
        _rope_table_kernel,
        grid=(B,),
        in_specs=[pl.BlockSpec((1, 1, T), lambda b: (b, 0, 0)),
                  pl.BlockSpec((ha, 1), lambda b: (0, 0)),
                  pl.BlockSpec((hb, 1), lambda b: (0, 0))],
        out_specs=pl.BlockSpec((1, LANES, T), lambda b: (b, 0, 0)),
        out_shape=jax.ShapeDtypeStruct((B, LANES, T), F32),
        compiler_params=_cparams(("arbitrary",)),
        name="rope_table",
    )(positions.reshape(B, 1, T), fa, fb)


def _rope_patterns(tab):
    ha, hb = ROPE_HALF_A, ROPE_HALF_B
    lane = lax.broadcasted_iota(jnp.int32, tab.shape, 1)
    shifted = lambda sh: pltpu.roll(tab, sh % LANES, axis=1)
    cos_a = jnp.where(lane < ha, tab, jnp.where(lane < 2 * ha, shifted(ha), 1.0))
    sin_a = jnp.where(lane < ha, -shifted(-ha), jnp.where(lane < 2 * ha, tab, 0.0))
    l64 = lane & (IDX_DIM - 1)
    first = lane < IDX_DIM
    cb0, sb0 = 2 * ha, 2 * ha + hb
    cos_b = jnp.where(l64 < hb, jnp.where(first, shifted(-cb0), shifted(IDX_DIM - cb0)),
                      jnp.where(l64 < 2 * hb, jnp.where(first, shifted(hb - cb0), shifted(IDX_DIM + hb - cb0)), 1.0))
    sin_b = jnp.where(l64 < hb, -jnp.where(first, shifted(-sb0), shifted(IDX_DIM - sb0)),
                      jnp.where(l64 < 2 * hb, jnp.where(first, shifted(hb - sb0), shifted(IDX_DIM + hb - sb0)), 0.0))
    return cos_a, sin_a, cos_b, sin_b


def _rope(xb, cos, sin, half, period):
    lane = lax.broadcasted_iota(jnp.int32, xb.shape, 1)
    fwd = pltpu.roll(xb, LANES - half, axis=1)
    bwd = pltpu.roll(xb, half, axis=1)
    partner = jnp.where((lane & (period - 1)) < half, fwd, bwd)
    return xb * cos + partner * sin


def _in_proj_kernel(x_ref, mod_ref, gain_ref, w_ref, wt_ref, lbp_ref, lnw_ref, lnb_ref, tab_ref,
                    hq_ref, lf_ref, hk_ref, hv_ref, hg_ref,
                    aq_ref, ak_ref, avt_ref, iq_ref, ik_ref, iwt_ref, nrm_ref, wbf_ref):
    @pl.when(pl.program_id(0) == 0)
    def _():
        wbf_ref[...] = w_ref[...].astype(BF16)

    x = x_ref[...]
    ms = jnp.mean(x * x, axis=-1, keepdims=True)
    mod = mod_ref[0]
    h = x * lax.rsqrt(ms + EPS) * gain_ref[...]
    h = h * (1.0 + mod[1:2, :]) + mod[0:1, :]
    hb = h.astype(BF16)

    wide = {}

    def proj(g):
        base = g - g % PROJ_GROUPS
        if base not in wide:
            wide[base] = _dot(hb, wbf_ref[:, base * GROUP:(base + PROJ_GROUPS) * GROUP])
        return wide[base][:, (g - base) * GROUP:(g - base + 1) * GROUP]

    hq_ref[...] = (_silu(proj(0)) * (HG_DIM ** -0.5)).astype(BF16)
    a = lbp_ref[...]
    amax = jnp.max(a, axis=0, keepdims=True)
    e = jnp.exp(a - amax)
    lb = e[0:1, :] / jnp.sum(e, axis=0, keepdims=True)
    fr = proj(1)
    f = lb + (1.0 - lb) * jax.nn.sigmoid(fr)
    lf_ref[...] = jnp.log2(f).astype(BF16)
    hk_ref[...] = ((1.0 - lb) * jax.nn.sigmoid(-fr)).astype(BF16)
    hv_ref[...] = proj(2).astype(BF16)
    hg_ref[...] = _silu(proj(3)).astype(BF16)

    cosa, sina, cosb, sinb = _rope_patterns(tab_ref[0].T)
    half_a = ROPE_HALF_A
    q = proj(4)
    k = proj(5)
    lane_a = lax.broadcasted_iota(jnp.int32, cosa.shape, 1)
    norms = jnp.zeros(cosa.shape, F32)
    for hh in range(ATT_HEADS):
        sl = slice(hh * ATT_DIM, (hh + 1) * ATT_DIM)
        qh = (_rope(q[:, sl], cosa, sina, half_a, ATT_DIM) * Q_SCALE).astype(BF16)
        kh = _rope(k[:, sl], cosa, sina, half_a, ATT_DIM).astype(BF16)
        aq_ref[:, sl] = qh
        ak_ref[:, sl] = kh
        for slot, a in ((hh, qh), (ATT_HEADS + hh, kh)):
            norms = jnp.where(lane_a == slot, jnp.sqrt(_dot(a * a, jnp.ones((ATT_DIM, LANES), BF16))), norms)
    nrm_ref[...] = norms.T[0:2 * ATT_HEADS, :]
    v = proj(6)
    for hh in range(ATT_HEADS):
        r0 = hh * V_ROWS
        avt_ref[0, 0, r0:r0 + ATT_DIM, :] = v[:, hh * ATT_DIM:(hh + 1) * ATT_DIM].T.astype(BF16)
        avt_ref[0, 0, r0 + ATT_DIM:r0 + V_ROWS, :] = jnp.ones((V_ROWS - ATT_DIM, v.shape[0]), BF16)

    half_b = ROPE_HALF_B
    qi = proj(7)
    for cc in range(GROUP // LANES):
        qr = _rope(qi[:, cc * LANES:(cc + 1) * LANES], cosb, sinb, half_b, IDX_DIM).astype(BF16)
        for hh in range(LANES // IDX_DIM):
            iq_ref[cc * (LANES // IDX_DIM) + hh] = qr[:, hh * IDX_DIM:(hh + 1) * IDX_DIM]
    kw = _dot(hb, wt_ref[...])
    ki = kw[:, 0:LANES]
    lane = lax.broadcasted_iota(jnp.int32, ki.shape, 1)
    real = lane < IDX_DIM
    mu = jnp.sum(ki, axis=-1, keepdims=True) * (1.0 / IDX_DIM)
    d = jnp.where(real, ki - mu, 0.0)
    var = jnp.sum(d * d, axis=-1, keepdims=True) * (1.0 / IDX_DIM)
    kn = d * lax.rsqrt(var + EPS) * lnw_ref[...] + lnb_ref[...]
    kn = _rope(kn, cosb, sinb, half_b, IDX_DIM)
    ik_ref[...] = kn[:, 0:IDX_DIM].astype(BF16)
    iwt_ref[...] = kw[:, LANES:2 * LANES].T[0:IDX_HEADS, :] * (IDX_HEADS ** -0.5 * IDX_DIM ** -0.5)


def _in_proj(x2, mod3, gain, w_main, w_tail, lbp, lnw, lnb, tab, T, tm, kt):
    M, D = x2.shape
    nt = T // tm
    row = lambda i: (i, 0)
    const = lambda i: (0, 0)
    bf16o = jax.ShapeDtypeStruct((M, GROUP), BF16)
    grp_spec = pl.BlockSpec((tm, GROUP), row)
    return pl.pallas_call(
        _in_proj_kernel,
        grid=(M // tm,),
        in_specs=[pl.BlockSpec((tm, D), row),
                  pl.BlockSpec((1, N_MOD, D), lambda i: (i // nt, 0, 0)),
                  pl.BlockSpec((1, D), const),
                  _resident((D, 8 * GROUP), const),
                  _resident(w_tail.shape, const),
                  pl.BlockSpec(lbp.shape, const),
                  pl.BlockSpec((1, LANES), const),
                  pl.BlockSpec((1, LANES), const),
                  pl.BlockSpec((1, LANES, tm), lambda i: (i // nt, 0, i % nt))],
        out_specs=[grp_spec] * 7 + [pl.BlockSpec((1, 1, ATT_HEADS * V_ROWS, tm),
                                                 lambda i: (i // nt, (i % nt) // (kt // tm), 0, (i % nt) % (kt // tm))),
                                    pl.BlockSpec((IDX_HEADS, tm, IDX_DIM), lambda i: (0, i, 0)),
                                    pl.BlockSpec((tm, IDX_DIM), row),
                                    pl.BlockSpec((IDX_HEADS, tm), lambda i: (0, i)),
                                    pl.BlockSpec((2 * ATT_HEADS, tm), lambda i: (0, i))],
        out_shape=[bf16o] * 7 + [jax.ShapeDtypeStruct((M // T, T // kt, ATT_HEADS * V_ROWS, kt), BF16),
                                 jax.ShapeDtypeStruct((IDX_HEADS, M, IDX_DIM), BF16),
                                 jax.ShapeDtypeStruct((M, IDX_DIM), BF16),
                                 jax.ShapeDtypeStruct((IDX_HEADS, M), F32),
                                 jax.ShapeDtypeStruct((2 * ATT_HEADS, M), F32)],
        scratch_shapes=[pltpu.VMEM((D, 8 * GROUP), BF16)],
        compiler_params=_cparams(("arbitrary",)),
        name="in_proj",
    )(x2, mod3, gain, w_main, w_tail, lbp, lnw, lnb, tab)


def _hgrn_consts():
    C = HG_CHUNK
    t = np.arange(C)
    tri = (t[None, :] <= t[:, None]).astype(np.float32)
    blocks = [tri]
    for m in HG_LEVELS:
        if m % 8:
            split = (t // (2 * m)) * (2 * m) + m - 1
            upper = ((t & m) != 0)[:, None]
            blocks.append(np.where(upper, tri - tri[split], tri[split] - tri))
    mall = np.concatenate(blocks, axis=0)
    x = t[:, None] ^ t[None, :]
    lvl = np.full((C, C), len(HG_LEVELS) + 1, np.int32)
    for li, m in enumerate(HG_LEVELS):
        lvl[(t[:, None] > t[None, :]) & (x >= m) & (x < 2 * m)] = li
    lvl[t[:, None] == t[None, :]] = len(HG_LEVELS)
    return jnp.asarray(mall, BF16), jnp.asarray(lvl)


def _hgrn_kernel(hq_ref, lf_ref, hk_ref, hv_ref, hg_ref, mall_ref, lvl_ref, onorm_ref, o_ref, st_ref):
    C = HG_CHUNK
    nl = len(HG_LEVELS)

    @pl.when(pl.program_id(1) == 0)
    def _():
        st_ref[...] = jnp.zeros_like(st_ref)

    lvl = lvl_ref[...]
    row = lax.broadcasted_iota(jnp.int32, (C, HG_DIM), 0)
    mall = mall_ref[...]

    for ci in range(hq_ref.shape[0] // C):
        rs = slice(ci * C, (ci + 1) * C)
        dheads = _dot(mall, lf_ref[rs, :])
        for h in range(HG_HEADS):
            cs = slice(h * HG_DIM, (h + 1) * HG_DIM)
            q, k, v = hq_ref[rs, cs], hk_ref[rs, cs], hv_ref[rs, cs]
            qf, kf = q.astype(F32), k.astype(F32)
            dall = dheads[:, cs]
            g = dall[0:C]

            a = jnp.zeros((C, C), F32)
            fine = 0
            for li, m in enumerate(HG_LEVELS):
                if m % 8 == 0:
                    qk = jnp.concatenate([(qf if (b & 1) else kf)[b * m:(b + 1) * m] for b in range(C // m)], axis=0)
                    parts = []
                    for b in range(C // (2 * m)):
                        lo_rows = slice(2 * b * m, (2 * b + 1) * m)
                        hi_rows = slice((2 * b + 1) * m, (2 * b + 2) * m)
                        ref = jnp.broadcast_to(g[lo_rows.stop - 1:lo_rows.stop, :], (m, HG_DIM))
                        parts += [ref - g[lo_rows], g[hi_rows] - ref]
                    dm = jnp.concatenate(parts, axis=0)
                else:
                    fine += 1
                    qk = jnp.where((row & m) != 0, qf, kf)
                    dm = dall[fine * C:(fine + 1) * C]
                xm = (qk * jnp.exp2(dm)).astype(BF16)
                a = jnp.where(lvl == li, _dot_nt(xm, xm), a)
            a = jnp.where(lvl == nl, _dot_nt(q, k), a)

            st = st_ref[h]
            o = _dot_nt((qf * jnp.exp2(g)).astype(BF16), st.astype(BF16)) + _dot(a.astype(BF16), v)
            g_last = g[C - 1:C, :]
            kd = (kf * jnp.exp2(g_last - g)).astype(BF16)
            st_ref[h] = jnp.exp2(g_last) * st + _dot(v.astype(F32).T.astype(BF16), kd)

            o = o * lax.rsqrt(jnp.mean(o * o, axis=-1, keepdims=True) + EPS)
            o_ref[rs, cs] = (o * onorm_ref[:, cs] * hg_ref[rs, cs]).astype(BF16)


def _hgrn(hq, lf, hk, hv, hg, onorm, B, T, ct):
    M = hq.shape[0]
    nct = T // ct
    mall, lvl = _hgrn_consts()
    blk = pl.BlockSpec((ct, GROUP), lambda b, c: (b * nct + c, 0))
    const = lambda b, c: (0, 0)
    return pl.pallas_call(
        _hgrn_kernel,
        grid=(B, nct),
        in_specs=[blk] * 5 + [pl.BlockSpec(mall.shape, const), pl.BlockSpec(lvl.shape, const),
                              pl.BlockSpec((1, GROUP), const)],
        out_specs=blk,
        out_shape=jax.ShapeDtypeStruct((M, GROUP), BF16),
        scratch_shapes=[pltpu.VMEM((HG_HEADS, HG_DIM, HG_DIM), F32)],
        compiler_params=_cparams(("parallel", "arbitrary")),
        name="hgrn2",
    )(hq, lf, hk, hv, hg, mall, lvl, onorm)


def _fold8(a, op, rows=8):
    chains = [None] * 4
    for r in range(a.shape[0] // rows):
        part = a[r * rows:(r + 1) * rows, :]
        c = r % len(chains)
        chains[c] = part if chains[c] is None else op(chains[c], part)
    return op(op(chains[0], chains[1]), op(chains[2], chains[3]))


def _floor_bf16(a):
    r = a.astype(BF16).astype(F32)
    below = (r - jnp.abs(r) * (5.0 / 1024.0)).astype(BF16)
    return jnp.where(r > a, below.astype(F32), r).astype(BF16)


def _dsa_kernel(iq_ref, iwt_ref, ik_ref, aq_ref, ak_ref, avt_ref, qnt_ref, knt_ref, o_ref,
                sc_ref, scb_ref, mm_ref, *head_refs, qb, kt, topk, seq):
    acc_refs, m_refs = head_refs[:ATT_HEADS], head_refs[ATT_HEADS:]
    i = pl.program_id(1)
    q0 = i * qb
    nfull = q0 // kt
    nkt = nfull + 1
    qpos = q0 + lax.broadcasted_iota(jnp.int32, (1, qb), 1)
    kk = jnp.minimum(qpos + 1, topk).astype(F32)

    mm_ref[0:8, :] = jnp.full((8, qb), jnp.inf, F32)
    mm_ref[8:16, :] = jnp.full((8, qb), -jnp.inf, F32)
    mm_ref[16:32, :] = jnp.zeros((16, qb), F32)

    def score_keys(j0, ntile, diag):
        nrow = ntile * kt
        ks = pl.multiple_of(j0 * kt, kt)
        rows = pl.ds(ks, nrow)
        ki = ik_ref[rows, :]
        s = None
        for h in range(IDX_HEADS):
            t = jnp.maximum(_dot_nt(ki, iq_ref[h]), 0.0) * iwt_ref[h:h + 1, :]
            s = t if s is None else s + t
        if diag:
            causal = ks + lax.broadcasted_iota(jnp.int32, (nrow, qb), 0) <= qpos
            s_hi = jnp.where(causal, s, -jnp.inf)
            s_lo = jnp.where(causal, s, jnp.inf)
        else:
            s_hi = s_lo = s
        sc_ref[rows, :] = s_hi
        fb = _floor_bf16(s_hi)
        scb_ref[rows, :] = fb
        fb_lo = _floor_bf16(s_lo) if diag else fb
        halves = lambda a: (a[0:8, :], a[8:16, :])
        lo_a, lo_b = halves(_fold8(fb_lo, jnp.minimum, rows=16).astype(F32))
        hi_a, hi_b = halves(_fold8(fb, jnp.maximum, rows=16).astype(F32))
        ge_a, ge_b = halves(_fold8(jnp.where(fb >= 0, jnp.ones((), BF16), jnp.zeros((), BF16)), jnp.add,
                                   rows=16).astype(F32))
        mm_ref[0:8, :] = jnp.minimum(mm_ref[0:8, :], jnp.minimum(lo_a, lo_b))
        mm_ref[8:16, :] = jnp.maximum(mm_ref[8:16, :], jnp.maximum(hi_a, hi_b))
        mm_ref[16:24, :] += ge_a + ge_b
        mm_ref[24:32, :] += _fold8(jnp.where(s_hi > 0.0, 1.0, 0.0), jnp.add)

    def score_group(jg, c):
        score_keys(SCORE_GROUP * jg, SCORE_GROUP, False)
        return c

    lax.fori_loop(0, nfull // SCORE_GROUP, score_group, 0)
    done = (nfull // SCORE_GROUP) * SCORE_GROUP
    size = SCORE_GROUP // 2
    while size:
        pl.when(((nfull - done) & size) != 0)(functools.partial(score_keys, done, size, False))
        done = done + ((nfull - done) & size)
        size //= 2
    score_keys(nfull, 1, True)
    mn = jnp.min(mm_ref[0:8, :], axis=0, keepdims=True)
    mx = jnp.max(mm_ref[8:16, :], axis=0, keepdims=True)
    c_nonneg = jnp.sum(mm_ref[16:24, :], axis=0, keepdims=True)
    c_pos = jnp.sum(mm_ref[24:32, :], axis=0, keepdims=True)

    def tile_hits(ref, j, hit_fn, slab, rows, cols=slice(None)):
        acc = None
        for r in range(kt // slab):
            ks = pl.multiple_of(j * kt + r * slab, slab)
            part = _fold8(hit_fn(ref[pl.ds(ks, slab), cols]), jnp.add, rows=rows)
            acc = part if acc is None else acc + part
        return acc

    def over_tiles(body, init):
        def group(jg, acc):
            for u in range(TILE_GROUP):
                acc = body(jg * TILE_GROUP + u, acc)
            return acc
        ngroups = nkt // TILE_GROUP
        return lax.fori_loop(ngroups * TILE_GROUP, nkt, body, lax.fori_loop(0, ngroups, group, init))

    def count_ge(th):
        def body(j, acc):
            return acc + tile_hits(sc_ref, j, lambda s: jnp.where(s >= th, 1.0, 0.0), COUNT_SLAB, 8)
        return jnp.sum(over_tiles(body, jnp.zeros((8, qb), F32)), axis=0, keepdims=True)

    above = c_pos >= kk
    below = c_nonneg < kk
    live = above | below
    zero = jnp.zeros((1, qb), F32)
    lo0 = jnp.where(below, mn, zero)
    cnt_lo0 = jnp.where(below, (qpos + 1).astype(F32), c_nonneg)
    hi0 = jnp.where(above, mx + jnp.abs(mx) * 2.0 ** -6 + 1e-30, zero)
    cnt_hi0 = jnp.where(above, zero, jnp.where(below, c_nonneg, c_pos))

    def midpoint(lo, hi):
        return lo + 0.5 * (hi - lo)

    assert kt // 16 <= 256

    def count_ge_bf16(th):
        def body(j, acc):
            hit = lambda s: jnp.where(s >= th, jnp.ones((), BF16), jnp.zeros((), BF16))
            part = tile_hits(scb_ref, j, hit, 2 * COUNT_SLAB, 16).astype(F32)
            return acc + part[0:8, :] + part[8:16, :]
        return jnp.sum(over_tiles(body, jnp.zeros((8, qb), F32)), axis=0, keepdims=True)

    def coarse_body(_, c):
        lo, hi, cnt_lo, cnt_hi = c
        th = midpoint(lo, hi).astype(BF16)
        mid = th.astype(F32)
        inside = live & (mid > lo) & (mid < hi)
        cnt = count_ge_bf16(th)
        up = inside & (cnt >= kk)
        dn = inside & (cnt < kk)
        return (jnp.where(up, mid, lo), jnp.where(dn, mid, hi),
                jnp.where(up, cnt, cnt_lo), jnp.where(dn, cnt, cnt_hi))

    lo1, hi1, cnt_lo1, cnt_hi1 = lax.fori_loop(0, COARSE_PASSES, coarse_body, (lo0, hi0, cnt_lo0, cnt_hi0))

    def fine_pass(c):
        lo, hi, cnt_lo, cnt_hi = c
        mid = midpoint(lo, hi)
        cnt = count_ge(mid)
        up = live & (cnt >= kk)
        dn = live & (cnt < kk)
        return (jnp.where(up, mid, lo), jnp.where(dn, mid, hi), jnp.where(up, cnt, cnt_lo), jnp.where(dn, cnt, cnt_hi))

    c2 = lax.fori_loop(0, FINE_PASSES, lambda _, c: fine_pass(c), (lo1, hi1, cnt_lo1, cnt_hi1))

    def finish_chunk(cs):
        kk_c, live_c = kk[:, cs], live[:, cs]

        def cond(c):
            it, (lo_c, hi_c, cnt_c, _) = c
            mid = midpoint(lo_c, hi_c)
            open_ = live_c & (cnt_c != kk_c) & (mid > lo_c) & (mid < hi_c)
            return (it < 320) & (jnp.max(open_.astype(jnp.int32)) > 0)

        def body(c):
            it, (lo_c, hi_c, cnt_lo_c, cnt_hi_c) = c
            mid = midpoint(lo_c, hi_c)
            tile = lambda j, acc: acc + tile_hits(sc_ref, j, lambda s: jnp.where(s >= mid, 1.0, 0.0), COUNT_SLAB, 8, cs)
            cnt = jnp.sum(over_tiles(tile, jnp.zeros((8, LANES), F32)), axis=0, keepdims=True)
            up = live_c & (cnt >= kk_c)
            dn = live_c & (cnt < kk_c)
            return it + 1, (jnp.where(up, mid, lo_c), jnp.where(dn, mid, hi_c),
                            jnp.where(up, cnt, cnt_lo_c), jnp.where(dn, cnt, cnt_hi_c))

        return lax.while_loop(cond, body, (jnp.int32(0), tuple(a[:, cs] for a in c2)))[1]

    chunks = [slice(c * LANES, (c + 1) * LANES) for c in range(qb // LANES)]
    lo, hi, cnt_lo, cnt_hi = (jnp.concatenate(parts, axis=1) for parts in zip(*[finish_chunk(cs) for cs in chunks]))

    tied = cnt_lo != kk
    need = kk - cnt_hi

    def resolve_ties(cs):
        lo_c, need_c = lo[:, cs], need[:, cs]
        tri = jnp.where(lax.broadcasted_iota(jnp.int32, (kt, kt), 0) >= lax.broadcasted_iota(jnp.int32, (kt, kt), 1),
                        1.0, 0.0).astype(BF16)

        def strike(j, seen):
            rows = pl.ds(pl.multiple_of(j * kt, kt), kt)
            s = sc_ref[rows, cs]
            eq = s == lo_c
            rank = seen + _dot(tri, jnp.where(eq, 1.0, 0.0).astype(BF16))
            sc_ref[rows, cs] = jnp.where(eq & (rank > need_c), -jnp.inf, s)
            return rank[kt - 1:kt, :]

        lax.fori_loop(0, nkt, strike, jnp.zeros((1, LANES), F32))

    for cs in chunks:
        pl.when(jnp.max(tied[:, cs].astype(jnp.int32)) > 0)(functools.partial(resolve_ties, cs))

    def logits(j, h, bias):
        ks = pl.multiple_of(j * kt, kt)
        sl = slice(h * ATT_DIM, (h + 1) * ATT_DIM)
        return _dot_nt(ak_ref[pl.ds(ks, kt), sl], aq_ref[:, sl]) + bias

    def select_bias(j):
        ks = pl.multiple_of(j * kt, kt)
        return jnp.where(sc_ref[pl.ds(ks, kt), :] >= lo, 0.0, NEG_BIG)

    def values(j, h):
        return avt_ref[0, j, h * V_ROWS:(h + 1) * V_ROWS, :]

    kmax = jnp.max(knt_ref[...], axis=1, keepdims=True)
    shift = [qnt_ref[h:h + 1, :] * kmax[ATT_HEADS + h:ATT_HEADS + h + 1, :] * NORM_SLACK for h in range(ATT_HEADS)]
    for h in range(ATT_HEADS):
        acc_refs[h][...] = jnp.zeros_like(acc_refs[h])

    def attn_keys_fast(j0, ntile):
        rows = pl.ds(pl.multiple_of(j0 * kt, kt), ntile * kt)
        keep = jnp.where(sc_ref[rows, :] >= lo, 1.0, 0.0).astype(BF16)
        for h in range(ATT_HEADS):
            sl = slice(h * ATT_DIM, (h + 1) * ATT_DIM)
            p = jnp.exp2(_dot_nt(ak_ref[rows, sl], aq_ref[:, sl]) - shift[h]).astype(BF16) * keep
            vals = [values(j0 + u, h) for u in range(ntile)]
            acc_refs[h][...] += _dot(vals[0] if ntile == 1 else jnp.concatenate(vals, axis=1), p)

    def attn_group(jg, c):
        attn_keys_fast(ATT_GROUP * jg, ATT_GROUP)
        return c

    lax.fori_loop(0, nkt // ATT_GROUP, attn_group, 0)
    done = (nkt // ATT_GROUP) * ATT_GROUP
    size = ATT_GROUP // 2
    while size:
        pl.when(((nkt - done) & size) != 0)(functools.partial(attn_keys_fast, done, size))
        done = done + ((nkt - done) & size)
        size //= 2
    denom_min = functools.reduce(jnp.minimum, [acc_refs[h][ATT_DIM:ATT_DIM + 1, :] for h in range(ATT_HEADS)])

    @pl.when(jnp.min(denom_min) < DENOM_FLOOR)
    def _():
        for h in range(ATT_HEADS):
            acc_refs[h][...] = jnp.zeros_like(acc_refs[h])
            m_refs[h][...] = jnp.full(m_refs[h].shape, NEG_BIG, F32)

        def attn_tile(j, c):
            bias = select_bias(j)
            for h in range(ATT_HEADS):
                acc_h, m_h = acc_refs[h], m_refs[h]
                lg = logits(j, h, bias)
                m_old = m_h[0:1, :]
                m_new = jnp.maximum(m_old, jnp.max(_fold8(lg, jnp.maximum), axis=0, keepdims=True))
                p = jnp.exp2(lg - m_new).astype(BF16)
                acc_h[...] = jnp.exp2(m_old - m_new) * acc_h[...] + _dot(values(j, h), p)
                m_h[0:1, :] = m_new
            return c

        lax.fori_loop(0, nkt, attn_tile, 0)

    for h in range(ATT_HEADS):
        o_t = acc_refs[h][0:ATT_DIM, :] / acc_refs[h][ATT_DIM:ATT_DIM + 1, :]
        o_ref[:, h * ATT_DIM:(h + 1) * ATT_DIM] = o_t.T.astype(BF16)


def _dsa(iq, iwt, ik, aq, ak, avt, nrm, B, T, qb, kt):
    M = ak.shape[0]
    nq = T // qb
    assert kt % qb == 0 and T % kt == 0 and avt.shape[1:] == (T // kt, ATT_HEADS * V_ROWS, kt)
    topk = min(TOPK_MAX, T // 4)
    qrow = lambda b, i: (b * nq + i, 0)
    brow = lambda b, i: (b, 0)
    kern = functools.partial(_dsa_kernel, qb=qb, kt=kt, topk=topk, seq=T)
    return pl.pallas_call(
        kern,
        grid=(B, nq),
        in_specs=[pl.BlockSpec((IDX_HEADS, qb, IDX_DIM), lambda b, i: (0, b * nq + i, 0)),
                  pl.BlockSpec((IDX_HEADS, qb), lambda b, i: (0, b * nq + i)),
                  _resident((T, IDX_DIM), brow),
                  pl.BlockSpec((qb, GROUP), qrow),
                  _resident((T, GROUP), brow),
                  _resident((1,) + avt.shape[1:], lambda b, i: (b, 0, 0, 0)),
                  pl.BlockSpec((2 * ATT_HEADS, qb), lambda b, i: (0, b * nq + i)),
                  _resident((2 * ATT_HEADS, T), lambda b, i: (0, b))],
        out_specs=pl.BlockSpec((qb, GROUP), qrow),
        out_shape=jax.ShapeDtypeStruct((M, GROUP), BF16),
        scratch_shapes=[pltpu.VMEM((T, qb), F32),
                        pltpu.VMEM((T, qb), BF16),
                        pltpu.VMEM((32, qb), F32),
                        ] + [pltpu.VMEM((V_ROWS, qb), F32)] * ATT_HEADS
                        + [pltpu.VMEM((8, qb), F32)] * ATT_HEADS,
        compiler_params=_cparams(("parallel", "arbitrary")),
        name="dsa",
    )(iq, iwt, ik, aq, ak, avt, nrm, nrm)


def _rms(a, gain):
    return a * lax.rsqrt(jnp.mean(a * a, axis=-1, keepdims=True) + EPS) * gain


def _out_ffn_kernel(ohg_ref, oatt_ref, x_ref, mod_ref, gains_ref, wo_ref, wi_ref, wf_ref, o_ref, acc_ref,
                    *, d_ff, fc):
    mod = mod_ref[0]
    gains = gains_ref[...]
    y = _dot(jnp.concatenate([ohg_ref[...], oatt_ref[...]], axis=1), wo_ref[...])
    x1 = x_ref[...] + mod[2:3, :] * _rms(y, gains[0:1, :])
    h2 = (_rms(x1, gains[1:2, :]) * (1.0 + mod[4:5, :]) + mod[3:4, :]).astype(BF16)
    for c in range(d_ff // fc):
        gate = _dot(h2, wi_ref[:, c * fc:(c + 1) * fc])
        up = _dot(h2, wi_ref[:, d_ff + c * fc:d_ff + (c + 1) * fc])
        part = _dot((_silu(gate) * up).astype(BF16), wf_ref[c * fc:(c + 1) * fc, :])
        if c == 0:
            acc_ref[...] = part
        else:
            acc_ref[...] += part
    o_ref[...] = x1 + mod[5:6, :] * _rms(acc_ref[...], gains[2:3, :])


def _out_ffn(ohg, oatt, x2, mod3, gains, wo, wi, wf, T, tm):
    M, D = x2.shape
    nt = T // tm
    d_ff = wf.shape[0]
    row = lambda i: (i, 0)
    const = lambda i: (0, 0)
    assert d_ff % FFN_CHUNK == 0
    kern = functools.partial(_out_ffn_kernel, d_ff=d_ff, fc=FFN_CHUNK)
    return pl.pallas_call(
        kern,
        grid=(M // tm,),
        in_specs=[pl.BlockSpec((tm, GROUP), row),
                  pl.BlockSpec((tm, GROUP), row),
                  pl.BlockSpec((tm, D), row),
                  pl.BlockSpec((1, N_MOD, D), lambda i: (i // nt, 0, 0)),
                  pl.BlockSpec((3, D), const),
                  _resident(wo.shape, const),
                  _resident(wi.shape, const),
                  _resident(wf.shape, const)],
        out_specs=pl.BlockSpec((tm, D), row),
        out_shape=jax.ShapeDtypeStruct((M, D), F32),
        scratch_shapes=[pltpu.VMEM((tm, D), F32)],
        compiler_params=_cparams(("parallel",)),
        name="out_ffn",
    )(ohg, oatt, x2, mod3, gains, wo, wi, wf)


def kernel(x, c, positions, w_ada, b_ada, norm_pre_mix, norm_post_mix, norm_pre_ffn, norm_post_ffn, w_in,
           hgrn_lower_bound, hgrn_out_norm, idx_k_norm_w, idx_k_norm_b, w_out, w_ffn_in, w_ffn_out):
    B, T, D = x.shape
    depth = w_ada.shape[0]
    assert depth == 1 and hgrn_lower_bound.shape[0] == 2
    tm = ROW_TILE
    assert T % tm == 0 and T % KEY_TILE == 0 and T % QUERY_BLOCK == 0 and D % LANES == 0 and tm == KEY_TILE

    mod3 = _adaln(c, w_ada, b_ada).reshape(B, N_MOD, D)
    tab = _rope_table(positions)

    main = 8 * GROUP
    w = w_in.reshape(w_in.shape[1:])
    w_main = w
    w_tail = jnp.concatenate([jnp.pad(w[:, main:main + IDX_DIM], ((0, 0), (0, LANES - IDX_DIM))),
                              jnp.pad(w[:, main + IDX_DIM:], ((0, 0), (0, LANES - IDX_HEADS)))], axis=1).astype(BF16)
    lnw = jnp.pad(idx_k_norm_w[0], (0, LANES - IDX_DIM)).reshape(1, LANES)
    lnb = jnp.pad(idx_k_norm_b[0], (0, LANES - IDX_DIM)).reshape(1, LANES)

    x2 = x.reshape(B * T, D)
    hq, lf, hk, hv, hg, aq, ak, avt, iq, ik, iwt, nrm = _in_proj(
        x2, mod3, norm_pre_mix, w_main, w_tail, hgrn_lower_bound, lnw, lnb, tab, T, tm, tm)

    o_hg = _hgrn(hq, lf, hk, hv, hg, hgrn_out_norm, B, T, tm)
    o_att = _dsa(iq, iwt, ik, aq, ak, avt, nrm, B, T, QUERY_BLOCK, KEY_TILE)

    gains = jnp.concatenate([norm_post_mix, norm_pre_ffn, norm_post_ffn], axis=0)
    out = _out_ffn(o_hg, o_att, x2, mod3, gains, w_out.reshape(w_out.shape[1:]).astype(BF16),
                   w_ffn_in.reshape(w_ffn_in.shape[1:]).astype(BF16),
                   w_ffn_out.reshape(w_ffn_out.shape[1:]).astype(BF16), T, tm)
    return out.reshape(B, T, D)
```

```python
import functools

import numpy as np
import jax
import jax.numpy as jnp
from jax import lax
from jax.experimental import pallas as pl
from jax.experimental.pallas import tpu as pltpu

F32 = jnp.float32
BF16 = jnp.bfloat16

HG_HEADS = 4
HG_DIM = 128
ATT_HEADS = 4
ATT_DIM = 128
IDX_HEADS = 8
IDX_DIM = 64
TOPK_MAX = 256
ROPE_THETA = 500000.0
ROPE_FRACTION = 4
N_MOD = 6
EPS = 1e-6

GROUP = 512
LANES = 128
VMEM_LIMIT = 56 * 1024 * 1024

HG_CHUNK = 256
HG_LEVELS = (128, 64, 32, 16, 8, 4, 2, 1)

NEG_BIG = -1e30
DENOM_FLOOR = 2.0 ** -64
NORM_SLACK = 1.01
COARSE_PASSES = 10
FINE_PASSES = 10
COUNT_SLAB = 64
TILE_GROUP = 4
ATT_GROUP = 2

ROW_TILE = 512
KEY_TILE = 512
QUERY_BLOCK = 512
SCORE_GROUP = 1
FFN_CHUNK = 256
PROJ_GROUPS = 8
Q_SCALE = ATT_DIM ** -0.5 * 1.4426950408889634
V_ROWS = ATT_DIM + 16


def _cparams(sem):
    return pltpu.CompilerParams(dimension_semantics=sem, vmem_limit_bytes=VMEM_LIMIT)


def _resident(shape, index_map):
    return pl.BlockSpec(shape, index_map, pipeline_mode=pl.Buffered(1))


def _split_bf16(a):
    hi = a.astype(BF16)
    lo = (a - hi.astype(F32)).astype(BF16)
    return hi, lo


def _dot(a, b):
    return jnp.dot(a, b, preferred_element_type=F32)


def _dot_nt(a, b):
    return lax.dot_general(a, b, (((1,), (1,)), ((), ())), preferred_element_type=F32)


def _silu(a):
    return a * jax.nn.sigmoid(a)


def _adaln_kernel(c_ref, w_ref, b_ref, o_ref):
    a = _silu(c_ref[...])
    a_hi, a_lo = _split_bf16(a)
    w_hi, w_lo = _split_bf16(w_ref[0])
    acc = _dot(a_hi, w_hi) + (_dot(a_hi, w_lo) + _dot(a_lo, w_hi))
    o_ref[...] = acc + b_ref[...]


def _adaln(c, w, b):
    B, D = c.shape
    N = w.shape[2]
    return pl.pallas_call(
        _adaln_kernel,
        grid=(N // D,),
        in_specs=[pl.BlockSpec((B, D), lambda j: (0, 0)),
                  pl.BlockSpec((1, D, D), lambda j: (0, 0, j)),
                  pl.BlockSpec((1, D), lambda j: (0, j))],
        out_specs=pl.BlockSpec((B, D), lambda j: (0, j)),
        out_shape=jax.ShapeDtypeStruct((B, N), F32),
        compiler_params=_cparams(("arbitrary",)),
        name="adaln",
    )(c, w, b)


def _rope_table_kernel(pos_ref, fa_ref, fb_ref, o_ref):
    pos = pos_ref[0].astype(F32)
    ang_a = fa_ref[...] * pos
    ang_b = fb_ref[...] * pos
    ha = fa_ref.shape[0]
    hb = fb_ref.shape[0]
    o_ref[0, 0:ha, :] = jnp.cos(ang_a)
    o_ref[0, ha:2 * ha, :] = jnp.sin(ang_a)
    o_ref[0, 2 * ha:2 * ha + hb, :] = jnp.cos(ang_b)
    o_ref[0, 2 * ha + hb:2 * ha + 2 * hb, :] = jnp.sin(ang_b)
    o_ref[0, 2 * ha + 2 * hb:, :] = jnp.zeros((LANES - 2 * ha - 2 * hb, pos.shape[1]), F32)


ROPE_HALF_A = ATT_DIM // ROPE_FRACTION // 2
ROPE_HALF_B = IDX_DIM // ROPE_FRACTION // 2


def _rope_table(positions):
    B, T = positions.shape
    ha, hb = ROPE_HALF_A, ROPE_HALF_B
    fa = (ROPE_THETA ** (-jnp.arange(ha, dtype=F32) / ha)).reshape(ha, 1)
    fb = (ROPE_THETA ** (-jnp.arange(hb, dtype=F32) / hb)).reshape(hb, 1)
    return pl.pallas_call(
        _rope_table_kernel,
        grid=(B,),
        in_specs=[pl.BlockSpec((1, 1, T), lambda b: (b, 0, 0)),
                  pl.BlockSpec((ha, 1), lambda b: (0, 0)),
                  pl.BlockSpec((hb, 1), lambda b: (0, 0))],
        out_specs=pl.BlockSpec((1, LANES, T), lambda b: (b, 0, 0)),
        out_shape=jax.ShapeDtypeStruct((B, LANES, T), F32),
        compiler_params=_cparams(("arbitrary",)),
        name="rope_table",
    )(positions.reshape(B, 1, T), fa, fb)


def _rope_patterns(tab):
    ha, hb = ROPE_HALF_A, ROPE_HALF_B
    lane = lax.broadcasted_iota(jnp.int32, tab.shape, 1)
    shifted = lambda sh: pltpu.roll(tab, sh % LANES, axis=1)
    cos_a = jnp.where(lane < ha, tab, jnp.where(lane < 2 * ha, shifted(ha), 1.0))
    sin_a = jnp.where(lane < ha, -shifted(-ha), jnp.where(lane < 2 * ha, tab, 0.0))
    l64 = lane & (IDX_DIM - 1)
    first = lane < IDX_DIM
    cb0, sb0 = 2 * ha, 2 * ha + hb
    cos_b = jnp.where(l64 < hb, jnp.where(first, shifted(-cb0), shifted(IDX_DIM - cb0)),
                      jnp.where(l64 < 2 * hb, jnp.where(first, shifted(hb - cb0), shifted(IDX_DIM + hb - cb0)), 1.0))
    sin_b = jnp.where(l64 < hb, -jnp.where(first, shifted(-sb0), shifted(IDX_DIM - sb0)),
                      jnp.where(l64 < 2 * hb, jnp.where(first, shifted(hb - sb0), shifted(IDX_DIM + hb - sb0)), 0.0))
    return cos_a, sin_a, cos_b, sin_b


def _rope(xb, cos, sin, half, period):
    lane = lax.broadcasted_iota(jnp.int32, xb.shape, 1)
    fwd = pltpu.roll(xb, LANES - half, axis=1)
    bwd = pltpu.roll(xb, half, axis=1)
    partner = jnp.where((lane & (period - 1)) < half, fwd, bwd)
    return xb * cos + partner * sin


def _in_proj_kernel(x_ref, mod_ref, gain_ref, w_ref, wt_ref, lbp_ref, lnw_ref, lnb_ref, tab_ref,
                    hq_ref, lf_ref, hk_ref, hv_ref, hg_ref,
                    aq_ref, ak_ref, avt_ref, iq_ref, ik_ref, iwt_ref, nrm_ref):
    x = x_ref[...]
    ms = jnp.mean(x * x, axis=-1, keepdims=True)
    mod = mod_ref[0]
    h = x * lax.rsqrt(ms + EPS) * gain_ref[...]
    h = h * (1.0 + mod[1:2, :]) + mod[0:1, :]
    hb = h.astype(BF16)

    wide = {}

    def proj(g):
        base = g - g % PROJ_GROUPS
        if base not in wide:
            wide[base] = _dot(hb, w_ref[:, base * GROUP:(base + PROJ_GROUPS) * GROUP])
        return wide[base][:, (g - base) * GROUP:(g - base + 1) * GROUP]

    hq_ref[...] = (_silu(proj(0)) * (HG_DIM ** -0.5)).astype(BF16)
    a = lbp_ref[...]
    amax = jnp.max(a, axis=0, keepdims=True)
    e = jnp.exp(a - amax)
    lb = e[0:1, :] / jnp.sum(e, axis=0, keepdims=True)
    fr = proj(1)
    f = lb + (1.0 - lb) * jax.nn.sigmoid(fr)
    lf_ref[...] = jnp.log2(f).astype(BF16)
    hk_ref[...] = ((1.0 - lb) * jax.nn.sigmoid(-fr)).astype(BF16)
    hv_ref[...] = proj(2).astype(BF16)
    hg_ref[...] = _silu(proj(3)).astype(BF16)

    cosa, sina, cosb, sinb = _rope_patterns(tab_ref[0].T)
    half_a = ROPE_HALF_A
    q = proj(4)
    k = proj(5)
    lane_a = lax.broadcasted_iota(jnp.int32, cosa.shape, 1)
    norms = jnp.zeros(cosa.shape, F32)
    for hh in range(ATT_HEADS):
        sl = slice(hh * ATT_DIM, (hh + 1) * ATT_DIM)
        qh = (_rope(q[:, sl], cosa, sina, half_a, ATT_DIM) * Q_SCALE).astype(BF16)
        kh = _rope(k[:, sl], cosa, sina, half_a, ATT_DIM).astype(BF16)
        aq_ref[:, sl] = qh
        ak_ref[:, sl] = kh
        for slot, a in ((hh, qh), (ATT_HEADS + hh, kh)):
            norms = jnp.where(lane_a == slot, jnp.sqrt(_dot(a * a, jnp.ones((ATT_DIM, LANES), BF16))), norms)
    nrm_ref[...] = norms.T[0:2 * ATT_HEADS, :]
    v = proj(6)
    for hh in range(ATT_HEADS):
        r0 = hh * V_ROWS
        avt_ref[0, 0, r0:r0 + ATT_DIM, :] = v[:, hh * ATT_DIM:(hh + 1) * ATT_DIM].T.astype(BF16)
        avt_ref[0, 0, r0 + ATT_DIM:r0 + V_ROWS, :] = jnp.ones((V_ROWS - ATT_DIM, v.shape[0]), BF16)

    half_b = ROPE_HALF_B
    qi = proj(7)
    for cc in range(GROUP // LANES):
        qr = _rope(qi[:, cc * LANES:(cc + 1) * LANES], cosb, sinb, half_b, IDX_DIM).astype(BF16)
        for hh in range(LANES // IDX_DIM):
            iq_ref[cc * (LANES // IDX_DIM) + hh] = qr[:, hh * IDX_DIM:(hh + 1) * IDX_DIM]
    kw = _dot(hb, wt_ref[...])
    ki = kw[:, 0:LANES]
    lane = lax.broadcasted_iota(jnp.int32, ki.shape, 1)
    real = lane < IDX_DIM
    mu = jnp.sum(ki, axis=-1, keepdims=True) * (1.0 / IDX_DIM)
    d = jnp.where(real, ki - mu, 0.0)
    var = jnp.sum(d * d, axis=-1, keepdims=True) * (1.0 / IDX_DIM)
    kn = d * lax.rsqrt(var + EPS) * lnw_ref[...] + lnb_ref[...]
    kn = _rope(kn, cosb, sinb, half_b, IDX_DIM)
    ik_ref[...] = kn[:, 0:IDX_DIM].astype(BF16)
    iwt_ref[...] = kw[:, LANES:2 * LANES].T[0:IDX_HEADS, :] * (IDX_HEADS ** -0.5 * IDX_DIM ** -0.5)


def _in_proj(x2, mod3, gain, w_main, w_tail, lbp, lnw, lnb, tab, T, tm, kt):
    M, D = x2.shape
    nt = T // tm
    row = lambda i: (i, 0)
    const = lambda i: (0, 0)
    bf16o = jax.ShapeDtypeStruct((M, GROUP), BF16)
    grp_spec = pl.BlockSpec((tm, GROUP), row)
    return pl.pallas_call(
        _in_proj_kernel,
        grid=(M // tm,),
        in_specs=[pl.BlockSpec((tm, D), row),
                  pl.BlockSpec((1, N_MOD, D), lambda i: (i // nt, 0, 0)),
                  pl.BlockSpec((1, D), const),
                  _resident(w_main.shape, const),
                  _resident(w_tail.shape, const),
                  pl.BlockSpec(lbp.shape, const),
                  pl.BlockSpec((1, LANES), const),
                  pl.BlockSpec((1, LANES), const),
                  pl.BlockSpec((1, LANES, tm), lambda i: (i // nt, 0, i % nt))],
        out_specs=[grp_spec] * 7 + [pl.BlockSpec((1, 1, ATT_HEADS * V_ROWS, tm),
                                                 lambda i: (i // nt, (i % nt) // (kt // tm), 0, (i % nt) % (kt // tm))),
                                    pl.BlockSpec((IDX_HEADS, tm, IDX_DIM), lambda i: (0, i, 0)),
                                    pl.BlockSpec((tm, IDX_DIM), row),
                                    pl.BlockSpec((IDX_HEADS, tm), lambda i: (0, i)),
                                    pl.BlockSpec((2 * ATT_HEADS, tm), lambda i: (0, i))],
        out_shape=[bf16o] * 7 + [jax.ShapeDtypeStruct((M // T, T // kt, ATT_HEADS * V_ROWS, kt), BF16),
                                 jax.ShapeDtypeStruct((IDX_HEADS, M, IDX_DIM), BF16),
                                 jax.ShapeDtypeStruct((M, IDX_DIM), BF16),
                                 jax.ShapeDtypeStruct((IDX_HEADS, M), F32),
                                 jax.ShapeDtypeStruct((2 * ATT_HEADS, M), F32)],
        compiler_params=_cparams(("parallel",)),
        name="in_proj",
    )(x2, mod3, gain, w_main, w_tail, lbp, lnw, lnb, tab)


def _hgrn_consts():
    C = HG_CHUNK
    t = np.arange(C)
    tri = (t[None, :] <= t[:, None]).astype(np.float32)
    blocks = [tri]
    for m in HG_LEVELS:
        if m % 8:
            split = (t // (2 * m)) * (2 * m) + m - 1
            upper = ((t & m) != 0)[:, None]
            blocks.append(np.where(upper, tri - tri[split], tri[split] - tri))
    mall = np.concatenate(blocks, axis=0)
    x = t[:, None] ^ t[None, :]
    lvl = np.full((C, C), len(HG_LEVELS) + 1, np.int32)
    for li, m in enumerate(HG_LEVELS):
        lvl[(t[:, None] > t[None, :]) & (x >= m) & (x < 2 * m)] = li
    lvl[t[:, None] == t[None, :]] = len(HG_LEVELS)
    return jnp.asarray(mall, BF16), jnp.asarray(lvl)


def _hgrn_kernel(hq_ref, lf_ref, hk_ref, hv_ref, hg_ref, mall_ref, lvl_ref, onorm_ref, o_ref, st_ref):
    C = HG_CHUNK
    nl = len(HG_LEVELS)

    @pl.when(pl.program_id(1) == 0)
    def _():
        st_ref[...] = jnp.zeros_like(st_ref)

    lvl = lvl_ref[...]
    row = lax.broadcasted_iota(jnp.int32, (C, HG_DIM), 0)
    mall = mall_ref[...]

    for ci in range(hq_ref.shape[0] // C):
        rs = slice(ci * C, (ci + 1) * C)
        dheads = _dot(mall, lf_ref[rs, :])
        for h in range(HG_HEADS):
            cs = slice(h * HG_DIM, (h + 1) * HG_DIM)
            q, k, v = hq_ref[rs, cs], hk_ref[rs, cs], hv_ref[rs, cs]
            qf, kf = q.astype(F32), k.astype(F32)
            dall = dheads[:, cs]
            g = dall[0:C]

            a = jnp.zeros((C, C), F32)
            fine = 0
            for li, m in enumerate(HG_LEVELS):
                if m % 8 == 0:
                    qk = jnp.concatenate([(qf if (b & 1) else kf)[b * m:(b + 1) * m] for b in range(C // m)], axis=0)
                    parts = []
                    for b in range(C // (2 * m)):
                        lo_rows = slice(2 * b * m, (2 * b + 1) * m)
                        hi_rows = slice((2 * b + 1) * m, (2 * b + 2) * m)
                        ref = jnp.broadcast_to(g[lo_rows.stop - 1:lo_rows.stop, :], (m, HG_DIM))
                        parts += [ref - g[lo_rows], g[hi_rows] - ref]
                    dm = jnp.concatenate(parts, axis=0)
                else:
                    fine += 1
                    qk = jnp.where((row & m) != 0, qf, kf)
                    dm = dall[fine * C:(fine + 1) * C]
                xm = (qk * jnp.exp2(dm)).astype(BF16)
                a = jnp.where(lvl == li, _dot_nt(xm, xm), a)
            a = jnp.where(lvl == nl, _dot_nt(q, k), a)

            st = st_ref[h]
            o = _dot_nt((qf * jnp.exp2(g)).astype(BF16), st.astype(BF16)) + _dot(a.astype(BF16), v)
            g_last = g[C - 1:C, :]
            kd = (kf * jnp.exp2(g_last - g)).astype(BF16)
            st_ref[h] = jnp.exp2(g_last) * st + _dot(v.astype(F32).T.astype(BF16), kd)

            o = o * lax.rsqrt(jnp.mean(o * o, axis=-1, keepdims=True) + EPS)
            o_ref[rs, cs] = (o * onorm_ref[:, cs] * hg_ref[rs, cs]).astype(BF16)


def _hgrn(hq, lf, hk, hv, hg, onorm, B, T, ct):
    M = hq.shape[0]
    nct = T // ct
    mall, lvl = _hgrn_consts()
    blk = pl.BlockSpec((ct, GROUP), lambda b, c: (b * nct + c, 0))
    const = lambda b, c: (0, 0)
    return pl.pallas_call(
        _hgrn_kernel,
        grid=(B, nct),
        in_specs=[blk] * 5 + [pl.BlockSpec(mall.shape, const), pl.BlockSpec(lvl.shape, const),
                              pl.BlockSpec((1, GROUP), const)],
        out_specs=blk,
        out_shape=jax.ShapeDtypeStruct((M, GROUP), BF16),
        scratch_shapes=[pltpu.VMEM((HG_HEADS, HG_DIM, HG_DIM), F32)],
        compiler_params=_cparams(("parallel", "arbitrary")),
        name="hgrn2",
    )(hq, lf, hk, hv, hg, mall, lvl, onorm)


def _fold8(a, op, rows=8):
    chains = [None] * 4
    for r in range(a.shape[0] // rows):
        part = a[r * rows:(r + 1) * rows, :]
        c = r % len(chains)
        chains[c] = part if chains[c] is None else op(chains[c], part)
    return op(op(chains[0], chains[1]), op(chains[2], chains[3]))


def _floor_bf16(a):
    r = a.astype(BF16).astype(F32)
    below = (r - jnp.abs(r) * (5.0 / 1024.0)).astype(BF16)
    return jnp.where(r > a, below.astype(F32), r).astype(BF16)


def _dsa_kernel(iq_ref, iwt_ref, ik_ref, aq_ref, ak_ref, avt_ref, qnt_ref, knt_ref, o_ref,
                sc_ref, scb_ref, mm_ref, *head_refs, qb, kt, topk, seq):
    acc_refs, m_refs = head_refs[:ATT_HEADS], head_refs[ATT_HEADS:]
    i = pl.program_id(1)
    q0 = i * qb
    nfull = q0 // kt
    nkt = nfull + 1
    qpos = q0 + lax.broadcasted_iota(jnp.int32, (1, qb), 1)
    kk = jnp.minimum(qpos + 1, topk).astype(F32)

    mm_ref[0:8, :] = jnp.full((8, qb), jnp.inf, F32)
    mm_ref[8:16, :] = jnp.full((8, qb), -jnp.inf, F32)
    mm_ref[16:32, :] = jnp.zeros((16, qb), F32)

    def score_keys(j0, ntile, diag):
        nrow = ntile * kt
        ks = pl.multiple_of(j0 * kt, kt)
        rows = pl.ds(ks, nrow)
        ki = ik_ref[rows, :]
        s = None
        for h in range(IDX_HEADS):
            t = jnp.maximum(_dot_nt(ki, iq_ref[h]), 0.0) * iwt_ref[h:h + 1, :]
            s = t if s is None else s + t
        if diag:
            causal = ks + lax.broadcasted_iota(jnp.int32, (nrow, qb), 0) <= qpos
            s_hi = jnp.where(causal, s, -jnp.inf)
            s_lo = jnp.where(causal, s, jnp.inf)
        else:
            s_hi = s_lo = s
        sc_ref[rows, :] = s_hi
        fb = _floor_bf16(s_hi)
        scb_ref[rows, :] = fb
        fb_lo = _floor_bf16(s_lo) if diag else fb
        halves = lambda a: (a[0:8, :], a[8:16, :])
        lo_a, lo_b = halves(_fold8(fb_lo, jnp.minimum, rows=16).astype(F32))
        hi_a, hi_b = halves(_fold8(fb, jnp.maximum, rows=16).astype(F32))
        ge_a, ge_b = halves(_fold8(jnp.where(fb >= 0, jnp.ones((), BF16), jnp.zeros((), BF16)), jnp.add,
                                   rows=16).astype(F32))
        mm_ref[0:8, :] = jnp.minimum(mm_ref[0:8, :], jnp.minimum(lo_a, lo_b))
        mm_ref[8:16, :] = jnp.maximum(mm_ref[8:16, :], jnp.maximum(hi_a, hi_b))
        mm_ref[16:24, :] += ge_a + ge_b
        mm_ref[24:32, :] += _fold8(jnp.where(s_hi > 0.0, 1.0, 0.0), jnp.add)

    def score_group(jg, c):
        score_keys(SCORE_GROUP * jg, SCORE_GROUP, False)
        return c

    lax.fori_loop(0, nfull // SCORE_GROUP, score_group, 0)
    done = (nfull // SCORE_GROUP) * SCORE_GROUP
    size = SCORE_GROUP // 2
    while size:
        pl.when(((nfull - done) & size) != 0)(functools.partial(score_keys, done, size, False))
        done = done + ((nfull - done) & size)
        size //= 2
    score_keys(nfull, 1, True)
    mn = jnp.min(mm_ref[0:8, :], axis=0, keepdims=True)
    mx = jnp.max(mm_ref[8:16, :], axis=0, keepdims=True)
    c_nonneg = jnp.sum(mm_ref[16:24, :], axis=0, keepdims=True)
    c_pos = jnp.sum(mm_ref[24:32, :], axis=0, keepdims=True)

    def tile_hits(ref, j, hit_fn, slab, rows, cols=slice(None)):
        acc = None
        for r in range(kt // slab):
            ks = pl.multiple_of(j * kt + r * slab, slab)
            part = _fold8(hit_fn(ref[pl.ds(ks, slab), cols]), jnp.add, rows=rows)
            acc = part if acc is None else acc + part
        return acc

    def over_tiles(body, init):
        def group(jg, acc):
            for u in range(TILE_GROUP):
                acc = body(jg * TILE_GROUP + u, acc)
            return acc
        ngroups = nkt // TILE_GROUP
        return lax.fori_loop(ngroups * TILE_GROUP, nkt, body, lax.fori_loop(0, ngroups, group, init))

    def count_ge(th):
        def body(j, acc):
            return acc + tile_hits(sc_ref, j, lambda s: jnp.where(s >= th, 1.0, 0.0), COUNT_SLAB, 8)
        return jnp.sum(over_tiles(body, jnp.zeros((8, qb), F32)), axis=0, keepdims=True)

    above = c_pos >= kk
    below = c_nonneg < kk
    live = above | below
    zero = jnp.zeros((1, qb), F32)
    lo0 = jnp.where(below, mn, zero)
    cnt_lo0 = jnp.where(below, (qpos + 1).astype(F32), c_nonneg)
    hi0 = jnp.where(above, mx + jnp.abs(mx) * 2.0 ** -6 + 1e-30, zero)
    cnt_hi0 = jnp.where(above, zero, jnp.where(below, c_nonneg, c_pos))

    def midpoint(lo, hi):
        return lo + 0.5 * (hi - lo)

    assert kt // 16 <= 256

    def count_ge_bf16(th):
        def body(j, acc):
            hit = lambda s: jnp.where(s >= th, jnp.ones((), BF16), jnp.zeros((), BF16))
            part = tile_hits(scb_ref, j, hit, 2 * COUNT_SLAB, 16).astype(F32)
            return acc + part[0:8, :] + part[8:16, :]
        return jnp.sum(over_tiles(body, jnp.zeros((8, qb), F32)), axis=0, keepdims=True)

    def coarse_body(_, c):
        lo, hi, cnt_lo, cnt_hi = c
        th = midpoint(lo, hi).astype(BF16)
        mid = th.astype(F32)
        inside = live & (mid > lo) & (mid < hi)
        cnt = count_ge_bf16(th)
        up = inside & (cnt >= kk)
        dn = inside & (cnt < kk)
        return (jnp.where(up, mid, lo), jnp.where(dn, mid, hi),
                jnp.where(up, cnt, cnt_lo), jnp.where(dn, cnt, cnt_hi))

    lo1, hi1, cnt_lo1, cnt_hi1 = lax.fori_loop(0, COARSE_PASSES, coarse_body, (lo0, hi0, cnt_lo0, cnt_hi0))

    def fine_pass(c):
        lo, hi, cnt_lo, cnt_hi = c
        mid = midpoint(lo, hi)
        cnt = count_ge(mid)
        up = live & (cnt >= kk)
        dn = live & (cnt < kk)
        return (jnp.where(up, mid, lo), jnp.where(dn, mid, hi), jnp.where(up, cnt, cnt_lo), jnp.where(dn, cnt, cnt_hi))

    c2 = lax.fori_loop(0, FINE_PASSES, lambda _, c: fine_pass(c), (lo1, hi1, cnt_lo1, cnt_hi1))

    def finish_chunk(cs):
        kk_c, live_c = kk[:, cs], live[:, cs]

        def cond(c):
            it, (lo_c, hi_c, cnt_c, _) = c
            mid = midpoint(lo_c, hi_c)
            open_ = live_c & (cnt_c != kk_c) & (mid > lo_c) & (mid < hi_c)
            return (it < 320) & (jnp.max(open_.astype(jnp.int32)) > 0)

        def body(c):
            it, (lo_c, hi_c, cnt_lo_c, cnt_hi_c) = c
            mid = midpoint(lo_c, hi_c)
            tile = lambda j, acc: acc + tile_hits(sc_ref, j, lambda s: jnp.where(s >= mid, 1.0, 0.0), COUNT_SLAB, 8, cs)
            cnt = jnp.sum(over_tiles(tile, jnp.zeros((8, LANES), F32)), axis=0, keepdims=True)
            up = live_c & (cnt >= kk_c)
            dn = live_c & (cnt < kk_c)
            return it + 1, (jnp.where(up, mid, lo_c), jnp.where(dn, mid, hi_c),
                            jnp.where(up, cnt, cnt_lo_c), jnp.where(dn, cnt, cnt_hi_c))

        return lax.while_loop(cond, body, (jnp.int32(0), tuple(a[:, cs] for a in c2)))[1]

    chunks = [slice(c * LANES, (c + 1) * LANES) for c in range(qb // LANES)]
    lo, hi, cnt_lo, cnt_hi = (jnp.concatenate(parts, axis=1) for parts in zip(*[finish_chunk(cs) for cs in chunks]))

    tied = cnt_lo != kk
    need = kk - cnt_hi

    def resolve_ties(cs):
        lo_c, need_c, tied_c = lo[:, cs], need[:, cs], tied[:, cs]

        def tile_rows(j):
            return pl.ds(pl.multiple_of(j * kt, kt), kt)

        def find_cross(j, c):
            seen, jstar, before = c
            n_eq = jnp.sum(_fold8(jnp.where(sc_ref[tile_rows(j), cs] == lo_c, 1.0, 0.0), jnp.add), axis=0, keepdims=True)
            here = tied_c & (seen < need_c) & (seen + n_eq >= need_c)
            return seen + n_eq, jnp.where(here, j, jstar), jnp.where(here, seen, before)

        zeros = jnp.zeros((1, LANES), F32)
        _, jstar, before = lax.fori_loop(0, nkt, find_cross, (zeros, jnp.full((1, LANES), seq, jnp.int32), zeros))
        tri = jnp.where(lax.broadcasted_iota(jnp.int32, (kt, kt), 0) >= lax.broadcasted_iota(jnp.int32, (kt, kt), 1),
                        1.0, 0.0).astype(BF16)

        def strike(j, c):
            past = tied_c & (jstar < j)
            cross = tied_c & (jstar == j)

            @pl.when(jnp.max((past | cross).astype(jnp.int32)) > 0)
            def _():
                s = sc_ref[tile_rows(j), cs]
                eq = s == lo_c
                rank = before + _dot(tri, jnp.where(eq, 1.0, 0.0).astype(BF16))
                sc_ref[tile_rows(j), cs] = jnp.where(eq & (past | (cross & (rank > need_c))), -jnp.inf, s)
            return c

        lax.fori_loop(0, nkt, strike, 0)

    for cs in chunks:
        pl.when(jnp.max(tied[:, cs].astype(jnp.int32)) > 0)(functools.partial(resolve_ties, cs))

    def logits(j, h, bias):
        ks = pl.multiple_of(j * kt, kt)
        sl = slice(h * ATT_DIM, (h + 1) * ATT_DIM)
        return _dot_nt(ak_ref[pl.ds(ks, kt), sl], aq_ref[:, sl]) + bias

    def select_bias(j):
        ks = pl.multiple_of(j * kt, kt)
        return jnp.where(sc_ref[pl.ds(ks, kt), :] >= lo, 0.0, NEG_BIG)

    def values(j, h):
        return avt_ref[0, j, h * V_ROWS:(h + 1) * V_ROWS, :]

    kmax = jnp.max(knt_ref[...], axis=1, keepdims=True)
    shift = [qnt_ref[h:h + 1, :] * kmax[ATT_HEADS + h:ATT_HEADS + h + 1, :] * NORM_SLACK for h in range(ATT_HEADS)]
    for h in range(ATT_HEADS):
        acc_refs[h][...] = jnp.zeros_like(acc_refs[h])

    def attn_keys_fast(j0, ntile):
        rows = pl.ds(pl.multiple_of(j0 * kt, kt), ntile * kt)
        keep = jnp.where(sc_ref[rows, :] >= lo, 1.0, 0.0).astype(BF16)
        for h in range(ATT_HEADS):
            sl = slice(h * ATT_DIM, (h + 1) * ATT_DIM)
            p = jnp.exp2(_dot_nt(ak_ref[rows, sl], aq_ref[:, sl]) - shift[h]).astype(BF16) * keep
            vals = [values(j0 + u, h) for u in range(ntile)]
            acc_refs[h][...] += _dot(vals[0] if ntile == 1 else jnp.concatenate(vals, axis=1), p)

    def attn_group(jg, c):
        attn_keys_fast(ATT_GROUP * jg, ATT_GROUP)
        return c

    lax.fori_loop(0, nkt // ATT_GROUP, attn_group, 0)
    done = (nkt // ATT_GROUP) * ATT_GROUP
    size = ATT_GROUP // 2
    while size:
        pl.when(((nkt - done) & size) != 0)(functools.partial(attn_keys_fast, done, size))
        done = done + ((nkt - done) & size)
        size //= 2
    denom_min = functools.reduce(jnp.minimum, [acc_refs[h][ATT_DIM:ATT_DIM + 1, :] for h in range(ATT_HEADS)])

    @pl.when(jnp.min(denom_min) < DENOM_FLOOR)
    def _():
        for h in range(ATT_HEADS):
            acc_refs[h][...] = jnp.zeros_like(acc_refs[h])
            m_refs[h][...] = jnp.full(m_refs[h].shape, NEG_BIG, F32)

        def attn_tile(j, c):
            bias = select_bias(j)
            for h in range(ATT_HEADS):
                acc_h, m_h = acc_refs[h], m_refs[h]
                lg = logits(j, h, bias)
                m_old = m_h[0:1, :]
                m_new = jnp.maximum(m_old, jnp.max(_fold8(lg, jnp.maximum), axis=0, keepdims=True))
                p = jnp.exp2(lg - m_new).astype(BF16)
                acc_h[...] = jnp.exp2(m_old - m_new) * acc_h[...] + _dot(values(j, h), p)
                m_h[0:1, :] = m_new
            return c

        lax.fori_loop(0, nkt, attn_tile, 0)

    for h in range(ATT_HEADS):
        o_t = acc_refs[h][0:ATT_DIM, :] / acc_refs[h][ATT_DIM:ATT_DIM + 1, :]
        o_ref[:, h * ATT_DIM:(h + 1) * ATT_DIM] = o_t.T.astype(BF16)


def _dsa(iq, iwt, ik, aq, ak, avt, nrm, B, T, qb, kt):
    M = ak.shape[0]
    nq = T // qb
    assert kt % qb == 0 and T % kt == 0 and avt.shape[1:] == (T // kt, ATT_HEADS * V_ROWS, kt)
    topk = min(TOPK_MAX, T // 4)
    qrow = lambda b, i: (b * nq + i, 0)
    brow = lambda b, i: (b, 0)
    kern = functools.partial(_dsa_kernel, qb=qb, kt=kt, topk=topk, seq=T)
    return pl.pallas_call(
        kern,
        grid=(B, nq),
        in_specs=[pl.BlockSpec((IDX_HEADS, qb, IDX_DIM), lambda b, i: (0, b * nq + i, 0)),
                  pl.BlockSpec((IDX_HEADS, qb), lambda b, i: (0, b * nq + i)),
                  _resident((T, IDX_DIM), brow),
                  pl.BlockSpec((qb, GROUP), qrow),
                  _resident((T, GROUP), brow),
                  _resident((1,) + avt.shape[1:], lambda b, i: (b, 0, 0, 0)),
                  pl.BlockSpec((2 * ATT_HEADS, qb), lambda b, i: (0, b * nq + i)),
                  _resident((2 * ATT_HEADS, T), lambda b, i: (0, b))],
        out_specs=pl.BlockSpec((qb, GROUP), qrow),
        out_shape=jax.ShapeDtypeStruct((M, GROUP), BF16),
        scratch_shapes=[pltpu.VMEM((T, qb), F32),
                        pltpu.VMEM((T, qb), BF16),
                        pltpu.VMEM((32, qb), F32),
                        ] + [pltpu.VMEM((V_ROWS, qb), F32)] * ATT_HEADS
                        + [pltpu.VMEM((8, qb), F32)] * ATT_HEADS,
        compiler_params=_cparams(("parallel", "arbitrary")),
        name="dsa",
    )(iq, iwt, ik, aq, ak, avt, nrm, nrm)


def _rms(a, gain):
    return a * lax.rsqrt(jnp.mean(a * a, axis=-1, keepdims=True) + EPS) * gain


def _out_ffn_kernel(ohg_ref, oatt_ref, x_ref, mod_ref, gains_ref, wo_ref, wi_ref, wf_ref, o_ref, acc_ref,
                    *, d_ff, fc):
    mod = mod_ref[0]
    gains = gains_ref[...]
    y = _dot(jnp.concatenate([ohg_ref[...], oatt_ref[...]], axis=1), wo_ref[...])
    x1 = x_ref[...] + mod[2:3, :] * _rms(y, gains[0:1, :])
    h2 = (_rms(x1, gains[1:2, :]) * (1.0 + mod[4:5, :]) + mod[3:4, :]).astype(BF16)
    for c in range(d_ff // fc):
        gate = _dot(h2, wi_ref[:, c * fc:(c + 1) * fc])
        up = _dot(h2, wi_ref[:, d_ff + c * fc:d_ff + (c + 1) * fc])
        part = _dot((_silu(gate) * up).astype(BF16), wf_ref[c * fc:(c + 1) * fc, :])
        if c == 0:
            acc_ref[...] = part
        else:
            acc_ref[...] += part
    o_ref[...] = x1 + mod[5:6, :] * _rms(acc_ref[...], gains[2:3, :])


def _out_ffn(ohg, oatt, x2, mod3, gains, wo, wi, wf, T, tm):
    M, D = x2.shape
    nt = T // tm
    d_ff = wf.shape[0]
    row = lambda i: (i, 0)
    const = lambda i: (0, 0)
    assert d_ff % FFN_CHUNK == 0
    kern = functools.partial(_out_ffn_kernel, d_ff=d_ff, fc=FFN_CHUNK)
    return pl.pallas_call(
        kern,
        grid=(M // tm,),
        in_specs=[pl.BlockSpec((tm, GROUP), row),
                  pl.BlockSpec((tm, GROUP), row),
                  pl.BlockSpec((tm, D), row),
                  pl.BlockSpec((1, N_MOD, D), lambda i: (i // nt, 0, 0)),
                  pl.BlockSpec((3, D), const),
                  _resident(wo.shape, const),
                  _resident(wi.shape, const),
                  _resident(wf.shape, const)],
        out_specs=pl.BlockSpec((tm, D), row),
        out_shape=jax.ShapeDtypeStruct((M, D), F32),
        scratch_shapes=[pltpu.VMEM((tm, D), F32)],
        compiler_params=_cparams(("parallel",)),
        name="out_ffn",
    )(ohg, oatt, x2, mod3, gains, wo, wi, wf)


def kernel(x, c, positions, w_ada, b_ada, norm_pre_mix, norm_post_mix, norm_pre_ffn, norm_post_ffn, w_in,
           hgrn_lower_bound, hgrn_out_norm, idx_k_norm_w, idx_k_norm_b, w_out, w_ffn_in, w_ffn_out):
    B, T, D = x.shape
    depth = w_ada.shape[0]
    assert depth == 1 and hgrn_lower_bound.shape[0] == 2
    tm = ROW_TILE
    assert T % tm == 0 and T % KEY_TILE == 0 and T % QUERY_BLOCK == 0 and D % LANES == 0 and tm == KEY_TILE

    mod3 = _adaln(c, w_ada, b_ada).reshape(B, N_MOD, D)
    tab = _rope_table(positions)

    main = 8 * GROUP
    w = w_in.reshape(w_in.shape[1:])
    w_main = w[:, :main].astype(BF16)
    w_tail = jnp.concatenate([jnp.pad(w[:, main:main + IDX_DIM], ((0, 0), (0, LANES - IDX_DIM))),
                              jnp.pad(w[:, main + IDX_DIM:], ((0, 0), (0, LANES - IDX_HEADS)))], axis=1).astype(BF16)
    lnw = jnp.pad(idx_k_norm_w[0], (0, LANES - IDX_DIM)).reshape(1, LANES)
    lnb = jnp.pad(idx_k_norm_b[0], (0, LANES - IDX_DIM)).reshape(1, LANES)

    x2 = x.reshape(B * T, D)
    hq, lf, hk, hv, hg, aq, ak, avt, iq, ik, iwt, nrm = _in_proj(
        x2, mod3, norm_pre_mix, w_main, w_tail, hgrn_lower_bound, lnw, lnb, tab, T, tm, tm)

    o_hg = _hgrn(hq, lf, hk, hv, hg, hgrn_out_norm, B, T, tm)
    o_att = _dsa(iq, iwt, ik, aq, ak, avt, nrm, B, T, QUERY_BLOCK, KEY_TILE)

    gains = jnp.concatenate([norm_post_mix, norm_pre_ffn, norm_post_ffn], axis=0)
    out = _out_ffn(o_hg, o_att, x2, mod3, gains, w_out.reshape(w_out.shape[1:]).astype(BF16),
                   w_ffn_in.reshape(w_ffn_in.shape[1:]).astype(BF16),
                   w_ffn_out.reshape(w_ffn_out.shape[1:]).astype(BF16), T, tm)
    return out.reshape(B, T, D)
```

```python
import functools

import numpy as np
import jax
import jax.numpy as jnp
from jax import lax
from jax.experimental import pallas as pl
from jax.experimental.pallas import tpu as pltpu

F32 = jnp.float32
BF16 = jnp.bfloat16

HG_HEADS = 4
HG_DIM = 128
ATT_HEADS = 4
ATT_DIM = 128
IDX_HEADS = 8
IDX_DIM = 64
TOPK_MAX = 256
ROPE_THETA = 500000.0
ROPE_FRACTION = 4
N_MOD = 6
EPS = 1e-6

GROUP = 512
LANES = 128
VMEM_LIMIT = 56 * 1024 * 1024

HG_CHUNK = 256
HG_LEVELS = (128, 64, 32, 16, 8, 4, 2, 1)

NEG_BIG = -1e30
DENOM_FLOOR = 2.0 ** -64
NORM_SLACK = 1.01
COARSE_PASSES = 10
FINE_PASSES = 10
COUNT_SLAB = 64
TILE_GROUP = 4
ATT_GROUP = 2

ROW_TILE = 512
KEY_TILE = 512
QUERY_BLOCK = 512
SCORE_GROUP = 1
FFN_CHUNK = 256
PROJ_GROUPS = 8
Q_SCALE = ATT_DIM ** -0.5 * 1.4426950408889634
V_ROWS = ATT_DIM + 16


def _cparams(sem):
    return pltpu.CompilerParams(dimension_semantics=sem, vmem_limit_bytes=VMEM_LIMIT)


def _resident(shape, index_map):
    return pl.BlockSpec(shape, index_map, pipeline_mode=pl.Buffered(1))


def _split_bf16(a):
    hi = a.astype(BF16)
    lo = (a - hi.astype(F32)).astype(BF16)
    return hi, lo


def _dot(a, b):
    return jnp.dot(a, b, preferred_element_type=F32)


def _dot_nt(a, b):
    return lax.dot_general(a, b, (((1,), (1,)), ((), ())), preferred_element_type=F32)


def _silu(a):
    return a * jax.nn.sigmoid(a)


def _adaln_kernel(c_ref, w_ref, b_ref, o_ref):
    a = _silu(c_ref[...])
    a_hi, a_lo = _split_bf16(a)
    w_hi, w_lo = _split_bf16(w_ref[0])
    acc = _dot(a_hi, w_hi) + (_dot(a_hi, w_lo) + _dot(a_lo, w_hi))
    o_ref[...] = acc + b_ref[...]


def _adaln(c, w, b):
    B, D = c.shape
    N = w.shape[2]
    return pl.pallas_call(
        _adaln_kernel,
        grid=(N // D,),
        in_specs=[pl.BlockSpec((B, D), lambda j: (0, 0)),
                  pl.BlockSpec((1, D, D), lambda j: (0, 0, j)),
                  pl.BlockSpec((1, D), lambda j: (0, j))],
        out_specs=pl.BlockSpec((B, D), lambda j: (0, j)),
        out_shape=jax.ShapeDtypeStruct((B, N), F32),
        compiler_params=_cparams(("arbitrary",)),
        name="adaln",
    )(c, w, b)


def _rope_table_kernel(pos_ref, fa_ref, fb_ref, o_ref):
    pos = pos_ref[0].astype(F32)
    ang_a = fa_ref[...] * pos
    ang_b = fb_ref[...] * pos
    ha = fa_ref.shape[0]
    hb = fb_ref.shape[0]
    o_ref[0, 0:ha, :] = jnp.cos(ang_a)
    o_ref[0, ha:2 * ha, :] = jnp.sin(ang_a)
    o_ref[0, 2 * ha:2 * ha + hb, :] = jnp.cos(ang_b)
    o_ref[0, 2 * ha + hb:2 * ha + 2 * hb, :] = jnp.sin(ang_b)
    o_ref[0, 2 * ha + 2 * hb:, :] = jnp.zeros((LANES - 2 * ha - 2 * hb, pos.shape[1]), F32)


ROPE_HALF_A = ATT_DIM // ROPE_FRACTION // 2
ROPE_HALF_B = IDX_DIM // ROPE_FRACTION // 2


def _rope_table(positions):
    B, T = positions.shape
    ha, hb = ROPE_HALF_A, ROPE_HALF_B
    fa = (ROPE_THETA ** (-jnp.arange(ha, dtype=F32) / ha)).reshape(ha, 1)
    fb = (ROPE_THETA ** (-jnp.arange(hb, dtype=F32) / hb)).reshape(hb, 1)
    return pl.pallas_call(
        _rope_table_kernel,
        grid=(B,),
        in_specs=[pl.BlockSpec((1, 1, T), lambda b: (b, 0, 0)),
                  pl.BlockSpec((ha, 1), lambda b: (0, 0)),
                  pl.BlockSpec((hb, 1), lambda b: (0, 0))],
        out_specs=pl.BlockSpec((1, LANES, T), lambda b: (b, 0, 0)),
        out_shape=jax.ShapeDtypeStruct((B, LANES, T), F32),
        compiler_params=_cparams(("arbitrary",)),
        name="rope_table",
    )(positions.reshape(B, 1, T), fa, fb)


def _rope_patterns(tab):
    ha, hb = ROPE_HALF_A, ROPE_HALF_B
    lane = lax.broadcasted_iota(jnp.int32, tab.shape, 1)
    shifted = lambda sh: pltpu.roll(tab, sh % LANES, axis=1)
    cos_a = jnp.where(lane < ha, tab, jnp.where(lane < 2 * ha, shifted(ha), 1.0))
    sin_a = jnp.where(lane < ha, -shifted(-ha), jnp.where(lane < 2 * ha, tab, 0.0))
    l64 = lane & (IDX_DIM - 1)
    first = lane < IDX_DIM
    cb0, sb0 = 2 * ha, 2 * ha + hb
    cos_b = jnp.where(l64 < hb, jnp.where(first, shifted(-cb0), shifted(IDX_DIM - cb0)),
                      jnp.where(l64 < 2 * hb, jnp.where(first, shifted(hb - cb0), shifted(IDX_DIM + hb - cb0)), 1.0))
    sin_b = jnp.where(l64 < hb, -jnp.where(first, shifted(-sb0), shifted(IDX_DIM - sb0)),
                      jnp.where(l64 < 2 * hb, jnp.where(first, shifted(hb - sb0), shifted(IDX_DIM + hb - sb0)), 0.0))
    return cos_a, sin_a, cos_b, sin_b


def _rope(xb, cos, sin, half, period):
    lane = lax.broadcasted_iota(jnp.int32, xb.shape, 1)
    fwd = pltpu.roll(xb, LANES - half, axis=1)
    bwd = pltpu.roll(xb, half, axis=1)
    partner = jnp.where((lane & (period - 1)) < half, fwd, bwd)
    return xb * cos + partner * sin


def _in_proj_kernel(x_ref, mod_ref, gain_ref, w_ref, wt_ref, lbp_ref, lnw_ref, lnb_ref, tab_ref,
                    hq_ref, lf_ref, hk_ref, hv_ref, hg_ref,
                    aq_ref, ak_ref, avt_ref, iq_ref, ik_ref, iwt_ref, nrm_ref):
    x = x_ref[...]
    ms = jnp.mean(x * x, axis=-1, keepdims=True)
    mod = mod_ref[0]
    h = x * lax.rsqrt(ms + EPS) * gain_ref[...]
    h = h * (1.0 + mod[1:2, :]) + mod[0:1, :]
    hb = h.astype(BF16)

    wide = {}

    def proj(g):
        base = g - g % PROJ_GROUPS
        if base not in wide:
            wide[base] = _dot(hb, w_ref[:, base * GROUP:(base + PROJ_GROUPS) * GROUP])
        return wide[base][:, (g - base) * GROUP:(g - base + 1) * GROUP]

    hq_ref[...] = (_silu(proj(0)) * (HG_DIM ** -0.5)).astype(BF16)
    a = lbp_ref[...]
    amax = jnp.max(a, axis=0, keepdims=True)
    e = jnp.exp(a - amax)
    lb = e[0:1, :] / jnp.sum(e, axis=0, keepdims=True)
    fr = proj(1)
    f = lb + (1.0 - lb) * jax.nn.sigmoid(fr)
    lf_ref[...] = jnp.log2(f).astype(BF16)
    hk_ref[...] = ((1.0 - lb) * jax.nn.sigmoid(-fr)).astype(BF16)
    hv_ref[...] = proj(2).astype(BF16)
    hg_ref[...] = _silu(proj(3)).astype(BF16)

    cosa, sina, cosb, sinb = _rope_patterns(tab_ref[0].T)
    half_a = ROPE_HALF_A
    q = proj(4)
    k = proj(5)
    lane_a = lax.broadcasted_iota(jnp.int32, cosa.shape, 1)
    norms = jnp.zeros(cosa.shape, F32)
    for hh in range(ATT_HEADS):
        sl = slice(hh * ATT_DIM, (hh + 1) * ATT_DIM)
        qh = (_rope(q[:, sl], cosa, sina, half_a, ATT_DIM) * Q_SCALE).astype(BF16)
        kh = _rope(k[:, sl], cosa, sina, half_a, ATT_DIM).astype(BF16)
        aq_ref[:, sl] = qh
        ak_ref[:, sl] = kh
        for slot, a in ((hh, qh), (ATT_HEADS + hh, kh)):
            norms = jnp.where(lane_a == slot, jnp.sqrt(_dot(a * a, jnp.ones((ATT_DIM, LANES), BF16))), norms)
    nrm_ref[...] = norms.T[0:2 * ATT_HEADS, :]
    v = proj(6)
    for hh in range(ATT_HEADS):
        r0 = hh * V_ROWS
        avt_ref[0, 0, r0:r0 + ATT_DIM, :] = v[:, hh * ATT_DIM:(hh + 1) * ATT_DIM].T.astype(BF16)
        avt_ref[0, 0, r0 + ATT_DIM:r0 + V_ROWS, :] = jnp.ones((V_ROWS - ATT_DIM, v.shape[0]), BF16)

    half_b = ROPE_HALF_B
    qi = proj(7)
    for cc in range(GROUP // LANES):
        qr = _rope(qi[:, cc * LANES:(cc + 1) * LANES], cosb, sinb, half_b, IDX_DIM).astype(BF16)
        for hh in range(LANES // IDX_DIM):
            iq_ref[cc * (LANES // IDX_DIM) + hh] = qr[:, hh * IDX_DIM:(hh + 1) * IDX_DIM]
    kw = _dot(hb, wt_ref[...])
    ki = kw[:, 0:LANES]
    lane = lax.broadcasted_iota(jnp.int32, ki.shape, 1)
    real = lane < IDX_DIM
    mu = jnp.sum(ki, axis=-1, keepdims=True) * (1.0 / IDX_DIM)
    d = jnp.where(real, ki - mu, 0.0)
    var = jnp.sum(d * d, axis=-1, keepdims=True) * (1.0 / IDX_DIM)
    kn = d * lax.rsqrt(var + EPS) * lnw_ref[...] + lnb_ref[...]
    kn = _rope(kn, cosb, sinb, half_b, IDX_DIM)
    ik_ref[...] = kn[:, 0:IDX_DIM].astype(BF16)
    iwt_ref[...] = kw[:, LANES:2 * LANES].T[0:IDX_HEADS, :] * (IDX_HEADS ** -0.5 * IDX_DIM ** -0.5)


def _in_proj(x2, mod3, gain, w_main, w_tail, lbp, lnw, lnb, tab, T, tm, kt):
    M, D = x2.shape
    nt = T // tm
    row = lambda i: (i, 0)
    const = lambda i: (0, 0)
    bf16o = jax.ShapeDtypeStruct((M, GROUP), BF16)
    grp_spec = pl.BlockSpec((tm, GROUP), row)
    return pl.pallas_call(
        _in_proj_kernel,
        grid=(M // tm,),
        in_specs=[pl.BlockSpec((tm, D), row),
                  pl.BlockSpec((1, N_MOD, D), lambda i: (i // nt, 0, 0)),
                  pl.BlockSpec((1, D), const),
                  _resident(w_main.shape, const),
                  _resident(w_tail.shape, const),
                  pl.BlockSpec(lbp.shape, const),
                  pl.BlockSpec((1, LANES), const),
                  pl.BlockSpec((1, LANES), const),
                  pl.BlockSpec((1, LANES, tm), lambda i: (i // nt, 0, i % nt))],
        out_specs=[grp_spec] * 7 + [pl.BlockSpec((1, 1, ATT_HEADS * V_ROWS, tm),
                                                 lambda i: (i // nt, (i % nt) // (kt // tm), 0, (i % nt) % (kt // tm))),
                                    pl.BlockSpec((IDX_HEADS, tm, IDX_DIM), lambda i: (0, i, 0)),
                                    pl.BlockSpec((tm, IDX_DIM), row),
                                    pl.BlockSpec((IDX_HEADS, tm), lambda i: (0, i)),
                                    pl.BlockSpec((2 * ATT_HEADS, tm), lambda i: (0, i))],
        out_shape=[bf16o] * 7 + [jax.ShapeDtypeStruct((M // T, T // kt, ATT_HEADS * V_ROWS, kt), BF16),
                                 jax.ShapeDtypeStruct((IDX_HEADS, M, IDX_DIM), BF16),
                                 jax.ShapeDtypeStruct((M, IDX_DIM), BF16),
                                 jax.ShapeDtypeStruct((IDX_HEADS, M), F32),
                                 jax.ShapeDtypeStruct((2 * ATT_HEADS, M), F32)],
        compiler_params=_cparams(("parallel",)),
        name="in_proj",
    )(x2, mod3, gain, w_main, w_tail, lbp, lnw, lnb, tab)


def _hgrn_consts():
    C = HG_CHUNK
    t = np.arange(C)
    tri = (t[None, :] <= t[:, None]).astype(np.float32)
    blocks = [tri]
    for m in HG_LEVELS:
        if m % 8:
            split = (t // (2 * m)) * (2 * m) + m - 1
            upper = ((t & m) != 0)[:, None]
            blocks.append(np.where(upper, tri - tri[split], tri[split] - tri))
    mall = np.concatenate(blocks, axis=0)
    x = t[:, None] ^ t[None, :]
    lvl = np.full((C, C), len(HG_LEVELS) + 1, np.int32)
    for li, m in enumerate(HG_LEVELS):
        lvl[(t[:, None] > t[None, :]) & (x >= m) & (x < 2 * m)] = li
    lvl[t[:, None] == t[None, :]] = len(HG_LEVELS)
    return jnp.asarray(mall, BF16), jnp.asarray(lvl)


def _hgrn_kernel(hq_ref, lf_ref, hk_ref, hv_ref, hg_ref, mall_ref, lvl_ref, onorm_ref, o_ref, st_ref):
    C = HG_CHUNK
    nl = len(HG_LEVELS)

    @pl.when(pl.program_id(1) == 0)
    def _():
        st_ref[...] = jnp.zeros_like(st_ref)

    lvl = lvl_ref[...]
    row = lax.broadcasted_iota(jnp.int32, (C, HG_DIM), 0)
    mall = mall_ref[...]

    for ci in range(hq_ref.shape[0] // C):
        rs = slice(ci * C, (ci + 1) * C)
        dheads = _dot(mall, lf_ref[rs, :])
        for h in range(HG_HEADS):
            cs = slice(h * HG_DIM, (h + 1) * HG_DIM)
            q, k, v = hq_ref[rs, cs], hk_ref[rs, cs], hv_ref[rs, cs]
            qf, kf = q.astype(F32), k.astype(F32)
            dall = dheads[:, cs]
            g = dall[0:C]

            a = jnp.zeros((C, C), F32)
            fine = 0
            for li, m in enumerate(HG_LEVELS):
                if m % 8 == 0:
                    qk = jnp.concatenate([(qf if (b & 1) else kf)[b * m:(b + 1) * m] for b in range(C // m)], axis=0)
                    parts = []
                    for b in range(C // (2 * m)):
                        lo_rows = slice(2 * b * m, (2 * b + 1) * m)
                        hi_rows = slice((2 * b + 1) * m, (2 * b + 2) * m)
                        ref = jnp.broadcast_to(g[lo_rows.stop - 1:lo_rows.stop, :], (m, HG_DIM))
                        parts += [ref - g[lo_rows], g[hi_rows] - ref]
                    dm = jnp.concatenate(parts, axis=0)
                else:
                    fine += 1
                    qk = jnp.where((row & m) != 0, qf, kf)
                    dm = dall[fine * C:(fine + 1) * C]
                xm = (qk * jnp.exp2(dm)).astype(BF16)
                a = jnp.where(lvl == li, _dot_nt(xm, xm), a)
            a = jnp.where(lvl == nl, _dot_nt(q, k), a)

            st = st_ref[h]
            o = _dot_nt((qf * jnp.exp2(g)).astype(BF16), st.astype(BF16)) + _dot(a.astype(BF16), v)
            g_last = g[C - 1:C, :]
            kd = (kf * jnp.exp2(g_last - g)).astype(BF16)
            st_ref[h] = jnp.exp2(g_last) * st + _dot(v.astype(F32).T.astype(BF16), kd)

            o = o * lax.rsqrt(jnp.mean(o * o, axis=-1, keepdims=True) + EPS)
            o_ref[rs, cs] = (o * onorm_ref[:, cs] * hg_ref[rs, cs]).astype(BF16)


def _hgrn(hq, lf, hk, hv, hg, onorm, B, T, ct):
    M = hq.shape[0]
    nct = T // ct
    mall, lvl = _hgrn_consts()
    blk = pl.BlockSpec((ct, GROUP), lambda b, c: (b * nct + c, 0))
    const = lambda b, c: (0, 0)
    return pl.pallas_call(
        _hgrn_kernel,
        grid=(B, nct),
        in_specs=[blk] * 5 + [pl.BlockSpec(mall.shape, const), pl.BlockSpec(lvl.shape, const),
                              pl.BlockSpec((1, GROUP), const)],
        out_specs=blk,
        out_shape=jax.ShapeDtypeStruct((M, GROUP), BF16),
        scratch_shapes=[pltpu.VMEM((HG_HEADS, HG_DIM, HG_DIM), F32)],
        compiler_params=_cparams(("parallel", "arbitrary")),
        name="hgrn2",
    )(hq, lf, hk, hv, hg, mall, lvl, onorm)


def _fold8(a, op, rows=8):
    chains = [None] * 4
    for r in range(a.shape[0] // rows):
        part = a[r * rows:(r + 1) * rows, :]
        c = r % len(chains)
        chains[c] = part if chains[c] is None else op(chains[c], part)
    return op(op(chains[0], chains[1]), op(chains[2], chains[3]))


def _floor_bf16(a):
    r = a.astype(BF16).astype(F32)
    below = (r - jnp.abs(r) * (5.0 / 1024.0)).astype(BF16)
    return jnp.where(r > a, below.astype(F32), r).astype(BF16)


def _dsa_kernel(iq_ref, iwt_ref, ik_ref, aq_ref, ak_ref, avt_ref, qnt_ref, knt_ref, o_ref,
                sc_ref, scb_ref, mm_ref, *head_refs, qb, kt, topk, seq):
    acc_refs, m_refs = head_refs[:ATT_HEADS], head_refs[ATT_HEADS:]
    i = pl.program_id(1)
    q0 = i * qb
    nfull = q0 // kt
    nkt = nfull + 1
    qpos = q0 + lax.broadcasted_iota(jnp.int32, (1, qb), 1)
    kk = jnp.minimum(qpos + 1, topk).astype(F32)

    mm_ref[0:8, :] = jnp.full((8, qb), jnp.inf, F32)
    mm_ref[8:16, :] = jnp.full((8, qb), -jnp.inf, F32)
    mm_ref[16:32, :] = jnp.zeros((16, qb), F32)

    def score_keys(j0, ntile, diag):
        nrow = ntile * kt
        ks = pl.multiple_of(j0 * kt, kt)
        rows = pl.ds(ks, nrow)
        ki = ik_ref[rows, :]
        s = None
        for h in range(IDX_HEADS):
            t = jnp.maximum(_dot_nt(ki, iq_ref[h]), 0.0) * iwt_ref[h:h + 1, :]
            s = t if s is None else s + t
        if diag:
            causal = ks + lax.broadcasted_iota(jnp.int32, (nrow, qb), 0) <= qpos
            s_hi = jnp.where(causal, s, -jnp.inf)
            s_lo = jnp.where(causal, s, jnp.inf)
        else:
            s_hi = s_lo = s
        sc_ref[rows, :] = s_hi
        fb = _floor_bf16(s_hi)
        scb_ref[rows, :] = fb
        fb_lo = _floor_bf16(s_lo) if diag else fb
        halves = lambda a: (a[0:8, :], a[8:16, :])
        lo_a, lo_b = halves(_fold8(fb_lo, jnp.minimum, rows=16).astype(F32))
        hi_a, hi_b = halves(_fold8(fb, jnp.maximum, rows=16).astype(F32))
        ge_a, ge_b = halves(_fold8(jnp.where(fb >= 0, jnp.ones((), BF16), jnp.zeros((), BF16)), jnp.add,
                                   rows=16).astype(F32))
        mm_ref[0:8, :] = jnp.minimum(mm_ref[0:8, :], jnp.minimum(lo_a, lo_b))
        mm_ref[8:16, :] = jnp.maximum(mm_ref[8:16, :], jnp.maximum(hi_a, hi_b))
        mm_ref[16:24, :] += ge_a + ge_b
        mm_ref[24:32, :] += _fold8(jnp.where(s_hi > 0.0, 1.0, 0.0), jnp.add)

    def score_group(jg, c):
        score_keys(SCORE_GROUP * jg, SCORE_GROUP, False)
        return c

    lax.fori_loop(0, nfull // SCORE_GROUP, score_group, 0)
    done = (nfull // SCORE_GROUP) * SCORE_GROUP
    size = SCORE_GROUP // 2
    while size:
        pl.when(((nfull - done) & size) != 0)(functools.partial(score_keys, done, size, False))
        done = done + ((nfull - done) & size)
        size //= 2
    score_keys(nfull, 1, True)
    mn = jnp.min(mm_ref[0:8, :], axis=0, keepdims=True)
    mx = jnp.max(mm_ref[8:16, :], axis=0, keepdims=True)
    c_nonneg = jnp.sum(mm_ref[16:24, :], axis=0, keepdims=True)
    c_pos = jnp.sum(mm_ref[24:32, :], axis=0, keepdims=True)

    def tile_hits(ref, j, hit_fn, slab, rows, cols=slice(None)):
        acc = None
        for r in range(kt // slab):
            ks = pl.multiple_of(j * kt + r * slab, slab)
            part = _fold8(hit_fn(ref[pl.ds(ks, slab), cols]), jnp.add, rows=rows)
            acc = part if acc is None else acc + part
        return acc

    def over_tiles(body, init):
        def group(jg, acc):
            for u in range(TILE_GROUP):
                acc = body(jg * TILE_GROUP + u, acc)
            return acc
        ngroups = nkt // TILE_GROUP
        return lax.fori_loop(ngroups * TILE_GROUP, nkt, body, lax.fori_loop(0, ngroups, group, init))

    def count_ge(th):
        def body(j, acc):
            return acc + tile_hits(sc_ref, j, lambda s: jnp.where(s >= th, 1.0, 0.0), COUNT_SLAB, 8)
        return jnp.sum(over_tiles(body, jnp.zeros((8, qb), F32)), axis=0, keepdims=True)

    above = c_pos >= kk
    below = c_nonneg < kk
    live = above | below
    zero = jnp.zeros((1, qb), F32)
    lo0 = jnp.where(below, mn, zero)
    cnt_lo0 = jnp.where(below, (qpos + 1).astype(F32), c_nonneg)
    hi0 = jnp.where(above, mx + jnp.abs(mx) * 2.0 ** -6 + 1e-30, zero)
    cnt_hi0 = jnp.where(above, zero, jnp.where(below, c_nonneg, c_pos))

    def midpoint(lo, hi):
        return lo + 0.5 * (hi - lo)

    assert kt // 16 <= 256

    def count_ge_bf16(th):
        def body(j, acc):
            hit = lambda s: jnp.where(s >= th, jnp.ones((), BF16), jnp.zeros((), BF16))
            part = tile_hits(scb_ref, j, hit, 2 * COUNT_SLAB, 16).astype(F32)
            return acc + part[0:8, :] + part[8:16, :]
        return jnp.sum(over_tiles(body, jnp.zeros((8, qb), F32)), axis=0, keepdims=True)

    def coarse_body(_, c):
        lo, hi, cnt_lo, cnt_hi = c
        th = midpoint(lo, hi).astype(BF16)
        mid = th.astype(F32)
        inside = live & (mid > lo) & (mid < hi)
        cnt = count_ge_bf16(th)
        up = inside & (cnt >= kk)
        dn = inside & (cnt < kk)
        return (jnp.where(up, mid, lo), jnp.where(dn, mid, hi),
                jnp.where(up, cnt, cnt_lo), jnp.where(dn, cnt, cnt_hi))

    lo1, hi1, cnt_lo1, cnt_hi1 = lax.fori_loop(0, COARSE_PASSES, coarse_body, (lo0, hi0, cnt_lo0, cnt_hi0))

    def fine_pass(c):
        lo, hi, cnt_lo, cnt_hi = c
        mid = midpoint(lo, hi)
        cnt = count_ge(mid)
        up = live & (cnt >= kk)
        dn = live & (cnt < kk)
        return (jnp.where(up, mid, lo), jnp.where(dn, mid, hi), jnp.where(up, cnt, cnt_lo), jnp.where(dn, cnt, cnt_hi))

    c2 = lax.fori_loop(0, FINE_PASSES, lambda _, c: fine_pass(c), (lo1, hi1, cnt_lo1, cnt_hi1))

    def finish_chunk(cs):
        kk_c, live_c = kk[:, cs], live[:, cs]

        def cond(c):
            it, (lo_c, hi_c, cnt_c, _) = c
            mid = midpoint(lo_c, hi_c)
            open_ = live_c & (cnt_c != kk_c) & (mid > lo_c) & (mid < hi_c)
            return (it < 320) & (jnp.max(open_.astype(jnp.int32)) > 0)

        def body(c):
            it, (lo_c, hi_c, cnt_lo_c, cnt_hi_c) = c
            mid = midpoint(lo_c, hi_c)
            tile = lambda j, acc: acc + tile_hits(sc_ref, j, lambda s: jnp.where(s >= mid, 1.0, 0.0), COUNT_SLAB, 8, cs)
            cnt = jnp.sum(over_tiles(tile, jnp.zeros((8, LANES), F32)), axis=0, keepdims=True)
            up = live_c & (cnt >= kk_c)
            dn = live_c & (cnt < kk_c)
            return it + 1, (jnp.where(up, mid, lo_c), jnp.where(dn, mid, hi_c),
                            jnp.where(up, cnt, cnt_lo_c), jnp.where(dn, cnt, cnt_hi_c))

        return lax.while_loop(cond, body, (jnp.int32(0), tuple(a[:, cs] for a in c2)))[1]

    chunks = [slice(c * LANES, (c + 1) * LANES) for c in range(qb // LANES)]
    lo, hi, cnt_lo, cnt_hi = (jnp.concatenate(parts, axis=1) for parts in zip(*[finish_chunk(cs) for cs in chunks]))

    tied = cnt_lo != kk
    need = kk - cnt_hi

    def resolve_ties(cs):
        lo_c, need_c = lo[:, cs], need[:, cs]
        tri = jnp.where(lax.broadcasted_iota(jnp.int32, (kt, kt), 0) >= lax.broadcasted_iota(jnp.int32, (kt, kt), 1),
                        1.0, 0.0).astype(BF16)

        def strike(j, seen):
            rows = pl.ds(pl.multiple_of(j * kt, kt), kt)
            s = sc_ref[rows, cs]
            eq = s == lo_c
            rank = seen + _dot(tri, jnp.where(eq, 1.0, 0.0).astype(BF16))
            sc_ref[rows, cs] = jnp.where(eq & (rank > need_c), -jnp.inf, s)
            return rank[kt - 1:kt, :]

        lax.fori_loop(0, nkt, strike, jnp.zeros((1, LANES), F32))

    tie_bits = jnp.max(sum(jnp.max(jnp.where(tied[:, cs], 1 << c, 0), axis=1, keepdims=True)
                           for c, cs in enumerate(chunks)))
    for c, cs in enumerate(chunks):
        pl.when(((tie_bits >> c) & 1) == 1)(functools.partial(resolve_ties, cs))

    def logits(j, h, bias):
        ks = pl.multiple_of(j * kt, kt)
        sl = slice(h * ATT_DIM, (h + 1) * ATT_DIM)
        return _dot_nt(ak_ref[pl.ds(ks, kt), sl], aq_ref[:, sl]) + bias

    def select_bias(j):
        ks = pl.multiple_of(j * kt, kt)
        return jnp.where(sc_ref[pl.ds(ks, kt), :] >= lo, 0.0, NEG_BIG)

    def values(j, h):
        return avt_ref[0, j, h * V_ROWS:(h + 1) * V_ROWS, :]

    kmax = jnp.max(knt_ref[...], axis=1, keepdims=True)
    shift = [qnt_ref[h:h + 1, :] * kmax[ATT_HEADS + h:ATT_HEADS + h + 1, :] * NORM_SLACK for h in range(ATT_HEADS)]
    for h in range(ATT_HEADS):
        acc_refs[h][...] = jnp.zeros_like(acc_refs[h])

    def attn_keys_fast(j0, ntile):
        rows = pl.ds(pl.multiple_of(j0 * kt, kt), ntile * kt)
        keep = jnp.where(sc_ref[rows, :] >= lo, 1.0, 0.0).astype(BF16)
        for h in range(ATT_HEADS):
            sl = slice(h * ATT_DIM, (h + 1) * ATT_DIM)
            p = jnp.exp2(_dot_nt(ak_ref[rows, sl], aq_ref[:, sl]) - shift[h]).astype(BF16) * keep
            vals = [values(j0 + u, h) for u in range(ntile)]
            acc_refs[h][...] += _dot(vals[0] if ntile == 1 else jnp.concatenate(vals, axis=1), p)

    def attn_group(jg, c):
        attn_keys_fast(ATT_GROUP * jg, ATT_GROUP)
        return c

    lax.fori_loop(0, nkt // ATT_GROUP, attn_group, 0)
    done = (nkt // ATT_GROUP) * ATT_GROUP
    size = ATT_GROUP // 2
    while size:
        pl.when(((nkt - done) & size) != 0)(functools.partial(attn_keys_fast, done, size))
        done = done + ((nkt - done) & size)
        size //= 2
    denom_min = functools.reduce(jnp.minimum, [acc_refs[h][ATT_DIM:ATT_DIM + 1, :] for h in range(ATT_HEADS)])

    @pl.when(jnp.min(denom_min) < DENOM_FLOOR)
    def _():
        for h in range(ATT_HEADS):
            acc_refs[h][...] = jnp.zeros_like(acc_refs[h])
            m_refs[h][...] = jnp.full(m_refs[h].shape, NEG_BIG, F32)

        def attn_tile(j, c):
            bias = select_bias(j)
            for h in range(ATT_HEADS):
                acc_h, m_h = acc_refs[h], m_refs[h]
                lg = logits(j, h, bias)
                m_old = m_h[0:1, :]
                m_new = jnp.maximum(m_old, jnp.max(_fold8(lg, jnp.maximum), axis=0, keepdims=True))
                p = jnp.exp2(lg - m_new).astype(BF16)
                acc_h[...] = jnp.exp2(m_old - m_new) * acc_h[...] + _dot(values(j, h), p)
                m_h[0:1, :] = m_new
            return c

        lax.fori_loop(0, nkt, attn_tile, 0)

    for h in range(ATT_HEADS):
        o_t = acc_refs[h][0:ATT_DIM, :] / acc_refs[h][ATT_DIM:ATT_DIM + 1, :]
        o_ref[:, h * ATT_DIM:(h + 1) * ATT_DIM] = o_t.T.astype(BF16)


def _dsa(iq, iwt, ik, aq, ak, avt, nrm, B, T, qb, kt):
    M = ak.shape[0]
    nq = T // qb
    assert kt % qb == 0 and T % kt == 0 and avt.shape[1:] == (T // kt, ATT_HEADS * V_ROWS, kt)
    topk = min(TOPK_MAX, T // 4)
    qrow = lambda b, i: (b * nq + i, 0)
    brow = lambda b, i: (b, 0)
    kern = functools.partial(_dsa_kernel, qb=qb, kt=kt, topk=topk, seq=T)
    return pl.pallas_call(
        kern,
        grid=(B, nq),
        in_specs=[pl.BlockSpec((IDX_HEADS, qb, IDX_DIM), lambda b, i: (0, b * nq + i, 0)),
                  pl.BlockSpec((IDX_HEADS, qb), lambda b, i: (0, b * nq + i)),
                  _resident((T, IDX_DIM), brow),
                  pl.BlockSpec((qb, GROUP), qrow),
                  _resident((T, GROUP), brow),
                  _resident((1,) + avt.shape[1:], lambda b, i: (b, 0, 0, 0)),
                  pl.BlockSpec((2 * ATT_HEADS, qb), lambda b, i: (0, b * nq + i)),
                  _resident((2 * ATT_HEADS, T), lambda b, i: (0, b))],
        out_specs=pl.BlockSpec((qb, GROUP), qrow),
        out_shape=jax.ShapeDtypeStruct((M, GROUP), BF16),
        scratch_shapes=[pltpu.VMEM((T, qb), F32),
                        pltpu.VMEM((T, qb), BF16),
                        pltpu.VMEM((32, qb), F32),
                        ] + [pltpu.VMEM((V_ROWS, qb), F32)] * ATT_HEADS
                        + [pltpu.VMEM((8, qb), F32)] * ATT_HEADS,
        compiler_params=_cparams(("parallel", "arbitrary")),
        name="dsa",
    )(iq, iwt, ik, aq, ak, avt, nrm, nrm)


def _rms(a, gain):
    return a * lax.rsqrt(jnp.mean(a * a, axis=-1, keepdims=True) + EPS) * gain


def _out_ffn_kernel(ohg_ref, oatt_ref, x_ref, mod_ref, gains_ref, wo_ref, wi_ref, wf_ref, o_ref, acc_ref,
                    *, d_ff, fc):
    mod = mod_ref[0]
    gains = gains_ref[...]
    y = _dot(jnp.concatenate([ohg_ref[...], oatt_ref[...]], axis=1), wo_ref[...])
    x1 = x_ref[...] + mod[2:3, :] * _rms(y, gains[0:1, :])
    h2 = (_rms(x1, gains[1:2, :]) * (1.0 + mod[4:5, :]) + mod[3:4, :]).astype(BF16)
    for c in range(d_ff // fc):
        gate = _dot(h2, wi_ref[:, c * fc:(c + 1) * fc])
        up = _dot(h2, wi_ref[:, d_ff + c * fc:d_ff + (c + 1) * fc])
        part = _dot((_silu(gate) * up).astype(BF16), wf_ref[c * fc:(c + 1) * fc, :])
        if c == 0:
            acc_ref[...] = part
        else:
            acc_ref[...] += part
    o_ref[...] = x1 + mod[5:6, :] * _rms(acc_ref[...], gains[2:3, :])


def _out_ffn(ohg, oatt, x2, mod3, gains, wo, wi, wf, T, tm):
    M, D = x2.shape
    nt = T // tm
    d_ff = wf.shape[0]
    row = lambda i: (i, 0)
    const = lambda i: (0, 0)
    assert d_ff % FFN_CHUNK == 0
    kern = functools.partial(_out_ffn_kernel, d_ff=d_ff, fc=FFN_CHUNK)
    return pl.pallas_call(
        kern,
        grid=(M // tm,),
        in_specs=[pl.BlockSpec((tm, GROUP), row),
                  pl.BlockSpec((tm, GROUP), row),
                  pl.BlockSpec((tm, D), row),
                  pl.BlockSpec((1, N_MOD, D), lambda i: (i // nt, 0, 0)),
                  pl.BlockSpec((3, D), const),
                  _resident(wo.shape, const),
                  _resident(wi.shape, const),
                  _resident(wf.shape, const)],
        out_specs=pl.BlockSpec((tm, D), row),
        out_shape=jax.ShapeDtypeStruct((M, D), F32),
        scratch_shapes=[pltpu.VMEM((tm, D), F32)],
        compiler_params=_cparams(("parallel",)),
        name="out_ffn",
    )(ohg, oatt, x2, mod3, gains, wo, wi, wf)


def kernel(x, c, positions, w_ada, b_ada, norm_pre_mix, norm_post_mix, norm_pre_ffn, norm_post_ffn, w_in,
           hgrn_lower_bound, hgrn_out_norm, idx_k_norm_w, idx_k_norm_b, w_out, w_ffn_in, w_ffn_out):
    B, T, D = x.shape
    depth = w_ada.shape[0]
    assert depth == 1 and hgrn_lower_bound.shape[0] == 2
    tm = ROW_TILE
    assert T % tm == 0 and T % KEY_TILE == 0 and T % QUERY_BLOCK == 0 and D % LANES == 0 and tm == KEY_TILE

    mod3 = _adaln(c, w_ada, b_ada).reshape(B, N_MOD, D)
    tab = _rope_table(positions)

    main = 8 * GROUP
    w = w_in.reshape(w_in.shape[1:])
    w_main = w[:, :main].astype(BF16)
    w_tail = jnp.concatenate([jnp.pad(w[:, main:main + IDX_DIM], ((0, 0), (0, LANES - IDX_DIM))),
                              jnp.pad(w[:, main + IDX_DIM:], ((0, 0), (0, LANES - IDX_HEADS)))], axis=1).astype(BF16)
    lnw = jnp.pad(idx_k_norm_w[0], (0, LANES - IDX_DIM)).reshape(1, LANES)
    lnb = jnp.pad(idx_k_norm_b[0], (0, LANES - IDX_DIM)).reshape(1, LANES)

    x2 = x.reshape(B * T, D)
    hq, lf, hk, hv, hg, aq, ak, avt, iq, ik, iwt, nrm = _in_proj(
        x2, mod3, norm_pre_mix, w_main, w_tail, hgrn_lower_bound, lnw, lnb, tab, T, tm, tm)

    o_hg = _hgrn(hq, lf, hk, hv, hg, hgrn_out_norm, B, T, tm)
    o_att = _dsa(iq, iwt, ik, aq, ak, avt, nrm, B, T, QUERY_BLOCK, KEY_TILE)

    gains = jnp.concatenate([norm_post_mix, norm_pre_ffn, norm_post_ffn], axis=0)
    out = _out_ffn(o_hg, o_att, x2, mod3, gains, w_out.reshape(w_out.shape[1:]).astype(BF16),
                   w_ffn_in.reshape(w_ffn_in.shape[1:]).astype(BF16),
                   w_ffn_out.reshape(w_ffn_out.shape[1:]).astype(BF16), T, tm)
    return out.reshape(B, T, D)
```
